```python
import jax, jax.numpy as jnp
from jax import lax
import numpy as np

D_MODEL = 2048
BATCH = 8
SEQ = 8192
DEPTH = 4

MIX_WIDTH = D_MODEL
ATT_HEADS = 8
ATT_KV_HEADS = 2
ATT_WIDTH = MIX_WIDTH // 2
ATT_HEAD_DIM = ATT_WIDTH // ATT_HEADS
KV_WIDTH = ATT_KV_HEADS * ATT_HEAD_DIM
WINDOW = 128
ATT_BLOCK = 128
ROPE_THETA = 10000.0
M_WIDTH = MIX_WIDTH - ATT_WIDTH
M_HEADS = 4
M_HEAD_DIM = M_WIDTH // M_HEADS
M_CHUNK = 128
CONV_WIDTH = 5
D_FF = ((8 * D_MODEL // 3 + 255) // 256) * 256
EPS = 1e-6
IN_SIZES = (ATT_WIDTH, KV_WIDTH, KV_WIDTH, M_WIDTH, M_WIDTH, M_WIDTH, M_WIDTH, 4 * M_HEADS)
IN_WIDTH = sum(IN_SIZES)
IN_SPLITS = tuple(int(s) for s in np.cumsum(IN_SIZES[:-1]))

kernel_name = "hymba_style_mlstm_swa_macaron_encoder"


def rms_norm(x, g):
    xf = x.astype(jnp.float32)
    y = xf * lax.rsqrt(jnp.mean(xf * xf, axis=-1, keepdims=True) + EPS)
    return (y * g.astype(jnp.float32)).astype(x.dtype)


def swiglu(x, w_gate, w_up, w_down):
    return (jax.nn.silu(x @ w_gate) * (x @ w_up)) @ w_down


def rope(x, pos):
    half = x.shape[-1] // 2
    inv_freq = ROPE_THETA ** (-jnp.arange(half, dtype=jnp.float32) / half)
    ang = pos[:, None] * inv_freq[None, :]
    cos = jnp.cos(ang)[None, :, None, :]
    sin = jnp.sin(ang)[None, :, None, :]
    xf = x.astype(jnp.float32)
    x1, x2 = xf[..., :half], xf[..., half:]
    out = jnp.concatenate([x1 * cos - x2 * sin, x2 * cos + x1 * sin], axis=-1)
    return out.astype(x.dtype)


def window_attention(q, k, v, sink):
    B, S, _, D = q.shape
    nb = S // ATT_BLOCK
    G = ATT_HEADS // ATT_KV_HEADS
    qb = q.reshape(B, nb, ATT_BLOCK, ATT_KV_HEADS, G, D)

    def neighbours(t):
        tp = jnp.pad(t, ((0, 0), (ATT_BLOCK, ATT_BLOCK), (0, 0), (0, 0)))
        tp = tp.reshape(B, nb + 2, ATT_BLOCK, ATT_KV_HEADS, D)
        return jnp.concatenate([tp[:, :-2], tp[:, 1:-1], tp[:, 2:]], axis=2)

    kb, vb = neighbours(k), neighbours(v)
    scores = jnp.einsum('bnqhgd,bnkhd->bnhgqk', qb, kb).astype(jnp.float32) * (D ** -0.5)
    r = jnp.arange(ATT_BLOCK)[:, None]
    c = jnp.arange(3 * ATT_BLOCK)[None, :]
    kpos = (jnp.arange(nb)[:, None, None] - 1) * ATT_BLOCK + c[None]
    mask = (jnp.abs(c - ATT_BLOCK - r) <= WINDOW)[None] & (kpos >= 0) & (kpos < S)
    scores = jnp.where(mask[None, :, None, None], scores, -jnp.inf)
    sink_l = sink.astype(jnp.float32).reshape(ATT_KV_HEADS, G)[None, None, :, :, None, None]
    m = jnp.maximum(scores.max(axis=-1, keepdims=True), sink_l)
    p = jnp.exp(scores - m)
    probs = p / (p.sum(axis=-1, keepdims=True) + jnp.exp(sink_l - m))
    out = jnp.einsum('bnhgqk,bnkhd->bnqhgd', probs.astype(v.dtype), vb)
    return out.reshape(B, S, ATT_HEADS * D)


def mlstm_scan(q, k, v, ig, lf):
    B, S, H, D = q.shape
    L = M_CHUNK
    nc = S // L
    to_c = lambda t: t.reshape(B, nc, L, H, D).transpose(1, 0, 3, 2, 4)
    to_cg = lambda g: g.reshape(B, nc, L, H).transpose(1, 0, 3, 2)
    tri = jnp.tril(jnp.ones((L, L), dtype=bool))

    def step(carry, xs):
        C, n, m = carry
        qc, kc, vc, igc, lfc = xs
        b = jnp.cumsum(lfc, axis=-1)
        log_d = b[..., :, None] - b[..., None, :] + igc[..., None, :]
        log_d = jnp.where(tri, log_d, -jnp.inf)
        log_inter = b + m[..., None]
        m_t = jnp.maximum(log_inter, log_d.max(axis=-1))
        d_mat = jnp.exp(log_d - m_t[..., None])
        inter = jnp.exp(log_inter - m_t)
        s = jnp.einsum('bhtd,bhsd->bhts', qc, kc) * d_mat
        num = jnp.einsum('bhts,bhsd->bhtd', s, vc) + inter[..., None] * jnp.einsum('bhvk,bhtk->bhtv', C, qc)
        den = s.sum(axis=-1) + inter * jnp.einsum('bhk,bhtk->bht', n, qc)
        h = num / jnp.maximum(jnp.abs(den), jnp.exp(-m_t))[..., None]
        b_last = b[..., -1]
        log_w = b_last[..., None] - b + igc
        m_new = jnp.maximum(b_last + m, log_w.max(axis=-1))
        w = jnp.exp(log_w - m_new[..., None])
        decay = jnp.exp(b_last + m - m_new)
        C_new = decay[..., None, None] * C + jnp.einsum('bhsv,bhsk->bhvk', w[..., None] * vc, kc)
        n_new = decay[..., None] * n + jnp.einsum('bhs,bhsk->bhk', w, kc)
        return (C_new, n_new, m_new), h

    init = (jnp.zeros((B, H, D, D), jnp.float32), jnp.zeros((B, H, D), jnp.float32),
            jnp.zeros((B, H), jnp.float32))
    _, hs = lax.scan(step, init, (to_c(q), to_c(k), to_c(v), to_cg(ig), to_cg(lf)))
    return hs.transpose(1, 0, 3, 2, 4).reshape(B, S, H, D)


def depthwise_conv_centred(x, w):
    C = x.shape[-1]
    pad = CONV_WIDTH // 2
    return lax.conv_general_dilated(x, w[:, None, :].astype(x.dtype), window_strides=(1,),
                                    padding=[(pad, pad)], dimension_numbers=('NWC', 'WIO', 'NWC'),
                                    feature_group_count=C)


def hybrid_mixer(h, w_in, b_gate, conv_w, sink, m_norm, w_out, pos):
    B, S, _ = h.shape
    proj = h @ w_in
    qa, ka, va, qm, km, vm, om, gates = jnp.split(proj, IN_SPLITS, axis=-1)
    qa = rope(qa.reshape(B, S, ATT_HEADS, ATT_HEAD_DIM), pos)
    ka = rope(ka.reshape(B, S, ATT_KV_HEADS, ATT_HEAD_DIM), pos)
    va = va.reshape(B, S, ATT_KV_HEADS, ATT_HEAD_DIM)
    y_att = window_attention(qa, ka, va, sink)
    qk = jax.nn.silu(depthwise_conv_centred(jnp.concatenate([qm, km], axis=-1), conv_w))
    qm, km = qk[..., :M_WIDTH], qk[..., M_WIDTH:]
    qm = qm.astype(jnp.float32).reshape(B, S, M_HEADS, M_HEAD_DIM)
    km = km.astype(jnp.float32).reshape(B, S, M_HEADS, M_HEAD_DIM) * (M_HEAD_DIM ** -0.5)
    vm = vm.astype(jnp.float32).reshape(B, S, M_HEADS, M_HEAD_DIM)
    g = gates.astype(jnp.float32) + b_gate.astype(jnp.float32)
    ig_f, ig_b, fg_f, fg_b = jnp.split(g, 4, axis=-1)
    h_fwd = mlstm_scan(qm, km, vm, ig_f, jax.nn.log_sigmoid(fg_f))
    flip = lambda t: jnp.flip(t, axis=1)
    h_bwd = flip(mlstm_scan(flip(qm), flip(km), flip(vm), flip(ig_b), flip(jax.nn.log_sigmoid(fg_b))))
    hm = h_fwd + h_bwd
    hm = hm * lax.rsqrt(jnp.mean(hm * hm, axis=-1, keepdims=True) + EPS)
    hm = hm * m_norm.astype(jnp.float32).reshape(M_HEADS, M_HEAD_DIM)
    y_m = (jax.nn.sigmoid(om.astype(jnp.float32)) * hm.reshape(B, S, M_WIDTH)).astype(h.dtype)
    return jnp.concatenate([y_att, y_m], axis=-1) @ w_out


def _fwd_setup_inputs(seed: int = 0) -> dict:
    key = jax.random.key(seed)
    ks = jax.random.split(key, 24)
    f32 = jnp.float32
    nrm = lambda k, shape, scale: jax.random.normal(k, shape, f32) * scale
    gain = lambda k: 1.0 + nrm(k, (DEPTH, D_MODEL), 0.05)
    f_bias = jnp.linspace(3.0, 6.0, M_HEADS, dtype=f32)
    b_gate = jnp.concatenate([
        nrm(ks[20], (DEPTH, 2 * M_HEADS), 0.1),
        jnp.tile(f_bias, 2)[None, :] + nrm(ks[21], (DEPTH, 2 * M_HEADS), 0.1),
    ], axis=-1)
    return {
        "x": nrm(ks[0], (BATCH, SEQ, D_MODEL), 1.0),
        "ffn1_norm_pre": gain(ks[1]),
        "ffn1_norm_post": gain(ks[2]),
        "ffn1_w_gate": nrm(ks[3], (DEPTH, D_MODEL, D_FF), D_MODEL ** -0.5),
        "ffn1_w_up": nrm(ks[4], (DEPTH, D_MODEL, D_FF), D_MODEL ** -0.5),
        "ffn1_w_down": nrm(ks[5], (DEPTH, D_FF, D_MODEL), D_FF ** -0.5),
        "mix_norm_pre": gain(ks[6]),
        "mix_norm_post": gain(ks[7]),
        "w_in": nrm(ks[8], (DEPTH, D_MODEL, IN_WIDTH), D_MODEL ** -0.5),
        "b_gate": b_gate,
        "conv_w": nrm(ks[9], (DEPTH, CONV_WIDTH, 2 * M_WIDTH), CONV_WIDTH ** -0.5),
        "attn_sink": nrm(ks[10], (DEPTH, ATT_HEADS), 0.5),
        "mlstm_norm": 1.0 + nrm(ks[11], (DEPTH, M_WIDTH), 0.05),
        "w_out": nrm(ks[12], (DEPTH, MIX_WIDTH, D_MODEL), MIX_WIDTH ** -0.5),
        "ffn2_norm_pre": gain(ks[13]),
        "ffn2_norm_post": gain(ks[14]),
        "ffn2_w_gate": nrm(ks[15], (DEPTH, D_MODEL, D_FF), D_MODEL ** -0.5),
        "ffn2_w_up": nrm(ks[16], (DEPTH, D_MODEL, D_FF), D_MODEL ** -0.5),
        "ffn2_w_down": nrm(ks[17], (DEPTH, D_FF, D_MODEL), D_FF ** -0.5),
    }


def _fwd_reference(x, ffn1_norm_pre, ffn1_norm_post, ffn1_w_gate, ffn1_w_up, ffn1_w_down,
              mix_norm_pre, mix_norm_post, w_in, b_gate, conv_w, attn_sink, mlstm_norm, w_out,
              ffn2_norm_pre, ffn2_norm_post, ffn2_w_gate, ffn2_w_up, ffn2_w_down):
    pos = jnp.arange(x.shape[1], dtype=jnp.float32)
    for l in range(DEPTH):
        f = swiglu(rms_norm(x, ffn1_norm_pre[l]), ffn1_w_gate[l], ffn1_w_up[l], ffn1_w_down[l])
        x = x + 0.5 * rms_norm(f, ffn1_norm_post[l])
        m = hybrid_mixer(rms_norm(x, mix_norm_pre[l]), w_in[l], b_gate[l], conv_w[l],
                         attn_sink[l], mlstm_norm[l], w_out[l], pos)
        x = x + rms_norm(m, mix_norm_post[l])
        f = swiglu(rms_norm(x, ffn2_norm_pre[l]), ffn2_w_gate[l], ffn2_w_up[l], ffn2_w_down[l])
        x = x + 0.5 * rms_norm(f, ffn2_norm_post[l])
    return x


import jax as _jax
import jax.numpy as _jnp

TWIN_FORMAT = 'train_step'
FWD_PARAMS = ['x', 'ffn1_norm_pre', 'ffn1_norm_post', 'ffn1_w_gate', 'ffn1_w_up', 'ffn1_w_down', 'mix_norm_pre', 'mix_norm_post', 'w_in', 'b_gate', 'conv_w', 'attn_sink', 'mlstm_norm', 'w_out', 'ffn2_norm_pre', 'ffn2_norm_post', 'ffn2_w_gate', 'ffn2_w_up', 'ffn2_w_down']
TWIN_WEIGHTS = ['ffn1_norm_pre', 'ffn1_norm_post', 'ffn1_w_gate', 'ffn1_w_up', 'ffn1_w_down', 'mix_norm_pre', 'mix_norm_post', 'w_in', 'b_gate', 'conv_w', 'attn_sink', 'mlstm_norm', 'w_out', 'ffn2_norm_pre', 'ffn2_norm_post', 'ffn2_w_gate', 'ffn2_w_up', 'ffn2_w_down']
TWIN_DIFF_INPUT = 'x'
TWIN_INPUTS = ['x', 'ffn1_norm_pre', 'ffn1_norm_post', 'ffn1_w_gate', 'ffn1_w_up', 'ffn1_w_down', 'mix_norm_pre', 'mix_norm_post', 'w_in', 'b_gate', 'conv_w', 'attn_sink', 'mlstm_norm', 'w_out', 'ffn2_norm_pre', 'ffn2_norm_post', 'ffn2_w_gate', 'ffn2_w_up', 'ffn2_w_down', 'loss_target', 'm_ffn1_norm_pre', 'm_ffn1_norm_post', 'm_ffn1_w_gate', 'm_ffn1_w_up', 'm_ffn1_w_down', 'm_mix_norm_pre', 'm_mix_norm_post', 'm_w_in', 'm_b_gate', 'm_conv_w', 'm_attn_sink', 'm_mlstm_norm', 'm_w_out', 'm_ffn2_norm_pre', 'm_ffn2_norm_post', 'm_ffn2_w_gate', 'm_ffn2_w_up', 'm_ffn2_w_down', 'v_ffn1_norm_pre', 'v_ffn1_norm_post', 'v_ffn1_w_gate', 'v_ffn1_w_up', 'v_ffn1_w_down', 'v_mix_norm_pre', 'v_mix_norm_post', 'v_w_in', 'v_b_gate', 'v_conv_w', 'v_attn_sink', 'v_mlstm_norm', 'v_w_out', 'v_ffn2_norm_pre', 'v_ffn2_norm_post', 'v_ffn2_w_gate', 'v_ffn2_w_up', 'v_ffn2_w_down']
TWIN_OUTPUTS = ['loss', 'grad_x', 'grad_ffn1_norm_pre', 'grad_ffn1_norm_post', 'grad_ffn1_w_gate', 'grad_ffn1_w_up', 'grad_ffn1_w_down', 'grad_mix_norm_pre', 'grad_mix_norm_post', 'grad_w_in', 'grad_b_gate', 'grad_conv_w', 'grad_attn_sink', 'grad_mlstm_norm', 'grad_w_out', 'grad_ffn2_norm_pre', 'grad_ffn2_norm_post', 'grad_ffn2_w_gate', 'grad_ffn2_w_up', 'grad_ffn2_w_down', 'delta_ffn1_norm_pre', 'delta_ffn1_norm_post', 'delta_ffn1_w_gate', 'delta_ffn1_w_up', 'delta_ffn1_w_down', 'delta_mix_norm_pre', 'delta_mix_norm_post', 'delta_w_in', 'delta_b_gate', 'delta_conv_w', 'delta_attn_sink', 'delta_mlstm_norm', 'delta_w_out', 'delta_ffn2_norm_pre', 'delta_ffn2_norm_post', 'delta_ffn2_w_gate', 'delta_ffn2_w_up', 'delta_ffn2_w_down', 'new_m_ffn1_norm_pre', 'new_m_ffn1_norm_post', 'new_m_ffn1_w_gate', 'new_m_ffn1_w_up', 'new_m_ffn1_w_down', 'new_m_mix_norm_pre', 'new_m_mix_norm_post', 'new_m_w_in', 'new_m_b_gate', 'new_m_conv_w', 'new_m_attn_sink', 'new_m_mlstm_norm', 'new_m_w_out', 'new_m_ffn2_norm_pre', 'new_m_ffn2_norm_post', 'new_m_ffn2_w_gate', 'new_m_ffn2_w_up', 'new_m_ffn2_w_down', 'new_v_ffn1_norm_pre', 'new_v_ffn1_norm_post', 'new_v_ffn1_w_gate', 'new_v_ffn1_w_up', 'new_v_ffn1_w_down', 'new_v_mix_norm_pre', 'new_v_mix_norm_post', 'new_v_w_in', 'new_v_b_gate', 'new_v_conv_w', 'new_v_attn_sink', 'new_v_mlstm_norm', 'new_v_w_out', 'new_v_ffn2_norm_pre', 'new_v_ffn2_norm_post', 'new_v_ffn2_w_gate', 'new_v_ffn2_w_up', 'new_v_ffn2_w_down']
TWIN_LEAF_KINDS = {'loss': 'loss', 'grad_x': 'grad_x', 'grad_ffn1_norm_pre': 'grad_w', 'grad_ffn1_norm_post': 'grad_w', 'grad_ffn1_w_gate': 'grad_w', 'grad_ffn1_w_up': 'grad_w', 'grad_ffn1_w_down': 'grad_w', 'grad_mix_norm_pre': 'grad_w', 'grad_mix_norm_post': 'grad_w', 'grad_w_in': 'grad_w', 'grad_b_gate': 'grad_w', 'grad_conv_w': 'grad_w', 'grad_attn_sink': 'grad_w', 'grad_mlstm_norm': 'grad_w', 'grad_w_out': 'grad_w', 'grad_ffn2_norm_pre': 'grad_w', 'grad_ffn2_norm_post': 'grad_w', 'grad_ffn2_w_gate': 'grad_w', 'grad_ffn2_w_up': 'grad_w', 'grad_ffn2_w_down': 'grad_w', 'delta_ffn1_norm_pre': 'delta_w', 'delta_ffn1_norm_post': 'delta_w', 'delta_ffn1_w_gate': 'delta_w', 'delta_ffn1_w_up': 'delta_w', 'delta_ffn1_w_down': 'delta_w', 'delta_mix_norm_pre': 'delta_w', 'delta_mix_norm_post': 'delta_w', 'delta_w_in': 'delta_w', 'delta_b_gate': 'delta_w', 'delta_conv_w': 'delta_w', 'delta_attn_sink': 'delta_w', 'delta_mlstm_norm': 'delta_w', 'delta_w_out': 'delta_w', 'delta_ffn2_norm_pre': 'delta_w', 'delta_ffn2_norm_post': 'delta_w', 'delta_ffn2_w_gate': 'delta_w', 'delta_ffn2_w_up': 'delta_w', 'delta_ffn2_w_down': 'delta_w', 'new_m_ffn1_norm_pre': 'new_m', 'new_m_ffn1_norm_post': 'new_m', 'new_m_ffn1_w_gate': 'new_m', 'new_m_ffn1_w_up': 'new_m', 'new_m_ffn1_w_down': 'new_m', 'new_m_mix_norm_pre': 'new_m', 'new_m_mix_norm_post': 'new_m', 'new_m_w_in': 'new_m', 'new_m_b_gate': 'new_m', 'new_m_conv_w': 'new_m', 'new_m_attn_sink': 'new_m', 'new_m_mlstm_norm': 'new_m', 'new_m_w_out': 'new_m', 'new_m_ffn2_norm_pre': 'new_m', 'new_m_ffn2_norm_post': 'new_m', 'new_m_ffn2_w_gate': 'new_m', 'new_m_ffn2_w_up': 'new_m', 'new_m_ffn2_w_down': 'new_m', 'new_v_ffn1_norm_pre': 'new_v', 'new_v_ffn1_norm_post': 'new_v', 'new_v_ffn1_w_gate': 'new_v', 'new_v_ffn1_w_up': 'new_v', 'new_v_ffn1_w_down': 'new_v', 'new_v_mix_norm_pre': 'new_v', 'new_v_mix_norm_post': 'new_v', 'new_v_w_in': 'new_v', 'new_v_b_gate': 'new_v', 'new_v_conv_w': 'new_v', 'new_v_attn_sink': 'new_v', 'new_v_mlstm_norm': 'new_v', 'new_v_w_out': 'new_v', 'new_v_ffn2_norm_pre': 'new_v', 'new_v_ffn2_norm_post': 'new_v', 'new_v_ffn2_w_gate': 'new_v', 'new_v_ffn2_w_up': 'new_v', 'new_v_ffn2_w_down': 'new_v'}


def _forward(args):
    return _fwd_reference(*[args[k] for k in FWD_PARAMS])


def _output_shape():
    def fwd():
        inp = _fwd_setup_inputs(0)
        return _fwd_reference(*[inp[k] for k in FWD_PARAMS])
    out = _jax.eval_shape(fwd)
    return out.shape, out.dtype

N_MICROBATCH = 1
ADAM_LR = 0.001
ADAM_B1 = 0.9
ADAM_B2 = 0.999
ADAM_EPS = 1e-08
ADAM_WD = 0.01
ADAM_STEP = 10
PER_EXAMPLE_BATCH_AXIS = {'x': 0, 'loss_target': 0}
SHARED_INPUTS = []
_WEIGHT_DTYPES = {'ffn1_norm_pre': _jnp.float32, 'ffn1_norm_post': _jnp.float32, 'ffn1_w_gate': _jnp.float32, 'ffn1_w_up': _jnp.float32, 'ffn1_w_down': _jnp.float32, 'mix_norm_pre': _jnp.float32, 'mix_norm_post': _jnp.float32, 'w_in': _jnp.float32, 'b_gate': _jnp.float32, 'conv_w': _jnp.float32, 'attn_sink': _jnp.float32, 'mlstm_norm': _jnp.float32, 'w_out': _jnp.float32, 'ffn2_norm_pre': _jnp.float32, 'ffn2_norm_post': _jnp.float32, 'ffn2_w_gate': _jnp.float32, 'ffn2_w_up': _jnp.float32, 'ffn2_w_down': _jnp.float32}
MOMENT_SCALE = {'ffn1_norm_pre': 5.225815e+00, 'ffn1_norm_post': 7.687446e+00, 'ffn1_w_gate': 2.031147e+00, 'ffn1_w_up': 2.082098e+00, 'ffn1_w_down': 3.522748e+00, 'mix_norm_pre': 1.153463e+01, 'mix_norm_post': 3.067359e+01, 'w_in': 6.126113e+00, 'b_gate': 2.273454e+01, 'conv_w': 1.220204e+00, 'attn_sink': 2.983920e-01, 'mlstm_norm': 1.068031e+01, 'w_out': 8.772680e+00, 'ffn2_norm_pre': 1.923414e+00, 'ffn2_norm_post': 7.372755e+00, 'ffn2_w_gate': 6.968613e-01, 'ffn2_w_up': 9.184379e-01, 'ffn2_w_down': 1.534112e+00}


def _to_microbatches(a, axis):
    t = _jnp.moveaxis(a, axis, 0)
    t = t.reshape((N_MICROBATCH, t.shape[0] // N_MICROBATCH) + t.shape[1:])
    return _jnp.moveaxis(t, 1, axis + 1)


def setup_inputs(seed: int = 0) -> dict:
    inp = _fwd_setup_inputs(seed)
    key = _jax.random.fold_in(_jax.random.key(seed), 7919)
    shape, _ = _output_shape()
    out = dict(inp)
    out["loss_target"] = _jax.random.normal(_jax.random.fold_in(key, 0), shape, _jnp.float32)
    for i, name in enumerate(TWIN_WEIGHTS):
        w = inp[name].astype(_jnp.float32)
        if MOMENT_SCALE is None:
            s = _jnp.sqrt(_jnp.mean(_jnp.square(w)) + 1e-30)
        else:
            s = MOMENT_SCALE[name]
        km, kv = _jax.random.split(_jax.random.fold_in(key, i + 1))
        out[name] = w
        out["m_" + name] = s * _jax.random.normal(km, w.shape, _jnp.float32)
        out["v_" + name] = (s * s) * _jax.random.uniform(kv, w.shape, _jnp.float32, 0.5, 1.5)
    if N_MICROBATCH > 1:
        for name, axis in PER_EXAMPLE_BATCH_AXIS.items():
            out[name] = _to_microbatches(out[name], axis)
    return {'x': out['x'], 'ffn1_norm_pre': out['ffn1_norm_pre'], 'ffn1_norm_post': out['ffn1_norm_post'], 'ffn1_w_gate': out['ffn1_w_gate'], 'ffn1_w_up': out['ffn1_w_up'], 'ffn1_w_down': out['ffn1_w_down'], 'mix_norm_pre': out['mix_norm_pre'], 'mix_norm_post': out['mix_norm_post'], 'w_in': out['w_in'], 'b_gate': out['b_gate'], 'conv_w': out['conv_w'], 'attn_sink': out['attn_sink'], 'mlstm_norm': out['mlstm_norm'], 'w_out': out['w_out'], 'ffn2_norm_pre': out['ffn2_norm_pre'], 'ffn2_norm_post': out['ffn2_norm_post'], 'ffn2_w_gate': out['ffn2_w_gate'], 'ffn2_w_up': out['ffn2_w_up'], 'ffn2_w_down': out['ffn2_w_down'], 'loss_target': out['loss_target'], 'm_ffn1_norm_pre': out['m_ffn1_norm_pre'], 'm_ffn1_norm_post': out['m_ffn1_norm_post'], 'm_ffn1_w_gate': out['m_ffn1_w_gate'], 'm_ffn1_w_up': out['m_ffn1_w_up'], 'm_ffn1_w_down': out['m_ffn1_w_down'], 'm_mix_norm_pre': out['m_mix_norm_pre'], 'm_mix_norm_post': out['m_mix_norm_post'], 'm_w_in': out['m_w_in'], 'm_b_gate': out['m_b_gate'], 'm_conv_w': out['m_conv_w'], 'm_attn_sink': out['m_attn_sink'], 'm_mlstm_norm': out['m_mlstm_norm'], 'm_w_out': out['m_w_out'], 'm_ffn2_norm_pre': out['m_ffn2_norm_pre'], 'm_ffn2_norm_post': out['m_ffn2_norm_post'], 'm_ffn2_w_gate': out['m_ffn2_w_gate'], 'm_ffn2_w_up': out['m_ffn2_w_up'], 'm_ffn2_w_down': out['m_ffn2_w_down'], 'v_ffn1_norm_pre': out['v_ffn1_norm_pre'], 'v_ffn1_norm_post': out['v_ffn1_norm_post'], 'v_ffn1_w_gate': out['v_ffn1_w_gate'], 'v_ffn1_w_up': out['v_ffn1_w_up'], 'v_ffn1_w_down': out['v_ffn1_w_down'], 'v_mix_norm_pre': out['v_mix_norm_pre'], 'v_mix_norm_post': out['v_mix_norm_post'], 'v_w_in': out['v_w_in'], 'v_b_gate': out['v_b_gate'], 'v_conv_w': out['v_conv_w'], 'v_attn_sink': out['v_attn_sink'], 'v_mlstm_norm': out['v_mlstm_norm'], 'v_w_out': out['v_w_out'], 'v_ffn2_norm_pre': out['v_ffn2_norm_pre'], 'v_ffn2_norm_post': out['v_ffn2_norm_post'], 'v_ffn2_w_gate': out['v_ffn2_w_gate'], 'v_ffn2_w_up': out['v_ffn2_w_up'], 'v_ffn2_w_down': out['v_ffn2_w_down']}


def _loss(weights, diff, rest, loss_target):
    with _jax.named_scope("forward"):
        args = {**rest, TWIN_DIFF_INPUT: diff, **{k: w.astype(_WEIGHT_DTYPES[k]) for k, w in weights.items()}}
        y = _forward(args)
    with _jax.named_scope("loss_head"):
        err = _jnp.square(y.astype(_jnp.float32) - loss_target)
        return 0.5 * _jnp.sum(_jnp.mean(err, axis=-1)) if err.ndim else 0.5 * err


def _adamw(w, g, m, v):
    m = ADAM_B1 * m + (1.0 - ADAM_B1) * g
    v = ADAM_B2 * v + (1.0 - ADAM_B2) * _jnp.square(g)
    m_hat = m / (1.0 - ADAM_B1 ** ADAM_STEP)
    v_hat = v / (1.0 - ADAM_B2 ** ADAM_STEP)
    delta = -ADAM_LR * (m_hat / (_jnp.sqrt(v_hat) + ADAM_EPS) + ADAM_WD * w)
    return delta, m, v


def reference(x, ffn1_norm_pre, ffn1_norm_post, ffn1_w_gate, ffn1_w_up, ffn1_w_down, mix_norm_pre, mix_norm_post, w_in, b_gate, conv_w, attn_sink, mlstm_norm, w_out, ffn2_norm_pre, ffn2_norm_post, ffn2_w_gate, ffn2_w_up, ffn2_w_down, loss_target, m_ffn1_norm_pre, m_ffn1_norm_post, m_ffn1_w_gate, m_ffn1_w_up, m_ffn1_w_down, m_mix_norm_pre, m_mix_norm_post, m_w_in, m_b_gate, m_conv_w, m_attn_sink, m_mlstm_norm, m_w_out, m_ffn2_norm_pre, m_ffn2_norm_post, m_ffn2_w_gate, m_ffn2_w_up, m_ffn2_w_down, v_ffn1_norm_pre, v_ffn1_norm_post, v_ffn1_w_gate, v_ffn1_w_up, v_ffn1_w_down, v_mix_norm_pre, v_mix_norm_post, v_w_in, v_b_gate, v_conv_w, v_attn_sink, v_mlstm_norm, v_w_out, v_ffn2_norm_pre, v_ffn2_norm_post, v_ffn2_w_gate, v_ffn2_w_up, v_ffn2_w_down):
    given = dict(x=x, ffn1_norm_pre=ffn1_norm_pre, ffn1_norm_post=ffn1_norm_post, ffn1_w_gate=ffn1_w_gate, ffn1_w_up=ffn1_w_up, ffn1_w_down=ffn1_w_down, mix_norm_pre=mix_norm_pre, mix_norm_post=mix_norm_post, w_in=w_in, b_gate=b_gate, conv_w=conv_w, attn_sink=attn_sink, mlstm_norm=mlstm_norm, w_out=w_out, ffn2_norm_pre=ffn2_norm_pre, ffn2_norm_post=ffn2_norm_post, ffn2_w_gate=ffn2_w_gate, ffn2_w_up=ffn2_w_up, ffn2_w_down=ffn2_w_down, loss_target=loss_target, m_ffn1_norm_pre=m_ffn1_norm_pre, m_ffn1_norm_post=m_ffn1_norm_post, m_ffn1_w_gate=m_ffn1_w_gate, m_ffn1_w_up=m_ffn1_w_up, m_ffn1_w_down=m_ffn1_w_down, m_mix_norm_pre=m_mix_norm_pre, m_mix_norm_post=m_mix_norm_post, m_w_in=m_w_in, m_b_gate=m_b_gate, m_conv_w=m_conv_w, m_attn_sink=m_attn_sink, m_mlstm_norm=m_mlstm_norm, m_w_out=m_w_out, m_ffn2_norm_pre=m_ffn2_norm_pre, m_ffn2_norm_post=m_ffn2_norm_post, m_ffn2_w_gate=m_ffn2_w_gate, m_ffn2_w_up=m_ffn2_w_up, m_ffn2_w_down=m_ffn2_w_down, v_ffn1_norm_pre=v_ffn1_norm_pre, v_ffn1_norm_post=v_ffn1_norm_post, v_ffn1_w_gate=v_ffn1_w_gate, v_ffn1_w_up=v_ffn1_w_up, v_ffn1_w_down=v_ffn1_w_down, v_mix_norm_pre=v_mix_norm_pre, v_mix_norm_post=v_mix_norm_post, v_w_in=v_w_in, v_b_gate=v_b_gate, v_conv_w=v_conv_w, v_attn_sink=v_attn_sink, v_mlstm_norm=v_mlstm_norm, v_w_out=v_w_out, v_ffn2_norm_pre=v_ffn2_norm_pre, v_ffn2_norm_post=v_ffn2_norm_post, v_ffn2_w_gate=v_ffn2_w_gate, v_ffn2_w_up=v_ffn2_w_up, v_ffn2_w_down=v_ffn2_w_down)
    weights = {n: given[n] for n in TWIN_WEIGHTS}
    shared = {n: given[n] for n in SHARED_INPUTS}
    per_example = {n: given[n] for n in ['x']}
    grad_fn = _jax.value_and_grad(_loss, argnums=(0, 1))

    def one_microbatch(ex, loss_target):
        ex = dict(ex)
        diff = ex.pop(TWIN_DIFF_INPUT)
        return grad_fn(weights, diff, {**shared, **ex}, loss_target)

    if N_MICROBATCH == 1:
        loss, (grad_w, grad_x) = one_microbatch(per_example, given["loss_target"])
    else:
        def body(carry, xs):
            loss_sum, grad_sum = carry
            l_k, (gw_k, gx_k) = one_microbatch(xs[0], xs[1])
            with _jax.named_scope("update"):
                return (loss_sum + l_k, _jax.tree.map(_jnp.add, grad_sum, gw_k)), gx_k

        init = (_jnp.zeros((), _jnp.float32), _jax.tree.map(_jnp.zeros_like, weights))
        (loss, grad_w), grad_x = _jax.lax.scan(body, init, (per_example, given["loss_target"]))
    with _jax.named_scope("update"):
        delta_w, new_m, new_v = {}, {}, {}
        for n in TWIN_WEIGHTS:
            delta_w[n], new_m[n], new_v[n] = _adamw(weights[n], grad_w[n], given["m_" + n], given["v_" + n])
    return (loss, grad_x, *[grad_w[n] for n in TWIN_WEIGHTS], *[delta_w[n] for n in TWIN_WEIGHTS],
            *[new_m[n] for n in TWIN_WEIGHTS], *[new_v[n] for n in TWIN_WEIGHTS])
```

```python
import functools

import jax
import jax.numpy as jnp
from jax import lax
from jax.experimental import pallas as pl
from jax.experimental.pallas import tpu as pltpu

F32 = jnp.float32
BF16 = jnp.bfloat16

D_MODEL = 2048
D_FF = 5632
ATT_HEADS = 8
ATT_KV_HEADS = 2
ATT_GROUP = ATT_HEADS // ATT_KV_HEADS
ATT_WIDTH = 1024
HEAD_DIM = 128
KV_WIDTH = 256
WINDOW = 128
BLK = 128
M_WIDTH = 1024
M_HEADS = 4
M_HEAD_DIM = 256
CONV_WIDTH = 5
EPS = 1e-6
ROPE_THETA = 10000.0
IN_WIDTH = 5648
N_GATES = 16
N_DEV = 8

ADAM_LR = 0.001
ADAM_B1 = 0.9
ADAM_B2 = 0.999
ADAM_EPS = 1e-08
ADAM_WD = 0.01
ADAM_STEP = 10

P_QK = 0
P_QA = 2048
P_VM = 3072
P_OM = 4096
P_KA = 5120
P_VA = 5376
P_G = 5632
P_WIDTH = 6144
GU_TILE = 512

LANES = 128
V7X_VMEM_LIMIT = 48 * 1024 * 1024
NEG = -1e30
MESH = pl.DeviceIdType.MESH
ANY = pl.BlockSpec(memory_space=pl.ANY)


def _tile(n, cands=(1024, 512, 256, 128)):
    for c in cands:
        if n % c == 0:
            return c
    return n


def _pcall(body, *, name, out_shape, in_specs, out_specs, grid=(), scratch=(), sem=None):
    return pl.pallas_call(
        body, name=name, out_shape=out_shape, in_specs=in_specs, out_specs=out_specs, grid=grid,
        scratch_shapes=list(scratch),
        compiler_params=pltpu.CompilerParams(dimension_semantics=sem, vmem_limit_bytes=V7X_VMEM_LIMIT))


def _ccall(body, *, name, out_shape, n_in, scratch):
    multi = isinstance(out_shape, (list, tuple))
    return pl.pallas_call(
        body, name=name, out_shape=out_shape, in_specs=[ANY] * n_in,
        out_specs=[ANY] * len(out_shape) if multi else ANY, scratch_shapes=list(scratch),
        compiler_params=pltpu.CompilerParams(has_side_effects=True))


def _dot(a, b):
    return jnp.dot(a, b, preferred_element_type=F32)


def _dot_nt(a, b):
    return lax.dot_general(a, b, (((1,), (1,)), ((), ())), preferred_element_type=F32)


def _dot_tn(a, b):
    return lax.dot_general(a, b, (((0,), (0,)), ((), ())), preferred_element_type=F32)


def _sigmoid(x):
    return 1.0 / (1.0 + jnp.exp(-x))


def mm_nn(a, b, *, name, out_dtype=F32, mode=None, extra=None):
    M, K = a.shape
    N = b.shape[1]
    tm, tk = _tile(M), _tile(K)
    if mode == "swiglu":
        tn = 2 * GU_TILE
    elif mode == "swiglu_bwd":
        tn = GU_TILE
    else:
        tn = _tile(N)
    nk = K // tk
    grid = (M // tm, N // tn, nk)

    def body(*refs):
        if mode == "swiglu":
            a_ref, b_ref, o_ref, act_ref, acc = refs
        elif mode == "swiglu_bwd":
            a_ref, b_ref, h_ref, o_ref, acc = refs
        else:
            a_ref, b_ref, o_ref, acc = refs
        k = pl.program_id(2)

        @pl.when(k == 0)
        def _():
            acc[...] = jnp.zeros_like(acc)

        acc[...] += _dot(a_ref[...], b_ref[...])

        @pl.when(k == nk - 1)
        def _():
            r = acc[...]
            if mode == "swiglu":
                o_ref[...] = r.astype(o_ref.dtype)
                hg, hu = r[:, :GU_TILE], r[:, GU_TILE:]
                act_ref[...] = (hg * _sigmoid(hg) * hu).astype(act_ref.dtype)
            elif mode == "swiglu_bwd":
                h = h_ref[...].astype(F32)
                hg, hu = h[:, :GU_TILE], h[:, GU_TILE:]
                sg = _sigmoid(hg)
                o_ref[:, :GU_TILE] = (r * hu * (sg * (1.0 + hg * (1.0 - sg)))).astype(o_ref.dtype)
                o_ref[:, GU_TILE:] = (r * hg * sg).astype(o_ref.dtype)
            else:
                o_ref[...] = r.astype(o_ref.dtype)

    in_specs = [pl.BlockSpec((tm, tk), lambda i, j, k: (i, k)),
                pl.BlockSpec((tk, tn), lambda i, j, k: (k, j))]
    args = [a, b]
    if mode == "swiglu":
        out_shape = (jax.ShapeDtypeStruct((M, N), BF16), jax.ShapeDtypeStruct((M, N // 2), BF16))
        out_specs = (pl.BlockSpec((tm, tn), lambda i, j, k: (i, j)),
                     pl.BlockSpec((tm, GU_TILE), lambda i, j, k: (i, j)))
    elif mode == "swiglu_bwd":
        in_specs.append(pl.BlockSpec((tm, 2 * GU_TILE), lambda i, j, k: (i, j)))
        args.append(extra)
        out_shape = jax.ShapeDtypeStruct((M, 2 * N), BF16)
        out_specs = pl.BlockSpec((tm, 2 * GU_TILE), lambda i, j, k: (i, j))
    else:
        out_shape = jax.ShapeDtypeStruct((M, N), out_dtype)
        out_specs = pl.BlockSpec((tm, tn), lambda i, j, k: (i, j))
    return _pcall(body, name=name, out_shape=out_shape, in_specs=in_specs, out_specs=out_specs, grid=grid,
                  scratch=[pltpu.VMEM((tm, tn), F32)], sem=("parallel", "parallel", "arbitrary"))(*args)


def mm_tn(a, g, *, name):
    M, K = a.shape
    N = g.shape[1]
    tm, tk, tn = _tile(M), _tile(K), _tile(N)
    nm = M // tm

    def body(a_ref, g_ref, o_ref):
        @pl.when(pl.program_id(2) == 0)
        def _():
            o_ref[...] = jnp.zeros_like(o_ref)

        o_ref[...] += _dot_tn(a_ref[...], g_ref[...])

    return _pcall(body, name=name, out_shape=jax.ShapeDtypeStruct((K, N), F32),
                  in_specs=[pl.BlockSpec((tm, tk), lambda i, j, m: (m, i)),
                            pl.BlockSpec((tm, tn), lambda i, j, m: (m, j))],
                  out_specs=pl.BlockSpec((tk, tn), lambda i, j, m: (i, j)),
                  grid=(K // tk, N // tn, nm), sem=("parallel", "parallel", "arbitrary"))(a, g)


def norm_fwd(x, g, *, name, scale=1.0, resid=None, out_dtype=F32):
    S, D = x.shape
    tm = _tile(S, (512, 256, 128))

    def body(*refs):
        if resid is None:
            x_ref, g_ref, o_ref = refs
        else:
            x_ref, g_ref, r_ref, o_ref = refs
        xv = x_ref[...].astype(F32)
        r = lax.rsqrt(jnp.mean(xv * xv, axis=-1, keepdims=True) + EPS)
        y = (xv * r) * g_ref[...]
        if scale != 1.0:
            y = y * scale
        if resid is not None:
            y = y + r_ref[...]
        o_ref[...] = y.astype(o_ref.dtype)

    row = pl.BlockSpec((tm, D), lambda i: (i, 0))
    in_specs = [row, pl.BlockSpec((1, D), lambda i: (0, 0))]
    args = [x, g.reshape(1, D)]
    if resid is not None:
        in_specs.append(row)
        args.append(resid)
    return _pcall(body, name=name, out_shape=jax.ShapeDtypeStruct((S, D), out_dtype), in_specs=in_specs,
                  out_specs=row, grid=(S // tm,), sem=("parallel",))(*args)


def norm_bwd(dy, x, g, *, name, scale=1.0, resid=None, out_dtype=F32):
    S, D = x.shape
    tm = _tile(S, (512, 256, 128))

    def body(*refs):
        if resid is None:
            dy_ref, x_ref, g_ref, dx_ref, dg_ref = refs
        else:
            dy_ref, x_ref, g_ref, r_ref, dx_ref, dg_ref = refs

        @pl.when(pl.program_id(0) == 0)
        def _():
            dg_ref[...] = jnp.zeros_like(dg_ref)

        xv = x_ref[...].astype(F32)
        d = dy_ref[...].astype(F32)
        if scale != 1.0:
            d = d * scale
        r = lax.rsqrt(jnp.mean(xv * xv, axis=-1, keepdims=True) + EPS)
        xh = xv * r
        dg_ref[...] += jnp.sum(d * xh, axis=0, keepdims=True)
        dxh = d * g_ref[...]
        dx = r * (dxh - xh * jnp.mean(dxh * xh, axis=-1, keepdims=True))
        if resid is not None:
            dx = dx + r_ref[...]
        dx_ref[...] = dx.astype(dx_ref.dtype)

    row = pl.BlockSpec((tm, D), lambda i: (i, 0))
    vec = pl.BlockSpec((1, D), lambda i: (0, 0))
    in_specs = [row, row, vec]
    args = [dy, x, g.reshape(1, D)]
    if resid is not None:
        in_specs.append(row)
        args.append(resid)
    return _pcall(body, name=name,
                  out_shape=(jax.ShapeDtypeStruct((S, D), out_dtype), jax.ShapeDtypeStruct((1, D), F32)),
                  in_specs=in_specs, out_specs=(row, vec), grid=(S // tm,), sem=("arbitrary",))(*args)


def loss_fwd_bwd(y, target, *, name):
    S, D = y.shape
    tm = _tile(S, (512, 256, 128))

    def body(y_ref, t_ref, dy_ref, l_ref):
        @pl.when(pl.program_id(0) == 0)
        def _():
            l_ref[...] = jnp.zeros_like(l_ref)

        e = y_ref[...] - t_ref[...]
        dy_ref[...] = e * (1.0 / D)
        l_ref[...] += jnp.sum(jnp.sum(e * e, axis=1, keepdims=True), axis=0, keepdims=True) * (0.5 / D)

    row = pl.BlockSpec((tm, D), lambda i: (i, 0))
    one = pl.BlockSpec((1, 1), lambda i: (0, 0))
    return _pcall(body, name=name,
                  out_shape=(jax.ShapeDtypeStruct((S, D), F32), jax.ShapeDtypeStruct((1, 1), F32)),
                  in_specs=[row, row], out_specs=(row, one), grid=(S // tm,), sem=("arbitrary",))(y, target)


def _rope_tables(S):
    half = HEAD_DIM // 2
    inv_freq = ROPE_THETA ** (-jnp.arange(half, dtype=F32) / half)
    ang = jnp.arange(S, dtype=F32)[:, None] * inv_freq[None, :]
    cos, sin = jnp.cos(ang), jnp.sin(ang)
    return jnp.concatenate([cos, cos], axis=1), jnp.concatenate([-sin, sin], axis=1)


def _rope(x, cos2, sin2):
    return x * cos2 + pltpu.roll(x, HEAD_DIM // 2, 1) * sin2


def _unrope(d, cos2, sin2):
    return d * cos2 + pltpu.roll(d * sin2, HEAD_DIM // 2, 1)


def _nbr_specs(width, col, nb):
    return [pl.BlockSpec((BLK, width), lambda n, c=col: (jnp.maximum(n - 1, 0), c)),
            pl.BlockSpec((BLK, width), lambda n, c=col: (n, c)),
            pl.BlockSpec((BLK, width), lambda n, c=col: (jnp.minimum(n + 1, nb - 1), c))]


def attn_fwd(proj, cos2, sin2, sink, *, name):
    S = proj.shape[0]
    nb = S // BLK
    scale = HEAD_DIM ** -0.5

    def body(sink_ref, q_ref, k0, k1, k2, v0, v1, v2, c0, c1, c2, s0, s1, s2, o_ref, lse_ref):
        n = pl.program_id(0)
        cosk = jnp.concatenate([c0[...], c1[...], c2[...]], axis=0)
        sink_ = jnp.concatenate([s0[...], s1[...], s2[...]], axis=0)
        kall = jnp.concatenate([k0[...], k1[...], k2[...]], axis=0)
        vall = jnp.concatenate([v0[...], v1[...], v2[...]], axis=0)
        rows = lax.broadcasted_iota(jnp.int32, (BLK, 3 * BLK), 0)
        cols = lax.broadcasted_iota(jnp.int32, (BLK, 3 * BLK), 1)
        kpos = (n - 1) * BLK + cols
        valid = (jnp.abs(cols - BLK - rows) <= WINDOW) & (kpos >= 0) & (kpos < S)
        valid = jnp.concatenate([valid] * ATT_GROUP, axis=0)
        lane = lax.broadcasted_iota(jnp.int32, (BLK, LANES), 1)
        lse_tile = jnp.zeros((BLK, LANES), F32)
        for hk in range(ATT_KV_HEADS):
            ks = slice(hk * HEAD_DIM, (hk + 1) * HEAD_DIM)
            kh = _rope(kall[:, ks], cosk, sink_).astype(BF16)
            vh = vall[:, ks].astype(BF16)
            qs = []
            for g in range(ATT_GROUP):
                hq = hk * ATT_GROUP + g
                qs.append(_rope(q_ref[:, hq * HEAD_DIM:(hq + 1) * HEAD_DIM], c1[...], s1[...]))
            qh = jnp.concatenate(qs, axis=0).astype(BF16)
            s = _dot_nt(qh, kh) * scale
            s = jnp.where(valid, s, NEG)
            snk = jnp.concatenate(
                [jnp.full((BLK, 1), sink_ref[hk * ATT_GROUP + g], F32) for g in range(ATT_GROUP)], axis=0)
            m = jnp.maximum(jnp.max(s, axis=1, keepdims=True), snk)
            p = jnp.exp(s - m)
            l = jnp.sum(p, axis=1, keepdims=True) + jnp.exp(snk - m)
            o = _dot(p.astype(BF16), vh) * (1.0 / l)
            lse = m + jnp.log(l)
            for g in range(ATT_GROUP):
                hq = hk * ATT_GROUP + g
                o_ref[:, hq * HEAD_DIM:(hq + 1) * HEAD_DIM] = o[g * BLK:(g + 1) * BLK].astype(o_ref.dtype)
                lse_tile = lse_tile + jnp.where(lane == hq, lse[g * BLK:(g + 1) * BLK], 0.0)
        lse_ref[...] = lse_tile

    in_specs = ([pl.BlockSpec(memory_space=pltpu.SMEM),
                 pl.BlockSpec((BLK, ATT_WIDTH), lambda n: (n, P_QA // ATT_WIDTH))]
                + _nbr_specs(KV_WIDTH, P_KA // KV_WIDTH, nb) + _nbr_specs(KV_WIDTH, P_VA // KV_WIDTH, nb)
                + _nbr_specs(HEAD_DIM, 0, nb) + _nbr_specs(HEAD_DIM, 0, nb))
    return _pcall(body, name=name,
                  out_shape=(jax.ShapeDtypeStruct((S, ATT_WIDTH), BF16), jax.ShapeDtypeStruct((S, LANES), F32)),
                  in_specs=in_specs,
                  out_specs=(pl.BlockSpec((BLK, ATT_WIDTH), lambda n: (n, 0)),
                             pl.BlockSpec((BLK, LANES), lambda n: (n, 0))),
                  grid=(nb,), sem=("parallel",))(sink, proj, proj, proj, proj, proj, proj, proj,
                                                 cos2, cos2, cos2, sin2, sin2, sin2)


def attn_bwd(proj, y, dy, lse, cos2, sin2, sink, *, name):
    S = proj.shape[0]
    nb = S // BLK
    scale = HEAD_DIM ** -0.5

    def body(sink_ref, q0, q1, q2, k0, k1, k2, v0, v1, v2, o0, o1, o2, d0, d1, d2, l0, l1, l2,
             c0, c1, c2, s0, s1, s2, dq_ref, dk_ref, dv_ref, dsink_ref):
        n = pl.program_id(0)

        @pl.when(n == 0)
        def _():
            dsink_ref[...] = jnp.zeros_like(dsink_ref)

        q_nb, o_nb, d_nb, l_nb = (q0, q1, q2), (o0, o1, o2), (d0, d1, d2), (l0, l1, l2)
        c_nb, s_nb = (c0, c1, c2), (s0, s1, s2)
        cosk = jnp.concatenate([c0[...], c1[...], c2[...]], axis=0)
        sink_ = jnp.concatenate([s0[...], s1[...], s2[...]], axis=0)
        kall = jnp.concatenate([k0[...], k1[...], k2[...]], axis=0)
        vall = jnp.concatenate([v0[...], v1[...], v2[...]], axis=0)
        lane = lax.broadcasted_iota(jnp.int32, (1, LANES), 1)
        rows = lax.broadcasted_iota(jnp.int32, (BLK, 3 * BLK), 0)
        cols = lax.broadcasted_iota(jnp.int32, (BLK, 3 * BLK), 1)
        kpos = (n - 1) * BLK + cols
        valid_q = (jnp.abs(cols - BLK - rows) <= WINDOW) & (kpos >= 0) & (kpos < S)
        valid_q = jnp.concatenate([valid_q] * ATT_GROUP, axis=0)
        qr = lax.broadcasted_iota(jnp.int32, (3 * BLK, BLK), 0)
        kc = lax.broadcasted_iota(jnp.int32, (3 * BLK, BLK), 1)
        qpos = (n - 1) * BLK + qr
        valid_k = (jnp.abs(qr - BLK - kc) <= WINDOW) & (qpos >= 0) & (qpos < S)
        valid_k = jnp.concatenate([valid_k] * ATT_GROUP, axis=0)
        dsink_acc = jnp.zeros((1, LANES), F32)

        def head_cols(ref, hq):
            return ref[:, hq * HEAD_DIM:(hq + 1) * HEAD_DIM]

        for hk in range(ATT_KV_HEADS):
            ks = slice(hk * HEAD_DIM, (hk + 1) * HEAD_DIM)
            kh = _rope(kall[:, ks], cosk, sink_).astype(BF16)
            vh = vall[:, ks].astype(BF16)
            qs, dos, lses, deltas = [], [], [], []
            for g in range(ATT_GROUP):
                hq = hk * ATT_GROUP + g
                qs.append(_rope(head_cols(q1, hq), c1[...], s1[...]))
                do = head_cols(d1, hq)
                dos.append(do)
                lses.append(l1[:, hq:hq + 1])
                deltas.append(jnp.sum(do * head_cols(o1, hq).astype(F32), axis=1, keepdims=True))
            qh = jnp.concatenate(qs, axis=0).astype(BF16)
            doh = jnp.concatenate(dos, axis=0).astype(BF16)
            lseh = jnp.concatenate(lses, axis=0)
            delh = jnp.concatenate(deltas, axis=0)
            s = jnp.where(valid_q, _dot_nt(qh, kh) * scale, NEG)
            p = jnp.exp(s - lseh)
            dp = _dot_nt(doh, vh)
            ds = (p * (dp - delh)).astype(BF16)
            dq = _dot(ds, kh) * scale
            for g in range(ATT_GROUP):
                hq = hk * ATT_GROUP + g
                dq_ref[:, hq * HEAD_DIM:(hq + 1) * HEAD_DIM] = _unrope(dq[g * BLK:(g + 1) * BLK], c1[...], s1[...])
                psink = jnp.exp(sink_ref[hq] - lses[g])
                dsink_acc = dsink_acc + jnp.where(lane == hq, -jnp.sum(psink * deltas[g]), 0.0)
            kown = _rope(k1[:, ks], c1[...], s1[...]).astype(BF16)
            vown = v1[:, ks].astype(BF16)
            qs, dos, lses, deltas = [], [], [], []
            for g in range(ATT_GROUP):
                hq = hk * ATT_GROUP + g
                for j in range(3):
                    qs.append(_rope(head_cols(q_nb[j], hq), c_nb[j][...], s_nb[j][...]))
                    do = head_cols(d_nb[j], hq)
                    dos.append(do)
                    lses.append(l_nb[j][:, hq:hq + 1])
                    deltas.append(jnp.sum(do * head_cols(o_nb[j], hq).astype(F32), axis=1, keepdims=True))
            qh = jnp.concatenate(qs, axis=0).astype(BF16)
            doh = jnp.concatenate(dos, axis=0).astype(BF16)
            lseh = jnp.concatenate(lses, axis=0)
            delh = jnp.concatenate(deltas, axis=0)
            s = jnp.where(valid_k, _dot_nt(qh, kown) * scale, NEG)
            p = jnp.where(valid_k, jnp.exp(s - lseh), 0.0)
            dv_ref[:, ks] = _dot_tn(p.astype(BF16), doh)
            dp = _dot_nt(doh, vown)
            ds = (p * (dp - delh)).astype(BF16)
            dk_ref[:, ks] = _unrope(_dot_tn(ds, qh) * scale, c1[...], s1[...])
        dsink_ref[...] += dsink_acc

    in_specs = ([pl.BlockSpec(memory_space=pltpu.SMEM)]
                + _nbr_specs(ATT_WIDTH, P_QA // ATT_WIDTH, nb)
                + _nbr_specs(KV_WIDTH, P_KA // KV_WIDTH, nb) + _nbr_specs(KV_WIDTH, P_VA // KV_WIDTH, nb)
                + _nbr_specs(ATT_WIDTH, 0, nb) + _nbr_specs(ATT_WIDTH, 0, nb) + _nbr_specs(LANES, 0, nb)
                + _nbr_specs(HEAD_DIM, 0, nb) + _nbr_specs(HEAD_DIM, 0, nb))
    args = [sink] + [proj] * 9 + [y] * 3 + [dy] * 3 + [lse] * 3 + [cos2] * 3 + [sin2] * 3
    return _pcall(body, name=name,
                  out_shape=(jax.ShapeDtypeStruct((S, ATT_WIDTH), F32), jax.ShapeDtypeStruct((S, KV_WIDTH), F32),
                             jax.ShapeDtypeStruct((S, KV_WIDTH), F32), jax.ShapeDtypeStruct((1, LANES), F32)),
                  in_specs=in_specs,
                  out_specs=(pl.BlockSpec((BLK, ATT_WIDTH), lambda n: (n, 0)),
                             pl.BlockSpec((BLK, KV_WIDTH), lambda n: (n, 0)),
                             pl.BlockSpec((BLK, KV_WIDTH), lambda n: (n, 0)),
                             pl.BlockSpec((1, LANES), lambda n: (0, 0))),
                  grid=(nb,), sem=("arbitrary",))(*args)


CONV_HALO = 8
CONV_COLS = 512


def _halo_specs(tm, nrow, col_of):
    hb = tm // CONV_HALO
    return [pl.BlockSpec((CONV_HALO, CONV_COLS), lambda i, j: (jnp.maximum(i * hb - 1, 0), col_of(j))),
            pl.BlockSpec((tm, CONV_COLS), lambda i, j: (i, col_of(j))),
            pl.BlockSpec((CONV_HALO, CONV_COLS),
                         lambda i, j: (jnp.minimum((i + 1) * hb, nrow * hb - 1), col_of(j)))]


def _with_halo(prev, cur, nxt, i, nrow):
    p = jnp.where(i > 0, prev[...], 0.0)
    q = jnp.where(i < nrow - 1, nxt[...], 0.0)
    return jnp.concatenate([p, cur[...], q], axis=0)


def _conv_taps(xt, w_ref, tm):
    n = xt.shape[0]
    acc = jnp.zeros_like(xt)
    for j in range(CONV_WIDTH):
        sh = (CONV_WIDTH // 2 - j) % n
        xs = xt if sh == 0 else pltpu.roll(xt, sh, 0)
        acc = acc + xs * w_ref[j:j + 1, :]
    return acc


def conv_fwd(proj, conv_w, *, name):
    S = proj.shape[0]
    tm = _tile(S, (512, 256, 128))
    nrow = S // tm

    def body(xp, xc, xn, w_ref, o_ref):
        i = pl.program_id(0)
        xt = _with_halo(xp, xc, xn, i, nrow)
        pre = _conv_taps(xt, w_ref, tm)[CONV_HALO:CONV_HALO + tm]
        o_ref[...] = pre * _sigmoid(pre)

    return _pcall(body, name=name, out_shape=jax.ShapeDtypeStruct((S, 2 * M_WIDTH), F32),
                  in_specs=_halo_specs(tm, nrow, lambda j: P_QK // CONV_COLS + j)
                  + [pl.BlockSpec((CONV_HALO, CONV_COLS), lambda i, j: (0, j))],
                  out_specs=pl.BlockSpec((tm, CONV_COLS), lambda i, j: (i, j)),
                  grid=(nrow, 2 * M_WIDTH // CONV_COLS), sem=("parallel", "parallel"))(proj, proj, proj, conv_w)


def conv_bwd(proj, conv_w, da, db, *, name):
    S = proj.shape[0]
    tm = _tile(S, (512, 256, 128))
    nrow = S // tm

    def body(xp, xc, xn, ap, ac, an, bp, bc, bn, w_ref, dx_ref, dw_ref):
        i = pl.program_id(1)

        @pl.when(i == 0)
        def _():
            dw_ref[...] = jnp.zeros_like(dw_ref)

        xt = _with_halo(xp, xc, xn, i, nrow)
        dt = _with_halo(ap, ac, an, i, nrow) + _with_halo(bp, bc, bn, i, nrow)
        pre = _conv_taps(xt, w_ref, tm)
        sg = _sigmoid(pre)
        dpre = dt * (sg * (1.0 + pre * (1.0 - sg)))
        n = xt.shape[0]
        ridx = lax.broadcasted_iota(jnp.int32, (n, 1), 0)
        dpre = jnp.where((ridx >= 2) & (ridx < n - 2), dpre, 0.0)
        dx = jnp.zeros_like(xt)
        own = (ridx >= CONV_HALO) & (ridx < CONV_HALO + tm)
        dpre_own = jnp.where(own, dpre, 0.0)
        dw_rows = []
        for j in range(CONV_WIDTH):
            sh = (j - CONV_WIDTH // 2) % n
            ds_ = dpre if sh == 0 else pltpu.roll(dpre, sh, 0)
            dx = dx + ds_ * w_ref[j:j + 1, :]
            shx = (CONV_WIDTH // 2 - j) % n
            xs = xt if shx == 0 else pltpu.roll(xt, shx, 0)
            dw_rows.append(jnp.sum(dpre_own * xs, axis=0, keepdims=True))
        dx_ref[...] = dx[CONV_HALO:CONV_HALO + tm]
        dw_rows.append(jnp.zeros((CONV_HALO - CONV_WIDTH, CONV_COLS), F32))
        dw_ref[...] += jnp.concatenate(dw_rows, axis=0)

    colq = lambda j: P_QK // CONV_COLS + j
    same = lambda j: j

    def swap(specs):
        return [pl.BlockSpec(s.block_shape, (lambda f: (lambda j, i: f(i, j)))(s.index_map)) for s in specs]

    in_specs = swap(_halo_specs(tm, nrow, colq) + _halo_specs(tm, nrow, same) + _halo_specs(tm, nrow, same)
                    + [pl.BlockSpec((CONV_HALO, CONV_COLS), lambda i, j: (0, j))])
    return _pcall(body, name=name,
                  out_shape=(jax.ShapeDtypeStruct((S, 2 * M_WIDTH), F32),
                             jax.ShapeDtypeStruct((CONV_HALO, 2 * M_WIDTH), F32)),
                  in_specs=in_specs,
                  out_specs=(pl.BlockSpec((tm, CONV_COLS), lambda j, i: (i, j)),
                             pl.BlockSpec((CONV_HALO, CONV_COLS), lambda j, i: (0, j))),
                  grid=(2 * M_WIDTH // CONV_COLS, nrow), sem=("parallel", "arbitrary"))(
                      proj, proj, proj, da, da, da, db, db, db, conv_w)


def _log_sigmoid(x):
    return jnp.minimum(x, 0.0) - jnp.log(1.0 + jnp.exp(-jnp.abs(x)))


def _scan_sum(x, axis, from_end):
    idx = lax.broadcasted_iota(jnp.int32, x.shape, axis)
    n = x.shape[axis]
    sh = 1
    while sh < n:
        if from_end:
            x = x + jnp.where(idx < n - sh, pltpu.roll(x, n - sh, axis), 0.0)
        else:
            x = x + jnp.where(idx >= sh, pltpu.roll(x, sh, axis), 0.0)
        sh *= 2
    return x


def _gate_setup(gc_ref, gr_ref, bgc_ref, bgr_ref, reverse):
    gc = gc_ref[...] + bgc_ref[...]
    gr = gr_ref[...] + bgr_ref[...]
    bc = _scan_sum(_log_sigmoid(gc), 0, reverse)
    br = _scan_sum(_log_sigmoid(gr), 1, reverse)
    return gc, gr, bc, br


def _head_gates(gc, gr, bc, br, h, m_in, reverse, tri):
    io = (M_HEADS if reverse else 0) + h
    fo = (3 * M_HEADS if reverse else 2 * M_HEADS) + h
    last = 0 if reverse else BLK - 1
    b_col, b_row = bc[:, fo:fo + 1], br[fo:fo + 1, :]
    ig_col, ig_row = gc[:, io:io + 1], gr[io:io + 1, :]
    logd = jnp.where(tri, b_col - b_row + ig_row, NEG)
    m_t = jnp.maximum(b_col + m_in, jnp.max(logd, axis=1, keepdims=True))
    dm = jnp.exp(logd - m_t)
    gi = jnp.exp(b_col + m_in - m_t)
    b_last = b_row[:, last:last + 1]
    logw = b_last - b_row + ig_row
    m_new = jnp.maximum(b_last + m_in, jnp.max(logw, axis=1, keepdims=True))
    w_col = jnp.exp(b_last - b_col + ig_col - m_new)
    dec = jnp.exp(b_last + m_in - m_new)
    return io, fo, m_t, dm, gi, m_new, w_col, dec


def _tri_mask(reverse):
    rows = lax.broadcasted_iota(jnp.int32, (BLK, BLK), 0)
    cols = lax.broadcasted_iota(jnp.int32, (BLK, BLK), 1)
    return (cols >= rows) if reverse else (cols <= rows)


def mlstm_fwd(qk, proj, gates_r, bg_c, bg_r, *, reverse, name):
    S = qk.shape[0]
    nc = S // BLK
    kscale = M_HEAD_DIM ** -0.5
    cidx = (lambda c: nc - 1 - c) if reverse else (lambda c: c)

    def body(qk_ref, v_ref, gc_ref, gr_ref, bgc_ref, bgr_ref, h_ref, den_ref, cst_ref, nm_ref, c_sc, n_sc, m_sc):
        @pl.when(pl.program_id(0) == 0)
        def _():
            c_sc[...] = jnp.zeros_like(c_sc)
            n_sc[...] = jnp.zeros_like(n_sc)
            m_sc[...] = jnp.zeros_like(m_sc)

        gc, gr, bc, br = _gate_setup(gc_ref, gr_ref, bgc_ref, bgr_ref, reverse)
        tri = _tri_mask(reverse)
        lane = lax.broadcasted_iota(jnp.int32, (BLK, LANES), 1)
        den_tile = jnp.zeros((BLK, LANES), F32)
        for h in range(M_HEADS):
            cs = slice(h * M_HEAD_DIM, (h + 1) * M_HEAD_DIM)
            m_in = m_sc[h][:, 0:1]
            _, _, m_t, dm, gi, m_new, w_col, dec = _head_gates(gc, gr, bc, br, h, m_in, reverse, tri)
            q = qk_ref[:, cs]
            k = qk_ref[:, M_WIDTH + h * M_HEAD_DIM:M_WIDTH + (h + 1) * M_HEAD_DIM] * kscale
            v = v_ref[:, cs]
            c_in, n_in = c_sc[h], n_sc[h]
            cst_ref[h] = c_in
            nm_ref[h, 0:1, :] = n_in
            nm_ref[h, 1:2, :] = m_sc[h]
            qb, kb, vb = q.astype(BF16), k.astype(BF16), v.astype(BF16)
            s = _dot_nt(qb, kb) * dm
            num = _dot(s.astype(BF16), vb) + gi * _dot_nt(qb, c_in.astype(BF16))
            den = jnp.sum(s, axis=1, keepdims=True) + gi * jnp.sum(q * n_in, axis=1, keepdims=True)
            z = jnp.maximum(jnp.abs(den), jnp.exp(-m_t))
            h_ref[:, cs] = num * (1.0 / z)
            den_tile = den_tile + jnp.where(lane == h, den, 0.0)
            c_sc[h] = dec * c_in + _dot_tn((w_col * v).astype(BF16), kb)
            n_sc[h] = dec * n_in + jnp.sum(w_col * k, axis=0, keepdims=True)
            m_sc[h] = jnp.broadcast_to(m_new, (1, M_HEAD_DIM))
        den_ref[...] = den_tile

    return _pcall(
        body, name=name,
        out_shape=(jax.ShapeDtypeStruct((S, M_WIDTH), F32), jax.ShapeDtypeStruct((S, LANES), F32),
                   jax.ShapeDtypeStruct((nc, M_HEADS, M_HEAD_DIM, M_HEAD_DIM), F32),
                   jax.ShapeDtypeStruct((nc, M_HEADS, 2, M_HEAD_DIM), F32)),
        in_specs=[pl.BlockSpec((BLK, 2 * M_WIDTH), lambda c: (cidx(c), 0)),
                  pl.BlockSpec((BLK, M_WIDTH), lambda c: (cidx(c), P_VM // M_WIDTH)),
                  pl.BlockSpec((BLK, LANES), lambda c: (cidx(c), P_G // LANES)),
                  pl.BlockSpec((N_GATES, BLK), lambda c: (0, cidx(c))),
                  pl.BlockSpec((1, LANES), lambda c: (0, 0)),
                  pl.BlockSpec((N_GATES, 1), lambda c: (0, 0))],
        out_specs=(pl.BlockSpec((BLK, M_WIDTH), lambda c: (cidx(c), 0)),
                   pl.BlockSpec((BLK, LANES), lambda c: (cidx(c), 0)),
                   pl.BlockSpec((None, M_HEADS, M_HEAD_DIM, M_HEAD_DIM), lambda c: (cidx(c), 0, 0, 0)),
                   pl.BlockSpec((None, M_HEADS, 2, M_HEAD_DIM), lambda c: (cidx(c), 0, 0, 0))),
        grid=(nc,),
        scratch=[pltpu.VMEM((M_HEADS, M_HEAD_DIM, M_HEAD_DIM), F32), pltpu.VMEM((M_HEADS, 1, M_HEAD_DIM), F32),
                 pltpu.VMEM((M_HEADS, 1, M_HEAD_DIM), F32)],
        sem=("arbitrary",))(qk, proj, proj, gates_r, bg_c, bg_r)


def mlstm_bwd(qk, proj, gates_r, bg_c, bg_r, hdir, den, cst, nm, dh, *, reverse, name):
    S = qk.shape[0]
    nc = S // BLK
    kscale = M_HEAD_DIM ** -0.5
    cidx = (lambda c: c) if reverse else (lambda c: nc - 1 - c)
    last = 0 if reverse else BLK - 1

    def body(qk_ref, v_ref, gc_ref, gr_ref, bgc_ref, bgr_ref, h_ref, den_ref, cst_ref, nm_ref, dh_ref,
             dqk_ref, dv_ref, dgc_ref, dgr_ref, dc_sc, dn_sc):
        @pl.when(pl.program_id(0) == 0)
        def _():
            dc_sc[...] = jnp.zeros_like(dc_sc)
            dn_sc[...] = jnp.zeros_like(dn_sc)

        gc, gr, bc, br = _gate_setup(gc_ref, gr_ref, bgc_ref, bgr_ref, reverse)
        tri = _tri_mask(reverse)
        lane_c = lax.broadcasted_iota(jnp.int32, (BLK, LANES), 1)
        row_c = lax.broadcasted_iota(jnp.int32, (BLK, 1), 0)
        row_r = lax.broadcasted_iota(jnp.int32, (N_GATES, BLK), 0)
        db_c = jnp.zeros((BLK, LANES), F32)
        dig_c = jnp.zeros((BLK, LANES), F32)
        db_r = jnp.zeros((N_GATES, BLK), F32)
        dig_r = jnp.zeros((N_GATES, BLK), F32)
        for h in range(M_HEADS):
            cs = slice(h * M_HEAD_DIM, (h + 1) * M_HEAD_DIM)
            ks = slice(M_WIDTH + h * M_HEAD_DIM, M_WIDTH + (h + 1) * M_HEAD_DIM)
            m_in = nm_ref[h, 1:2, 0:1]
            io, fo, m_t, dm, gi, m_new, w_col, dec = _head_gates(gc, gr, bc, br, h, m_in, reverse, tri)
            q = qk_ref[:, cs]
            k = qk_ref[:, ks] * kscale
            v = v_ref[:, cs]
            c_in, n_in = cst_ref[h], nm_ref[h, 0:1, :]
            qb, kb, vb, cb = q.astype(BF16), k.astype(BF16), v.astype(BF16), c_in.astype(BF16)
            s = _dot_nt(qb, kb) * dm
            den_h = den_ref[:, h:h + 1]
            emt = jnp.exp(-m_t)
            rz = 1.0 / jnp.maximum(jnp.abs(den_h), emt)
            dhh = dh_ref[:, cs]
            dnum = dhh * rz
            hdh = jnp.sum(dhh * h_ref[:, cs], axis=1, keepdims=True)
            dden = jnp.where(jnp.abs(den_h) > emt, -hdh * rz * jnp.sign(den_h), 0.0)
            dnb = dnum.astype(BF16)
            ds = _dot_nt(dnb, vb) + dden
            e = ds * s
            dsd = (ds * dm).astype(BF16)
            gd = (gi * dnum).astype(BF16)
            gdd = gi * dden
            dq = _dot(dsd, kb) + _dot(gd, cb) + gdd * n_in
            dk = _dot_tn(dsd, qb)
            dv = _dot_tn(s.astype(BF16), dnb)
            dc_in = _dot_tn(gd, qb)
            dn_in = jnp.sum(gdd * q, axis=0, keepdims=True)
            cq = _dot_nt(qb, cb)
            dg = jnp.sum(dnum * cq, axis=1, keepdims=True) + dden * jnp.sum(q * n_in, axis=1, keepdims=True)
            eg = dg * gi
            dco, dno = dc_sc[h], dn_sc[h]
            dcob = dco.astype(BF16)
            dwv = _dot_nt(kb, dcob)
            dv = dv + w_col * dwv
            dw = jnp.sum(v * dwv, axis=1, keepdims=True) + jnp.sum(k * dno, axis=1, keepdims=True)
            dk = dk + _dot((w_col * v).astype(BF16), dcob) + w_col * dno
            ew = dw * w_col
            ddec = (jnp.sum(jnp.sum(dco * c_in, axis=1, keepdims=True), axis=0, keepdims=True)
                    + jnp.sum(dno * n_in, axis=1, keepdims=True))
            dc_sc[h] = dec * dco + dc_in
            dn_sc[h] = dec * dno + dn_in
            dqk_ref[:, cs] = dq
            dqk_ref[:, ks] = dk * kscale
            dv_ref[:, cs] = dv
            csum = jnp.sum(e, axis=0, keepdims=True)
            db_last = jnp.sum(ew, axis=0, keepdims=True) + ddec * dec
            db_col = jnp.sum(e, axis=1, keepdims=True) + eg - ew + jnp.where(row_c == last, db_last, 0.0)
            db_c = db_c + jnp.where(lane_c == fo, db_col, 0.0)
            dig_c = dig_c + jnp.where(lane_c == io, ew, 0.0)
            db_r = db_r + jnp.where(row_r == fo, -csum, 0.0)
            dig_r = dig_r + jnp.where(row_r == io, csum, 0.0)
        dgc_ref[...] = dig_c + _scan_sum(db_c, 0, not reverse) * _sigmoid(-gc)
        dgr_ref[...] = dig_r + _scan_sum(db_r, 1, not reverse) * _sigmoid(-gr)

    chunk = lambda w, col=0: pl.BlockSpec((BLK, w), lambda c: (cidx(c), col))
    return _pcall(
        body, name=name,
        out_shape=(jax.ShapeDtypeStruct((S, 2 * M_WIDTH), F32), jax.ShapeDtypeStruct((S, M_WIDTH), F32),
                   jax.ShapeDtypeStruct((S, LANES), F32), jax.ShapeDtypeStruct((N_GATES, S), F32)),
        in_specs=[chunk(2 * M_WIDTH), chunk(M_WIDTH, P_VM // M_WIDTH), chunk(LANES, P_G // LANES),
                  pl.BlockSpec((N_GATES, BLK), lambda c: (0, cidx(c))),
                  pl.BlockSpec((1, LANES), lambda c: (0, 0)),
                  pl.BlockSpec((N_GATES, 1), lambda c: (0, 0)),
                  chunk(M_WIDTH), chunk(LANES),
                  pl.BlockSpec((None, M_HEADS, M_HEAD_DIM, M_HEAD_DIM), lambda c: (cidx(c), 0, 0, 0)),
                  pl.BlockSpec((None, M_HEADS, 2, M_HEAD_DIM), lambda c: (cidx(c), 0, 0, 0)),
                  chunk(M_WIDTH)],
        out_specs=(chunk(2 * M_WIDTH), chunk(M_WIDTH), chunk(LANES),
                   pl.BlockSpec((N_GATES, BLK), lambda c: (0, cidx(c)))),
        grid=(nc,),
        scratch=[pltpu.VMEM((M_HEADS, M_HEAD_DIM, M_HEAD_DIM), F32), pltpu.VMEM((M_HEADS, 1, M_HEAD_DIM), F32)],
        sem=("arbitrary",))(qk, proj, proj, gates_r, bg_c, bg_r, hdir, den, cst, nm, dh)


def headnorm_fwd(hf, hb, proj, mnorm, *, name):
    S = hf.shape[0]
    tm = _tile(S, (512, 256, 128))

    def body(hf_ref, hb_ref, om_ref, mn_ref, y_ref):
        for h in range(M_HEADS):
            cs = slice(h * M_HEAD_DIM, (h + 1) * M_HEAD_DIM)
            hm = hf_ref[:, cs] + hb_ref[:, cs]
            r = lax.rsqrt(jnp.mean(hm * hm, axis=-1, keepdims=True) + EPS)
            y_ref[:, cs] = (_sigmoid(om_ref[:, cs]) * ((hm * r) * mn_ref[:, cs])).astype(y_ref.dtype)

    row = pl.BlockSpec((tm, M_WIDTH), lambda i: (i, 0))
    return _pcall(body, name=name, out_shape=jax.ShapeDtypeStruct((S, M_WIDTH), BF16),
                  in_specs=[row, row, pl.BlockSpec((tm, M_WIDTH), lambda i: (i, P_OM // M_WIDTH)),
                            pl.BlockSpec((1, M_WIDTH), lambda i: (0, 0))],
                  out_specs=row, grid=(S // tm,), sem=("parallel",))(hf, hb, proj, mnorm)


def headnorm_bwd(hf, hb, proj, mnorm, dy, *, name):
    S = hf.shape[0]
    tm = _tile(S, (512, 256, 128))

    def body(hf_ref, hb_ref, om_ref, mn_ref, dy_ref, dh_ref, dom_ref, dmn_ref):
        @pl.when(pl.program_id(0) == 0)
        def _():
            dmn_ref[...] = jnp.zeros_like(dmn_ref)

        for h in range(M_HEADS):
            cs = slice(h * M_HEAD_DIM, (h + 1) * M_HEAD_DIM)
            hm = hf_ref[:, cs] + hb_ref[:, cs]
            r = lax.rsqrt(jnp.mean(hm * hm, axis=-1, keepdims=True) + EPS)
            xh = hm * r
            so = _sigmoid(om_ref[:, cs])
            d = dy_ref[:, cs]
            mn = mn_ref[:, cs]
            dom_ref[:, cs] = d * (xh * mn) * (so * (1.0 - so))
            dxm = d * so
            dmn_ref[:, cs] += jnp.sum(dxm * xh, axis=0, keepdims=True)
            dxh = dxm * mn
            dh_ref[:, cs] = r * (dxh - xh * jnp.mean(dxh * xh, axis=-1, keepdims=True))

    row = pl.BlockSpec((tm, M_WIDTH), lambda i: (i, 0))
    vec = pl.BlockSpec((1, M_WIDTH), lambda i: (0, 0))
    return _pcall(body, name=name,
                  out_shape=(jax.ShapeDtypeStruct((S, M_WIDTH), F32), jax.ShapeDtypeStruct((S, M_WIDTH), F32),
                             jax.ShapeDtypeStruct((1, M_WIDTH), F32)),
                  in_specs=[row, row, pl.BlockSpec((tm, M_WIDTH), lambda i: (i, P_OM // M_WIDTH)), vec,
                            pl.BlockSpec((tm, M_WIDTH), lambda i: (i, 1))],
                  out_specs=(row, row, vec), grid=(S // tm,), sem=("arbitrary",))(hf, hb, proj, mnorm, dy)


def _place():
    return lax.axis_index("x"), lax.axis_index("y"), lax.axis_index("c")


def all_gather(shards, *, name):
    T = len(shards)

    def body(*refs):
        x_refs, out_refs = refs[:T], refs[T:2 * T]
        send_sems, recv_sems, local_sems = refs[2 * T:]
        x, y, c = _place()
        me, sibling = (x, y, c), (x, y, 1 - c)
        chips = [(1 - x, y), (x, 1 - y), (1 - x, 1 - y)]

        def copy(t, k, block, to, src=None):
            px, py, pc = block
            dst = out_refs[t].at[4 * px + 2 * py + pc]
            return pltpu.make_async_remote_copy(
                src_ref=dst if src is None else src, dst_ref=dst, send_sem=send_sems.at[7 * t + k],
                recv_sem=recv_sems.at[7 * t + k], device_id=to, device_id_type=MESH)

        mine = [pltpu.make_async_copy(x_refs[t], out_refs[t].at[4 * x + 2 * y + c], local_sems.at[t])
                for t in range(T)]
        for cp in mine:
            cp.start()
        first = []
        for t in range(T):
            first.append(copy(t, 0, me, sibling, src=x_refs[t]))
            first += [copy(t, 1 + j, me, (*chip, c), src=x_refs[t]) for j, chip in enumerate(chips)]
        for cp in first:
            cp.start()
        passed = []
        for j, chip in enumerate(chips):
            for t in range(T):
                copy(t, 1 + j, (*chip, c), me).wait_recv()
                cp = copy(t, 4 + j, (*chip, c), sibling)
                cp.start()
                passed.append(cp)
        for t in range(T):
            copy(t, 0, sibling, me).wait_recv()
            for j, chip in enumerate(chips):
                copy(t, 4 + j, (*chip, 1 - c), me).wait_recv()
        for cp in first + passed:
            cp.wait_send()
        for cp in mine:
            cp.wait()

    out_shape = [jax.ShapeDtypeStruct((N_DEV,) + s.shape, s.dtype) for s in shards]
    return _ccall(body, name=name, out_shape=out_shape, n_in=T,
                  scratch=[pltpu.SemaphoreType.DMA((7 * T,)), pltpu.SemaphoreType.DMA((7 * T,)),
                           pltpu.SemaphoreType.DMA((T,))])(*shards)


def pair_exchange(grads, *, name):
    T = len(grads)

    def body(*refs):
        g_refs, out_refs = refs[:T], refs[T:2 * T]
        send_sems, recv_sems = refs[2 * T:]
        x, y, c = _place()
        cps = [pltpu.make_async_remote_copy(
            src_ref=g_refs[t].at[1 - c], dst_ref=out_refs[t], send_sem=send_sems.at[t], recv_sem=recv_sems.at[t],
            device_id=(x, y, 1 - c), device_id_type=MESH) for t in range(T)]
        for cp in cps:
            cp.start()
        for cp in cps:
            cp.wait()

    out_shape = [jax.ShapeDtypeStruct(g.shape[1:], g.dtype) for g in grads]
    return _ccall(body, name=name, out_shape=out_shape, n_in=T,
                  scratch=[pltpu.SemaphoreType.DMA((T,)), pltpu.SemaphoreType.DMA((T,))])(*grads)


def chip_exchange(parts, *, name):
    T = len(parts)

    def body(*refs):
        p_refs, out_refs = refs[:T], refs[T:2 * T]
        send_sems, recv_sems, local_sems = refs[2 * T:]
        x, y, c = _place()
        mychip = 2 * x + y
        chips = [(1 - x, y), (x, 1 - y), (1 - x, 1 - y)]
        mine = [pltpu.make_async_copy(p_refs[t].at[mychip], out_refs[t].at[mychip], local_sems.at[t])
                for t in range(T)]
        for cp in mine:
            cp.start()
        cps = []
        for t in range(T):
            for j, (px, py) in enumerate(chips):
                cps.append(pltpu.make_async_remote_copy(
                    src_ref=p_refs[t].at[2 * px + py], dst_ref=out_refs[t].at[mychip],
                    send_sem=send_sems.at[3 * t + j], recv_sem=recv_sems.at[3 * t + j],
                    device_id=(px, py, c), device_id_type=MESH))
        for cp in cps:
            cp.start()
        for cp in cps:
            cp.wait()
        for cp in mine:
            cp.wait()

    out_shape = [jax.ShapeDtypeStruct(p.shape, p.dtype) for p in parts]
    return _ccall(body, name=name, out_shape=out_shape, n_in=T,
                  scratch=[pltpu.SemaphoreType.DMA((3 * T,)), pltpu.SemaphoreType.DMA((3 * T,)),
                           pltpu.SemaphoreType.DMA((T,))])(*parts)


def pair_add(g2, recv, core, *, name):
    R = recv.shape[0]
    tr = _tile(R, (2048, 1024, 512, 256, 128, 64, 32, 16, 8))

    def body(c_ref, a_ref, b_ref, o_ref):
        o_ref[...] = a_ref[...] + b_ref[...]

    grid_spec = pltpu.PrefetchScalarGridSpec(
        num_scalar_prefetch=1, grid=(R // tr,),
        in_specs=[pl.BlockSpec((None, tr, LANES), lambda i, c_ref: (c_ref[0], i, 0)),
                  pl.BlockSpec((tr, LANES), lambda i, c_ref: (i, 0))],
        out_specs=pl.BlockSpec((tr, LANES), lambda i, c_ref: (i, 0)))
    return pl.pallas_call(body, name=name, out_shape=jax.ShapeDtypeStruct((R, LANES), F32), grid_spec=grid_spec,
                          compiler_params=pltpu.CompilerParams(dimension_semantics=("parallel",)))(core, g2, recv)


def _adam_math(w, g, m, v):
    m = ADAM_B1 * m + (1.0 - ADAM_B1) * g
    v = ADAM_B2 * v + (1.0 - ADAM_B2) * (g * g)
    m_hat = m / (1.0 - ADAM_B1 ** ADAM_STEP)
    v_hat = v / (1.0 - ADAM_B2 ** ADAM_STEP)
    delta = -ADAM_LR * (m_hat / (jnp.sqrt(v_hat) + ADAM_EPS) + ADAM_WD * w)
    return delta, m, v


def adam_update(w, parts, m, v, *, name):
    P, R, _ = parts.shape
    tr = _tile(R, (1024, 512, 256, 128, 64, 32, 16, 8))

    def body(w_ref, p_ref, m_ref, v_ref, g_ref, d_ref, nm_ref, nv_ref):
        g = p_ref[0]
        for k in range(1, P):
            g = g + p_ref[k]
        d, nm, nv = _adam_math(w_ref[...], g, m_ref[...], v_ref[...])
        g_ref[...] = g
        d_ref[...] = d
        nm_ref[...] = nm
        nv_ref[...] = nv

    row = pl.BlockSpec((tr, LANES), lambda i: (i, 0))
    shp = jax.ShapeDtypeStruct((R, LANES), F32)
    return _pcall(body, name=name, out_shape=(shp, shp, shp, shp),
                  in_specs=[row, pl.BlockSpec((P, tr, LANES), lambda i: (0, i, 0)), row, row],
                  out_specs=(row, row, row, row), grid=(R // tr,), sem=("parallel",))(w, parts, m, v)


def _rows(n_elems):
    r = -(-n_elems // LANES)
    return -(-r // 1024) * 1024 if r > 1024 else -(-r // 16) * 16


def _flat(a, dtype=None):
    n = a.size
    r = _rows(n)
    f = a.reshape(-1)
    if dtype is not None:
        f = f.astype(dtype)
    if r * LANES != n:
        f = jnp.pad(f, (0, r * LANES - n))
    return f.reshape(r, LANES)


def _unflat(f, shape):
    n = 1
    for s in shape:
        n *= s
    return f.reshape(-1)[:n].reshape(shape)


def _gathered_cols(g, rows, cols):
    b = g.reshape(N_DEV, -1)[:, :rows * cols].reshape(N_DEV, rows, cols)
    return b.transpose(1, 0, 2).reshape(rows, N_DEV * cols)


def _gathered_rows(g, rows, cols):
    return g.reshape(N_DEV, -1)[:, :rows * cols].reshape(N_DEV * rows, cols)


def _by_dest(blocks):
    n = blocks.shape[1]
    r = _rows(n)
    if r * LANES != n:
        blocks = jnp.pad(blocks, ((0, 0), (0, r * LANES - n)))
    return blocks.reshape(4, 2, r, LANES).transpose(1, 0, 2, 3).reshape(2, 4 * r, LANES)


def _col_blocks(dw, cols):
    rows = dw.shape[0]
    return dw.reshape(rows, N_DEV, cols).transpose(1, 0, 2).reshape(N_DEV, rows * cols)


def _row_blocks(dw):
    return dw.reshape(N_DEV, -1)


def _interleave_gu(wg, wu):
    k = wg.shape[0]
    nt = D_FF // GU_TILE
    return jnp.stack([wg.reshape(k, nt, GU_TILE), wu.reshape(k, nt, GU_TILE)], axis=2).reshape(k, 2 * D_FF)


def _split_gu(dw):
    k = dw.shape[0]
    nt = D_FF // GU_TILE
    d = dw.reshape(k, nt, 2, GU_TILE)
    return d[:, :, 0, :].reshape(k, D_FF), d[:, :, 1, :].reshape(k, D_FF)


_IN_NAT = dict(qa=(0, 1024), ka=(1024, 1280), va=(1280, 1536), qm=(1536, 2560), km=(2560, 3584),
               vm=(3584, 4608), om=(4608, 5632), g=(5632, 5648))


def _permute_w_in(w):
    sl = lambda k: w[:, _IN_NAT[k][0]:_IN_NAT[k][1]]
    pad = jnp.zeros((w.shape[0], P_WIDTH - P_G - N_GATES), w.dtype)
    return jnp.concatenate([sl("qm"), sl("km"), sl("qa"), sl("vm"), sl("om"), sl("ka"), sl("va"), sl("g"), pad],
                           axis=1)


def _unpermute_dw_in(dw):
    qm, km = dw[:, P_QK:P_QK + 1024], dw[:, P_QK + 1024:P_QK + 2048]
    return jnp.concatenate([dw[:, P_QA:P_QA + 1024], dw[:, P_KA:P_KA + 256], dw[:, P_VA:P_VA + 256], qm, km,
                            dw[:, P_VM:P_VM + 1024], dw[:, P_OM:P_OM + 1024], dw[:, P_G:P_G + N_GATES]], axis=1)


BIG = ("ffn1_w_gate", "ffn1_w_up", "ffn1_w_down", "w_in", "w_out", "ffn2_w_gate", "ffn2_w_up", "ffn2_w_down")
SMALL = ("ffn1_norm_pre", "ffn1_norm_post", "mix_norm_pre", "mix_norm_post", "b_gate", "attn_sink", "mlstm_norm",
         "ffn2_norm_pre", "ffn2_norm_post")
WEIGHTS = ("ffn1_norm_pre", "ffn1_norm_post", "ffn1_w_gate", "ffn1_w_up", "ffn1_w_down", "mix_norm_pre",
           "mix_norm_post", "w_in", "b_gate", "conv_w", "attn_sink", "mlstm_norm", "w_out", "ffn2_norm_pre",
           "ffn2_norm_post", "ffn2_w_gate", "ffn2_w_up", "ffn2_w_down")


def _ffn_fwd(x, g_pre, g_post, wgu, wd, tag):
    xn = norm_fwd(x, g_pre, name=f"{tag}_pre", out_dtype=BF16)
    hgu, act = mm_nn(xn, wgu, name=f"{tag}_gu", mode="swiglu")
    f = mm_nn(act, wd, name=f"{tag}_down")
    x_new = norm_fwd(f, g_post, name=f"{tag}_post", scale=0.5, resid=x)
    return x_new, (x, xn, hgu, act, f)


def _ffn_bwd(dx, saved, g_pre, g_post, wgu_t, wd_t, tag):
    x, xn, hgu, act, f = saved
    df, dg_post = norm_bwd(dx, f, g_post, name=f"{tag}_post_b", scale=0.5, out_dtype=BF16)
    dwd = mm_tn(act, df, name=f"{tag}_dwd")
    dhgu = mm_nn(df, wd_t, name=f"{tag}_dact", mode="swiglu_bwd", extra=hgu)
    dwgu = mm_tn(xn, dhgu, name=f"{tag}_dwgu")
    dxn = mm_nn(dhgu, wgu_t, name=f"{tag}_dxn")
    dx_new, dg_pre = norm_bwd(dxn, x, g_pre, name=f"{tag}_pre_b", resid=dx)
    return dx_new, dg_pre, dg_post, dwgu, dwd


def _mix_fwd(x, g_pre, g_post, w_in_p, b_gate, conv_full, sink, mnorm, w_out, cos2, sin2, tag):
    S = x.shape[0]
    xn = norm_fwd(x, g_pre, name=f"{tag}_pre", out_dtype=BF16)
    proj = mm_nn(xn, w_in_p, name=f"{tag}_in")
    gates_r = proj[:, P_G:P_G + N_GATES].T
    bg_c = jnp.pad(b_gate, (0, LANES - N_GATES)).reshape(1, LANES)
    bg_r = b_gate.reshape(N_GATES, 1)
    y_att, lse = attn_fwd(proj, cos2, sin2, sink, name=f"{tag}_att")
    qk = conv_fwd(proj, conv_full, name=f"{tag}_conv")
    hf, denf, cf, nmf = mlstm_fwd(qk, proj, gates_r, bg_c, bg_r, reverse=False, name=f"{tag}_mf")
    hb, denb, cb, nmb = mlstm_fwd(qk, proj, gates_r, bg_c, bg_r, reverse=True, name=f"{tag}_mb")
    y_m = headnorm_fwd(hf, hb, proj, mnorm.reshape(1, M_WIDTH), name=f"{tag}_hn")
    y = jnp.concatenate([y_att, y_m], axis=1)
    mo = mm_nn(y, w_out, name=f"{tag}_out")
    x_new = norm_fwd(mo, g_post, name=f"{tag}_post", resid=x)
    saved = (x, xn, proj, gates_r, bg_c, bg_r, lse, qk, hf, denf, cf, nmf, hb, denb, cb, nmb, y, mo)
    return x_new, saved


def _mix_bwd(dx, saved, g_pre, g_post, w_in_t, conv_full, sink, mnorm, w_out_t, cos2, sin2, tag):
    x, xn, proj, gates_r, bg_c, bg_r, lse, qk, hf, denf, cf, nmf, hb, denb, cb, nmb, y, mo = saved
    S = x.shape[0]
    dmo, dg_post = norm_bwd(dx, mo, g_post, name=f"{tag}_post_b", out_dtype=BF16)
    dw_out = mm_tn(y, dmo, name=f"{tag}_dwo")
    dy = mm_nn(dmo, w_out_t, name=f"{tag}_dy")
    mn = mnorm.reshape(1, M_WIDTH)
    dh, dom, dmn = headnorm_bwd(hf, hb, proj, mn, dy, name=f"{tag}_hn_b")
    dqk_f, dv_f, dgc_f, dgr_f = mlstm_bwd(qk, proj, gates_r, bg_c, bg_r, hf, denf, cf, nmf, dh,
                                           reverse=False, name=f"{tag}_mf_b")
    dqk_b, dv_b, dgc_b, dgr_b = mlstm_bwd(qk, proj, gates_r, bg_c, bg_r, hb, denb, cb, nmb, dh,
                                           reverse=True, name=f"{tag}_mb_b")
    dqk_in, dconv = conv_bwd(proj, conv_full, dqk_f, dqk_b, name=f"{tag}_conv_b")
    dqa, dka, dva, dsink = attn_bwd(proj, y, dy, lse, cos2, sin2, sink, name=f"{tag}_att_b")
    dgates = dgc_f + dgc_b + jnp.pad((dgr_f + dgr_b).T, ((0, 0), (0, LANES - N_GATES)))
    dproj = jnp.concatenate([dqk_in.astype(BF16), dqa.astype(BF16), (dv_f + dv_b).astype(BF16), dom.astype(BF16),
                             dka.astype(BF16), dva.astype(BF16), dgates.astype(BF16),
                             jnp.zeros((S, P_WIDTH - P_G - LANES), BF16)], axis=1)
    db_gate = colsum(dgates, name=f"{tag}_dbg")[0, :N_GATES]
    dw_in = mm_tn(xn, dproj, name=f"{tag}_dwi")
    dxn = mm_nn(dproj, w_in_t, name=f"{tag}_dxn")
    dx_new, dg_pre = norm_bwd(dxn, x, g_pre, name=f"{tag}_pre_b", resid=dx)
    return dx_new, dg_pre, dg_post, dw_in, db_gate, dconv[:CONV_WIDTH], dsink[0, :ATT_HEADS], dmn[0], dw_out


def colsum(a, *, name):
    S, C = a.shape
    tm = _tile(S, (512, 256, 128))

    def body(a_ref, o_ref):
        @pl.when(pl.program_id(0) == 0)
        def _():
            o_ref[...] = jnp.zeros_like(o_ref)

        o_ref[...] += jnp.sum(a_ref[...], axis=0, keepdims=True)

    return _pcall(body, name=name, out_shape=jax.ShapeDtypeStruct((1, C), F32),
                  in_specs=[pl.BlockSpec((tm, C), lambda i: (i, 0))], out_specs=pl.BlockSpec((1, C), lambda i: (0, 0)),
                  grid=(S // tm,), sem=("arbitrary",))(a)


def _layer_weights(gathered):
    wg1, wu1, wd1, win, wout, wg2, wu2, wd2 = gathered
    fs = D_FF // N_DEV
    out = {}
    for tag, wg, wu, wd in (("ffn1", wg1, wu1, wd1), ("ffn2", wg2, wu2, wd2)):
        wgu = _interleave_gu(_gathered_cols(wg, D_MODEL, fs), _gathered_cols(wu, D_MODEL, fs))
        wdn = _gathered_rows(wd, fs, D_MODEL)
        out[tag] = (wgu, wdn, wgu.T, wdn.T)
    w_in_p = _permute_w_in(_gathered_cols(win, D_MODEL, IN_WIDTH // N_DEV))
    w_out = _gathered_rows(wout, D_MODEL // N_DEV, D_MODEL)
    out["mix"] = (w_in_p, w_out, w_in_p.T, w_out.T)
    return out


def kernel(x, ffn1_norm_pre, ffn1_norm_post, ffn1_w_gate, ffn1_w_up, ffn1_w_down, mix_norm_pre, mix_norm_post, w_in, b_gate, conv_w, attn_sink, mlstm_norm, w_out, ffn2_norm_pre, ffn2_norm_post, ffn2_w_gate, ffn2_w_up, ffn2_w_down, loss_target, m_ffn1_norm_pre, m_ffn1_norm_post, m_ffn1_w_gate, m_ffn1_w_up, m_ffn1_w_down, m_mix_norm_pre, m_mix_norm_post, m_w_in, m_b_gate, m_conv_w, m_attn_sink, m_mlstm_norm, m_w_out, m_ffn2_norm_pre, m_ffn2_norm_post, m_ffn2_w_gate, m_ffn2_w_up, m_ffn2_w_down, v_ffn1_norm_pre, v_ffn1_norm_post, v_ffn1_w_gate, v_ffn1_w_up, v_ffn1_w_down, v_mix_norm_pre, v_mix_norm_post, v_w_in, v_b_gate, v_conv_w, v_attn_sink, v_mlstm_norm, v_w_out, v_ffn2_norm_pre, v_ffn2_norm_post, v_ffn2_w_gate, v_ffn2_w_up, v_ffn2_w_down):
    W = dict(ffn1_norm_pre=ffn1_norm_pre, ffn1_norm_post=ffn1_norm_post, ffn1_w_gate=ffn1_w_gate,
             ffn1_w_up=ffn1_w_up, ffn1_w_down=ffn1_w_down, mix_norm_pre=mix_norm_pre, mix_norm_post=mix_norm_post,
             w_in=w_in, b_gate=b_gate, conv_w=conv_w, attn_sink=attn_sink, mlstm_norm=mlstm_norm, w_out=w_out,
             ffn2_norm_pre=ffn2_norm_pre, ffn2_norm_post=ffn2_norm_post, ffn2_w_gate=ffn2_w_gate,
             ffn2_w_up=ffn2_w_up, ffn2_w_down=ffn2_w_down)
    M1 = dict(ffn1_norm_pre=m_ffn1_norm_pre, ffn1_norm_post=m_ffn1_norm_post, ffn1_w_gate=m_ffn1_w_gate,
              ffn1_w_up=m_ffn1_w_up, ffn1_w_down=m_ffn1_w_down, mix_norm_pre=m_mix_norm_pre,
              mix_norm_post=m_mix_norm_post, w_in=m_w_in, b_gate=m_b_gate, conv_w=m_conv_w, attn_sink=m_attn_sink,
              mlstm_norm=m_mlstm_norm, w_out=m_w_out, ffn2_norm_pre=m_ffn2_norm_pre,
              ffn2_norm_post=m_ffn2_norm_post, ffn2_w_gate=m_ffn2_w_gate, ffn2_w_up=m_ffn2_w_up,
              ffn2_w_down=m_ffn2_w_down)
    V2 = dict(ffn1_norm_pre=v_ffn1_norm_pre, ffn1_norm_post=v_ffn1_norm_post, ffn1_w_gate=v_ffn1_w_gate,
              ffn1_w_up=v_ffn1_w_up, ffn1_w_down=v_ffn1_w_down, mix_norm_pre=v_mix_norm_pre,
              mix_norm_post=v_mix_norm_post, w_in=v_w_in, b_gate=v_b_gate, conv_w=v_conv_w, attn_sink=v_attn_sink,
              mlstm_norm=v_mlstm_norm, w_out=v_w_out, ffn2_norm_pre=v_ffn2_norm_pre,
              ffn2_norm_post=v_ffn2_norm_post, ffn2_w_gate=v_ffn2_w_gate, ffn2_w_up=v_ffn2_w_up,
              ffn2_w_down=v_ffn2_w_down)
    depth = w_in.shape[0]
    S = x.shape[1]
    xs = x[0]
    cos2, sin2 = _rope_tables(S)
    core = lax.axis_index("c").astype(jnp.int32).reshape(1)

    conv_g = all_gather([_flat(conv_w)], name="ag_conv")[0]
    cs = conv_w.shape[2]
    conv_all = conv_g.reshape(N_DEV, -1)[:, :depth * CONV_WIDTH * cs].reshape(N_DEV, depth, CONV_WIDTH, cs)
    conv_all = conv_all.transpose(1, 2, 0, 3).reshape(depth, CONV_WIDTH, N_DEV * cs)
    conv_all = jnp.pad(conv_all, ((0, 0), (0, CONV_HALO - CONV_WIDTH), (0, 0)))

    lw, saved = [], []
    for l in range(depth):
        gathered = all_gather([_flat(W[n][l], BF16) for n in BIG], name=f"ag_l{l}")
        wl = _layer_weights(gathered)
        lw.append(wl)
        xs, s1 = _ffn_fwd(xs, W["ffn1_norm_pre"][l], W["ffn1_norm_post"][l], wl["ffn1"][0], wl["ffn1"][1],
                          f"l{l}_f1")
        xs, s2 = _mix_fwd(xs, W["mix_norm_pre"][l], W["mix_norm_post"][l], wl["mix"][0], W["b_gate"][l],
                          conv_all[l], W["attn_sink"][l], W["mlstm_norm"][l], wl["mix"][1], cos2, sin2, f"l{l}_mx")
        xs, s3 = _ffn_fwd(xs, W["ffn2_norm_pre"][l], W["ffn2_norm_post"][l], wl["ffn2"][0], wl["ffn2"][1],
                          f"l{l}_f2")
        saved.append((s1, s2, s3))

    dx, loss_part = loss_fwd_bwd(xs, loss_target[0], name="loss")

    outs = {k: {n: [None] * depth for n in WEIGHTS} for k in ("g", "d", "m", "v")}
    small_parts = [None] * depth
    for l in reversed(range(depth)):
        wl = lw[l]
        s1, s2, s3 = saved[l]
        dx, dpre2, dpost2, dwgu2, dwd2 = _ffn_bwd(dx, s3, W["ffn2_norm_pre"][l], W["ffn2_norm_post"][l],
                                                  wl["ffn2"][2], wl["ffn2"][3], f"l{l}_f2")
        dx, dpre_m, dpost_m, dw_in_p, db_gate, dconv, dsink, dmn, dw_out = _mix_bwd(
            dx, s2, W["mix_norm_pre"][l], W["mix_norm_post"][l], wl["mix"][2], conv_all[l], W["attn_sink"][l],
            W["mlstm_norm"][l], wl["mix"][3], cos2, sin2, f"l{l}_mx")
        dx, dpre1, dpost1, dwgu1, dwd1 = _ffn_bwd(dx, s1, W["ffn1_norm_pre"][l], W["ffn1_norm_post"][l],
                                                  wl["ffn1"][2], wl["ffn1"][3], f"l{l}_f1")
        small_parts[l] = dict(ffn1_norm_pre=dpre1[0], ffn1_norm_post=dpost1[0], mix_norm_pre=dpre_m[0],
                              mix_norm_post=dpost_m[0], b_gate=db_gate, attn_sink=dsink, mlstm_norm=dmn,
                              ffn2_norm_pre=dpre2[0], ffn2_norm_post=dpost2[0])
        fs = D_FF // N_DEV
        dg1, du1 = _split_gu(dwgu1)
        dg2, du2 = _split_gu(dwgu2)
        full = dict(ffn1_w_gate=_col_blocks(dg1, fs), ffn1_w_up=_col_blocks(du1, fs), ffn1_w_down=_row_blocks(dwd1),
                    w_in=_col_blocks(_unpermute_dw_in(dw_in_p), IN_WIDTH // N_DEV), w_out=_row_blocks(dw_out),
                    ffn2_w_gate=_col_blocks(dg2, fs), ffn2_w_up=_col_blocks(du2, fs), ffn2_w_down=_row_blocks(dwd2),
                    conv_w=_col_blocks(dconv, conv_w.shape[2]))
        names = BIG + ("conv_w",)
        g2 = [_by_dest(full[n]) for n in names]
        recv = pair_exchange(g2, name=f"rs1_l{l}")
        halves = [pair_add(g2[t], recv[t], core, name=f"rs_add_l{l}_{t}") for t in range(len(names))]
        parts = chip_exchange([h.reshape(4, -1, LANES) for h in halves], name=f"rs2_l{l}")
        for t, n in enumerate(names):
            shp = W[n].shape[1:]
            res = adam_update(_flat(W[n][l]), parts[t], _flat(M1[n][l]), _flat(V2[n][l]), name=f"adam_l{l}_{t}")
            for k, r in zip(("g", "d", "m", "v"), res):
                outs[k][n][l] = _unflat(r, shp)

    vec = jnp.concatenate([small_parts[l][n].reshape(-1) for l in range(depth) for n in SMALL]
                          + [loss_part.reshape(-1)])
    n_small = vec.shape[0]
    gathered_small = all_gather([_flat(vec)], name="ag_small")[0]
    wvec = _flat(jnp.concatenate([W[n][l].reshape(-1) for l in range(depth) for n in SMALL] + [jnp.zeros((1,), F32)]))
    mvec = _flat(jnp.concatenate([M1[n][l].reshape(-1) for l in range(depth) for n in SMALL] + [jnp.zeros((1,), F32)]))
    vvec = _flat(jnp.concatenate([V2[n][l].reshape(-1) for l in range(depth) for n in SMALL] + [jnp.ones((1,), F32)]))
    res = adam_update(wvec, gathered_small, mvec, vvec, name="adam_small")
    res = [r.reshape(-1)[:n_small] for r in res]
    off = 0
    for l in range(depth):
        for n in SMALL:
            sz = W[n].shape[1]
            for k, r in zip(("g", "d", "m", "v"), res):
                outs[k][n][l] = r[off:off + sz]
            off += sz
    loss = res[0][off]

    stack = lambda k, n: jnp.stack(outs[k][n], axis=0)
    return (loss, dx[None], *[stack("g", n) for n in WEIGHTS], *[stack("d", n) for n in WEIGHTS],
            *[stack("m", n) for n in WEIGHTS], *[stack("v", n) for n in WEIGHTS])
```

```python
import jax
import jax.numpy as jnp
from jax import lax
from jax.experimental import pallas as pl
from jax.experimental.pallas import tpu as pltpu

F32 = jnp.float32
BF16 = jnp.bfloat16

D_MODEL = 2048
D_FF = 5632
ATT_HEADS = 8
ATT_KV_HEADS = 2
ATT_GROUP = ATT_HEADS // ATT_KV_HEADS
ATT_WIDTH = 1024
HEAD_DIM = 128
KV_WIDTH = 256
WINDOW = 128
BLK = 128
M_WIDTH = 1024
M_HEADS = 4
M_HEAD_DIM = 256
CONV_WIDTH = 5
EPS = 1e-6
ROPE_THETA = 10000.0
IN_WIDTH = 5648
N_GATES = 16
N_DEV = 8

ADAM_LR = 0.001
ADAM_B1 = 0.9
ADAM_B2 = 0.999
ADAM_EPS = 1e-08
ADAM_WD = 0.01
ADAM_STEP = 10

P_QK = 0
P_QA = 2048
P_VM = 3072
P_OM = 4096
P_KA = 5120
P_VA = 5376
P_G = 5632
P_WIDTH = 6144

LANES = 128
V7X_VMEM_LIMIT = 48 * 1024 * 1024
NEG = -1e30
MESH = pl.DeviceIdType.MESH
ANY = pl.BlockSpec(memory_space=pl.ANY)


def _tile(n, cands=(1024, 512, 256, 128)):
    for c in cands:
        if n % c == 0:
            return c
    return n


def _pcall(body, *, name, out_shape, in_specs, out_specs, grid=(), scratch=(), sem=None):
    return pl.pallas_call(
        body, name=name, out_shape=out_shape, in_specs=in_specs, out_specs=out_specs, grid=grid,
        scratch_shapes=list(scratch),
        compiler_params=pltpu.CompilerParams(dimension_semantics=sem, vmem_limit_bytes=V7X_VMEM_LIMIT))


def _ccall(body, *, name, out_shape, n_in, scratch):
    multi = isinstance(out_shape, (list, tuple))
    return pl.pallas_call(
        body, name=name, out_shape=out_shape, in_specs=[ANY] * n_in,
        out_specs=[ANY] * len(out_shape) if multi else ANY, scratch_shapes=list(scratch),
        compiler_params=pltpu.CompilerParams(has_side_effects=True))


def _dot(a, b):
    return jnp.dot(a, b, preferred_element_type=F32)


def _dot_nt(a, b):
    return lax.dot_general(a, b, (((1,), (1,)), ((), ())), preferred_element_type=F32)


def _dot_tn(a, b):
    return lax.dot_general(a, b, (((0,), (0,)), ((), ())), preferred_element_type=F32)


def _sigmoid(x):
    return 1.0 / (1.0 + jnp.exp(-x))


def mm_nn(a, b, *, name, out_dtype=F32):
    M, K = a.shape
    N = b.shape[1]
    tm, tk, tn = _tile(M), _tile(K), _tile(N)
    nk = K // tk

    def body(a_ref, b_ref, o_ref, acc):
        k = pl.program_id(2)

        @pl.when(k == 0)
        def _():
            acc[...] = jnp.zeros_like(acc)

        acc[...] += _dot(a_ref[...], b_ref[...])

        @pl.when(k == nk - 1)
        def _():
            o_ref[...] = acc[...].astype(o_ref.dtype)

    return _pcall(body, name=name, out_shape=jax.ShapeDtypeStruct((M, N), out_dtype),
                  in_specs=[pl.BlockSpec((tm, tk), lambda i, j, k: (i, k)),
                            pl.BlockSpec((tk, tn), lambda i, j, k: (k, j))],
                  out_specs=pl.BlockSpec((tm, tn), lambda i, j, k: (i, j)), grid=(M // tm, N // tn, nk),
                  scratch=[pltpu.VMEM((tm, tn), F32)], sem=("parallel", "parallel", "arbitrary"))(a, b)


def mm_tn(a, g, *, name, dest_rows=None, out_dtype=F32):
    M, K = a.shape
    N = g.shape[1]
    tm, tn = _tile(M), _tile(N)
    tk = _tile(K) if dest_rows is None else dest_rows
    nm = M // tm

    def body(a_ref, g_ref, o_ref, acc):
        m = pl.program_id(2)

        @pl.when(m == 0)
        def _():
            acc[...] = jnp.zeros_like(acc)

        acc[...] += _dot_tn(a_ref[...], g_ref[...])

        @pl.when(m == nm - 1)
        def _():
            o_ref[...] = acc[...].astype(o_ref.dtype)

    if dest_rows is None:
        out_shape = jax.ShapeDtypeStruct((K, N), out_dtype)
        out_spec = pl.BlockSpec((tk, tn), lambda i, j, m: (i, j))
    else:
        out_shape = jax.ShapeDtypeStruct((2, N_DEV // 2, tk, N), out_dtype)
        out_spec = pl.BlockSpec((None, None, tk, tn), lambda i, j, m: (i % 2, i // 2, 0, j))
    return _pcall(body, name=name, out_shape=out_shape,
                  in_specs=[pl.BlockSpec((tm, tk), lambda i, j, m: (m, i)),
                            pl.BlockSpec((tm, tn), lambda i, j, m: (m, j))],
                  out_specs=out_spec, grid=(K // tk, N // tn, nm), scratch=[pltpu.VMEM((tk, tn), F32)],
                  sem=("parallel", "parallel", "arbitrary"))(a, g)


def mm_nt(a, b, *, name, out_dtype=F32):
    M, K = a.shape
    N = b.shape[0]
    tm, tn, tk = _tile(M), _tile(N), _tile(K)
    nk = K // tk

    def body(a_ref, b_ref, o_ref, acc):
        k = pl.program_id(2)

        @pl.when(k == 0)
        def _():
            acc[...] = jnp.zeros_like(acc)

        acc[...] += _dot_nt(a_ref[...], b_ref[...])

        @pl.when(k == nk - 1)
        def _():
            o_ref[...] = acc[...].astype(o_ref.dtype)

    return _pcall(body, name=name, out_shape=jax.ShapeDtypeStruct((M, N), out_dtype),
                  in_specs=[pl.BlockSpec((tm, tk), lambda i, j, k: (i, k)),
                            pl.BlockSpec((tn, tk), lambda i, j, k: (j, k))],
                  out_specs=pl.BlockSpec((tm, tn), lambda i, j, k: (i, j)), grid=(M // tm, N // tn, nk),
                  scratch=[pltpu.VMEM((tm, tn), F32)], sem=("parallel", "parallel", "arbitrary"))(a, b)


FS = D_FF // N_DEV
FSP = 768


def ffn_gu(xn, wg8, wu8, *, name):
    S, D = xn.shape
    tm = _tile(S)

    def body(x_ref, wg_ref, wu_ref, hg_ref, hu_ref, act_ref):
        xv = x_ref[...]
        hg = _dot(xv, wg_ref[...])
        hu = _dot(xv, wu_ref[...])
        hg_ref[...] = hg.astype(BF16)
        hu_ref[...] = hu.astype(BF16)
        act_ref[...] = (hg * _sigmoid(hg) * hu).astype(BF16)

    wspec = pl.BlockSpec((None, D, FSP), lambda i, j: (j, 0, 0))
    ospec = pl.BlockSpec((None, tm, FSP), lambda i, j: (j, i, 0))
    shp = jax.ShapeDtypeStruct((N_DEV, S, FSP), BF16)
    return _pcall(body, name=name, out_shape=(shp, shp, shp),
                  in_specs=[pl.BlockSpec((tm, D), lambda i, j: (i, 0)), wspec, wspec],
                  out_specs=(ospec, ospec, ospec), grid=(S // tm, N_DEV), sem=("parallel", "arbitrary"))(xn, wg8, wu8)


def ffn_down(act8, wd8, *, name):
    _, S, _ = act8.shape
    D = wd8.shape[2]
    tm, tn = _tile(S), _tile(D)

    def body(a_ref, w_ref, o_ref):
        @pl.when(pl.program_id(2) == 0)
        def _():
            o_ref[...] = jnp.zeros_like(o_ref)

        o_ref[...] += _dot(a_ref[...], w_ref[...])

    return _pcall(body, name=name, out_shape=jax.ShapeDtypeStruct((S, D), F32),
                  in_specs=[pl.BlockSpec((None, tm, FSP), lambda i, n, j: (j, i, 0)),
                            pl.BlockSpec((None, FSP, tn), lambda i, n, j: (j, 0, n))],
                  out_specs=pl.BlockSpec((tm, tn), lambda i, n, j: (i, n)),
                  grid=(S // tm, D // tn, N_DEV), sem=("parallel", "parallel", "arbitrary"))(act8, wd8)


def ffn_dact(df, wd8, hg8, hu8, *, name):
    S, D = df.shape
    tm = _tile(S)

    def body(d_ref, w_ref, hg_ref, hu_ref, dg_ref, du_ref):
        da = _dot_nt(d_ref[...], w_ref[...])
        hg = hg_ref[...].astype(F32)
        hu = hu_ref[...].astype(F32)
        sg = _sigmoid(hg)
        dg_ref[...] = (da * hu * (sg * (1.0 + hg * (1.0 - sg)))).astype(BF16)
        du_ref[...] = (da * hg * sg).astype(BF16)

    blk = pl.BlockSpec((None, tm, FSP), lambda i, j: (j, i, 0))
    shp = jax.ShapeDtypeStruct((N_DEV, S, FSP), BF16)
    return _pcall(body, name=name, out_shape=(shp, shp),
                  in_specs=[pl.BlockSpec((tm, D), lambda i, j: (i, 0)),
                            pl.BlockSpec((None, FSP, D), lambda i, j: (j, 0, 0)), blk, blk],
                  out_specs=(blk, blk), grid=(S // tm, N_DEV), sem=("parallel", "arbitrary"))(df, wd8, hg8, hu8)


def ffn_dwd(act8, df, *, name, out_dtype):
    _, S, _ = act8.shape
    D = df.shape[1]
    tm, tn = _tile(S), _tile(D)
    nm = S // tm

    def body(a_ref, d_ref, o_ref, acc):
        m = pl.program_id(2)

        @pl.when(m == 0)
        def _():
            acc[...] = jnp.zeros_like(acc)

        acc[...] += _dot_tn(a_ref[...], d_ref[...])

        @pl.when(m == nm - 1)
        def _():
            o_ref[...] = acc[0:FS, :].astype(o_ref.dtype)

    return _pcall(body, name=name, out_shape=jax.ShapeDtypeStruct((2, N_DEV // 2, FS, D), out_dtype),
                  in_specs=[pl.BlockSpec((None, tm, FSP), lambda j, n, m: (j, m, 0)),
                            pl.BlockSpec((tm, tn), lambda j, n, m: (m, n))],
                  out_specs=pl.BlockSpec((None, None, FS, tn), lambda j, n, m: (j % 2, j // 2, 0, n)),
                  grid=(N_DEV, D // tn, nm), scratch=[pltpu.VMEM((FSP, tn), F32)],
                  sem=("parallel", "parallel", "arbitrary"))(act8, df)


def ffn_dwgu(xn, dg8, du8, *, name, out_dtype):
    S, D = xn.shape
    tm, tk = _tile(S), _tile(D)
    nm = S // tm

    def body(x_ref, dg_ref, du_ref, og_ref, ou_ref, accg, accu):
        m = pl.program_id(2)

        @pl.when(m == 0)
        def _():
            accg[...] = jnp.zeros_like(accg)
            accu[...] = jnp.zeros_like(accu)

        xv = x_ref[...]
        accg[...] += _dot_tn(xv, dg_ref[...])
        accu[...] += _dot_tn(xv, du_ref[...])

        @pl.when(m == nm - 1)
        def _():
            og_ref[...] = accg[:, 0:FS].astype(og_ref.dtype)
            ou_ref[...] = accu[:, 0:FS].astype(ou_ref.dtype)

    blk = pl.BlockSpec((None, tm, FSP), lambda j, k, m: (j, m, 0))
    ospec = pl.BlockSpec((None, None, tk, FS), lambda j, k, m: (j % 2, j // 2, k, 0))
    shp = jax.ShapeDtypeStruct((2, N_DEV // 2, D, FS), out_dtype)
    return _pcall(body, name=name, out_shape=(shp, shp),
                  in_specs=[pl.BlockSpec((tm, tk), lambda j, k, m: (m, k)), blk, blk],
                  out_specs=(ospec, ospec), grid=(N_DEV, D // tk, nm),
                  scratch=[pltpu.VMEM((tk, FSP), F32), pltpu.VMEM((tk, FSP), F32)],
                  sem=("parallel", "parallel", "arbitrary"))(xn, dg8, du8)


def ffn_dxn(dg8, du8, wg8, wu8, *, name):
    _, S, _ = dg8.shape
    D = wg8.shape[1]
    tm, tn = _tile(S), _tile(D)

    def body(dg_ref, du_ref, wg_ref, wu_ref, o_ref):
        @pl.when(pl.program_id(2) == 0)
        def _():
            o_ref[...] = jnp.zeros_like(o_ref)

        o_ref[...] += _dot_nt(dg_ref[...], wg_ref[...]) + _dot_nt(du_ref[...], wu_ref[...])

    blk = pl.BlockSpec((None, tm, FSP), lambda i, n, j: (j, i, 0))
    wspec = pl.BlockSpec((None, tn, FSP), lambda i, n, j: (j, n, 0))
    return _pcall(body, name=name, out_shape=jax.ShapeDtypeStruct((S, D), F32),
                  in_specs=[blk, blk, wspec, wspec], out_specs=pl.BlockSpec((tm, tn), lambda i, n, j: (i, n)),
                  grid=(S // tm, D // tn, N_DEV), sem=("parallel", "parallel", "arbitrary"))(dg8, du8, wg8, wu8)


def norm_fwd(x, g, *, name, scale=1.0, resid=None, out_dtype=F32):
    S, D = x.shape
    tm = _tile(S, (512, 256, 128))

    def body(*refs):
        if resid is None:
            x_ref, g_ref, o_ref = refs
        else:
            x_ref, g_ref, r_ref, o_ref = refs
        xv = x_ref[...].astype(F32)
        r = lax.rsqrt(jnp.mean(xv * xv, axis=-1, keepdims=True) + EPS)
        y = (xv * r) * g_ref[...]
        if scale != 1.0:
            y = y * scale
        if resid is not None:
            y = y + r_ref[...]
        o_ref[...] = y.astype(o_ref.dtype)

    row = pl.BlockSpec((tm, D), lambda i: (i, 0))
    in_specs = [row, pl.BlockSpec((1, D), lambda i: (0, 0))]
    args = [x, g.reshape(1, D)]
    if resid is not None:
        in_specs.append(row)
        args.append(resid)
    return _pcall(body, name=name, out_shape=jax.ShapeDtypeStruct((S, D), out_dtype), in_specs=in_specs,
                  out_specs=row, grid=(S // tm,), sem=("parallel",))(*args)


def norm_bwd(dy, x, g, *, name, scale=1.0, resid=None, out_dtype=F32):
    S, D = x.shape
    tm = _tile(S, (512, 256, 128))

    def body(*refs):
        if resid is None:
            dy_ref, x_ref, g_ref, dx_ref, dg_ref = refs
        else:
            dy_ref, x_ref, g_ref, r_ref, dx_ref, dg_ref = refs

        @pl.when(pl.program_id(0) == 0)
        def _():
            dg_ref[...] = jnp.zeros_like(dg_ref)

        xv = x_ref[...].astype(F32)
        d = dy_ref[...].astype(F32)
        if scale != 1.0:
            d = d * scale
        r = lax.rsqrt(jnp.mean(xv * xv, axis=-1, keepdims=True) + EPS)
        xh = xv * r
        dg_ref[...] += jnp.sum(d * xh, axis=0, keepdims=True)
        dxh = d * g_ref[...]
        dx = r * (dxh - xh * jnp.mean(dxh * xh, axis=-1, keepdims=True))
        if resid is not None:
            dx = dx + r_ref[...]
        dx_ref[...] = dx.astype(dx_ref.dtype)

    row = pl.BlockSpec((tm, D), lambda i: (i, 0))
    vec = pl.BlockSpec((1, D), lambda i: (0, 0))
    in_specs = [row, row, vec]
    args = [dy, x, g.reshape(1, D)]
    if resid is not None:
        in_specs.append(row)
        args.append(resid)
    return _pcall(body, name=name,
                  out_shape=(jax.ShapeDtypeStruct((S, D), out_dtype), jax.ShapeDtypeStruct((1, D), F32)),
                  in_specs=in_specs, out_specs=(row, vec), grid=(S // tm,), sem=("arbitrary",))(*args)


def loss_fwd_bwd(y, target, *, name):
    S, D = y.shape
    tm = _tile(S, (512, 256, 128))

    def body(y_ref, t_ref, dy_ref, l_ref):
        @pl.when(pl.program_id(0) == 0)
        def _():
            l_ref[...] = jnp.zeros_like(l_ref)

        e = y_ref[...] - t_ref[...]
        dy_ref[...] = e * (1.0 / D)
        l_ref[...] += jnp.sum(jnp.sum(e * e, axis=1, keepdims=True), axis=0, keepdims=True) * (0.5 / D)

    row = pl.BlockSpec((tm, D), lambda i: (i, 0))
    one = pl.BlockSpec((1, 1), lambda i: (0, 0))
    return _pcall(body, name=name,
                  out_shape=(jax.ShapeDtypeStruct((S, D), F32), jax.ShapeDtypeStruct((1, 1), F32)),
                  in_specs=[row, row], out_specs=(row, one), grid=(S // tm,), sem=("arbitrary",))(y, target)


def _rope_tables(S):
    half = HEAD_DIM // 2
    inv_freq = ROPE_THETA ** (-jnp.arange(half, dtype=F32) / half)
    ang = jnp.arange(S, dtype=F32)[:, None] * inv_freq[None, :]
    cos, sin = jnp.cos(ang), jnp.sin(ang)
    return jnp.concatenate([cos, cos], axis=1), jnp.concatenate([-sin, sin], axis=1)


def _rope(x, cos2, sin2):
    return x * cos2 + pltpu.roll(x, HEAD_DIM // 2, 1) * sin2


def _unrope(d, cos2, sin2):
    return d * cos2 + pltpu.roll(d * sin2, HEAD_DIM // 2, 1)


def _nbr_specs(width, col, nb):
    return [pl.BlockSpec((BLK, width), lambda n, c=col: (jnp.maximum(n - 1, 0), c)),
            pl.BlockSpec((BLK, width), lambda n, c=col: (n, c)),
            pl.BlockSpec((BLK, width), lambda n, c=col: (jnp.minimum(n + 1, nb - 1), c))]


def attn_fwd(proj, cos2, sin2, sink, *, name):
    S = proj.shape[0]
    nb = S // BLK
    scale = HEAD_DIM ** -0.5

    def body(sink_ref, q_ref, k0, k1, k2, v0, v1, v2, c0, c1, c2, s0, s1, s2, o_ref, lse_ref):
        n = pl.program_id(0)
        cosk = jnp.concatenate([c0[...], c1[...], c2[...]], axis=0)
        sink_ = jnp.concatenate([s0[...], s1[...], s2[...]], axis=0)
        kall = jnp.concatenate([k0[...], k1[...], k2[...]], axis=0)
        vall = jnp.concatenate([v0[...], v1[...], v2[...]], axis=0)
        rows = lax.broadcasted_iota(jnp.int32, (BLK, 3 * BLK), 0)
        cols = lax.broadcasted_iota(jnp.int32, (BLK, 3 * BLK), 1)
        kpos = (n - 1) * BLK + cols
        valid = (jnp.abs(cols - BLK - rows) <= WINDOW) & (kpos >= 0) & (kpos < S)
        valid = jnp.concatenate([valid] * ATT_GROUP, axis=0)
        lane = lax.broadcasted_iota(jnp.int32, (BLK, LANES), 1)
        lse_tile = jnp.zeros((BLK, LANES), F32)
        for hk in range(ATT_KV_HEADS):
            ks = slice(hk * HEAD_DIM, (hk + 1) * HEAD_DIM)
            kh = _rope(kall[:, ks], cosk, sink_).astype(BF16)
            vh = vall[:, ks].astype(BF16)
            qs = []
            for g in range(ATT_GROUP):
                hq = hk * ATT_GROUP + g
                qs.append(_rope(q_ref[:, hq * HEAD_DIM:(hq + 1) * HEAD_DIM], c1[...], s1[...]))
            qh = jnp.concatenate(qs, axis=0).astype(BF16)
            s = _dot_nt(qh, kh) * scale
            s = jnp.where(valid, s, NEG)
            snk = jnp.concatenate(
                [jnp.full((BLK, 1), sink_ref[hk * ATT_GROUP + g], F32) for g in range(ATT_GROUP)], axis=0)
            m = jnp.maximum(jnp.max(s, axis=1, keepdims=True), snk)
            p = jnp.exp(s - m)
            l = jnp.sum(p, axis=1, keepdims=True) + jnp.exp(snk - m)
            o = _dot(p.astype(BF16), vh) * (1.0 / l)
            lse = m + jnp.log(l)
            for g in range(ATT_GROUP):
                hq = hk * ATT_GROUP + g
                o_ref[:, hq * HEAD_DIM:(hq + 1) * HEAD_DIM] = o[g * BLK:(g + 1) * BLK].astype(o_ref.dtype)
                lse_tile = lse_tile + jnp.where(lane == hq, lse[g * BLK:(g + 1) * BLK], 0.0)
        lse_ref[...] = lse_tile

    in_specs = ([pl.BlockSpec(memory_space=pltpu.SMEM),
                 pl.BlockSpec((BLK, ATT_WIDTH), lambda n: (n, P_QA // ATT_WIDTH))]
                + _nbr_specs(KV_WIDTH, P_KA // KV_WIDTH, nb) + _nbr_specs(KV_WIDTH, P_VA // KV_WIDTH, nb)
                + _nbr_specs(HEAD_DIM, 0, nb) + _nbr_specs(HEAD_DIM, 0, nb))
    return _pcall(body, name=name,
                  out_shape=(jax.ShapeDtypeStruct((S, ATT_WIDTH), BF16), jax.ShapeDtypeStruct((S, LANES), F32)),
                  in_specs=in_specs,
                  out_specs=(pl.BlockSpec((BLK, ATT_WIDTH), lambda n: (n, 0)),
                             pl.BlockSpec((BLK, LANES), lambda n: (n, 0))),
                  grid=(nb,), sem=("parallel",))(sink, proj, proj, proj, proj, proj, proj, proj,
                                                 cos2, cos2, cos2, sin2, sin2, sin2)


def attn_bwd(proj, y, dy, lse, cos2, sin2, sink, *, name):
    S = proj.shape[0]
    nb = S // BLK
    scale = HEAD_DIM ** -0.5

    def body(sink_ref, q0, q1, q2, k0, k1, k2, v0, v1, v2, o0, o1, o2, d0, d1, d2, l0, l1, l2,
             c0, c1, c2, s0, s1, s2, dq_ref, dk_ref, dv_ref, dsink_ref):
        n = pl.program_id(0)

        @pl.when(n == 0)
        def _():
            dsink_ref[...] = jnp.zeros_like(dsink_ref)

        q_nb, o_nb, d_nb, l_nb = (q0, q1, q2), (o0, o1, o2), (d0, d1, d2), (l0, l1, l2)
        c_nb, s_nb = (c0, c1, c2), (s0, s1, s2)
        cosk = jnp.concatenate([c0[...], c1[...], c2[...]], axis=0)
        sink_ = jnp.concatenate([s0[...], s1[...], s2[...]], axis=0)
        kall = jnp.concatenate([k0[...], k1[...], k2[...]], axis=0)
        vall = jnp.concatenate([v0[...], v1[...], v2[...]], axis=0)
        lane = lax.broadcasted_iota(jnp.int32, (1, LANES), 1)
        rows = lax.broadcasted_iota(jnp.int32, (BLK, 3 * BLK), 0)
        cols = lax.broadcasted_iota(jnp.int32, (BLK, 3 * BLK), 1)
        kpos = (n - 1) * BLK + cols
        valid_q = (jnp.abs(cols - BLK - rows) <= WINDOW) & (kpos >= 0) & (kpos < S)
        valid_q = jnp.concatenate([valid_q] * ATT_GROUP, axis=0)
        qr = lax.broadcasted_iota(jnp.int32, (3 * BLK, BLK), 0)
        kc = lax.broadcasted_iota(jnp.int32, (3 * BLK, BLK), 1)
        qpos = (n - 1) * BLK + qr
        valid_k = (jnp.abs(qr - BLK - kc) <= WINDOW) & (qpos >= 0) & (qpos < S)
        valid_k = jnp.concatenate([valid_k] * ATT_GROUP, axis=0)
        dsink_acc = jnp.zeros((1, LANES), F32)

        def head_cols(ref, hq):
            return ref[:, hq * HEAD_DIM:(hq + 1) * HEAD_DIM]

        for hk in range(ATT_KV_HEADS):
            ks = slice(hk * HEAD_DIM, (hk + 1) * HEAD_DIM)
            kh = _rope(kall[:, ks], cosk, sink_).astype(BF16)
            vh = vall[:, ks].astype(BF16)
            qs, dos, lses, deltas = [], [], [], []
            for g in range(ATT_GROUP):
                hq = hk * ATT_GROUP + g
                qs.append(_rope(head_cols(q1, hq), c1[...], s1[...]))
                do = head_cols(d1, hq)
                dos.append(do)
                lses.append(l1[:, hq:hq + 1])
                deltas.append(jnp.sum(do * head_cols(o1, hq).astype(F32), axis=1, keepdims=True))
            qh = jnp.concatenate(qs, axis=0).astype(BF16)
            doh = jnp.concatenate(dos, axis=0).astype(BF16)
            lseh = jnp.concatenate(lses, axis=0)
            delh = jnp.concatenate(deltas, axis=0)
            s = jnp.where(valid_q, _dot_nt(qh, kh) * scale, NEG)
            p = jnp.exp(s - lseh)
            dp = _dot_nt(doh, vh)
            ds = (p * (dp - delh)).astype(BF16)
            dq = _dot(ds, kh) * scale
            for g in range(ATT_GROUP):
                hq = hk * ATT_GROUP + g
                dq_ref[:, hq * HEAD_DIM:(hq + 1) * HEAD_DIM] = _unrope(dq[g * BLK:(g + 1) * BLK], c1[...], s1[...])
                psink = jnp.exp(sink_ref[hq] - lses[g])
                dsink_acc = dsink_acc + jnp.where(lane == hq, -jnp.sum(psink * deltas[g]), 0.0)
            kown = _rope(k1[:, ks], c1[...], s1[...]).astype(BF16)
            vown = v1[:, ks].astype(BF16)
            qs, dos, lses, deltas = [], [], [], []
            for g in range(ATT_GROUP):
                hq = hk * ATT_GROUP + g
                for j in range(3):
                    qs.append(_rope(head_cols(q_nb[j], hq), c_nb[j][...], s_nb[j][...]))
                    do = head_cols(d_nb[j], hq)
                    dos.append(do)
                    lses.append(l_nb[j][:, hq:hq + 1])
                    deltas.append(jnp.sum(do * head_cols(o_nb[j], hq).astype(F32), axis=1, keepdims=True))
            qh = jnp.concatenate(qs, axis=0).astype(BF16)
            doh = jnp.concatenate(dos, axis=0).astype(BF16)
            lseh = jnp.concatenate(lses, axis=0)
            delh = jnp.concatenate(deltas, axis=0)
            s = jnp.where(valid_k, _dot_nt(qh, kown) * scale, NEG)
            p = jnp.where(valid_k, jnp.exp(s - lseh), 0.0)
            dv_ref[:, ks] = _dot_tn(p.astype(BF16), doh)
            dp = _dot_nt(doh, vown)
            ds = (p * (dp - delh)).astype(BF16)
            dk_ref[:, ks] = _unrope(_dot_tn(ds, qh) * scale, c1[...], s1[...])
        dsink_ref[...] += dsink_acc

    in_specs = ([pl.BlockSpec(memory_space=pltpu.SMEM)]
                + _nbr_specs(ATT_WIDTH, P_QA // ATT_WIDTH, nb)
                + _nbr_specs(KV_WIDTH, P_KA // KV_WIDTH, nb) + _nbr_specs(KV_WIDTH, P_VA // KV_WIDTH, nb)
                + _nbr_specs(ATT_WIDTH, 0, nb) + _nbr_specs(ATT_WIDTH, 0, nb) + _nbr_specs(LANES, 0, nb)
                + _nbr_specs(HEAD_DIM, 0, nb) + _nbr_specs(HEAD_DIM, 0, nb))
    args = [sink] + [proj] * 9 + [y] * 3 + [dy] * 3 + [lse] * 3 + [cos2] * 3 + [sin2] * 3
    return _pcall(body, name=name,
                  out_shape=(jax.ShapeDtypeStruct((S, ATT_WIDTH), F32), jax.ShapeDtypeStruct((S, KV_WIDTH), F32),
                             jax.ShapeDtypeStruct((S, KV_WIDTH), F32), jax.ShapeDtypeStruct((1, LANES), F32)),
                  in_specs=in_specs,
                  out_specs=(pl.BlockSpec((BLK, ATT_WIDTH), lambda n: (n, 0)),
                             pl.BlockSpec((BLK, KV_WIDTH), lambda n: (n, 0)),
                             pl.BlockSpec((BLK, KV_WIDTH), lambda n: (n, 0)),
                             pl.BlockSpec((1, LANES), lambda n: (0, 0))),
                  grid=(nb,), sem=("arbitrary",))(*args)


CONV_HALO = 8
CONV_COLS = 512


def _halo_specs(tm, nrow, col_of):
    hb = tm // CONV_HALO
    return [pl.BlockSpec((CONV_HALO, CONV_COLS), lambda i, j: (jnp.maximum(i * hb - 1, 0), col_of(j))),
            pl.BlockSpec((tm, CONV_COLS), lambda i, j: (i, col_of(j))),
            pl.BlockSpec((CONV_HALO, CONV_COLS),
                         lambda i, j: (jnp.minimum((i + 1) * hb, nrow * hb - 1), col_of(j)))]


def _with_halo(prev, cur, nxt, i, nrow):
    p = jnp.where(i > 0, prev[...], 0.0)
    q = jnp.where(i < nrow - 1, nxt[...], 0.0)
    return jnp.concatenate([p, cur[...], q], axis=0)


def _conv_taps(xt, w_ref, tm):
    n = xt.shape[0]
    acc = jnp.zeros_like(xt)
    for j in range(CONV_WIDTH):
        sh = (CONV_WIDTH // 2 - j) % n
        xs = xt if sh == 0 else pltpu.roll(xt, sh, 0)
        acc = acc + xs * w_ref[j:j + 1, :]
    return acc


def conv_fwd(proj, conv_w, *, name):
    S = proj.shape[0]
    tm = _tile(S, (512, 256, 128))
    nrow = S // tm

    def body(xp, xc, xn, w_ref, o_ref):
        i = pl.program_id(0)
        xt = _with_halo(xp, xc, xn, i, nrow)
        pre = _conv_taps(xt, w_ref, tm)[CONV_HALO:CONV_HALO + tm]
        o_ref[...] = pre * _sigmoid(pre)

    return _pcall(body, name=name, out_shape=jax.ShapeDtypeStruct((S, 2 * M_WIDTH), F32),
                  in_specs=_halo_specs(tm, nrow, lambda j: P_QK // CONV_COLS + j)
                  + [pl.BlockSpec((CONV_HALO, CONV_COLS), lambda i, j: (0, j))],
                  out_specs=pl.BlockSpec((tm, CONV_COLS), lambda i, j: (i, j)),
                  grid=(nrow, 2 * M_WIDTH // CONV_COLS), sem=("parallel", "parallel"))(proj, proj, proj, conv_w)


def conv_bwd(proj, conv_w, da, db, *, name):
    S = proj.shape[0]
    tm = _tile(S, (512, 256, 128))
    nrow = S // tm

    def body(xp, xc, xn, ap, ac, an, bp, bc, bn, w_ref, dx_ref, dw_ref):
        i = pl.program_id(1)

        @pl.when(i == 0)
        def _():
            dw_ref[...] = jnp.zeros_like(dw_ref)

        xt = _with_halo(xp, xc, xn, i, nrow)
        dt = _with_halo(ap, ac, an, i, nrow) + _with_halo(bp, bc, bn, i, nrow)
        pre = _conv_taps(xt, w_ref, tm)
        sg = _sigmoid(pre)
        dpre = dt * (sg * (1.0 + pre * (1.0 - sg)))
        n = xt.shape[0]
        ridx = lax.broadcasted_iota(jnp.int32, (n, 1), 0)
        dpre = jnp.where((ridx >= 2) & (ridx < n - 2), dpre, 0.0)
        dx = jnp.zeros_like(xt)
        own = (ridx >= CONV_HALO) & (ridx < CONV_HALO + tm)
        dpre_own = jnp.where(own, dpre, 0.0)
        dw_rows = []
        for j in range(CONV_WIDTH):
            sh = (j - CONV_WIDTH // 2) % n
            ds_ = dpre if sh == 0 else pltpu.roll(dpre, sh, 0)
            dx = dx + ds_ * w_ref[j:j + 1, :]
            shx = (CONV_WIDTH // 2 - j) % n
            xs = xt if shx == 0 else pltpu.roll(xt, shx, 0)
            dw_rows.append(jnp.sum(dpre_own * xs, axis=0, keepdims=True))
        dx_ref[...] = dx[CONV_HALO:CONV_HALO + tm]
        dw_rows.append(jnp.zeros((CONV_HALO - CONV_WIDTH, CONV_COLS), F32))
        dw_ref[...] += jnp.concatenate(dw_rows, axis=0)

    colq = lambda j: P_QK // CONV_COLS + j
    same = lambda j: j

    def swap(specs):
        return [pl.BlockSpec(s.block_shape, (lambda f: (lambda j, i: f(i, j)))(s.index_map)) for s in specs]

    in_specs = swap(_halo_specs(tm, nrow, colq) + _halo_specs(tm, nrow, same) + _halo_specs(tm, nrow, same)
                    + [pl.BlockSpec((CONV_HALO, CONV_COLS), lambda i, j: (0, j))])
    return _pcall(body, name=name,
                  out_shape=(jax.ShapeDtypeStruct((S, 2 * M_WIDTH), F32),
                             jax.ShapeDtypeStruct((CONV_HALO, 2 * M_WIDTH), F32)),
                  in_specs=in_specs,
                  out_specs=(pl.BlockSpec((tm, CONV_COLS), lambda j, i: (i, j)),
                             pl.BlockSpec((CONV_HALO, CONV_COLS), lambda j, i: (0, j))),
                  grid=(2 * M_WIDTH // CONV_COLS, nrow), sem=("parallel", "arbitrary"))(
                      proj, proj, proj, da, da, da, db, db, db, conv_w)


def _log_sigmoid(x):
    return jnp.minimum(x, 0.0) - jnp.log(1.0 + jnp.exp(-jnp.abs(x)))


def _scan_sum(x, axis, from_end):
    idx = lax.broadcasted_iota(jnp.int32, x.shape, axis)
    n = x.shape[axis]
    sh = 1
    while sh < n:
        if from_end:
            x = x + jnp.where(idx < n - sh, pltpu.roll(x, n - sh, axis), 0.0)
        else:
            x = x + jnp.where(idx >= sh, pltpu.roll(x, sh, axis), 0.0)
        sh *= 2
    return x


def _gate_setup(gc_ref, gr_ref, bgc_ref, bgr_ref, reverse):
    gc = gc_ref[...] + bgc_ref[...]
    gr = gr_ref[...] + bgr_ref[...]
    bc = _scan_sum(_log_sigmoid(gc), 0, reverse)
    br = _scan_sum(_log_sigmoid(gr), 1, reverse)
    return gc, gr, bc, br


def _head_gates(gc, gr, bc, br, h, m_in, reverse, tri):
    io = (M_HEADS if reverse else 0) + h
    fo = (3 * M_HEADS if reverse else 2 * M_HEADS) + h
    last = 0 if reverse else BLK - 1
    b_col, b_row = bc[:, fo:fo + 1], br[fo:fo + 1, :]
    ig_col, ig_row = gc[:, io:io + 1], gr[io:io + 1, :]
    logd = jnp.where(tri, b_col - b_row + ig_row, NEG)
    m_t = jnp.maximum(b_col + m_in, jnp.max(logd, axis=1, keepdims=True))
    dm = jnp.exp(logd - m_t)
    gi = jnp.exp(b_col + m_in - m_t)
    b_last = b_row[:, last:last + 1]
    logw = b_last - b_row + ig_row
    m_new = jnp.maximum(b_last + m_in, jnp.max(logw, axis=1, keepdims=True))
    w_col = jnp.exp(b_last - b_col + ig_col - m_new)
    dec = jnp.exp(b_last + m_in - m_new)
    return io, fo, m_t, dm, gi, m_new, w_col, dec


def _tri_mask(reverse):
    rows = lax.broadcasted_iota(jnp.int32, (BLK, BLK), 0)
    cols = lax.broadcasted_iota(jnp.int32, (BLK, BLK), 1)
    return (cols >= rows) if reverse else (cols <= rows)


def mlstm_fwd(qk, proj, gates_r, bg_c, bg_r, *, reverse, name):
    S = qk.shape[0]
    nc = S // BLK
    kscale = M_HEAD_DIM ** -0.5
    cidx = (lambda c: nc - 1 - c) if reverse else (lambda c: c)

    def body(qk_ref, v_ref, gc_ref, gr_ref, bgc_ref, bgr_ref, h_ref, den_ref, cst_ref, nm_ref, c_sc, n_sc, m_sc):
        @pl.when(pl.program_id(0) == 0)
        def _():
            c_sc[...] = jnp.zeros_like(c_sc)
            n_sc[...] = jnp.zeros_like(n_sc)
            m_sc[...] = jnp.zeros_like(m_sc)

        gc, gr, bc, br = _gate_setup(gc_ref, gr_ref, bgc_ref, bgr_ref, reverse)
        tri = _tri_mask(reverse)
        lane = lax.broadcasted_iota(jnp.int32, (BLK, LANES), 1)
        den_tile = jnp.zeros((BLK, LANES), F32)
        for h in range(M_HEADS):
            cs = slice(h * M_HEAD_DIM, (h + 1) * M_HEAD_DIM)
            m_in = m_sc[h][:, 0:1]
            _, _, m_t, dm, gi, m_new, w_col, dec = _head_gates(gc, gr, bc, br, h, m_in, reverse, tri)
            q = qk_ref[:, cs]
            k = qk_ref[:, M_WIDTH + h * M_HEAD_DIM:M_WIDTH + (h + 1) * M_HEAD_DIM] * kscale
            v = v_ref[:, cs]
            c_in, n_in = c_sc[h], n_sc[h]
            cst_ref[h] = c_in
            nm_ref[h, 0:1, :] = n_in
            nm_ref[h, 1:2, :] = m_sc[h]
            qb, kb, vb = q.astype(BF16), k.astype(BF16), v.astype(BF16)
            s = _dot_nt(qb, kb) * dm
            num = _dot(s.astype(BF16), vb) + gi * _dot_nt(qb, c_in.astype(BF16))
            den = jnp.sum(s, axis=1, keepdims=True) + gi * jnp.sum(q * n_in, axis=1, keepdims=True)
            z = jnp.maximum(jnp.abs(den), jnp.exp(-m_t))
            h_ref[:, cs] = num * (1.0 / z)
            den_tile = den_tile + jnp.where(lane == h, den, 0.0)
            c_sc[h] = dec * c_in + _dot_tn((w_col * v).astype(BF16), kb)
            n_sc[h] = dec * n_in + jnp.sum(w_col * k, axis=0, keepdims=True)
            m_sc[h] = jnp.broadcast_to(m_new, (1, M_HEAD_DIM))
        den_ref[...] = den_tile

    return _pcall(
        body, name=name,
        out_shape=(jax.ShapeDtypeStruct((S, M_WIDTH), F32), jax.ShapeDtypeStruct((S, LANES), F32),
                   jax.ShapeDtypeStruct((nc, M_HEADS, M_HEAD_DIM, M_HEAD_DIM), F32),
                   jax.ShapeDtypeStruct((nc, M_HEADS, 2, M_HEAD_DIM), F32)),
        in_specs=[pl.BlockSpec((BLK, 2 * M_WIDTH), lambda c: (cidx(c), 0)),
                  pl.BlockSpec((BLK, M_WIDTH), lambda c: (cidx(c), P_VM // M_WIDTH)),
                  pl.BlockSpec((BLK, LANES), lambda c: (cidx(c), P_G // LANES)),
                  pl.BlockSpec((N_GATES, BLK), lambda c: (0, cidx(c))),
                  pl.BlockSpec((1, LANES), lambda c: (0, 0)),
                  pl.BlockSpec((N_GATES, 1), lambda c: (0, 0))],
        out_specs=(pl.BlockSpec((BLK, M_WIDTH), lambda c: (cidx(c), 0)),
                   pl.BlockSpec((BLK, LANES), lambda c: (cidx(c), 0)),
                   pl.BlockSpec((None, M_HEADS, M_HEAD_DIM, M_HEAD_DIM), lambda c: (cidx(c), 0, 0, 0)),
                   pl.BlockSpec((None, M_HEADS, 2, M_HEAD_DIM), lambda c: (cidx(c), 0, 0, 0))),
        grid=(nc,),
        scratch=[pltpu.VMEM((M_HEADS, M_HEAD_DIM, M_HEAD_DIM), F32), pltpu.VMEM((M_HEADS, 1, M_HEAD_DIM), F32),
                 pltpu.VMEM((M_HEADS, 1, M_HEAD_DIM), F32)],
        sem=("arbitrary",))(qk, proj, proj, gates_r, bg_c, bg_r)


def mlstm_bwd(qk, proj, gates_r, bg_c, bg_r, hdir, den, cst, nm, dh, *, reverse, name):
    S = qk.shape[0]
    nc = S // BLK
    kscale = M_HEAD_DIM ** -0.5
    cidx = (lambda c: c) if reverse else (lambda c: nc - 1 - c)
    last = 0 if reverse else BLK - 1

    def body(qk_ref, v_ref, gc_ref, gr_ref, bgc_ref, bgr_ref, h_ref, den_ref, cst_ref, nm_ref, dh_ref,
             dqk_ref, dv_ref, dgc_ref, dgr_ref, dc_sc, dn_sc):
        @pl.when(pl.program_id(0) == 0)
        def _():
            dc_sc[...] = jnp.zeros_like(dc_sc)
            dn_sc[...] = jnp.zeros_like(dn_sc)

        gc, gr, bc, br = _gate_setup(gc_ref, gr_ref, bgc_ref, bgr_ref, reverse)
        tri = _tri_mask(reverse)
        lane_c = lax.broadcasted_iota(jnp.int32, (BLK, LANES), 1)
        row_c = lax.broadcasted_iota(jnp.int32, (BLK, 1), 0)
        row_r = lax.broadcasted_iota(jnp.int32, (N_GATES, BLK), 0)
        db_c = jnp.zeros((BLK, LANES), F32)
        dig_c = jnp.zeros((BLK, LANES), F32)
        db_r = jnp.zeros((N_GATES, BLK), F32)
        dig_r = jnp.zeros((N_GATES, BLK), F32)
        for h in range(M_HEADS):
            cs = slice(h * M_HEAD_DIM, (h + 1) * M_HEAD_DIM)
            ks = slice(M_WIDTH + h * M_HEAD_DIM, M_WIDTH + (h + 1) * M_HEAD_DIM)
            m_in = nm_ref[h, 1:2, 0:1]
            io, fo, m_t, dm, gi, m_new, w_col, dec = _head_gates(gc, gr, bc, br, h, m_in, reverse, tri)
            q = qk_ref[:, cs]
            k = qk_ref[:, ks] * kscale
            v = v_ref[:, cs]
            c_in, n_in = cst_ref[h], nm_ref[h, 0:1, :]
            qb, kb, vb, cb = q.astype(BF16), k.astype(BF16), v.astype(BF16), c_in.astype(BF16)
            s = _dot_nt(qb, kb) * dm
            den_h = den_ref[:, h:h + 1]
            emt = jnp.exp(-m_t)
            rz = 1.0 / jnp.maximum(jnp.abs(den_h), emt)
            dhh = dh_ref[:, cs]
            dnum = dhh * rz
            hdh = jnp.sum(dhh * h_ref[:, cs], axis=1, keepdims=True)
            dden = jnp.where(jnp.abs(den_h) > emt, -hdh * rz * jnp.sign(den_h), 0.0)
            dnb = dnum.astype(BF16)
            ds = _dot_nt(dnb, vb) + dden
            e = ds * s
            dsd = (ds * dm).astype(BF16)
            gd = (gi * dnum).astype(BF16)
            gdd = gi * dden
            dq = _dot(dsd, kb) + _dot(gd, cb) + gdd * n_in
            dk = _dot_tn(dsd, qb)
            dv = _dot_tn(s.astype(BF16), dnb)
            dc_in = _dot_tn(gd, qb)
            dn_in = jnp.sum(gdd * q, axis=0, keepdims=True)
            cq = _dot_nt(qb, cb)
            dg = jnp.sum(dnum * cq, axis=1, keepdims=True) + dden * jnp.sum(q * n_in, axis=1, keepdims=True)
            eg = dg * gi
            dco, dno = dc_sc[h], dn_sc[h]
            dcob = dco.astype(BF16)
            dwv = _dot_nt(kb, dcob)
            dv = dv + w_col * dwv
            dw = jnp.sum(v * dwv, axis=1, keepdims=True) + jnp.sum(k * dno, axis=1, keepdims=True)
            dk = dk + _dot((w_col * v).astype(BF16), dcob) + w_col * dno
            ew = dw * w_col
            ddec = (jnp.sum(jnp.sum(dco * c_in, axis=1, keepdims=True), axis=0, keepdims=True)
                    + jnp.sum(dno * n_in, axis=1, keepdims=True))
            dc_sc[h] = dec * dco + dc_in
            dn_sc[h] = dec * dno + dn_in
            dqk_ref[:, cs] = dq
            dqk_ref[:, ks] = dk * kscale
            dv_ref[:, cs] = dv
            csum = jnp.sum(e, axis=0, keepdims=True)
            db_last = jnp.sum(ew, axis=0, keepdims=True) + ddec * dec
            db_col = jnp.sum(e, axis=1, keepdims=True) + eg - ew + jnp.where(row_c == last, db_last, 0.0)
            db_c = db_c + jnp.where(lane_c == fo, db_col, 0.0)
            dig_c = dig_c + jnp.where(lane_c == io, ew, 0.0)
            db_r = db_r + jnp.where(row_r == fo, -csum, 0.0)
            dig_r = dig_r + jnp.where(row_r == io, csum, 0.0)
        dgc_ref[...] = dig_c + _scan_sum(db_c, 0, not reverse) * _sigmoid(-gc)
        dgr_ref[...] = dig_r + _scan_sum(db_r, 1, not reverse) * _sigmoid(-gr)

    chunk = lambda w, col=0: pl.BlockSpec((BLK, w), lambda c: (cidx(c), col))
    return _pcall(
        body, name=name,
        out_shape=(jax.ShapeDtypeStruct((S, 2 * M_WIDTH), F32), jax.ShapeDtypeStruct((S, M_WIDTH), F32),
                   jax.ShapeDtypeStruct((S, LANES), F32), jax.ShapeDtypeStruct((N_GATES, S), F32)),
        in_specs=[chunk(2 * M_WIDTH), chunk(M_WIDTH, P_VM // M_WIDTH), chunk(LANES, P_G // LANES),
                  pl.BlockSpec((N_GATES, BLK), lambda c: (0, cidx(c))),
                  pl.BlockSpec((1, LANES), lambda c: (0, 0)),
                  pl.BlockSpec((N_GATES, 1), lambda c: (0, 0)),
                  chunk(M_WIDTH), chunk(LANES),
                  pl.BlockSpec((None, M_HEADS, M_HEAD_DIM, M_HEAD_DIM), lambda c: (cidx(c), 0, 0, 0)),
                  pl.BlockSpec((None, M_HEADS, 2, M_HEAD_DIM), lambda c: (cidx(c), 0, 0, 0)),
                  chunk(M_WIDTH)],
        out_specs=(chunk(2 * M_WIDTH), chunk(M_WIDTH), chunk(LANES),
                   pl.BlockSpec((N_GATES, BLK), lambda c: (0, cidx(c)))),
        grid=(nc,),
        scratch=[pltpu.VMEM((M_HEADS, M_HEAD_DIM, M_HEAD_DIM), F32), pltpu.VMEM((M_HEADS, 1, M_HEAD_DIM), F32)],
        sem=("arbitrary",))(qk, proj, proj, gates_r, bg_c, bg_r, hdir, den, cst, nm, dh)


def headnorm_fwd(hf, hb, proj, mnorm, *, name):
    S = hf.shape[0]
    tm = _tile(S, (512, 256, 128))

    def body(hf_ref, hb_ref, om_ref, mn_ref, y_ref):
        for h in range(M_HEADS):
            cs = slice(h * M_HEAD_DIM, (h + 1) * M_HEAD_DIM)
            hm = hf_ref[:, cs] + hb_ref[:, cs]
            r = lax.rsqrt(jnp.mean(hm * hm, axis=-1, keepdims=True) + EPS)
            y_ref[:, cs] = (_sigmoid(om_ref[:, cs]) * ((hm * r) * mn_ref[:, cs])).astype(y_ref.dtype)

    row = pl.BlockSpec((tm, M_WIDTH), lambda i: (i, 0))
    return _pcall(body, name=name, out_shape=jax.ShapeDtypeStruct((S, M_WIDTH), BF16),
                  in_specs=[row, row, pl.BlockSpec((tm, M_WIDTH), lambda i: (i, P_OM // M_WIDTH)),
                            pl.BlockSpec((1, M_WIDTH), lambda i: (0, 0))],
                  out_specs=row, grid=(S // tm,), sem=("parallel",))(hf, hb, proj, mnorm)


def headnorm_bwd(hf, hb, proj, mnorm, dy, *, name):
    S = hf.shape[0]
    tm = _tile(S, (512, 256, 128))

    def body(hf_ref, hb_ref, om_ref, mn_ref, dy_ref, dh_ref, dom_ref, dmn_ref):
        @pl.when(pl.program_id(0) == 0)
        def _():
            dmn_ref[...] = jnp.zeros_like(dmn_ref)

        for h in range(M_HEADS):
            cs = slice(h * M_HEAD_DIM, (h + 1) * M_HEAD_DIM)
            hm = hf_ref[:, cs] + hb_ref[:, cs]
            r = lax.rsqrt(jnp.mean(hm * hm, axis=-1, keepdims=True) + EPS)
            xh = hm * r
            so = _sigmoid(om_ref[:, cs])
            d = dy_ref[:, cs]
            mn = mn_ref[:, cs]
            dom_ref[:, cs] = d * (xh * mn) * (so * (1.0 - so))
            dxm = d * so
            dmn_ref[:, cs] += jnp.sum(dxm * xh, axis=0, keepdims=True)
            dxh = dxm * mn
            dh_ref[:, cs] = r * (dxh - xh * jnp.mean(dxh * xh, axis=-1, keepdims=True))

    row = pl.BlockSpec((tm, M_WIDTH), lambda i: (i, 0))
    vec = pl.BlockSpec((1, M_WIDTH), lambda i: (0, 0))
    return _pcall(body, name=name,
                  out_shape=(jax.ShapeDtypeStruct((S, M_WIDTH), F32), jax.ShapeDtypeStruct((S, M_WIDTH), F32),
                             jax.ShapeDtypeStruct((1, M_WIDTH), F32)),
                  in_specs=[row, row, pl.BlockSpec((tm, M_WIDTH), lambda i: (i, P_OM // M_WIDTH)), vec,
                            pl.BlockSpec((tm, M_WIDTH), lambda i: (i, 1))],
                  out_specs=(row, row, vec), grid=(S // tm,), sem=("arbitrary",))(hf, hb, proj, mnorm, dy)


def _place():
    return lax.axis_index("x"), lax.axis_index("y"), lax.axis_index("c")


def all_gather(shards, *, name):
    T = len(shards)

    def body(*refs):
        x_refs, out_refs = refs[:T], refs[T:2 * T]
        send_sems, recv_sems, local_sems = refs[2 * T:]
        x, y, c = _place()
        me, sibling = (x, y, c), (x, y, 1 - c)
        chips = [(1 - x, y), (x, 1 - y), (1 - x, 1 - y)]

        def copy(t, k, block, to, src=None):
            px, py, pc = block
            dst = out_refs[t].at[4 * px + 2 * py + pc]
            return pltpu.make_async_remote_copy(
                src_ref=dst if src is None else src, dst_ref=dst, send_sem=send_sems.at[7 * t + k],
                recv_sem=recv_sems.at[7 * t + k], device_id=to, device_id_type=MESH)

        mine = [pltpu.make_async_copy(x_refs[t], out_refs[t].at[4 * x + 2 * y + c], local_sems.at[t])
                for t in range(T)]
        for cp in mine:
            cp.start()
        first = []
        for t in range(T):
            first.append(copy(t, 0, me, sibling, src=x_refs[t]))
            first += [copy(t, 1 + j, me, (*chip, c), src=x_refs[t]) for j, chip in enumerate(chips)]
        for cp in first:
            cp.start()
        passed = []
        for j, chip in enumerate(chips):
            for t in range(T):
                copy(t, 1 + j, (*chip, c), me).wait_recv()
                cp = copy(t, 4 + j, (*chip, c), sibling)
                cp.start()
                passed.append(cp)
        for t in range(T):
            copy(t, 0, sibling, me).wait_recv()
            for j, chip in enumerate(chips):
                copy(t, 4 + j, (*chip, 1 - c), me).wait_recv()
        for cp in first + passed:
            cp.wait_send()
        for cp in mine:
            cp.wait()

    out_shape = [jax.ShapeDtypeStruct((N_DEV,) + s.shape, s.dtype) for s in shards]
    return _ccall(body, name=name, out_shape=out_shape, n_in=T,
                  scratch=[pltpu.SemaphoreType.DMA((7 * T,)), pltpu.SemaphoreType.DMA((7 * T,)),
                           pltpu.SemaphoreType.DMA((T,))])(*shards)


def pair_exchange(grads, *, name):
    T = len(grads)

    def body(*refs):
        g_refs, out_refs = refs[:T], refs[T:2 * T]
        send_sems, recv_sems = refs[2 * T:]
        x, y, c = _place()
        cps = [pltpu.make_async_remote_copy(
            src_ref=g_refs[t].at[1 - c], dst_ref=out_refs[t], send_sem=send_sems.at[t], recv_sem=recv_sems.at[t],
            device_id=(x, y, 1 - c), device_id_type=MESH) for t in range(T)]
        for cp in cps:
            cp.start()
        for cp in cps:
            cp.wait()

    out_shape = [jax.ShapeDtypeStruct(g.shape[1:], g.dtype) for g in grads]
    return _ccall(body, name=name, out_shape=out_shape, n_in=T,
                  scratch=[pltpu.SemaphoreType.DMA((T,)), pltpu.SemaphoreType.DMA((T,))])(*grads)


def chip_exchange(parts, *, name):
    T = len(parts)

    def body(*refs):
        p_refs, out_refs = refs[:T], refs[T:2 * T]
        send_sems, recv_sems, local_sems = refs[2 * T:]
        x, y, c = _place()
        mychip = 2 * x + y
        chips = [(1 - x, y), (x, 1 - y), (1 - x, 1 - y)]
        mine = [pltpu.make_async_copy(p_refs[t].at[mychip], out_refs[t].at[mychip], local_sems.at[t])
                for t in range(T)]
        for cp in mine:
            cp.start()
        cps = []
        for t in range(T):
            for j, (px, py) in enumerate(chips):
                cps.append(pltpu.make_async_remote_copy(
                    src_ref=p_refs[t].at[2 * px + py], dst_ref=out_refs[t].at[mychip],
                    send_sem=send_sems.at[3 * t + j], recv_sem=recv_sems.at[3 * t + j],
                    device_id=(px, py, c), device_id_type=MESH))
        for cp in cps:
            cp.start()
        for cp in cps:
            cp.wait()
        for cp in mine:
            cp.wait()

    out_shape = [jax.ShapeDtypeStruct(p.shape, p.dtype) for p in parts]
    return _ccall(body, name=name, out_shape=out_shape, n_in=T,
                  scratch=[pltpu.SemaphoreType.DMA((3 * T,)), pltpu.SemaphoreType.DMA((3 * T,)),
                           pltpu.SemaphoreType.DMA((T,))])(*parts)


def pair_add(g, recv, core, *, name):
    _, nchip, R, C = g.shape
    tr = _tile(R, (512, 256, 128, 64))

    def body(c_ref, a_ref, b_ref, o_ref):
        o_ref[...] = (a_ref[...].astype(F32) + b_ref[...].astype(F32)).astype(o_ref.dtype)

    grid_spec = pltpu.PrefetchScalarGridSpec(
        num_scalar_prefetch=1, grid=(nchip, R // tr),
        in_specs=[pl.BlockSpec((None, None, tr, C), lambda k, i, c_ref: (c_ref[0], k, i, 0)),
                  pl.BlockSpec((None, tr, C), lambda k, i, c_ref: (k, i, 0))],
        out_specs=pl.BlockSpec((None, tr, C), lambda k, i, c_ref: (k, i, 0)))
    return pl.pallas_call(body, name=name, out_shape=jax.ShapeDtypeStruct(recv.shape, recv.dtype),
                          grid_spec=grid_spec,
                          compiler_params=pltpu.CompilerParams(dimension_semantics=("parallel", "parallel")))(
                              core, g, recv)


def _adam_math(w, g, m, v):
    m = ADAM_B1 * m + (1.0 - ADAM_B1) * g
    v = ADAM_B2 * v + (1.0 - ADAM_B2) * (g * g)
    m_hat = m / (1.0 - ADAM_B1 ** ADAM_STEP)
    v_hat = v / (1.0 - ADAM_B2 ** ADAM_STEP)
    delta = -ADAM_LR * (m_hat / (jnp.sqrt(v_hat) + ADAM_EPS) + ADAM_WD * w)
    return delta, m, v


def adam_update(w, parts, m, v, *, name):
    P, R, _ = parts.shape
    tr = _tile(R, (1024, 512, 256, 128, 64, 32, 16, 8))

    def body(w_ref, p_ref, m_ref, v_ref, g_ref, d_ref, nm_ref, nv_ref):
        g = p_ref[0]
        for k in range(1, P):
            g = g + p_ref[k]
        d, nm, nv = _adam_math(w_ref[...], g, m_ref[...], v_ref[...])
        g_ref[...] = g
        d_ref[...] = d
        nm_ref[...] = nm
        nv_ref[...] = nv

    row = pl.BlockSpec((tr, LANES), lambda i: (i, 0))
    shp = jax.ShapeDtypeStruct((R, LANES), F32)
    return _pcall(body, name=name, out_shape=(shp, shp, shp, shp),
                  in_specs=[row, pl.BlockSpec((P, tr, LANES), lambda i: (0, i, 0)), row, row],
                  out_specs=(row, row, row, row), grid=(R // tr,), sem=("parallel",))(w, parts, m, v)


ADAM_STEP_BYTES = 6 * 1024 * 1024


def adam_tensor(w, parts, m, v, *, name):
    L, R, C = w.shape
    per_row = L * C * (7 * 4 + 4 * parts[0].dtype.itemsize)
    tr = R
    for cand in (256, 128, 64, 32, 16):
        if R % cand == 0 and cand * per_row <= ADAM_STEP_BYTES:
            tr = cand
            break

    def body(*refs):
        w_ref, m_ref, v_ref = refs[:3]
        p_refs = refs[3:3 + L]
        g_ref, d_ref, nm_ref, nv_ref = refs[3 + L:]
        for l in range(L):
            g = p_refs[l][0].astype(F32)
            for k in range(1, 4):
                g = g + p_refs[l][k].astype(F32)
            d, nm, nv = _adam_math(w_ref[l], g, m_ref[l], v_ref[l])
            g_ref[l] = g
            d_ref[l] = d
            nm_ref[l] = nm
            nv_ref[l] = nv

    blk = pl.BlockSpec((L, tr, C), lambda i: (0, i, 0))
    pblk = pl.BlockSpec((4, tr, C), lambda i: (0, i, 0))
    shp = jax.ShapeDtypeStruct((L, R, C), F32)
    return _pcall(body, name=name, out_shape=(shp, shp, shp, shp), in_specs=[blk, blk, blk] + [pblk] * L,
                  out_specs=(blk, blk, blk, blk), grid=(R // tr,), sem=("parallel",))(w, m, v, *parts)


def _rows(n_elems):
    r = -(-n_elems // LANES)
    return -(-r // 1024) * 1024 if r > 1024 else -(-r // 16) * 16


def _flat(a, dtype=None):
    n = a.size
    r = _rows(n)
    f = a.reshape(-1)
    if dtype is not None:
        f = f.astype(dtype)
    if r * LANES != n:
        f = jnp.pad(f, (0, r * LANES - n))
    return f.reshape(r, LANES)


def _gathered_cols(g):
    n, rows, cols = g.shape
    return g.transpose(1, 0, 2).reshape(rows, n * cols)


def _owner_cols(dw, dtype):
    rows = dw.shape[0]
    cols = dw.shape[1] // N_DEV
    return dw.reshape(rows, N_DEV // 2, 2, cols).transpose(2, 1, 0, 3).astype(dtype)


_IN_NAT = dict(qa=(0, 1024), ka=(1024, 1280), va=(1280, 1536), qm=(1536, 2560), km=(2560, 3584),
               vm=(3584, 4608), om=(4608, 5632), g=(5632, 5648))


def _permute_w_in(w):
    sl = lambda k: w[:, _IN_NAT[k][0]:_IN_NAT[k][1]]
    pad = jnp.zeros((w.shape[0], P_WIDTH - P_G - N_GATES), w.dtype)
    return jnp.concatenate([sl("qm"), sl("km"), sl("qa"), sl("vm"), sl("om"), sl("ka"), sl("va"), sl("g"), pad],
                           axis=1)


def _unpermute_dw_in(dw):
    qm, km = dw[:, P_QK:P_QK + 1024], dw[:, P_QK + 1024:P_QK + 2048]
    return jnp.concatenate([dw[:, P_QA:P_QA + 1024], dw[:, P_KA:P_KA + 256], dw[:, P_VA:P_VA + 256], qm, km,
                            dw[:, P_VM:P_VM + 1024], dw[:, P_OM:P_OM + 1024], dw[:, P_G:P_G + N_GATES]], axis=1)


BIG = ("ffn1_w_gate", "ffn1_w_up", "ffn1_w_down", "w_in", "w_out", "ffn2_w_gate", "ffn2_w_up", "ffn2_w_down")
SMALL = ("ffn1_norm_pre", "ffn1_norm_post", "mix_norm_pre", "mix_norm_post", "b_gate", "attn_sink", "mlstm_norm",
         "ffn2_norm_pre", "ffn2_norm_post")
WEIGHTS = ("ffn1_norm_pre", "ffn1_norm_post", "ffn1_w_gate", "ffn1_w_up", "ffn1_w_down", "mix_norm_pre",
           "mix_norm_post", "w_in", "b_gate", "conv_w", "attn_sink", "mlstm_norm", "w_out", "ffn2_norm_pre",
           "ffn2_norm_post", "ffn2_w_gate", "ffn2_w_up", "ffn2_w_down")


GRAD_DT = BF16


def _ffn_fwd(x, g_pre, g_post, wg8, wu8, wd8, tag):
    xn = norm_fwd(x, g_pre, name=f"{tag}_pre", out_dtype=BF16)
    hg, hu, act = ffn_gu(xn, wg8, wu8, name=f"{tag}_gu")
    f = ffn_down(act, wd8, name=f"{tag}_down")
    x_new = norm_fwd(f, g_post, name=f"{tag}_post", scale=0.5, resid=x)
    return x_new, (x, xn, hg, hu, act, f)


def _ffn_bwd(dx, saved, g_pre, g_post, wg8, wu8, wd8, tag):
    x, xn, hg, hu, act, f = saved
    df, dg_post = norm_bwd(dx, f, g_post, name=f"{tag}_post_b", scale=0.5, out_dtype=BF16)
    dwd = ffn_dwd(act, df, name=f"{tag}_dwd", out_dtype=GRAD_DT)
    dhg, dhu = ffn_dact(df, wd8, hg, hu, name=f"{tag}_dact")
    dwg, dwu = ffn_dwgu(xn, dhg, dhu, name=f"{tag}_dwgu", out_dtype=GRAD_DT)
    dxn = ffn_dxn(dhg, dhu, wg8, wu8, name=f"{tag}_dxn")
    dx_new, dg_pre = norm_bwd(dxn, x, g_pre, name=f"{tag}_pre_b", resid=dx)
    return dx_new, dg_pre, dg_post, dwg, dwu, dwd


def _mix_fwd(x, g_pre, g_post, w_in_p, b_gate, conv_full, sink, mnorm, w_out, cos2, sin2, tag):
    S = x.shape[0]
    xn = norm_fwd(x, g_pre, name=f"{tag}_pre", out_dtype=BF16)
    proj = mm_nn(xn, w_in_p, name=f"{tag}_in")
    gates_r = proj[:, P_G:P_G + N_GATES].T
    bg_c = jnp.pad(b_gate, (0, LANES - N_GATES)).reshape(1, LANES)
    bg_r = b_gate.reshape(N_GATES, 1)
    y_att, lse = attn_fwd(proj, cos2, sin2, sink, name=f"{tag}_att")
    qk = conv_fwd(proj, conv_full, name=f"{tag}_conv")
    hf, denf, cf, nmf = mlstm_fwd(qk, proj, gates_r, bg_c, bg_r, reverse=False, name=f"{tag}_mf")
    hb, denb, cb, nmb = mlstm_fwd(qk, proj, gates_r, bg_c, bg_r, reverse=True, name=f"{tag}_mb")
    y_m = headnorm_fwd(hf, hb, proj, mnorm.reshape(1, M_WIDTH), name=f"{tag}_hn")
    y = jnp.concatenate([y_att, y_m], axis=1)
    mo = mm_nn(y, w_out, name=f"{tag}_out")
    x_new = norm_fwd(mo, g_post, name=f"{tag}_post", resid=x)
    saved = (x, xn, proj, gates_r, bg_c, bg_r, lse, qk, hf, denf, cf, nmf, hb, denb, cb, nmb, y, mo)
    return x_new, saved


def _mix_bwd(dx, saved, g_pre, g_post, w_in_p, conv_full, sink, mnorm, w_out, cos2, sin2, tag):
    x, xn, proj, gates_r, bg_c, bg_r, lse, qk, hf, denf, cf, nmf, hb, denb, cb, nmb, y, mo = saved
    S = x.shape[0]
    dmo, dg_post = norm_bwd(dx, mo, g_post, name=f"{tag}_post_b", out_dtype=BF16)
    dw_out = mm_tn(y, dmo, name=f"{tag}_dwo", dest_rows=D_MODEL // N_DEV, out_dtype=GRAD_DT)
    dy = mm_nt(dmo, w_out, name=f"{tag}_dy")
    mn = mnorm.reshape(1, M_WIDTH)
    dh, dom, dmn = headnorm_bwd(hf, hb, proj, mn, dy, name=f"{tag}_hn_b")
    dqk_f, dv_f, dgc_f, dgr_f = mlstm_bwd(qk, proj, gates_r, bg_c, bg_r, hf, denf, cf, nmf, dh,
                                           reverse=False, name=f"{tag}_mf_b")
    dqk_b, dv_b, dgc_b, dgr_b = mlstm_bwd(qk, proj, gates_r, bg_c, bg_r, hb, denb, cb, nmb, dh,
                                           reverse=True, name=f"{tag}_mb_b")
    dqk_in, dconv = conv_bwd(proj, conv_full, dqk_f, dqk_b, name=f"{tag}_conv_b")
    dqa, dka, dva, dsink = attn_bwd(proj, y, dy, lse, cos2, sin2, sink, name=f"{tag}_att_b")
    dgates = dgc_f + dgc_b + jnp.pad((dgr_f + dgr_b).T, ((0, 0), (0, LANES - N_GATES)))
    dproj = jnp.concatenate([dqk_in.astype(BF16), dqa.astype(BF16), (dv_f + dv_b).astype(BF16), dom.astype(BF16),
                             dka.astype(BF16), dva.astype(BF16), dgates.astype(BF16),
                             jnp.zeros((S, P_WIDTH - P_G - LANES), BF16)], axis=1)
    db_gate = colsum(dgates, name=f"{tag}_dbg")[0, :N_GATES]
    dw_in = mm_tn(xn, dproj, name=f"{tag}_dwi")
    dxn = mm_nt(dproj, w_in_p, name=f"{tag}_dxn")
    dx_new, dg_pre = norm_bwd(dxn, x, g_pre, name=f"{tag}_pre_b", resid=dx)
    return dx_new, dg_pre, dg_post, dw_in, db_gate, dconv[:CONV_WIDTH], dsink[0, :ATT_HEADS], dmn[0], dw_out


def colsum(a, *, name):
    S, C = a.shape
    tm = _tile(S, (512, 256, 128))

    def body(a_ref, o_ref):
        @pl.when(pl.program_id(0) == 0)
        def _():
            o_ref[...] = jnp.zeros_like(o_ref)

        o_ref[...] += jnp.sum(a_ref[...], axis=0, keepdims=True)

    return _pcall(body, name=name, out_shape=jax.ShapeDtypeStruct((1, C), F32),
                  in_specs=[pl.BlockSpec((tm, C), lambda i: (i, 0))], out_specs=pl.BlockSpec((1, C), lambda i: (0, 0)),
                  grid=(S // tm,), sem=("arbitrary",))(a)


def _layer_weights(gathered):
    wg1, wu1, wd1, win, wout, wg2, wu2, wd2 = gathered
    return dict(ffn1=(wg1, wu1, wd1), ffn2=(wg2, wu2, wd2),
                mix=(_permute_w_in(_gathered_cols(win)), wout.reshape(D_MODEL, D_MODEL)))


def _layer_shards(W, l):
    pad_c = lambda a: jnp.pad(a.astype(BF16), ((0, 0), (0, FSP - FS)))
    pad_r = lambda a: jnp.pad(a.astype(BF16), ((0, FSP - FS), (0, 0)))
    return [pad_c(W["ffn1_w_gate"][l]), pad_c(W["ffn1_w_up"][l]), pad_r(W["ffn1_w_down"][l]),
            W["w_in"][l].astype(BF16), W["w_out"][l].astype(BF16),
            pad_c(W["ffn2_w_gate"][l]), pad_c(W["ffn2_w_up"][l]), pad_r(W["ffn2_w_down"][l])]


def kernel(x, ffn1_norm_pre, ffn1_norm_post, ffn1_w_gate, ffn1_w_up, ffn1_w_down, mix_norm_pre, mix_norm_post, w_in, b_gate, conv_w, attn_sink, mlstm_norm, w_out, ffn2_norm_pre, ffn2_norm_post, ffn2_w_gate, ffn2_w_up, ffn2_w_down, loss_target, m_ffn1_norm_pre, m_ffn1_norm_post, m_ffn1_w_gate, m_ffn1_w_up, m_ffn1_w_down, m_mix_norm_pre, m_mix_norm_post, m_w_in, m_b_gate, m_conv_w, m_attn_sink, m_mlstm_norm, m_w_out, m_ffn2_norm_pre, m_ffn2_norm_post, m_ffn2_w_gate, m_ffn2_w_up, m_ffn2_w_down, v_ffn1_norm_pre, v_ffn1_norm_post, v_ffn1_w_gate, v_ffn1_w_up, v_ffn1_w_down, v_mix_norm_pre, v_mix_norm_post, v_w_in, v_b_gate, v_conv_w, v_attn_sink, v_mlstm_norm, v_w_out, v_ffn2_norm_pre, v_ffn2_norm_post, v_ffn2_w_gate, v_ffn2_w_up, v_ffn2_w_down):
    W = dict(ffn1_norm_pre=ffn1_norm_pre, ffn1_norm_post=ffn1_norm_post, ffn1_w_gate=ffn1_w_gate,
             ffn1_w_up=ffn1_w_up, ffn1_w_down=ffn1_w_down, mix_norm_pre=mix_norm_pre, mix_norm_post=mix_norm_post,
             w_in=w_in, b_gate=b_gate, conv_w=conv_w, attn_sink=attn_sink, mlstm_norm=mlstm_norm, w_out=w_out,
             ffn2_norm_pre=ffn2_norm_pre, ffn2_norm_post=ffn2_norm_post, ffn2_w_gate=ffn2_w_gate,
             ffn2_w_up=ffn2_w_up, ffn2_w_down=ffn2_w_down)
    M1 = dict(ffn1_norm_pre=m_ffn1_norm_pre, ffn1_norm_post=m_ffn1_norm_post, ffn1_w_gate=m_ffn1_w_gate,
              ffn1_w_up=m_ffn1_w_up, ffn1_w_down=m_ffn1_w_down, mix_norm_pre=m_mix_norm_pre,
              mix_norm_post=m_mix_norm_post, w_in=m_w_in, b_gate=m_b_gate, conv_w=m_conv_w, attn_sink=m_attn_sink,
              mlstm_norm=m_mlstm_norm, w_out=m_w_out, ffn2_norm_pre=m_ffn2_norm_pre,
              ffn2_norm_post=m_ffn2_norm_post, ffn2_w_gate=m_ffn2_w_gate, ffn2_w_up=m_ffn2_w_up,
              ffn2_w_down=m_ffn2_w_down)
    V2 = dict(ffn1_norm_pre=v_ffn1_norm_pre, ffn1_norm_post=v_ffn1_norm_post, ffn1_w_gate=v_ffn1_w_gate,
              ffn1_w_up=v_ffn1_w_up, ffn1_w_down=v_ffn1_w_down, mix_norm_pre=v_mix_norm_pre,
              mix_norm_post=v_mix_norm_post, w_in=v_w_in, b_gate=v_b_gate, conv_w=v_conv_w, attn_sink=v_attn_sink,
              mlstm_norm=v_mlstm_norm, w_out=v_w_out, ffn2_norm_pre=v_ffn2_norm_pre,
              ffn2_norm_post=v_ffn2_norm_post, ffn2_w_gate=v_ffn2_w_gate, ffn2_w_up=v_ffn2_w_up,
              ffn2_w_down=v_ffn2_w_down)
    depth = w_in.shape[0]
    S = x.shape[1]
    xs = x[0]
    cos2, sin2 = _rope_tables(S)
    core = lax.axis_index("c").astype(jnp.int32).reshape(1)

    cs = conv_w.shape[2]
    conv_g = all_gather([conv_w.reshape(depth * CONV_WIDTH, cs)], name="ag_conv")[0]
    conv_all = conv_g.reshape(N_DEV, depth, CONV_WIDTH, cs).transpose(1, 2, 0, 3)
    conv_all = conv_all.reshape(depth, CONV_WIDTH, N_DEV * cs)
    conv_all = jnp.pad(conv_all, ((0, 0), (0, CONV_HALO - CONV_WIDTH), (0, 0)))

    lw, saved = [], []
    for l in range(depth):
        wl = _layer_weights(all_gather(_layer_shards(W, l), name=f"ag_l{l}"))
        lw.append(wl)
        xs, s1 = _ffn_fwd(xs, W["ffn1_norm_pre"][l], W["ffn1_norm_post"][l], *wl["ffn1"], f"l{l}_f1")
        xs, s2 = _mix_fwd(xs, W["mix_norm_pre"][l], W["mix_norm_post"][l], wl["mix"][0], W["b_gate"][l],
                          conv_all[l], W["attn_sink"][l], W["mlstm_norm"][l], wl["mix"][1], cos2, sin2, f"l{l}_mx")
        xs, s3 = _ffn_fwd(xs, W["ffn2_norm_pre"][l], W["ffn2_norm_post"][l], *wl["ffn2"], f"l{l}_f2")
        saved.append((s1, s2, s3))

    dx, loss_part = loss_fwd_bwd(xs, loss_target[0], name="loss")

    names = BIG + ("conv_w",)
    parts = {n: [None] * depth for n in names}
    small_parts = [None] * depth
    for l in reversed(range(depth)):
        wl = lw[l]
        s1, s2, s3 = saved[l]
        dx, dpre2, dpost2, dwg2, dwu2, dwd2 = _ffn_bwd(dx, s3, W["ffn2_norm_pre"][l], W["ffn2_norm_post"][l],
                                                       *wl["ffn2"], f"l{l}_f2")
        dx, dpre_m, dpost_m, dw_in_p, db_gate, dconv, dsink, dmn, dw_out = _mix_bwd(
            dx, s2, W["mix_norm_pre"][l], W["mix_norm_post"][l], wl["mix"][0], conv_all[l], W["attn_sink"][l],
            W["mlstm_norm"][l], wl["mix"][1], cos2, sin2, f"l{l}_mx")
        dx, dpre1, dpost1, dwg1, dwu1, dwd1 = _ffn_bwd(dx, s1, W["ffn1_norm_pre"][l], W["ffn1_norm_post"][l],
                                                       *wl["ffn1"], f"l{l}_f1")
        small_parts[l] = dict(ffn1_norm_pre=dpre1[0], ffn1_norm_post=dpost1[0], mix_norm_pre=dpre_m[0],
                              mix_norm_post=dpost_m[0], b_gate=db_gate, attn_sink=dsink, mlstm_norm=dmn,
                              ffn2_norm_pre=dpre2[0], ffn2_norm_post=dpost2[0])
        grads = [dwg1, dwu1, dwd1, _owner_cols(_unpermute_dw_in(dw_in_p), GRAD_DT), dw_out, dwg2, dwu2, dwd2,
                 _owner_cols(dconv, F32)]
        recv = pair_exchange(grads, name=f"rs1_l{l}")
        halves = [pair_add(grads[t], recv[t], core, name=f"rs_add_l{l}_{t}") for t in range(len(names))]
        got = chip_exchange(halves, name=f"rs2_l{l}")
        for t, n in enumerate(names):
            parts[n][l] = got[t]

    outs = {k: {} for k in ("g", "d", "m", "v")}
    for n in names:
        res = adam_tensor(W[n], parts[n], M1[n], V2[n], name=f"adam_{n}")
        for k, r in zip(("g", "d", "m", "v"), res):
            outs[k][n] = r
    small_out = {k: {n: [None] * depth for n in SMALL} for k in ("g", "d", "m", "v")}

    vec = jnp.concatenate([small_parts[l][n].reshape(-1) for l in range(depth) for n in SMALL]
                          + [loss_part.reshape(-1)])
    n_small = vec.shape[0]
    gathered_small = all_gather([_flat(vec)], name="ag_small")[0]
    wvec = _flat(jnp.concatenate([W[n][l].reshape(-1) for l in range(depth) for n in SMALL] + [jnp.zeros((1,), F32)]))
    mvec = _flat(jnp.concatenate([M1[n][l].reshape(-1) for l in range(depth) for n in SMALL] + [jnp.zeros((1,), F32)]))
    vvec = _flat(jnp.concatenate([V2[n][l].reshape(-1) for l in range(depth) for n in SMALL] + [jnp.ones((1,), F32)]))
    res = adam_update(wvec, gathered_small, mvec, vvec, name="adam_small")
    res = [r.reshape(-1)[:n_small] for r in res]
    off = 0
    for l in range(depth):
        for n in SMALL:
            sz = W[n].shape[1]
            for k, r in zip(("g", "d", "m", "v"), res):
                small_out[k][n][l] = r[off:off + sz]
            off += sz
    loss = res[0][off]
    for k in outs:
        for n in SMALL:
            outs[k][n] = jnp.stack(small_out[k][n], axis=0)

    return (loss, dx[None], *[outs["g"][n] for n in WEIGHTS], *[outs["d"][n] for n in WEIGHTS],
            *[outs["m"][n] for n in WEIGHTS], *[outs["v"][n] for n in WEIGHTS])
```

```python
import jax
import jax.numpy as jnp
from jax import lax
from jax.experimental import pallas as pl
from jax.experimental.pallas import tpu as pltpu

F32 = jnp.float32
BF16 = jnp.bfloat16

D_MODEL = 2048
D_FF = 5632
ATT_HEADS = 8
ATT_KV_HEADS = 2
ATT_GROUP = ATT_HEADS // ATT_KV_HEADS
ATT_WIDTH = 1024
HEAD_DIM = 128
KV_WIDTH = 256
WINDOW = 128
BLK = 128
M_WIDTH = 1024
M_HEADS = 4
M_HEAD_DIM = 256
CONV_WIDTH = 5
EPS = 1e-6
ROPE_THETA = 10000.0
IN_WIDTH = 5648
N_GATES = 16
N_DEV = 8

ADAM_LR = 0.001
ADAM_B1 = 0.9
ADAM_B2 = 0.999
ADAM_EPS = 1e-08
ADAM_WD = 0.01
ADAM_STEP = 10

P_QK = 0
P_QA = 2048
P_VM = 3072
P_OM = 4096
P_KA = 5120
P_VA = 5376
P_G = 5632
P_WIDTH = 6144

LANES = 128
V7X_VMEM_LIMIT = 48 * 1024 * 1024
NEG = -1e30
MESH = pl.DeviceIdType.MESH
ANY = pl.BlockSpec(memory_space=pl.ANY)


def _tile(n, cands=(1024, 512, 256, 128)):
    for c in cands:
        if n % c == 0:
            return c
    return n


class Comm:
    def __init__(self, ins, outs, sems, phases):
        self.ins, self.outs, self.sems, self.phases = list(ins), list(outs), list(sems), list(phases)


def run_comm(comm, *, name):
    n_in, n_out = len(comm.ins), len(comm.outs)

    def body(*refs):
        ins, outs, sems = refs[:n_in], refs[n_in:n_in + n_out], refs[n_in + n_out:]
        for phase in comm.phases:
            phase(ins, outs, sems)

    return pl.pallas_call(body, name=name, out_shape=comm.outs, in_specs=[ANY] * n_in, out_specs=[ANY] * n_out,
                          scratch_shapes=comm.sems,
                          compiler_params=pltpu.CompilerParams(has_side_effects=True))(*comm.ins)


def _pcall(body, *, name, out_shape, in_specs, out_specs, grid=(), scratch=(), sem=None, comm=None):
    if comm is None:
        return pl.pallas_call(
            body, name=name, out_shape=out_shape, in_specs=in_specs, out_specs=out_specs, grid=grid,
            scratch_shapes=list(scratch),
            compiler_params=pltpu.CompilerParams(dimension_semantics=sem, vmem_limit_bytes=V7X_VMEM_LIMIT))
    multi = isinstance(out_shape, (tuple, list))
    outs = list(out_shape) if multi else [out_shape]
    ospecs = list(out_specs) if multi else [out_specs]
    n_in, n_out, n_scr = len(in_specs), len(outs), len(scratch)
    nci, nco = len(comm.ins), len(comm.outs)
    steps = 1
    for g in grid:
        steps *= g
    n_ph = len(comm.phases)
    at = [0, steps - 1] if n_ph == 2 else [0, (3 * steps) // 4, steps - 1]
    assert steps >= n_ph and at == sorted(set(at))

    def wrapped(*refs):
        ins, cins = refs[:n_in], refs[n_in:n_in + nci]
        o0 = n_in + nci
        res, couts = refs[o0:o0 + n_out], refs[o0 + n_out:o0 + n_out + nco]
        s0 = o0 + n_out + nco
        scr, csems = refs[s0:s0 + n_scr], refs[s0 + n_scr:]
        lin = 0
        for k, g in enumerate(grid):
            lin = lin * g + pl.program_id(k)

        @pl.when(lin == at[0])
        def _():
            comm.phases[0](cins, couts, csems)

        body(*ins, *res, *scr)
        for p in range(1, n_ph):
            @pl.when(lin == at[p])
            def _(p=p):
                comm.phases[p](cins, couts, csems)

    call = pl.pallas_call(
        wrapped, name=name, out_shape=outs + comm.outs, in_specs=list(in_specs) + [ANY] * nci,
        out_specs=ospecs + [ANY] * nco, grid=grid, scratch_shapes=list(scratch) + comm.sems,
        compiler_params=pltpu.CompilerParams(dimension_semantics=("arbitrary",) * len(grid),
                                             vmem_limit_bytes=V7X_VMEM_LIMIT, has_side_effects=True))

    def run(*args):
        got = list(call(*args, *comm.ins))
        return (tuple(got[:n_out]) if multi else got[0]), got[n_out:]

    return run


def _dot(a, b):
    return jnp.dot(a, b, preferred_element_type=F32)


def _dot_nt(a, b):
    return lax.dot_general(a, b, (((1,), (1,)), ((), ())), preferred_element_type=F32)


def _dot_tn(a, b):
    return lax.dot_general(a, b, (((0,), (0,)), ((), ())), preferred_element_type=F32)


def _sigmoid(x):
    return 1.0 / (1.0 + jnp.exp(-x))


def mm_nn(a, b, *, name, out_dtype=F32, comm=None):
    M, K = a.shape
    N = b.shape[1]
    tm, tk, tn = _tile(M), _tile(K), _tile(N)
    nk = K // tk

    def body(a_ref, b_ref, o_ref, acc):
        k = pl.program_id(2)

        @pl.when(k == 0)
        def _():
            acc[...] = jnp.zeros_like(acc)

        acc[...] += _dot(a_ref[...], b_ref[...])

        @pl.when(k == nk - 1)
        def _():
            o_ref[...] = acc[...].astype(o_ref.dtype)

    return _pcall(body, name=name, out_shape=jax.ShapeDtypeStruct((M, N), out_dtype),
                  in_specs=[pl.BlockSpec((tm, tk), lambda i, j, k: (i, k)),
                            pl.BlockSpec((tk, tn), lambda i, j, k: (k, j))],
                  out_specs=pl.BlockSpec((tm, tn), lambda i, j, k: (i, j)), grid=(M // tm, N // tn, nk),
                  scratch=[pltpu.VMEM((tm, tn), F32)], sem=("parallel", "parallel", "arbitrary"), comm=comm)(a, b)


def mm_tn(a, g, *, name, owner_rows=None, out_dtype=F32, comm=None):
    M, K = a.shape
    N = g.shape[1]
    tm, tk, tn = _tile(M), _tile(K), _tile(N)
    nm = M // tm
    per_tile = 1 if owner_rows is None else tk // owner_rows

    def body(a_ref, g_ref, o_ref, acc):
        m = pl.program_id(2)

        @pl.when(m == 0)
        def _():
            acc[...] = jnp.zeros_like(acc)

        acc[...] += _dot_tn(a_ref[...], g_ref[...])

        @pl.when(m == nm - 1)
        def _():
            if owner_rows is None:
                o_ref[...] = acc[...].astype(o_ref.dtype)
            else:
                for d in range(per_tile):
                    o_ref[d % 2, d // 2] = acc[d * owner_rows:(d + 1) * owner_rows, :].astype(o_ref.dtype)

    if owner_rows is None:
        out_shape = jax.ShapeDtypeStruct((K, N), out_dtype)
        out_spec = pl.BlockSpec((tk, tn), lambda i, j, m: (i, j))
    else:
        assert per_tile % 2 == 0 and K == N_DEV * owner_rows
        out_shape = jax.ShapeDtypeStruct((2, N_DEV // 2, owner_rows, N), out_dtype)
        out_spec = pl.BlockSpec((2, per_tile // 2, owner_rows, tn), lambda i, j, m: (0, i, 0, j))
    return _pcall(body, name=name, out_shape=out_shape,
                  in_specs=[pl.BlockSpec((tm, tk), lambda i, j, m: (m, i)),
                            pl.BlockSpec((tm, tn), lambda i, j, m: (m, j))],
                  out_specs=out_spec, grid=(K // tk, N // tn, nm), scratch=[pltpu.VMEM((tk, tn), F32)],
                  sem=("parallel", "parallel", "arbitrary"), comm=comm)(a, g)


def mm_nt(a, b, *, name, out_dtype=F32):
    M, K = a.shape
    N = b.shape[0]
    tm, tn, tk = _tile(M), _tile(N), _tile(K)
    nk = K // tk

    def body(a_ref, b_ref, o_ref, acc):
        k = pl.program_id(2)

        @pl.when(k == 0)
        def _():
            acc[...] = jnp.zeros_like(acc)

        acc[...] += _dot_nt(a_ref[...], b_ref[...])

        @pl.when(k == nk - 1)
        def _():
            o_ref[...] = acc[...].astype(o_ref.dtype)

    return _pcall(body, name=name, out_shape=jax.ShapeDtypeStruct((M, N), out_dtype),
                  in_specs=[pl.BlockSpec((tm, tk), lambda i, j, k: (i, k)),
                            pl.BlockSpec((tn, tk), lambda i, j, k: (j, k))],
                  out_specs=pl.BlockSpec((tm, tn), lambda i, j, k: (i, j)), grid=(M // tm, N // tn, nk),
                  scratch=[pltpu.VMEM((tm, tn), F32)], sem=("parallel", "parallel", "arbitrary"))(a, b)


FS = D_FF // N_DEV
FSP = 768


def ffn_gu(xn, wg8, wu8, *, name, comm=None):
    S, D = xn.shape
    tm = _tile(S)

    def body(x_ref, wg_ref, wu_ref, hg_ref, hu_ref, act_ref):
        xv = x_ref[...]
        hg = _dot(xv, wg_ref[...])
        hu = _dot(xv, wu_ref[...])
        hg_ref[...] = hg.astype(BF16)
        hu_ref[...] = hu.astype(BF16)
        act_ref[...] = (hg * _sigmoid(hg) * hu).astype(BF16)

    wspec = pl.BlockSpec((None, D, FSP), lambda i, j: (j, 0, 0))
    ospec = pl.BlockSpec((None, tm, FSP), lambda i, j: (j, i, 0))
    shp = jax.ShapeDtypeStruct((N_DEV, S, FSP), BF16)
    return _pcall(body, name=name, out_shape=(shp, shp, shp),
                  in_specs=[pl.BlockSpec((tm, D), lambda i, j: (i, 0)), wspec, wspec],
                  out_specs=(ospec, ospec, ospec), grid=(S // tm, N_DEV), sem=("parallel", "arbitrary"),
                  comm=comm)(xn, wg8, wu8)


def ffn_down(act8, wd8, *, name):
    _, S, _ = act8.shape
    D = wd8.shape[2]
    tm, tn = _tile(S), _tile(D)

    def body(a_ref, w_ref, o_ref):
        @pl.when(pl.program_id(2) == 0)
        def _():
            o_ref[...] = jnp.zeros_like(o_ref)

        o_ref[...] += _dot(a_ref[...], w_ref[...])

    return _pcall(body, name=name, out_shape=jax.ShapeDtypeStruct((S, D), F32),
                  in_specs=[pl.BlockSpec((None, tm, FSP), lambda i, n, j: (j, i, 0)),
                            pl.BlockSpec((None, FSP, tn), lambda i, n, j: (j, 0, n))],
                  out_specs=pl.BlockSpec((tm, tn), lambda i, n, j: (i, n)),
                  grid=(S // tm, D // tn, N_DEV), sem=("parallel", "parallel", "arbitrary"))(act8, wd8)


def ffn_dact(df, wd8, hg8, hu8, *, name):
    S, D = df.shape
    tm = _tile(S)

    def body(d_ref, w_ref, hg_ref, hu_ref, dg_ref, du_ref):
        da = _dot_nt(d_ref[...], w_ref[...])
        hg = hg_ref[...].astype(F32)
        hu = hu_ref[...].astype(F32)
        sg = _sigmoid(hg)
        dg_ref[...] = (da * hu * (sg * (1.0 + hg * (1.0 - sg)))).astype(BF16)
        du_ref[...] = (da * hg * sg).astype(BF16)

    blk = pl.BlockSpec((None, tm, FSP), lambda i, j: (j, i, 0))
    shp = jax.ShapeDtypeStruct((N_DEV, S, FSP), BF16)
    return _pcall(body, name=name, out_shape=(shp, shp),
                  in_specs=[pl.BlockSpec((tm, D), lambda i, j: (i, 0)),
                            pl.BlockSpec((None, FSP, D), lambda i, j: (j, 0, 0)), blk, blk],
                  out_specs=(blk, blk), grid=(S // tm, N_DEV), sem=("parallel", "arbitrary"))(df, wd8, hg8, hu8)


def ffn_dwd(act8, df, *, name, out_dtype, comm=None):
    _, S, _ = act8.shape
    D = df.shape[1]
    tm, tn = _tile(S), _tile(D)
    nm = S // tm

    def body(a_ref, d_ref, o_ref, acc):
        m = pl.program_id(2)

        @pl.when(m == 0)
        def _():
            acc[...] = jnp.zeros_like(acc)

        acc[...] += _dot_tn(a_ref[...], d_ref[...])

        @pl.when(m == nm - 1)
        def _():
            o_ref[...] = acc[0:FS, :].astype(o_ref.dtype)

    return _pcall(body, name=name, out_shape=jax.ShapeDtypeStruct((2, N_DEV // 2, FS, D), out_dtype),
                  in_specs=[pl.BlockSpec((None, tm, FSP), lambda j, n, m: (j, m, 0)),
                            pl.BlockSpec((tm, tn), lambda j, n, m: (m, n))],
                  out_specs=pl.BlockSpec((None, None, FS, tn), lambda j, n, m: (j % 2, j // 2, 0, n)),
                  grid=(N_DEV, D // tn, nm), scratch=[pltpu.VMEM((FSP, tn), F32)],
                  sem=("parallel", "parallel", "arbitrary"), comm=comm)(act8, df)


def ffn_dwgu(xn, dg8, du8, *, name, out_dtype, comm=None):
    S, D = xn.shape
    tm, tk = _tile(S), _tile(D)
    nm = S // tm

    def body(x_ref, dg_ref, du_ref, og_ref, ou_ref, accg, accu):
        m = pl.program_id(2)

        @pl.when(m == 0)
        def _():
            accg[...] = jnp.zeros_like(accg)
            accu[...] = jnp.zeros_like(accu)

        xv = x_ref[...]
        accg[...] += _dot_tn(xv, dg_ref[...])
        accu[...] += _dot_tn(xv, du_ref[...])

        @pl.when(m == nm - 1)
        def _():
            og_ref[...] = accg[:, 0:FS].astype(og_ref.dtype)
            ou_ref[...] = accu[:, 0:FS].astype(ou_ref.dtype)

    blk = pl.BlockSpec((None, tm, FSP), lambda j, k, m: (j, m, 0))
    ospec = pl.BlockSpec((None, None, tk, FS), lambda j, k, m: (j % 2, j // 2, k, 0))
    shp = jax.ShapeDtypeStruct((2, N_DEV // 2, D, FS), out_dtype)
    return _pcall(body, name=name, out_shape=(shp, shp),
                  in_specs=[pl.BlockSpec((tm, tk), lambda j, k, m: (m, k)), blk, blk],
                  out_specs=(ospec, ospec), grid=(N_DEV, D // tk, nm),
                  scratch=[pltpu.VMEM((tk, FSP), F32), pltpu.VMEM((tk, FSP), F32)],
                  sem=("parallel", "parallel", "arbitrary"), comm=comm)(xn, dg8, du8)


def ffn_dxn(dg8, du8, wg8, wu8, *, name):
    _, S, _ = dg8.shape
    D = wg8.shape[1]
    tm, tn = _tile(S), _tile(D)

    def body(dg_ref, du_ref, wg_ref, wu_ref, o_ref):
        @pl.when(pl.program_id(2) == 0)
        def _():
            o_ref[...] = jnp.zeros_like(o_ref)

        o_ref[...] += _dot_nt(dg_ref[...], wg_ref[...]) + _dot_nt(du_ref[...], wu_ref[...])

    blk = pl.BlockSpec((None, tm, FSP), lambda i, n, j: (j, i, 0))
    wspec = pl.BlockSpec((None, tn, FSP), lambda i, n, j: (j, n, 0))
    return _pcall(body, name=name, out_shape=jax.ShapeDtypeStruct((S, D), F32),
                  in_specs=[blk, blk, wspec, wspec], out_specs=pl.BlockSpec((tm, tn), lambda i, n, j: (i, n)),
                  grid=(S // tm, D // tn, N_DEV), sem=("parallel", "parallel", "arbitrary"))(dg8, du8, wg8, wu8)


def norm_fwd(x, g, *, name, scale=1.0, resid=None, out_dtype=F32):
    S, D = x.shape
    tm = _tile(S, (512, 256, 128))

    def body(*refs):
        if resid is None:
            x_ref, g_ref, o_ref = refs
        else:
            x_ref, g_ref, r_ref, o_ref = refs
        xv = x_ref[...].astype(F32)
        r = lax.rsqrt(jnp.mean(xv * xv, axis=-1, keepdims=True) + EPS)
        y = (xv * r) * g_ref[...]
        if scale != 1.0:
            y = y * scale
        if resid is not None:
            y = y + r_ref[...]
        o_ref[...] = y.astype(o_ref.dtype)

    row = pl.BlockSpec((tm, D), lambda i: (i, 0))
    in_specs = [row, pl.BlockSpec((1, D), lambda i: (0, 0))]
    args = [x, g.reshape(1, D)]
    if resid is not None:
        in_specs.append(row)
        args.append(resid)
    return _pcall(body, name=name, out_shape=jax.ShapeDtypeStruct((S, D), out_dtype), in_specs=in_specs,
                  out_specs=row, grid=(S // tm,), sem=("parallel",))(*args)


def norm_bwd(dy, x, g, *, name, scale=1.0, resid=None, out_dtype=F32):
    S, D = x.shape
    tm = _tile(S, (512, 256, 128))

    def body(*refs):
        if resid is None:
            dy_ref, x_ref, g_ref, dx_ref, dg_ref = refs
        else:
            dy_ref, x_ref, g_ref, r_ref, dx_ref, dg_ref = refs

        @pl.when(pl.program_id(0) == 0)
        def _():
            dg_ref[...] = jnp.zeros_like(dg_ref)

        xv = x_ref[...].astype(F32)
        d = dy_ref[...].astype(F32)
        if scale != 1.0:
            d = d * scale
        r = lax.rsqrt(jnp.mean(xv * xv, axis=-1, keepdims=True) + EPS)
        xh = xv * r
        dg_ref[...] += jnp.sum(d * xh, axis=0, keepdims=True)
        dxh = d * g_ref[...]
        dx = r * (dxh - xh * jnp.mean(dxh * xh, axis=-1, keepdims=True))
        if resid is not None:
            dx = dx + r_ref[...]
        dx_ref[...] = dx.astype(dx_ref.dtype)

    row = pl.BlockSpec((tm, D), lambda i: (i, 0))
    vec = pl.BlockSpec((1, D), lambda i: (0, 0))
    in_specs = [row, row, vec]
    args = [dy, x, g.reshape(1, D)]
    if resid is not None:
        in_specs.append(row)
        args.append(resid)
    return _pcall(body, name=name,
                  out_shape=(jax.ShapeDtypeStruct((S, D), out_dtype), jax.ShapeDtypeStruct((1, D), F32)),
                  in_specs=in_specs, out_specs=(row, vec), grid=(S // tm,), sem=("arbitrary",))(*args)


def loss_fwd_bwd(y, target, *, name):
    S, D = y.shape
    tm = _tile(S, (512, 256, 128))

    def body(y_ref, t_ref, dy_ref, l_ref):
        @pl.when(pl.program_id(0) == 0)
        def _():
            l_ref[...] = jnp.zeros_like(l_ref)

        e = y_ref[...] - t_ref[...]
        dy_ref[...] = e * (1.0 / D)
        l_ref[...] += jnp.sum(jnp.sum(e * e, axis=1, keepdims=True), axis=0, keepdims=True) * (0.5 / D)

    row = pl.BlockSpec((tm, D), lambda i: (i, 0))
    one = pl.BlockSpec((1, 1), lambda i: (0, 0))
    return _pcall(body, name=name,
                  out_shape=(jax.ShapeDtypeStruct((S, D), F32), jax.ShapeDtypeStruct((1, 1), F32)),
                  in_specs=[row, row], out_specs=(row, one), grid=(S // tm,), sem=("arbitrary",))(y, target)


def _rope_tables(S):
    half = HEAD_DIM // 2
    inv_freq = ROPE_THETA ** (-jnp.arange(half, dtype=F32) / half)
    ang = jnp.arange(S, dtype=F32)[:, None] * inv_freq[None, :]
    cos, sin = jnp.cos(ang), jnp.sin(ang)
    return jnp.concatenate([cos, cos], axis=1), jnp.concatenate([-sin, sin], axis=1)


def _rope(x, cos2, sin2):
    return x * cos2 + pltpu.roll(x, HEAD_DIM // 2, 1) * sin2


def _unrope(d, cos2, sin2):
    return d * cos2 + pltpu.roll(d * sin2, HEAD_DIM // 2, 1)


def _nbr_specs(width, col, nb):
    return [pl.BlockSpec((BLK, width), lambda n, c=col: (jnp.maximum(n - 1, 0), c)),
            pl.BlockSpec((BLK, width), lambda n, c=col: (n, c)),
            pl.BlockSpec((BLK, width), lambda n, c=col: (jnp.minimum(n + 1, nb - 1), c))]


def attn_fwd(proj, cos2, sin2, sink, *, name):
    S = proj.shape[0]
    nb = S // BLK
    scale = HEAD_DIM ** -0.5

    def body(sink_ref, q_ref, k0, k1, k2, v0, v1, v2, c0, c1, c2, s0, s1, s2, o_ref, lse_ref):
        n = pl.program_id(0)
        cosk = jnp.concatenate([c0[...], c1[...], c2[...]], axis=0)
        sink_ = jnp.concatenate([s0[...], s1[...], s2[...]], axis=0)
        kall = jnp.concatenate([k0[...], k1[...], k2[...]], axis=0)
        vall = jnp.concatenate([v0[...], v1[...], v2[...]], axis=0)
        rows = lax.broadcasted_iota(jnp.int32, (BLK, 3 * BLK), 0)
        cols = lax.broadcasted_iota(jnp.int32, (BLK, 3 * BLK), 1)
        kpos = (n - 1) * BLK + cols
        valid = (jnp.abs(cols - BLK - rows) <= WINDOW) & (kpos >= 0) & (kpos < S)
        valid = jnp.concatenate([valid] * ATT_GROUP, axis=0)
        lane = lax.broadcasted_iota(jnp.int32, (BLK, LANES), 1)
        lse_tile = jnp.zeros((BLK, LANES), F32)
        for hk in range(ATT_KV_HEADS):
            ks = slice(hk * HEAD_DIM, (hk + 1) * HEAD_DIM)
            kh = _rope(kall[:, ks], cosk, sink_).astype(BF16)
            vh = vall[:, ks].astype(BF16)
            qs = []
            for g in range(ATT_GROUP):
                hq = hk * ATT_GROUP + g
                qs.append(_rope(q_ref[:, hq * HEAD_DIM:(hq + 1) * HEAD_DIM], c1[...], s1[...]))
            qh = jnp.concatenate(qs, axis=0).astype(BF16)
            s = _dot_nt(qh, kh) * scale
            s = jnp.where(valid, s, NEG)
            snk = jnp.concatenate(
                [jnp.full((BLK, 1), sink_ref[hk * ATT_GROUP + g], F32) for g in range(ATT_GROUP)], axis=0)
            m = jnp.maximum(jnp.max(s, axis=1, keepdims=True), snk)
            p = jnp.exp(s - m)
            l = jnp.sum(p, axis=1, keepdims=True) + jnp.exp(snk - m)
            o = _dot(p.astype(BF16), vh) * (1.0 / l)
            lse = m + jnp.log(l)
            for g in range(ATT_GROUP):
                hq = hk * ATT_GROUP + g
                o_ref[:, hq * HEAD_DIM:(hq + 1) * HEAD_DIM] = o[g * BLK:(g + 1) * BLK].astype(o_ref.dtype)
                lse_tile = lse_tile + jnp.where(lane == hq, lse[g * BLK:(g + 1) * BLK], 0.0)
        lse_ref[...] = lse_tile

    in_specs = ([pl.BlockSpec(memory_space=pltpu.SMEM),
                 pl.BlockSpec((BLK, ATT_WIDTH), lambda n: (n, P_QA // ATT_WIDTH))]
                + _nbr_specs(KV_WIDTH, P_KA // KV_WIDTH, nb) + _nbr_specs(KV_WIDTH, P_VA // KV_WIDTH, nb)
                + _nbr_specs(HEAD_DIM, 0, nb) + _nbr_specs(HEAD_DIM, 0, nb))
    return _pcall(body, name=name,
                  out_shape=(jax.ShapeDtypeStruct((S, ATT_WIDTH), BF16), jax.ShapeDtypeStruct((S, LANES), F32)),
                  in_specs=in_specs,
                  out_specs=(pl.BlockSpec((BLK, ATT_WIDTH), lambda n: (n, 0)),
                             pl.BlockSpec((BLK, LANES), lambda n: (n, 0))),
                  grid=(nb,), sem=("parallel",))(sink, proj, proj, proj, proj, proj, proj, proj,
                                                 cos2, cos2, cos2, sin2, sin2, sin2)


def attn_bwd(proj, y, dy, lse, cos2, sin2, sink, *, name):
    S = proj.shape[0]
    nb = S // BLK
    scale = HEAD_DIM ** -0.5

    def body(sink_ref, q0, q1, q2, k0, k1, k2, v0, v1, v2, o0, o1, o2, d0, d1, d2, l0, l1, l2,
             c0, c1, c2, s0, s1, s2, dq_ref, dk_ref, dv_ref, dsink_ref):
        n = pl.program_id(0)

        @pl.when(n == 0)
        def _():
            dsink_ref[...] = jnp.zeros_like(dsink_ref)

        q_nb, o_nb, d_nb, l_nb = (q0, q1, q2), (o0, o1, o2), (d0, d1, d2), (l0, l1, l2)
        c_nb, s_nb = (c0, c1, c2), (s0, s1, s2)
        cosk = jnp.concatenate([c0[...], c1[...], c2[...]], axis=0)
        sink_ = jnp.concatenate([s0[...], s1[...], s2[...]], axis=0)
        kall = jnp.concatenate([k0[...], k1[...], k2[...]], axis=0)
        vall = jnp.concatenate([v0[...], v1[...], v2[...]], axis=0)
        lane = lax.broadcasted_iota(jnp.int32, (1, LANES), 1)
        rows = lax.broadcasted_iota(jnp.int32, (BLK, 3 * BLK), 0)
        cols = lax.broadcasted_iota(jnp.int32, (BLK, 3 * BLK), 1)
        kpos = (n - 1) * BLK + cols
        valid_q = (jnp.abs(cols - BLK - rows) <= WINDOW) & (kpos >= 0) & (kpos < S)
        valid_q = jnp.concatenate([valid_q] * ATT_GROUP, axis=0)
        qr = lax.broadcasted_iota(jnp.int32, (3 * BLK, BLK), 0)
        kc = lax.broadcasted_iota(jnp.int32, (3 * BLK, BLK), 1)
        qpos = (n - 1) * BLK + qr
        valid_k = (jnp.abs(qr - BLK - kc) <= WINDOW) & (qpos >= 0) & (qpos < S)
        valid_k = jnp.concatenate([valid_k] * ATT_GROUP, axis=0)
        dsink_acc = jnp.zeros((1, LANES), F32)

        def head_cols(ref, hq):
            return ref[:, hq * HEAD_DIM:(hq + 1) * HEAD_DIM]

        for hk in range(ATT_KV_HEADS):
            ks = slice(hk * HEAD_DIM, (hk + 1) * HEAD_DIM)
            kh = _rope(kall[:, ks], cosk, sink_).astype(BF16)
            vh = vall[:, ks].astype(BF16)
            qs, dos, lses, deltas = [], [], [], []
            for g in range(ATT_GROUP):
                hq = hk * ATT_GROUP + g
                qs.append(_rope(head_cols(q1, hq), c1[...], s1[...]))
                do = head_cols(d1, hq)
                dos.append(do)
                lses.append(l1[:, hq:hq + 1])
                deltas.append(jnp.sum(do * head_cols(o1, hq).astype(F32), axis=1, keepdims=True))
            qh = jnp.concatenate(qs, axis=0).astype(BF16)
            doh = jnp.concatenate(dos, axis=0).astype(BF16)
            lseh = jnp.concatenate(lses, axis=0)
            delh = jnp.concatenate(deltas, axis=0)
            s = jnp.where(valid_q, _dot_nt(qh, kh) * scale, NEG)
            p = jnp.exp(s - lseh)
            dp = _dot_nt(doh, vh)
            ds = (p * (dp - delh)).astype(BF16)
            dq = _dot(ds, kh) * scale
            for g in range(ATT_GROUP):
                hq = hk * ATT_GROUP + g
                dq_ref[:, hq * HEAD_DIM:(hq + 1) * HEAD_DIM] = _unrope(dq[g * BLK:(g + 1) * BLK], c1[...], s1[...])
                psink = jnp.exp(sink_ref[hq] - lses[g])
                dsink_acc = dsink_acc + jnp.where(lane == hq, -jnp.sum(psink * deltas[g]), 0.0)
            kown = _rope(k1[:, ks], c1[...], s1[...]).astype(BF16)
            vown = v1[:, ks].astype(BF16)
            qs, dos, lses, deltas = [], [], [], []
            for g in range(ATT_GROUP):
                hq = hk * ATT_GROUP + g
                for j in range(3):
                    qs.append(_rope(head_cols(q_nb[j], hq), c_nb[j][...], s_nb[j][...]))
                    do = head_cols(d_nb[j], hq)
                    dos.append(do)
                    lses.append(l_nb[j][:, hq:hq + 1])
                    deltas.append(jnp.sum(do * head_cols(o_nb[j], hq).astype(F32), axis=1, keepdims=True))
            qh = jnp.concatenate(qs, axis=0).astype(BF16)
            doh = jnp.concatenate(dos, axis=0).astype(BF16)
            lseh = jnp.concatenate(lses, axis=0)
            delh = jnp.concatenate(deltas, axis=0)
            s = jnp.where(valid_k, _dot_nt(qh, kown) * scale, NEG)
            p = jnp.where(valid_k, jnp.exp(s - lseh), 0.0)
            dv_ref[:, ks] = _dot_tn(p.astype(BF16), doh)
            dp = _dot_nt(doh, vown)
            ds = (p * (dp - delh)).astype(BF16)
            dk_ref[:, ks] = _unrope(_dot_tn(ds, qh) * scale, c1[...], s1[...])
        dsink_ref[...] += dsink_acc

    in_specs = ([pl.BlockSpec(memory_space=pltpu.SMEM)]
                + _nbr_specs(ATT_WIDTH, P_QA // ATT_WIDTH, nb)
                + _nbr_specs(KV_WIDTH, P_KA // KV_WIDTH, nb) + _nbr_specs(KV_WIDTH, P_VA // KV_WIDTH, nb)
                + _nbr_specs(ATT_WIDTH, 0, nb) + _nbr_specs(ATT_WIDTH, 0, nb) + _nbr_specs(LANES, 0, nb)
                + _nbr_specs(HEAD_DIM, 0, nb) + _nbr_specs(HEAD_DIM, 0, nb))
    args = [sink] + [proj] * 9 + [y] * 3 + [dy] * 3 + [lse] * 3 + [cos2] * 3 + [sin2] * 3
    return _pcall(body, name=name,
                  out_shape=(jax.ShapeDtypeStruct((S, ATT_WIDTH), F32), jax.ShapeDtypeStruct((S, KV_WIDTH), F32),
                             jax.ShapeDtypeStruct((S, KV_WIDTH), F32), jax.ShapeDtypeStruct((1, LANES), F32)),
                  in_specs=in_specs,
                  out_specs=(pl.BlockSpec((BLK, ATT_WIDTH), lambda n: (n, 0)),
                             pl.BlockSpec((BLK, KV_WIDTH), lambda n: (n, 0)),
                             pl.BlockSpec((BLK, KV_WIDTH), lambda n: (n, 0)),
                             pl.BlockSpec((1, LANES), lambda n: (0, 0))),
                  grid=(nb,), sem=("arbitrary",))(*args)


CONV_HALO = 8
CONV_COLS = 512


def _halo_specs(tm, nrow, col_of):
    hb = tm // CONV_HALO
    return [pl.BlockSpec((CONV_HALO, CONV_COLS), lambda i, j: (jnp.maximum(i * hb - 1, 0), col_of(j))),
            pl.BlockSpec((tm, CONV_COLS), lambda i, j: (i, col_of(j))),
            pl.BlockSpec((CONV_HALO, CONV_COLS),
                         lambda i, j: (jnp.minimum((i + 1) * hb, nrow * hb - 1), col_of(j)))]


def _with_halo(prev, cur, nxt, i, nrow):
    p = jnp.where(i > 0, prev[...], 0.0)
    q = jnp.where(i < nrow - 1, nxt[...], 0.0)
    return jnp.concatenate([p, cur[...], q], axis=0)


def _conv_taps(xt, w_ref, tm):
    n = xt.shape[0]
    acc = jnp.zeros_like(xt)
    for j in range(CONV_WIDTH):
        sh = (CONV_WIDTH // 2 - j) % n
        xs = xt if sh == 0 else pltpu.roll(xt, sh, 0)
        acc = acc + xs * w_ref[j:j + 1, :]
    return acc


def conv_fwd(proj, conv_w, *, name):
    S = proj.shape[0]
    tm = _tile(S, (512, 256, 128))
    nrow = S // tm

    def body(xp, xc, xn, w_ref, o_ref):
        i = pl.program_id(0)
        xt = _with_halo(xp, xc, xn, i, nrow)
        pre = _conv_taps(xt, w_ref, tm)[CONV_HALO:CONV_HALO + tm]
        o_ref[...] = pre * _sigmoid(pre)

    return _pcall(body, name=name, out_shape=jax.ShapeDtypeStruct((S, 2 * M_WIDTH), F32),
                  in_specs=_halo_specs(tm, nrow, lambda j: P_QK // CONV_COLS + j)
                  + [pl.BlockSpec((CONV_HALO, CONV_COLS), lambda i, j: (0, j))],
                  out_specs=pl.BlockSpec((tm, CONV_COLS), lambda i, j: (i, j)),
                  grid=(nrow, 2 * M_WIDTH // CONV_COLS), sem=("parallel", "parallel"))(proj, proj, proj, conv_w)


def conv_bwd(proj, conv_w, da, db, *, name):
    S = proj.shape[0]
    tm = _tile(S, (512, 256, 128))
    nrow = S // tm

    def body(xp, xc, xn, ap, ac, an, bp, bc, bn, w_ref, dx_ref, dw_ref):
        i = pl.program_id(1)

        @pl.when(i == 0)
        def _():
            dw_ref[...] = jnp.zeros_like(dw_ref)

        xt = _with_halo(xp, xc, xn, i, nrow)
        dt = _with_halo(ap, ac, an, i, nrow) + _with_halo(bp, bc, bn, i, nrow)
        pre = _conv_taps(xt, w_ref, tm)
        sg = _sigmoid(pre)
        dpre = dt * (sg * (1.0 + pre * (1.0 - sg)))
        n = xt.shape[0]
        ridx = lax.broadcasted_iota(jnp.int32, (n, 1), 0)
        dpre = jnp.where((ridx >= 2) & (ridx < n - 2), dpre, 0.0)
        dx = jnp.zeros_like(xt)
        own = (ridx >= CONV_HALO) & (ridx < CONV_HALO + tm)
        dpre_own = jnp.where(own, dpre, 0.0)
        dw_rows = []
        for j in range(CONV_WIDTH):
            sh = (j - CONV_WIDTH // 2) % n
            ds_ = dpre if sh == 0 else pltpu.roll(dpre, sh, 0)
            dx = dx + ds_ * w_ref[j:j + 1, :]
            shx = (CONV_WIDTH // 2 - j) % n
            xs = xt if shx == 0 else pltpu.roll(xt, shx, 0)
            dw_rows.append(jnp.sum(dpre_own * xs, axis=0, keepdims=True))
        dx_ref[...] = dx[CONV_HALO:CONV_HALO + tm]
        dw_rows.append(jnp.zeros((CONV_HALO - CONV_WIDTH, CONV_COLS), F32))
        dw_ref[...] += jnp.concatenate(dw_rows, axis=0)

    colq = lambda j: P_QK // CONV_COLS + j
    same = lambda j: j

    def swap(specs):
        return [pl.BlockSpec(s.block_shape, (lambda f: (lambda j, i: f(i, j)))(s.index_map)) for s in specs]

    in_specs = swap(_halo_specs(tm, nrow, colq) + _halo_specs(tm, nrow, same) + _halo_specs(tm, nrow, same)
                    + [pl.BlockSpec((CONV_HALO, CONV_COLS), lambda i, j: (0, j))])
    return _pcall(body, name=name,
                  out_shape=(jax.ShapeDtypeStruct((S, 2 * M_WIDTH), F32),
                             jax.ShapeDtypeStruct((CONV_HALO, 2 * M_WIDTH), F32)),
                  in_specs=in_specs,
                  out_specs=(pl.BlockSpec((tm, CONV_COLS), lambda j, i: (i, j)),
                             pl.BlockSpec((CONV_HALO, CONV_COLS), lambda j, i: (0, j))),
                  grid=(2 * M_WIDTH // CONV_COLS, nrow), sem=("parallel", "arbitrary"))(
                      proj, proj, proj, da, da, da, db, db, db, conv_w)


def _log_sigmoid(x):
    return jnp.minimum(x, 0.0) - jnp.log(1.0 + jnp.exp(-jnp.abs(x)))


def _scan_sum(x, axis, from_end):
    idx = lax.broadcasted_iota(jnp.int32, x.shape, axis)
    n = x.shape[axis]
    sh = 1
    while sh < n:
        if from_end:
            x = x + jnp.where(idx < n - sh, pltpu.roll(x, n - sh, axis), 0.0)
        else:
            x = x + jnp.where(idx >= sh, pltpu.roll(x, sh, axis), 0.0)
        sh *= 2
    return x


def _gate_setup(gc_ref, gr_ref, bgc_ref, bgr_ref, reverse):
    gc = gc_ref[...] + bgc_ref[...]
    gr = gr_ref[...] + bgr_ref[...]
    bc = _scan_sum(_log_sigmoid(gc), 0, reverse)
    br = _scan_sum(_log_sigmoid(gr), 1, reverse)
    return gc, gr, bc, br


def _head_gates(gc, gr, bc, br, h, m_in, reverse, tri):
    io = (M_HEADS if reverse else 0) + h
    fo = (3 * M_HEADS if reverse else 2 * M_HEADS) + h
    last = 0 if reverse else BLK - 1
    b_col, b_row = bc[:, fo:fo + 1], br[fo:fo + 1, :]
    ig_col, ig_row = gc[:, io:io + 1], gr[io:io + 1, :]
    logd = jnp.where(tri, b_col - b_row + ig_row, NEG)
    m_t = jnp.maximum(b_col + m_in, jnp.max(logd, axis=1, keepdims=True))
    dm = jnp.exp(logd - m_t)
    gi = jnp.exp(b_col + m_in - m_t)
    b_last = b_row[:, last:last + 1]
    logw = b_last - b_row + ig_row
    m_new = jnp.maximum(b_last + m_in, jnp.max(logw, axis=1, keepdims=True))
    w_col = jnp.exp(b_last - b_col + ig_col - m_new)
    dec = jnp.exp(b_last + m_in - m_new)
    return io, fo, m_t, dm, gi, m_new, w_col, dec


def _tri_mask(reverse):
    rows = lax.broadcasted_iota(jnp.int32, (BLK, BLK), 0)
    cols = lax.broadcasted_iota(jnp.int32, (BLK, BLK), 1)
    return (cols >= rows) if reverse else (cols <= rows)


def mlstm_fwd(qk, proj, gates_r, bg_c, bg_r, *, reverse, name):
    S = qk.shape[0]
    nc = S // BLK
    kscale = M_HEAD_DIM ** -0.5
    cidx = (lambda c: nc - 1 - c) if reverse else (lambda c: c)

    def body(qk_ref, v_ref, gc_ref, gr_ref, bgc_ref, bgr_ref, h_ref, den_ref, cst_ref, nm_ref, c_sc, n_sc, m_sc):
        @pl.when(pl.program_id(0) == 0)
        def _():
            c_sc[...] = jnp.zeros_like(c_sc)
            n_sc[...] = jnp.zeros_like(n_sc)
            m_sc[...] = jnp.zeros_like(m_sc)

        gc, gr, bc, br = _gate_setup(gc_ref, gr_ref, bgc_ref, bgr_ref, reverse)
        tri = _tri_mask(reverse)
        lane = lax.broadcasted_iota(jnp.int32, (BLK, LANES), 1)
        den_tile = jnp.zeros((BLK, LANES), F32)
        for h in range(M_HEADS):
            cs = slice(h * M_HEAD_DIM, (h + 1) * M_HEAD_DIM)
            m_in = m_sc[h][:, 0:1]
            _, _, m_t, dm, gi, m_new, w_col, dec = _head_gates(gc, gr, bc, br, h, m_in, reverse, tri)
            q = qk_ref[:, cs]
            k = qk_ref[:, M_WIDTH + h * M_HEAD_DIM:M_WIDTH + (h + 1) * M_HEAD_DIM] * kscale
            v = v_ref[:, cs]
            c_in, n_in = c_sc[h], n_sc[h]
            cst_ref[h] = c_in
            nm_ref[h, 0:1, :] = n_in
            nm_ref[h, 1:2, :] = m_sc[h]
            qb, kb, vb = q.astype(BF16), k.astype(BF16), v.astype(BF16)
            s = _dot_nt(qb, kb) * dm
            num = _dot(s.astype(BF16), vb) + gi * _dot_nt(qb, c_in.astype(BF16))
            den = jnp.sum(s, axis=1, keepdims=True) + gi * jnp.sum(q * n_in, axis=1, keepdims=True)
            z = jnp.maximum(jnp.abs(den), jnp.exp(-m_t))
            h_ref[:, cs] = num * (1.0 / z)
            den_tile = den_tile + jnp.where(lane == h, den, 0.0)
            c_sc[h] = dec * c_in + _dot_tn((w_col * v).astype(BF16), kb)
            n_sc[h] = dec * n_in + jnp.sum(w_col * k, axis=0, keepdims=True)
            m_sc[h] = jnp.broadcast_to(m_new, (1, M_HEAD_DIM))
        den_ref[...] = den_tile

    return _pcall(
        body, name=name,
        out_shape=(jax.ShapeDtypeStruct((S, M_WIDTH), F32), jax.ShapeDtypeStruct((S, LANES), F32),
                   jax.ShapeDtypeStruct((nc, M_HEADS, M_HEAD_DIM, M_HEAD_DIM), F32),
                   jax.ShapeDtypeStruct((nc, M_HEADS, 2, M_HEAD_DIM), F32)),
        in_specs=[pl.BlockSpec((BLK, 2 * M_WIDTH), lambda c: (cidx(c), 0)),
                  pl.BlockSpec((BLK, M_WIDTH), lambda c: (cidx(c), P_VM // M_WIDTH)),
                  pl.BlockSpec((BLK, LANES), lambda c: (cidx(c), P_G // LANES)),
                  pl.BlockSpec((N_GATES, BLK), lambda c: (0, cidx(c))),
                  pl.BlockSpec((1, LANES), lambda c: (0, 0)),
                  pl.BlockSpec((N_GATES, 1), lambda c: (0, 0))],
        out_specs=(pl.BlockSpec((BLK, M_WIDTH), lambda c: (cidx(c), 0)),
                   pl.BlockSpec((BLK, LANES), lambda c: (cidx(c), 0)),
                   pl.BlockSpec((None, M_HEADS, M_HEAD_DIM, M_HEAD_DIM), lambda c: (cidx(c), 0, 0, 0)),
                   pl.BlockSpec((None, M_HEADS, 2, M_HEAD_DIM), lambda c: (cidx(c), 0, 0, 0))),
        grid=(nc,),
        scratch=[pltpu.VMEM((M_HEADS, M_HEAD_DIM, M_HEAD_DIM), F32), pltpu.VMEM((M_HEADS, 1, M_HEAD_DIM), F32),
                 pltpu.VMEM((M_HEADS, 1, M_HEAD_DIM), F32)],
        sem=("arbitrary",))(qk, proj, proj, gates_r, bg_c, bg_r)


def mlstm_bwd(qk, proj, gates_r, bg_c, bg_r, hdir, den, cst, nm, dh, *, reverse, name, comm=None):
    S = qk.shape[0]
    nc = S // BLK
    kscale = M_HEAD_DIM ** -0.5
    cidx = (lambda c: c) if reverse else (lambda c: nc - 1 - c)
    last = 0 if reverse else BLK - 1

    def body(qk_ref, v_ref, gc_ref, gr_ref, bgc_ref, bgr_ref, h_ref, den_ref, cst_ref, nm_ref, dh_ref,
             dqk_ref, dv_ref, dgc_ref, dgr_ref, dc_sc, dn_sc):
        @pl.when(pl.program_id(0) == 0)
        def _():
            dc_sc[...] = jnp.zeros_like(dc_sc)
            dn_sc[...] = jnp.zeros_like(dn_sc)

        gc, gr, bc, br = _gate_setup(gc_ref, gr_ref, bgc_ref, bgr_ref, reverse)
        tri = _tri_mask(reverse)
        lane_c = lax.broadcasted_iota(jnp.int32, (BLK, LANES), 1)
        row_c = lax.broadcasted_iota(jnp.int32, (BLK, 1), 0)
        row_r = lax.broadcasted_iota(jnp.int32, (N_GATES, BLK), 0)
        db_c = jnp.zeros((BLK, LANES), F32)
        dig_c = jnp.zeros((BLK, LANES), F32)
        db_r = jnp.zeros((N_GATES, BLK), F32)
        dig_r = jnp.zeros((N_GATES, BLK), F32)
        for h in range(M_HEADS):
            cs = slice(h * M_HEAD_DIM, (h + 1) * M_HEAD_DIM)
            ks = slice(M_WIDTH + h * M_HEAD_DIM, M_WIDTH + (h + 1) * M_HEAD_DIM)
            m_in = nm_ref[h, 1:2, 0:1]
            io, fo, m_t, dm, gi, m_new, w_col, dec = _head_gates(gc, gr, bc, br, h, m_in, reverse, tri)
            q = qk_ref[:, cs]
            k = qk_ref[:, ks] * kscale
            v = v_ref[:, cs]
            c_in, n_in = cst_ref[h], nm_ref[h, 0:1, :]
            qb, kb, vb, cb = q.astype(BF16), k.astype(BF16), v.astype(BF16), c_in.astype(BF16)
            s = _dot_nt(qb, kb) * dm
            den_h = den_ref[:, h:h + 1]
            emt = jnp.exp(-m_t)
            rz = 1.0 / jnp.maximum(jnp.abs(den_h), emt)
            dhh = dh_ref[:, cs]
            dnum = dhh * rz
            hdh = jnp.sum(dhh * h_ref[:, cs], axis=1, keepdims=True)
            dden = jnp.where(jnp.abs(den_h) > emt, -hdh * rz * jnp.sign(den_h), 0.0)
            dnb = dnum.astype(BF16)
            ds = _dot_nt(dnb, vb) + dden
            e = ds * s
            dsd = (ds * dm).astype(BF16)
            gd = (gi * dnum).astype(BF16)
            gdd = gi * dden
            dq = _dot(dsd, kb) + _dot(gd, cb) + gdd * n_in
            dk = _dot_tn(dsd, qb)
            dv = _dot_tn(s.astype(BF16), dnb)
            dc_in = _dot_tn(gd, qb)
            dn_in = jnp.sum(gdd * q, axis=0, keepdims=True)
            cq = _dot_nt(qb, cb)
            dg = jnp.sum(dnum * cq, axis=1, keepdims=True) + dden * jnp.sum(q * n_in, axis=1, keepdims=True)
            eg = dg * gi
            dco, dno = dc_sc[h], dn_sc[h]
            dcob = dco.astype(BF16)
            dwv = _dot_nt(kb, dcob)
            dv = dv + w_col * dwv
            dw = jnp.sum(v * dwv, axis=1, keepdims=True) + jnp.sum(k * dno, axis=1, keepdims=True)
            dk = dk + _dot((w_col * v).astype(BF16), dcob) + w_col * dno
            ew = dw * w_col
            ddec = (jnp.sum(jnp.sum(dco * c_in, axis=1, keepdims=True), axis=0, keepdims=True)
                    + jnp.sum(dno * n_in, axis=1, keepdims=True))
            dc_sc[h] = dec * dco + dc_in
            dn_sc[h] = dec * dno + dn_in
            dqk_ref[:, cs] = dq
            dqk_ref[:, ks] = dk * kscale
            dv_ref[:, cs] = dv
            csum = jnp.sum(e, axis=0, keepdims=True)
            db_last = jnp.sum(ew, axis=0, keepdims=True) + ddec * dec
            db_col = jnp.sum(e, axis=1, keepdims=True) + eg - ew + jnp.where(row_c == last, db_last, 0.0)
            db_c = db_c + jnp.where(lane_c == fo, db_col, 0.0)
            dig_c = dig_c + jnp.where(lane_c == io, ew, 0.0)
            db_r = db_r + jnp.where(row_r == fo, -csum, 0.0)
            dig_r = dig_r + jnp.where(row_r == io, csum, 0.0)
        dgc_ref[...] = dig_c + _scan_sum(db_c, 0, not reverse) * _sigmoid(-gc)
        dgr_ref[...] = dig_r + _scan_sum(db_r, 1, not reverse) * _sigmoid(-gr)

    chunk = lambda w, col=0: pl.BlockSpec((BLK, w), lambda c: (cidx(c), col))
    return _pcall(
        body, name=name,
        out_shape=(jax.ShapeDtypeStruct((S, 2 * M_WIDTH), F32), jax.ShapeDtypeStruct((S, M_WIDTH), F32),
                   jax.ShapeDtypeStruct((S, LANES), F32), jax.ShapeDtypeStruct((N_GATES, S), F32)),
        in_specs=[chunk(2 * M_WIDTH), chunk(M_WIDTH, P_VM // M_WIDTH), chunk(LANES, P_G // LANES),
                  pl.BlockSpec((N_GATES, BLK), lambda c: (0, cidx(c))),
                  pl.BlockSpec((1, LANES), lambda c: (0, 0)),
                  pl.BlockSpec((N_GATES, 1), lambda c: (0, 0)),
                  chunk(M_WIDTH), chunk(LANES),
                  pl.BlockSpec((None, M_HEADS, M_HEAD_DIM, M_HEAD_DIM), lambda c: (cidx(c), 0, 0, 0)),
                  pl.BlockSpec((None, M_HEADS, 2, M_HEAD_DIM), lambda c: (cidx(c), 0, 0, 0)),
                  chunk(M_WIDTH)],
        out_specs=(chunk(2 * M_WIDTH), chunk(M_WIDTH), chunk(LANES),
                   pl.BlockSpec((N_GATES, BLK), lambda c: (0, cidx(c)))),
        grid=(nc,),
        scratch=[pltpu.VMEM((M_HEADS, M_HEAD_DIM, M_HEAD_DIM), F32), pltpu.VMEM((M_HEADS, 1, M_HEAD_DIM), F32)],
        sem=("arbitrary",), comm=comm)(qk, proj, proj, gates_r, bg_c, bg_r, hdir, den, cst, nm, dh)


def headnorm_fwd(hf, hb, proj, mnorm, *, name):
    S = hf.shape[0]
    tm = _tile(S, (512, 256, 128))

    def body(hf_ref, hb_ref, om_ref, mn_ref, y_ref):
        for h in range(M_HEADS):
            cs = slice(h * M_HEAD_DIM, (h + 1) * M_HEAD_DIM)
            hm = hf_ref[:, cs] + hb_ref[:, cs]
            r = lax.rsqrt(jnp.mean(hm * hm, axis=-1, keepdims=True) + EPS)
            y_ref[:, cs] = (_sigmoid(om_ref[:, cs]) * ((hm * r) * mn_ref[:, cs])).astype(y_ref.dtype)

    row = pl.BlockSpec((tm, M_WIDTH), lambda i: (i, 0))
    return _pcall(body, name=name, out_shape=jax.ShapeDtypeStruct((S, M_WIDTH), BF16),
                  in_specs=[row, row, pl.BlockSpec((tm, M_WIDTH), lambda i: (i, P_OM // M_WIDTH)),
                            pl.BlockSpec((1, M_WIDTH), lambda i: (0, 0))],
                  out_specs=row, grid=(S // tm,), sem=("parallel",))(hf, hb, proj, mnorm)


def headnorm_bwd(hf, hb, proj, mnorm, dy, *, name):
    S = hf.shape[0]
    tm = _tile(S, (512, 256, 128))

    def body(hf_ref, hb_ref, om_ref, mn_ref, dy_ref, dh_ref, dom_ref, dmn_ref):
        @pl.when(pl.program_id(0) == 0)
        def _():
            dmn_ref[...] = jnp.zeros_like(dmn_ref)

        for h in range(M_HEADS):
            cs = slice(h * M_HEAD_DIM, (h + 1) * M_HEAD_DIM)
            hm = hf_ref[:, cs] + hb_ref[:, cs]
            r = lax.rsqrt(jnp.mean(hm * hm, axis=-1, keepdims=True) + EPS)
            xh = hm * r
            so = _sigmoid(om_ref[:, cs])
            d = dy_ref[:, cs]
            mn = mn_ref[:, cs]
            dom_ref[:, cs] = d * (xh * mn) * (so * (1.0 - so))
            dxm = d * so
            dmn_ref[:, cs] += jnp.sum(dxm * xh, axis=0, keepdims=True)
            dxh = dxm * mn
            dh_ref[:, cs] = r * (dxh - xh * jnp.mean(dxh * xh, axis=-1, keepdims=True))

    row = pl.BlockSpec((tm, M_WIDTH), lambda i: (i, 0))
    vec = pl.BlockSpec((1, M_WIDTH), lambda i: (0, 0))
    return _pcall(body, name=name,
                  out_shape=(jax.ShapeDtypeStruct((S, M_WIDTH), F32), jax.ShapeDtypeStruct((S, M_WIDTH), F32),
                             jax.ShapeDtypeStruct((1, M_WIDTH), F32)),
                  in_specs=[row, row, pl.BlockSpec((tm, M_WIDTH), lambda i: (i, P_OM // M_WIDTH)), vec,
                            pl.BlockSpec((tm, M_WIDTH), lambda i: (i, 1))],
                  out_specs=(row, row, vec), grid=(S // tm,), sem=("arbitrary",))(hf, hb, proj, mnorm, dy)


def _place():
    return lax.axis_index("x"), lax.axis_index("y"), lax.axis_index("c")


def _ag_plan(x_refs, out_refs, sems):
    send_sems, recv_sems, local_sems = sems
    T = len(x_refs)
    x, y, c = _place()
    me, sibling = (x, y, c), (x, y, 1 - c)
    chips = [(1 - x, y), (x, 1 - y), (1 - x, 1 - y)]

    def copy(t, k, block, to, src=None):
        px, py, pc = block
        dst = out_refs[t].at[4 * px + 2 * py + pc]
        return pltpu.make_async_remote_copy(
            src_ref=dst if src is None else src, dst_ref=dst, send_sem=send_sems.at[7 * t + k],
            recv_sem=recv_sems.at[7 * t + k], device_id=to, device_id_type=MESH)

    mine = [pltpu.make_async_copy(x_refs[t], out_refs[t].at[4 * x + 2 * y + c], local_sems.at[t]) for t in range(T)]
    first = []
    for t in range(T):
        first.append(copy(t, 0, me, sibling, src=x_refs[t]))
        first += [copy(t, 1 + j, me, (*chip, c), src=x_refs[t]) for j, chip in enumerate(chips)]
    landed = [copy(t, 1 + j, (*chip, c), me) for j, chip in enumerate(chips) for t in range(T)]
    passed = [copy(t, 4 + j, (*chip, c), sibling) for j, chip in enumerate(chips) for t in range(T)]
    from_sibling = [copy(t, 0, sibling, me) for t in range(T)]
    from_sibling += [copy(t, 4 + j, (*chip, 1 - c), me) for j, chip in enumerate(chips) for t in range(T)]
    return mine, first, landed, passed, from_sibling


def _ag_start(x_refs, out_refs, sems):
    mine, first, _, _, _ = _ag_plan(x_refs, out_refs, sems)
    for cp in mine + first:
        cp.start()


def _ag_forward(x_refs, out_refs, sems):
    _, _, landed, passed, _ = _ag_plan(x_refs, out_refs, sems)
    for got, on in zip(landed, passed):
        got.wait_recv()
        on.start()


def _ag_finish(x_refs, out_refs, sems):
    mine, first, _, passed, from_sibling = _ag_plan(x_refs, out_refs, sems)
    for cp in from_sibling:
        cp.wait_recv()
    for cp in first + passed:
        cp.wait_send()
    for cp in mine:
        cp.wait()


def ag_comm(shards):
    T = len(shards)
    return Comm(shards, [jax.ShapeDtypeStruct((N_DEV,) + s.shape, s.dtype) for s in shards],
                [pltpu.SemaphoreType.DMA((7 * T,)), pltpu.SemaphoreType.DMA((7 * T,)), pltpu.SemaphoreType.DMA((T,))],
                [_ag_start, _ag_forward, _ag_finish])


def _pair_plan(g_refs, out_refs, sems):
    send_sems, recv_sems = sems
    x, y, c = _place()
    return [pltpu.make_async_remote_copy(
        src_ref=g_refs[t].at[1 - c], dst_ref=out_refs[t], send_sem=send_sems.at[t], recv_sem=recv_sems.at[t],
        device_id=(x, y, 1 - c), device_id_type=MESH) for t in range(len(g_refs))]


def _pair_start(g_refs, out_refs, sems):
    for cp in _pair_plan(g_refs, out_refs, sems):
        cp.start()


def _pair_finish(g_refs, out_refs, sems):
    for cp in _pair_plan(g_refs, out_refs, sems):
        cp.wait()


def pair_comm(grads):
    T = len(grads)
    return Comm(grads, [jax.ShapeDtypeStruct(g.shape[1:], g.dtype) for g in grads],
                [pltpu.SemaphoreType.DMA((T,)), pltpu.SemaphoreType.DMA((T,))], [_pair_start, _pair_finish])


def _chip_plan(p_refs, out_refs, sems):
    send_sems, recv_sems, local_sems = sems
    T = len(p_refs)
    x, y, c = _place()
    mychip = 2 * x + y
    chips = [(1 - x, y), (x, 1 - y), (1 - x, 1 - y)]
    mine = [pltpu.make_async_copy(p_refs[t].at[mychip], out_refs[t].at[mychip], local_sems.at[t]) for t in range(T)]
    cps = [pltpu.make_async_remote_copy(
        src_ref=p_refs[t].at[2 * px + py], dst_ref=out_refs[t].at[mychip], send_sem=send_sems.at[3 * t + j],
        recv_sem=recv_sems.at[3 * t + j], device_id=(px, py, c), device_id_type=MESH)
        for t in range(T) for j, (px, py) in enumerate(chips)]
    return mine, cps


def _chip_start(p_refs, out_refs, sems):
    mine, cps = _chip_plan(p_refs, out_refs, sems)
    for cp in mine + cps:
        cp.start()


def _chip_finish(p_refs, out_refs, sems):
    mine, cps = _chip_plan(p_refs, out_refs, sems)
    for cp in cps + mine:
        cp.wait()


def chip_comm(parts):
    T = len(parts)
    return Comm(parts, [jax.ShapeDtypeStruct(p.shape, p.dtype) for p in parts],
                [pltpu.SemaphoreType.DMA((3 * T,)), pltpu.SemaphoreType.DMA((3 * T,)), pltpu.SemaphoreType.DMA((T,))],
                [_chip_start, _chip_finish])


def pair_add(g, recv, core, *, name):
    _, nchip, R, C = g.shape
    tr = _tile(R, (512, 256, 128, 64))

    def body(c_ref, a_ref, b_ref, o_ref):
        o_ref[...] = (a_ref[...].astype(F32) + b_ref[...].astype(F32)).astype(o_ref.dtype)

    grid_spec = pltpu.PrefetchScalarGridSpec(
        num_scalar_prefetch=1, grid=(nchip, R // tr),
        in_specs=[pl.BlockSpec((None, None, tr, C), lambda k, i, c_ref: (c_ref[0], k, i, 0)),
                  pl.BlockSpec((None, tr, C), lambda k, i, c_ref: (k, i, 0))],
        out_specs=pl.BlockSpec((None, tr, C), lambda k, i, c_ref: (k, i, 0)))
    return pl.pallas_call(body, name=name, out_shape=jax.ShapeDtypeStruct(recv.shape, recv.dtype),
                          grid_spec=grid_spec,
                          compiler_params=pltpu.CompilerParams(dimension_semantics=("parallel", "parallel")))(
                              core, g, recv)


def _adam_math(w, g, m, v):
    m = ADAM_B1 * m + (1.0 - ADAM_B1) * g
    v = ADAM_B2 * v + (1.0 - ADAM_B2) * (g * g)
    m_hat = m / (1.0 - ADAM_B1 ** ADAM_STEP)
    v_hat = v / (1.0 - ADAM_B2 ** ADAM_STEP)
    delta = -ADAM_LR * (m_hat / (jnp.sqrt(v_hat) + ADAM_EPS) + ADAM_WD * w)
    return delta, m, v


def adam_update(w, parts, m, v, *, name):
    P, R, _ = parts.shape
    tr = _tile(R, (1024, 512, 256, 128, 64, 32, 16, 8))

    def body(w_ref, p_ref, m_ref, v_ref, g_ref, d_ref, nm_ref, nv_ref):
        g = p_ref[0]
        for k in range(1, P):
            g = g + p_ref[k]
        d, nm, nv = _adam_math(w_ref[...], g, m_ref[...], v_ref[...])
        g_ref[...] = g
        d_ref[...] = d
        nm_ref[...] = nm
        nv_ref[...] = nv

    row = pl.BlockSpec((tr, LANES), lambda i: (i, 0))
    shp = jax.ShapeDtypeStruct((R, LANES), F32)
    return _pcall(body, name=name, out_shape=(shp, shp, shp, shp),
                  in_specs=[row, pl.BlockSpec((P, tr, LANES), lambda i: (0, i, 0)), row, row],
                  out_specs=(row, row, row, row), grid=(R // tr,), sem=("parallel",))(w, parts, m, v)


ADAM_STEP_BYTES = 6 * 1024 * 1024


def adam_tensor(w, parts, m, v, *, name):
    L, R, C = w.shape
    per_row = L * C * (7 * 4 + 4 * parts[0].dtype.itemsize)
    tr = R
    for cand in (256, 128, 64, 32, 16):
        if R % cand == 0 and cand * per_row <= ADAM_STEP_BYTES:
            tr = cand
            break

    def body(*refs):
        w_ref, m_ref, v_ref = refs[:3]
        p_refs = refs[3:3 + L]
        g_ref, d_ref, nm_ref, nv_ref = refs[3 + L:]
        for l in range(L):
            g = p_refs[l][0].astype(F32)
            for k in range(1, 4):
                g = g + p_refs[l][k].astype(F32)
            d, nm, nv = _adam_math(w_ref[l], g, m_ref[l], v_ref[l])
            g_ref[l] = g
            d_ref[l] = d
            nm_ref[l] = nm
            nv_ref[l] = nv

    blk = pl.BlockSpec((L, tr, C), lambda i: (0, i, 0))
    pblk = pl.BlockSpec((4, tr, C), lambda i: (0, i, 0))
    shp = jax.ShapeDtypeStruct((L, R, C), F32)
    return _pcall(body, name=name, out_shape=(shp, shp, shp, shp), in_specs=[blk, blk, blk] + [pblk] * L,
                  out_specs=(blk, blk, blk, blk), grid=(R // tr,), sem=("parallel",))(w, m, v, *parts)


def _rows(n_elems):
    r = -(-n_elems // LANES)
    return -(-r // 1024) * 1024 if r > 1024 else -(-r // 16) * 16


def _flat(a, dtype=None):
    n = a.size
    r = _rows(n)
    f = a.reshape(-1)
    if dtype is not None:
        f = f.astype(dtype)
    if r * LANES != n:
        f = jnp.pad(f, (0, r * LANES - n))
    return f.reshape(r, LANES)


def _gathered_cols(g):
    n, rows, cols = g.shape
    return g.transpose(1, 0, 2).reshape(rows, n * cols)


def _owner_cols(dw, dtype):
    rows = dw.shape[0]
    cols = dw.shape[1] // N_DEV
    return dw.reshape(rows, N_DEV // 2, 2, cols).transpose(2, 1, 0, 3).astype(dtype)


_IN_NAT = dict(qa=(0, 1024), ka=(1024, 1280), va=(1280, 1536), qm=(1536, 2560), km=(2560, 3584),
               vm=(3584, 4608), om=(4608, 5632), g=(5632, 5648))


def _permute_w_in(w):
    sl = lambda k: w[:, _IN_NAT[k][0]:_IN_NAT[k][1]]
    pad = jnp.zeros((w.shape[0], P_WIDTH - P_G - N_GATES), w.dtype)
    return jnp.concatenate([sl("qm"), sl("km"), sl("qa"), sl("vm"), sl("om"), sl("ka"), sl("va"), sl("g"), pad],
                           axis=1)


def _unpermute_dw_in(dw):
    qm, km = dw[:, P_QK:P_QK + 1024], dw[:, P_QK + 1024:P_QK + 2048]
    return jnp.concatenate([dw[:, P_QA:P_QA + 1024], dw[:, P_KA:P_KA + 256], dw[:, P_VA:P_VA + 256], qm, km,
                            dw[:, P_VM:P_VM + 1024], dw[:, P_OM:P_OM + 1024], dw[:, P_G:P_G + N_GATES]], axis=1)


BIG = ("ffn1_w_gate", "ffn1_w_up", "ffn1_w_down", "w_in", "w_out", "ffn2_w_gate", "ffn2_w_up", "ffn2_w_down")
SMALL = ("ffn1_norm_pre", "ffn1_norm_post", "mix_norm_pre", "mix_norm_post", "b_gate", "attn_sink", "mlstm_norm",
         "ffn2_norm_pre", "ffn2_norm_post")
WEIGHTS = ("ffn1_norm_pre", "ffn1_norm_post", "ffn1_w_gate", "ffn1_w_up", "ffn1_w_down", "mix_norm_pre",
           "mix_norm_post", "w_in", "b_gate", "conv_w", "attn_sink", "mlstm_norm", "w_out", "ffn2_norm_pre",
           "ffn2_norm_post", "ffn2_w_gate", "ffn2_w_up", "ffn2_w_down")


GRAD_DT = BF16


def _carried(result, comm):
    return result if comm is not None else (result, None)


def _pair_adds(grads, recv, core, tag):
    return [pair_add(g, r, core, name=f"{tag}_add{t}") for t, (g, r) in enumerate(zip(grads, recv))]


def _ffn_fwd(x, g_pre, g_post, wg8, wu8, wd8, tag, gather=None):
    xn = norm_fwd(x, g_pre, name=f"{tag}_pre", out_dtype=BF16)
    comm = None if gather is None else ag_comm(gather)
    (hg, hu, act), gathered = _carried(ffn_gu(xn, wg8, wu8, name=f"{tag}_gu", comm=comm), comm)
    f = ffn_down(act, wd8, name=f"{tag}_down")
    x_new = norm_fwd(f, g_post, name=f"{tag}_post", scale=0.5, resid=x)
    return x_new, (x, xn, hg, hu, act, f), gathered


def _ffn_bwd(dx, saved, g_pre, g_post, wg8, wu8, wd8, core, tag, reduce=None):
    x, xn, hg, hu, act, f = saved
    df, dg_post = norm_bwd(dx, f, g_post, name=f"{tag}_post_b", scale=0.5, out_dtype=BF16)
    comm = None if reduce is None else pair_comm(reduce)
    dwd, recv = _carried(ffn_dwd(act, df, name=f"{tag}_dwd", out_dtype=GRAD_DT, comm=comm), comm)
    dhg, dhu = ffn_dact(df, wd8, hg, hu, name=f"{tag}_dact")
    comm = None if reduce is None else chip_comm(_pair_adds(reduce, recv, core, tag))
    (dwg, dwu), reduced = _carried(ffn_dwgu(xn, dhg, dhu, name=f"{tag}_dwgu", out_dtype=GRAD_DT, comm=comm), comm)
    dxn = ffn_dxn(dhg, dhu, wg8, wu8, name=f"{tag}_dxn")
    dx_new, dg_pre = norm_bwd(dxn, x, g_pre, name=f"{tag}_pre_b", resid=dx)
    return dx_new, dg_pre, dg_post, [dwg, dwu, dwd], reduced


def _mix_fwd(x, g_pre, g_post, w_in_p, b_gate, conv_full, sink, mnorm, w_out, cos2, sin2, tag, gather=None):
    S = x.shape[0]
    xn = norm_fwd(x, g_pre, name=f"{tag}_pre", out_dtype=BF16)
    comm = None if gather is None else ag_comm(gather)
    proj, gathered = _carried(mm_nn(xn, w_in_p, name=f"{tag}_in", comm=comm), comm)
    gates_r = proj[:, P_G:P_G + N_GATES].T
    bg_c = jnp.pad(b_gate, (0, LANES - N_GATES)).reshape(1, LANES)
    bg_r = b_gate.reshape(N_GATES, 1)
    y_att, lse = attn_fwd(proj, cos2, sin2, sink, name=f"{tag}_att")
    qk = conv_fwd(proj, conv_full, name=f"{tag}_conv")
    hf, denf, cf, nmf = mlstm_fwd(qk, proj, gates_r, bg_c, bg_r, reverse=False, name=f"{tag}_mf")
    hb, denb, cb, nmb = mlstm_fwd(qk, proj, gates_r, bg_c, bg_r, reverse=True, name=f"{tag}_mb")
    y_m = headnorm_fwd(hf, hb, proj, mnorm.reshape(1, M_WIDTH), name=f"{tag}_hn")
    y = jnp.concatenate([y_att, y_m], axis=1)
    mo = mm_nn(y, w_out, name=f"{tag}_out")
    x_new = norm_fwd(mo, g_post, name=f"{tag}_post", resid=x)
    saved = (x, xn, proj, gates_r, bg_c, bg_r, lse, qk, hf, denf, cf, nmf, hb, denb, cb, nmb, y, mo)
    return x_new, saved, gathered


def _mix_bwd(dx, saved, g_pre, g_post, w_in_p, conv_full, sink, mnorm, w_out, cos2, sin2, core, tag, reduce=None):
    x, xn, proj, gates_r, bg_c, bg_r, lse, qk, hf, denf, cf, nmf, hb, denb, cb, nmb, y, mo = saved
    S = x.shape[0]
    dmo, dg_post = norm_bwd(dx, mo, g_post, name=f"{tag}_post_b", out_dtype=BF16)
    comm = None if reduce is None else pair_comm(reduce)
    dw_out, recv = _carried(mm_tn(y, dmo, name=f"{tag}_dwo", owner_rows=D_MODEL // N_DEV, out_dtype=GRAD_DT,
                                  comm=comm), comm)
    dy = mm_nt(dmo, w_out, name=f"{tag}_dy")
    mn = mnorm.reshape(1, M_WIDTH)
    dh, dom, dmn = headnorm_bwd(hf, hb, proj, mn, dy, name=f"{tag}_hn_b")
    comm = None if reduce is None else chip_comm(_pair_adds(reduce, recv, core, tag))
    (dqk_f, dv_f, dgc_f, dgr_f), reduced = _carried(
        mlstm_bwd(qk, proj, gates_r, bg_c, bg_r, hf, denf, cf, nmf, dh, reverse=False, name=f"{tag}_mf_b",
                  comm=comm), comm)
    dqk_b, dv_b, dgc_b, dgr_b = mlstm_bwd(qk, proj, gates_r, bg_c, bg_r, hb, denb, cb, nmb, dh,
                                           reverse=True, name=f"{tag}_mb_b")
    dqk_in, dconv = conv_bwd(proj, conv_full, dqk_f, dqk_b, name=f"{tag}_conv_b")
    dqa, dka, dva, dsink = attn_bwd(proj, y, dy, lse, cos2, sin2, sink, name=f"{tag}_att_b")
    dgates = dgc_f + dgc_b + jnp.pad((dgr_f + dgr_b).T, ((0, 0), (0, LANES - N_GATES)))
    dproj = jnp.concatenate([dqk_in.astype(BF16), dqa.astype(BF16), (dv_f + dv_b).astype(BF16), dom.astype(BF16),
                             dka.astype(BF16), dva.astype(BF16), dgates.astype(BF16),
                             jnp.zeros((S, P_WIDTH - P_G - LANES), BF16)], axis=1)
    db_gate = colsum(dgates, name=f"{tag}_dbg")[0, :N_GATES]
    dw_in = mm_tn(xn, dproj, name=f"{tag}_dwi")
    dxn = mm_nt(dproj, w_in_p, name=f"{tag}_dxn")
    dx_new, dg_pre = norm_bwd(dxn, x, g_pre, name=f"{tag}_pre_b", resid=dx)
    grads = [_owner_cols(_unpermute_dw_in(dw_in), GRAD_DT), dw_out, _owner_cols(dconv[:CONV_WIDTH], F32)]
    return dx_new, dg_pre, dg_post, db_gate, dsink[0, :ATT_HEADS], dmn[0], grads, reduced


def colsum(a, *, name):
    S, C = a.shape
    tm = _tile(S, (512, 256, 128))

    def body(a_ref, o_ref):
        @pl.when(pl.program_id(0) == 0)
        def _():
            o_ref[...] = jnp.zeros_like(o_ref)

        o_ref[...] += jnp.sum(a_ref[...], axis=0, keepdims=True)

    return _pcall(body, name=name, out_shape=jax.ShapeDtypeStruct((1, C), F32),
                  in_specs=[pl.BlockSpec((tm, C), lambda i: (i, 0))], out_specs=pl.BlockSpec((1, C), lambda i: (0, 0)),
                  grid=(S // tm,), sem=("arbitrary",))(a)


def _layer_weights(gathered):
    wg1, wu1, wd1, win, wout, wg2, wu2, wd2 = gathered
    return dict(ffn1=(wg1, wu1, wd1), ffn2=(wg2, wu2, wd2),
                mix=(_permute_w_in(_gathered_cols(win)), wout.reshape(D_MODEL, D_MODEL)))


def _layer_shards(W, l):
    pad_c = lambda a: jnp.pad(a.astype(BF16), ((0, 0), (0, FSP - FS)))
    pad_r = lambda a: jnp.pad(a.astype(BF16), ((0, FSP - FS), (0, 0)))
    return [pad_c(W["ffn1_w_gate"][l]), pad_c(W["ffn1_w_up"][l]), pad_r(W["ffn1_w_down"][l]),
            W["w_in"][l].astype(BF16), W["w_out"][l].astype(BF16),
            pad_c(W["ffn2_w_gate"][l]), pad_c(W["ffn2_w_up"][l]), pad_r(W["ffn2_w_down"][l])]


def kernel(x, ffn1_norm_pre, ffn1_norm_post, ffn1_w_gate, ffn1_w_up, ffn1_w_down, mix_norm_pre, mix_norm_post, w_in, b_gate, conv_w, attn_sink, mlstm_norm, w_out, ffn2_norm_pre, ffn2_norm_post, ffn2_w_gate, ffn2_w_up, ffn2_w_down, loss_target, m_ffn1_norm_pre, m_ffn1_norm_post, m_ffn1_w_gate, m_ffn1_w_up, m_ffn1_w_down, m_mix_norm_pre, m_mix_norm_post, m_w_in, m_b_gate, m_conv_w, m_attn_sink, m_mlstm_norm, m_w_out, m_ffn2_norm_pre, m_ffn2_norm_post, m_ffn2_w_gate, m_ffn2_w_up, m_ffn2_w_down, v_ffn1_norm_pre, v_ffn1_norm_post, v_ffn1_w_gate, v_ffn1_w_up, v_ffn1_w_down, v_mix_norm_pre, v_mix_norm_post, v_w_in, v_b_gate, v_conv_w, v_attn_sink, v_mlstm_norm, v_w_out, v_ffn2_norm_pre, v_ffn2_norm_post, v_ffn2_w_gate, v_ffn2_w_up, v_ffn2_w_down):
    W = dict(ffn1_norm_pre=ffn1_norm_pre, ffn1_norm_post=ffn1_norm_post, ffn1_w_gate=ffn1_w_gate,
             ffn1_w_up=ffn1_w_up, ffn1_w_down=ffn1_w_down, mix_norm_pre=mix_norm_pre, mix_norm_post=mix_norm_post,
             w_in=w_in, b_gate=b_gate, conv_w=conv_w, attn_sink=attn_sink, mlstm_norm=mlstm_norm, w_out=w_out,
             ffn2_norm_pre=ffn2_norm_pre, ffn2_norm_post=ffn2_norm_post, ffn2_w_gate=ffn2_w_gate,
             ffn2_w_up=ffn2_w_up, ffn2_w_down=ffn2_w_down)
    M1 = dict(ffn1_norm_pre=m_ffn1_norm_pre, ffn1_norm_post=m_ffn1_norm_post, ffn1_w_gate=m_ffn1_w_gate,
              ffn1_w_up=m_ffn1_w_up, ffn1_w_down=m_ffn1_w_down, mix_norm_pre=m_mix_norm_pre,
              mix_norm_post=m_mix_norm_post, w_in=m_w_in, b_gate=m_b_gate, conv_w=m_conv_w, attn_sink=m_attn_sink,
              mlstm_norm=m_mlstm_norm, w_out=m_w_out, ffn2_norm_pre=m_ffn2_norm_pre,
              ffn2_norm_post=m_ffn2_norm_post, ffn2_w_gate=m_ffn2_w_gate, ffn2_w_up=m_ffn2_w_up,
              ffn2_w_down=m_ffn2_w_down)
    V2 = dict(ffn1_norm_pre=v_ffn1_norm_pre, ffn1_norm_post=v_ffn1_norm_post, ffn1_w_gate=v_ffn1_w_gate,
              ffn1_w_up=v_ffn1_w_up, ffn1_w_down=v_ffn1_w_down, mix_norm_pre=v_mix_norm_pre,
              mix_norm_post=v_mix_norm_post, w_in=v_w_in, b_gate=v_b_gate, conv_w=v_conv_w, attn_sink=v_attn_sink,
              mlstm_norm=v_mlstm_norm, w_out=v_w_out, ffn2_norm_pre=v_ffn2_norm_pre,
              ffn2_norm_post=v_ffn2_norm_post, ffn2_w_gate=v_ffn2_w_gate, ffn2_w_up=v_ffn2_w_up,
              ffn2_w_down=v_ffn2_w_down)
    depth = w_in.shape[0]
    S = x.shape[1]
    xs = x[0]
    cos2, sin2 = _rope_tables(S)
    core = lax.axis_index("c").astype(jnp.int32).reshape(1)

    cs = conv_w.shape[2]
    conv_g = run_comm(ag_comm([conv_w.reshape(depth * CONV_WIDTH, cs)]), name="ag_conv")[0]
    conv_all = conv_g.reshape(N_DEV, depth, CONV_WIDTH, cs).transpose(1, 2, 0, 3)
    conv_all = conv_all.reshape(depth, CONV_WIDTH, N_DEV * cs)
    conv_all = jnp.pad(conv_all, ((0, 0), (0, CONV_HALO - CONV_WIDTH), (0, 0)))

    lw, saved = [], []
    gathered = run_comm(ag_comm(_layer_shards(W, 0)), name="ag_first")
    for l in range(depth):
        wl = _layer_weights(gathered)
        lw.append(wl)
        nxt = _layer_shards(W, l + 1) if l + 1 < depth else None
        part = lambda a, b: None if nxt is None else nxt[a:b]
        xs, s1, g1 = _ffn_fwd(xs, W["ffn1_norm_pre"][l], W["ffn1_norm_post"][l], *wl["ffn1"], "f1", part(0, 3))
        xs, s2, gm = _mix_fwd(xs, W["mix_norm_pre"][l], W["mix_norm_post"][l], wl["mix"][0], W["b_gate"][l],
                              conv_all[l], W["attn_sink"][l], W["mlstm_norm"][l], wl["mix"][1], cos2, sin2, "mx",
                              part(3, 5))
        xs, s3, g2 = _ffn_fwd(xs, W["ffn2_norm_pre"][l], W["ffn2_norm_post"][l], *wl["ffn2"], "f2", part(5, 8))
        saved.append((s1, s2, s3))
        if nxt is not None:
            gathered = g1 + gm + g2

    dx, loss_part = loss_fwd_bwd(xs, loss_target[0], name="loss")

    F1, MX, F2 = BIG[0:3], (BIG[3], BIG[4], "conv_w"), BIG[5:8]
    names = BIG + ("conv_w",)
    parts = {n: [None] * depth for n in names}
    small_parts = [None] * depth
    waiting = None
    for l in reversed(range(depth)):
        wl = lw[l]
        s1, s2, s3 = saved[l]
        dx, dpre2, dpost2, grads2, reduced = _ffn_bwd(dx, s3, W["ffn2_norm_pre"][l], W["ffn2_norm_post"][l],
                                                      *wl["ffn2"], core, "f2", waiting)
        if waiting is not None:
            for n, r in zip(F1, reduced):
                parts[n][l + 1] = r
        dx, dpre_m, dpost_m, db_gate, dsink, dmn, grads_m, reduced = _mix_bwd(
            dx, s2, W["mix_norm_pre"][l], W["mix_norm_post"][l], wl["mix"][0], conv_all[l], W["attn_sink"][l],
            W["mlstm_norm"][l], wl["mix"][1], cos2, sin2, core, "mx", grads2)
        for n, r in zip(F2, reduced):
            parts[n][l] = r
        dx, dpre1, dpost1, waiting, reduced = _ffn_bwd(dx, s1, W["ffn1_norm_pre"][l], W["ffn1_norm_post"][l],
                                                       *wl["ffn1"], core, "f1", grads_m)
        for n, r in zip(MX, reduced):
            parts[n][l] = r
        small_parts[l] = dict(ffn1_norm_pre=dpre1[0], ffn1_norm_post=dpost1[0], mix_norm_pre=dpre_m[0],
                              mix_norm_post=dpost_m[0], b_gate=db_gate, attn_sink=dsink, mlstm_norm=dmn,
                              ffn2_norm_pre=dpre2[0], ffn2_norm_post=dpost2[0])
    recv = run_comm(pair_comm(waiting), name="rs1_last")
    reduced = run_comm(chip_comm(_pair_adds(waiting, recv, core, "last")), name="rs2_last")
    for n, r in zip(F1, reduced):
        parts[n][0] = r

    outs = {k: {} for k in ("g", "d", "m", "v")}
    for n in names:
        res = adam_tensor(W[n], parts[n], M1[n], V2[n], name=f"adam_{n}")
        for k, r in zip(("g", "d", "m", "v"), res):
            outs[k][n] = r
    small_out = {k: {n: [None] * depth for n in SMALL} for k in ("g", "d", "m", "v")}

    vec = jnp.concatenate([small_parts[l][n].reshape(-1) for l in range(depth) for n in SMALL]
                          + [loss_part.reshape(-1)])
    n_small = vec.shape[0]
    gathered_small = run_comm(ag_comm([_flat(vec)]), name="ag_small")[0]
    wvec = _flat(jnp.concatenate([W[n][l].reshape(-1) for l in range(depth) for n in SMALL] + [jnp.zeros((1,), F32)]))
    mvec = _flat(jnp.concatenate([M1[n][l].reshape(-1) for l in range(depth) for n in SMALL] + [jnp.zeros((1,), F32)]))
    vvec = _flat(jnp.concatenate([V2[n][l].reshape(-1) for l in range(depth) for n in SMALL] + [jnp.ones((1,), F32)]))
    res = adam_update(wvec, gathered_small, mvec, vvec, name="adam_small")
    res = [r.reshape(-1)[:n_small] for r in res]
    off = 0
    for l in range(depth):
        for n in SMALL:
            sz = W[n].shape[1]
            for k, r in zip(("g", "d", "m", "v"), res):
                small_out[k][n][l] = r[off:off + sz]
            off += sz
    loss = res[0][off]
    for k in outs:
        for n in SMALL:
            outs[k][n] = jnp.stack(small_out[k][n], axis=0)

    return (loss, dx[None], *[outs["g"][n] for n in WEIGHTS], *[outs["d"][n] for n in WEIGHTS],
            *[outs["m"][n] for n in WEIGHTS], *[outs["v"][n] for n in WEIGHTS])
```

```python
import jax
import jax.numpy as jnp
from jax import lax
from jax.experimental import pallas as pl
from jax.experimental.pallas import tpu as pltpu

F32 = jnp.float32
BF16 = jnp.bfloat16

D_MODEL = 2048
D_FF = 5632
ATT_HEADS = 8
ATT_KV_HEADS = 2
ATT_GROUP = ATT_HEADS // ATT_KV_HEADS
ATT_WIDTH = 1024
HEAD_DIM = 128
KV_WIDTH = 256
WINDOW = 128
BLK = 128
M_WIDTH = 1024
M_HEADS = 4
M_HEAD_DIM = 256
CONV_WIDTH = 5
EPS = 1e-6
ROPE_THETA = 10000.0
IN_WIDTH = 5648
N_GATES = 16
N_DEV = 8

ADAM_LR = 0.001
ADAM_B1 = 0.9
ADAM_B2 = 0.999
ADAM_EPS = 1e-08
ADAM_WD = 0.01
ADAM_STEP = 10

P_QK = 0
P_QA = 2048
P_VM = 3072
P_OM = 4096
P_KA = 5120
P_VA = 5376
P_G = 5632
P_WIDTH = 6144

LANES = 128
V7X_VMEM_LIMIT = 48 * 1024 * 1024
NEG = -1e30
MESH = pl.DeviceIdType.MESH
ANY = pl.BlockSpec(memory_space=pl.ANY)


def _tile(n, cands=(1024, 512, 256, 128)):
    for c in cands:
        if n % c == 0:
            return c
    return n


class Comm:
    def __init__(self, ins, outs, sems, phases):
        self.ins, self.outs, self.sems, self.phases = list(ins), list(outs), list(sems), list(phases)


def run_comm(comm, *, name):
    n_in, n_out = len(comm.ins), len(comm.outs)

    def body(*refs):
        ins, outs, sems = refs[:n_in], refs[n_in:n_in + n_out], refs[n_in + n_out:]
        for phase in comm.phases:
            phase(ins, outs, sems)

    return pl.pallas_call(body, name=name, out_shape=comm.outs, in_specs=[ANY] * n_in, out_specs=[ANY] * n_out,
                          scratch_shapes=comm.sems,
                          compiler_params=pltpu.CompilerParams(has_side_effects=True))(*comm.ins)


def _pcall(body, *, name, out_shape, in_specs, out_specs, grid=(), scratch=(), sem=None, comm=None):
    if comm is None:
        return pl.pallas_call(
            body, name=name, out_shape=out_shape, in_specs=in_specs, out_specs=out_specs, grid=grid,
            scratch_shapes=list(scratch),
            compiler_params=pltpu.CompilerParams(dimension_semantics=sem, vmem_limit_bytes=V7X_VMEM_LIMIT))
    multi = isinstance(out_shape, (tuple, list))
    outs = list(out_shape) if multi else [out_shape]
    ospecs = list(out_specs) if multi else [out_specs]
    n_in, n_out, n_scr = len(in_specs), len(outs), len(scratch)
    nci, nco = len(comm.ins), len(comm.outs)
    steps = 1
    for g in grid:
        steps *= g
    n_ph = len(comm.phases)
    at = [0, steps - 1] if n_ph == 2 else [0, min(max(1, (3 * steps) // 4), steps - 2), steps - 1]
    assert steps >= n_ph and at == sorted(set(at))

    def wrapped(*refs):
        ins, cins = refs[:n_in], refs[n_in:n_in + nci]
        o0 = n_in + nci
        res, couts = refs[o0:o0 + n_out], refs[o0 + n_out:o0 + n_out + nco]
        s0 = o0 + n_out + nco
        scr, csems = refs[s0:s0 + n_scr], refs[s0 + n_scr:]
        lin = 0
        for k, g in enumerate(grid):
            lin = lin * g + pl.program_id(k)

        @pl.when(lin == at[0])
        def _():
            comm.phases[0](cins, couts, csems)

        body(*ins, *res, *scr)
        for p in range(1, n_ph):
            @pl.when(lin == at[p])
            def _(p=p):
                comm.phases[p](cins, couts, csems)

    call = pl.pallas_call(
        wrapped, name=name, out_shape=outs + comm.outs, in_specs=list(in_specs) + [ANY] * nci,
        out_specs=ospecs + [ANY] * nco, grid=grid, scratch_shapes=list(scratch) + comm.sems,
        compiler_params=pltpu.CompilerParams(dimension_semantics=("arbitrary",) * len(grid),
                                             vmem_limit_bytes=V7X_VMEM_LIMIT, has_side_effects=True))

    def run(*args):
        got = list(call(*args, *comm.ins))
        return (tuple(got[:n_out]) if multi else got[0]), got[n_out:]

    return run


def _dot(a, b):
    return jnp.dot(a, b, preferred_element_type=F32)


def _dot_nt(a, b):
    return lax.dot_general(a, b, (((1,), (1,)), ((), ())), preferred_element_type=F32)


def _dot_tn(a, b):
    return lax.dot_general(a, b, (((0,), (0,)), ((), ())), preferred_element_type=F32)


def _sigmoid(x):
    return 1.0 / (1.0 + jnp.exp(-x))


def mm_nn(a, b, *, name, out_dtype=F32, comm=None):
    M, K = a.shape
    N = b.shape[1]
    tm, tk, tn = _tile(M), _tile(K), _tile(N)
    nk = K // tk

    def body(a_ref, b_ref, o_ref, acc):
        k = pl.program_id(2)

        @pl.when(k == 0)
        def _():
            acc[...] = jnp.zeros_like(acc)

        acc[...] += _dot(a_ref[...], b_ref[...])

        @pl.when(k == nk - 1)
        def _():
            o_ref[...] = acc[...].astype(o_ref.dtype)

    return _pcall(body, name=name, out_shape=jax.ShapeDtypeStruct((M, N), out_dtype),
                  in_specs=[pl.BlockSpec((tm, tk), lambda i, j, k: (i, k)),
                            pl.BlockSpec((tk, tn), lambda i, j, k: (k, j))],
                  out_specs=pl.BlockSpec((tm, tn), lambda i, j, k: (i, j)), grid=(M // tm, N // tn, nk),
                  scratch=[pltpu.VMEM((tm, tn), F32)], sem=("parallel", "parallel", "arbitrary"), comm=comm)(a, b)


def mm_tn(a, g, *, name, owner_rows=None, out_dtype=F32, comm=None):
    M, K = a.shape
    N = g.shape[1]
    tm, tk, tn = _tile(M), _tile(K), _tile(N)
    nm = M // tm
    per_tile = 1 if owner_rows is None else tk // owner_rows

    def body(a_ref, g_ref, o_ref, acc):
        m = pl.program_id(2)

        @pl.when(m == 0)
        def _():
            acc[...] = jnp.zeros_like(acc)

        acc[...] += _dot_tn(a_ref[...], g_ref[...])

        @pl.when(m == nm - 1)
        def _():
            if owner_rows is None:
                o_ref[...] = acc[...].astype(o_ref.dtype)
            else:
                for d in range(per_tile):
                    o_ref[d % 2, d // 2] = acc[d * owner_rows:(d + 1) * owner_rows, :].astype(o_ref.dtype)

    if owner_rows is None:
        out_shape = jax.ShapeDtypeStruct((K, N), out_dtype)
        out_spec = pl.BlockSpec((tk, tn), lambda i, j, m: (i, j))
    else:
        assert per_tile % 2 == 0 and K == N_DEV * owner_rows
        out_shape = jax.ShapeDtypeStruct((2, N_DEV // 2, owner_rows, N), out_dtype)
        out_spec = pl.BlockSpec((2, per_tile // 2, owner_rows, tn), lambda i, j, m: (0, i, 0, j))
    return _pcall(body, name=name, out_shape=out_shape,
                  in_specs=[pl.BlockSpec((tm, tk), lambda i, j, m: (m, i)),
                            pl.BlockSpec((tm, tn), lambda i, j, m: (m, j))],
                  out_specs=out_spec, grid=(K // tk, N // tn, nm), scratch=[pltpu.VMEM((tk, tn), F32)],
                  sem=("parallel", "parallel", "arbitrary"), comm=comm)(a, g)


def mm_nt(a, b, *, name, out_dtype=F32):
    M, K = a.shape
    N = b.shape[0]
    tm, tn, tk = _tile(M), _tile(N), _tile(K)
    nk = K // tk

    def body(a_ref, b_ref, o_ref, acc):
        k = pl.program_id(2)

        @pl.when(k == 0)
        def _():
            acc[...] = jnp.zeros_like(acc)

        acc[...] += _dot_nt(a_ref[...], b_ref[...])

        @pl.when(k == nk - 1)
        def _():
            o_ref[...] = acc[...].astype(o_ref.dtype)

    return _pcall(body, name=name, out_shape=jax.ShapeDtypeStruct((M, N), out_dtype),
                  in_specs=[pl.BlockSpec((tm, tk), lambda i, j, k: (i, k)),
                            pl.BlockSpec((tn, tk), lambda i, j, k: (j, k))],
                  out_specs=pl.BlockSpec((tm, tn), lambda i, j, k: (i, j)), grid=(M // tm, N // tn, nk),
                  scratch=[pltpu.VMEM((tm, tn), F32)], sem=("parallel", "parallel", "arbitrary"))(a, b)


FS = D_FF // N_DEV
FSP = 768


def ffn_gu(xn, wg8, wu8, *, name, comm=None):
    S, D = xn.shape
    tm = _tile(S)

    def body(x_ref, wg_ref, wu_ref, hg_ref, hu_ref, act_ref):
        xv = x_ref[...]
        hg = _dot(xv, wg_ref[...])
        hu = _dot(xv, wu_ref[...])
        hg_ref[...] = hg.astype(BF16)
        hu_ref[...] = hu.astype(BF16)
        act_ref[...] = (hg * _sigmoid(hg) * hu).astype(BF16)

    wspec = pl.BlockSpec((None, D, FSP), lambda i, j: (j, 0, 0))
    ospec = pl.BlockSpec((None, tm, FSP), lambda i, j: (j, i, 0))
    shp = jax.ShapeDtypeStruct((N_DEV, S, FSP), BF16)
    return _pcall(body, name=name, out_shape=(shp, shp, shp),
                  in_specs=[pl.BlockSpec((tm, D), lambda i, j: (i, 0)), wspec, wspec],
                  out_specs=(ospec, ospec, ospec), grid=(S // tm, N_DEV), sem=("parallel", "arbitrary"),
                  comm=comm)(xn, wg8, wu8)


def ffn_down(act8, wd8, *, name):
    _, S, _ = act8.shape
    D = wd8.shape[2]
    tm, tn = _tile(S), D

    def body(a_ref, w_ref, o_ref):
        @pl.when(pl.program_id(2) == 0)
        def _():
            o_ref[...] = jnp.zeros_like(o_ref)

        o_ref[...] += _dot(a_ref[...], w_ref[...])

    return _pcall(body, name=name, out_shape=jax.ShapeDtypeStruct((S, D), F32),
                  in_specs=[pl.BlockSpec((None, tm, FSP), lambda i, n, j: (j, i, 0)),
                            pl.BlockSpec((None, FSP, tn), lambda i, n, j: (j, 0, n))],
                  out_specs=pl.BlockSpec((tm, tn), lambda i, n, j: (i, n)),
                  grid=(S // tm, D // tn, N_DEV), sem=("parallel", "parallel", "arbitrary"))(act8, wd8)


def ffn_dact(df, wd8, hg8, hu8, *, name):
    S, D = df.shape
    tm = _tile(S)

    def body(d_ref, w_ref, hg_ref, hu_ref, dg_ref, du_ref):
        da = _dot_nt(d_ref[...], w_ref[...])
        hg = hg_ref[...].astype(F32)
        hu = hu_ref[...].astype(F32)
        sg = _sigmoid(hg)
        dg_ref[...] = (da * hu * (sg * (1.0 + hg * (1.0 - sg)))).astype(BF16)
        du_ref[...] = (da * hg * sg).astype(BF16)

    blk = pl.BlockSpec((None, tm, FSP), lambda i, j: (j, i, 0))
    shp = jax.ShapeDtypeStruct((N_DEV, S, FSP), BF16)
    return _pcall(body, name=name, out_shape=(shp, shp),
                  in_specs=[pl.BlockSpec((tm, D), lambda i, j: (i, 0)),
                            pl.BlockSpec((None, FSP, D), lambda i, j: (j, 0, 0)), blk, blk],
                  out_specs=(blk, blk), grid=(S // tm, N_DEV), sem=("parallel", "arbitrary"))(df, wd8, hg8, hu8)


def ffn_dwd(act8, df, *, name, out_dtype, comm=None):
    _, S, _ = act8.shape
    D = df.shape[1]
    tm, tn = _tile(S), D
    nm = S // tm

    def body(a_ref, d_ref, o_ref, acc):
        m = pl.program_id(2)

        @pl.when(m == 0)
        def _():
            acc[...] = jnp.zeros_like(acc)

        acc[...] += _dot_tn(a_ref[...], d_ref[...])

        @pl.when(m == nm - 1)
        def _():
            o_ref[...] = acc[0:FS, :].astype(o_ref.dtype)

    return _pcall(body, name=name, out_shape=jax.ShapeDtypeStruct((2, N_DEV // 2, FS, D), out_dtype),
                  in_specs=[pl.BlockSpec((None, tm, FSP), lambda j, n, m: (j, m, 0)),
                            pl.BlockSpec((tm, tn), lambda j, n, m: (m, n))],
                  out_specs=pl.BlockSpec((None, None, FS, tn), lambda j, n, m: (j % 2, j // 2, 0, n)),
                  grid=(N_DEV, D // tn, nm), scratch=[pltpu.VMEM((FSP, tn), F32)],
                  sem=("parallel", "parallel", "arbitrary"), comm=comm)(act8, df)


def ffn_dwgu(xn, dg8, du8, *, name, out_dtype, comm=None):
    S, D = xn.shape
    tm, tk = _tile(S), _tile(D)
    nm = S // tm

    def body(x_ref, dg_ref, du_ref, og_ref, ou_ref, accg, accu):
        m = pl.program_id(2)

        @pl.when(m == 0)
        def _():
            accg[...] = jnp.zeros_like(accg)
            accu[...] = jnp.zeros_like(accu)

        xv = x_ref[...]
        accg[...] += _dot_tn(xv, dg_ref[...])
        accu[...] += _dot_tn(xv, du_ref[...])

        @pl.when(m == nm - 1)
        def _():
            og_ref[...] = accg[:, 0:FS].astype(og_ref.dtype)
            ou_ref[...] = accu[:, 0:FS].astype(ou_ref.dtype)

    blk = pl.BlockSpec((None, tm, FSP), lambda j, k, m: (j, m, 0))
    ospec = pl.BlockSpec((None, None, tk, FS), lambda j, k, m: (j % 2, j // 2, k, 0))
    shp = jax.ShapeDtypeStruct((2, N_DEV // 2, D, FS), out_dtype)
    return _pcall(body, name=name, out_shape=(shp, shp),
                  in_specs=[pl.BlockSpec((tm, tk), lambda j, k, m: (m, k)), blk, blk],
                  out_specs=(ospec, ospec), grid=(N_DEV, D // tk, nm),
                  scratch=[pltpu.VMEM((tk, FSP), F32), pltpu.VMEM((tk, FSP), F32)],
                  sem=("parallel", "parallel", "arbitrary"), comm=comm)(xn, dg8, du8)


def ffn_dxn(dg8, du8, wg8, wu8, *, name):
    _, S, _ = dg8.shape
    D = wg8.shape[1]
    tm, tn = _tile(S), _tile(D)

    def body(dg_ref, du_ref, wg_ref, wu_ref, o_ref):
        @pl.when(pl.program_id(2) == 0)
        def _():
            o_ref[...] = jnp.zeros_like(o_ref)

        o_ref[...] += _dot_nt(dg_ref[...], wg_ref[...]) + _dot_nt(du_ref[...], wu_ref[...])

    blk = pl.BlockSpec((None, tm, FSP), lambda i, n, j: (j, i, 0))
    wspec = pl.BlockSpec((None, tn, FSP), lambda i, n, j: (j, n, 0))
    return _pcall(body, name=name, out_shape=jax.ShapeDtypeStruct((S, D), F32),
                  in_specs=[blk, blk, wspec, wspec], out_specs=pl.BlockSpec((tm, tn), lambda i, n, j: (i, n)),
                  grid=(S // tm, D // tn, N_DEV), sem=("parallel", "parallel", "arbitrary"))(dg8, du8, wg8, wu8)


def norm_fwd(x, g, *, name, scale=1.0, resid=None, out_dtype=F32):
    S, D = x.shape
    tm = _tile(S, (512, 256, 128))

    def body(*refs):
        if resid is None:
            x_ref, g_ref, o_ref = refs
        else:
            x_ref, g_ref, r_ref, o_ref = refs
        xv = x_ref[...].astype(F32)
        r = lax.rsqrt(jnp.mean(xv * xv, axis=-1, keepdims=True) + EPS)
        y = (xv * r) * g_ref[...]
        if scale != 1.0:
            y = y * scale
        if resid is not None:
            y = y + r_ref[...]
        o_ref[...] = y.astype(o_ref.dtype)

    row = pl.BlockSpec((tm, D), lambda i: (i, 0))
    in_specs = [row, pl.BlockSpec((1, D), lambda i: (0, 0))]
    args = [x, g.reshape(1, D)]
    if resid is not None:
        in_specs.append(row)
        args.append(resid)
    return _pcall(body, name=name, out_shape=jax.ShapeDtypeStruct((S, D), out_dtype), in_specs=in_specs,
                  out_specs=row, grid=(S // tm,), sem=("parallel",))(*args)


def norm_bwd(dy, x, g, *, name, scale=1.0, resid=None, out_dtype=F32):
    S, D = x.shape
    tm = _tile(S, (512, 256, 128))

    def body(*refs):
        if resid is None:
            dy_ref, x_ref, g_ref, dx_ref, dg_ref = refs
        else:
            dy_ref, x_ref, g_ref, r_ref, dx_ref, dg_ref = refs

        @pl.when(pl.program_id(0) == 0)
        def _():
            dg_ref[...] = jnp.zeros_like(dg_ref)

        xv = x_ref[...].astype(F32)
        d = dy_ref[...].astype(F32)
        if scale != 1.0:
            d = d * scale
        r = lax.rsqrt(jnp.mean(xv * xv, axis=-1, keepdims=True) + EPS)
        xh = xv * r
        dg_ref[...] += jnp.sum(d * xh, axis=0, keepdims=True)
        dxh = d * g_ref[...]
        dx = r * (dxh - xh * jnp.mean(dxh * xh, axis=-1, keepdims=True))
        if resid is not None:
            dx = dx + r_ref[...]
        dx_ref[...] = dx.astype(dx_ref.dtype)

    row = pl.BlockSpec((tm, D), lambda i: (i, 0))
    vec = pl.BlockSpec((1, D), lambda i: (0, 0))
    in_specs = [row, row, vec]
    args = [dy, x, g.reshape(1, D)]
    if resid is not None:
        in_specs.append(row)
        args.append(resid)
    return _pcall(body, name=name,
                  out_shape=(jax.ShapeDtypeStruct((S, D), out_dtype), jax.ShapeDtypeStruct((1, D), F32)),
                  in_specs=in_specs, out_specs=(row, vec), grid=(S // tm,), sem=("arbitrary",))(*args)


def loss_fwd_bwd(y, target, *, name):
    S, D = y.shape
    tm = _tile(S, (512, 256, 128))

    def body(y_ref, t_ref, dy_ref, l_ref):
        @pl.when(pl.program_id(0) == 0)
        def _():
            l_ref[...] = jnp.zeros_like(l_ref)

        e = y_ref[...] - t_ref[...]
        dy_ref[...] = e * (1.0 / D)
        l_ref[...] += jnp.sum(jnp.sum(e * e, axis=1, keepdims=True), axis=0, keepdims=True) * (0.5 / D)

    row = pl.BlockSpec((tm, D), lambda i: (i, 0))
    one = pl.BlockSpec((1, 1), lambda i: (0, 0))
    return _pcall(body, name=name,
                  out_shape=(jax.ShapeDtypeStruct((S, D), F32), jax.ShapeDtypeStruct((1, 1), F32)),
                  in_specs=[row, row], out_specs=(row, one), grid=(S // tm,), sem=("arbitrary",))(y, target)


def _rope_tables(S):
    half = HEAD_DIM // 2
    inv_freq = ROPE_THETA ** (-jnp.arange(half, dtype=F32) / half)
    ang = jnp.arange(S, dtype=F32)[:, None] * inv_freq[None, :]
    cos, sin = jnp.cos(ang), jnp.sin(ang)
    return jnp.concatenate([cos, cos], axis=1), jnp.concatenate([-sin, sin], axis=1)


def _rope(x, cos2, sin2):
    return x * cos2 + pltpu.roll(x, HEAD_DIM // 2, 1) * sin2


def _unrope(d, cos2, sin2):
    return d * cos2 + pltpu.roll(d * sin2, HEAD_DIM // 2, 1)


def _nbr_specs(width, col, nb):
    return [pl.BlockSpec((BLK, width), lambda n, c=col: (jnp.maximum(n - 1, 0), c)),
            pl.BlockSpec((BLK, width), lambda n, c=col: (n, c)),
            pl.BlockSpec((BLK, width), lambda n, c=col: (jnp.minimum(n + 1, nb - 1), c))]


def attn_fwd(proj, cos2, sin2, sink, *, name):
    S = proj.shape[0]
    nb = S // BLK
    scale = HEAD_DIM ** -0.5

    def body(sink_ref, q_ref, k0, k1, k2, v0, v1, v2, c0, c1, c2, s0, s1, s2, o_ref, lse_ref):
        n = pl.program_id(0)
        cosk = jnp.concatenate([c0[...], c1[...], c2[...]], axis=0)
        sink_ = jnp.concatenate([s0[...], s1[...], s2[...]], axis=0)
        kall = jnp.concatenate([k0[...], k1[...], k2[...]], axis=0)
        vall = jnp.concatenate([v0[...], v1[...], v2[...]], axis=0)
        rows = lax.broadcasted_iota(jnp.int32, (BLK, 3 * BLK), 0)
        cols = lax.broadcasted_iota(jnp.int32, (BLK, 3 * BLK), 1)
        kpos = (n - 1) * BLK + cols
        valid = (jnp.abs(cols - BLK - rows) <= WINDOW) & (kpos >= 0) & (kpos < S)
        valid = jnp.concatenate([valid] * ATT_GROUP, axis=0)
        lane = lax.broadcasted_iota(jnp.int32, (BLK, LANES), 1)
        lse_tile = jnp.zeros((BLK, LANES), F32)
        for hk in range(ATT_KV_HEADS):
            ks = slice(hk * HEAD_DIM, (hk + 1) * HEAD_DIM)
            kh = _rope(kall[:, ks], cosk, sink_).astype(BF16)
            vh = vall[:, ks].astype(BF16)
            qs = []
            for g in range(ATT_GROUP):
                hq = hk * ATT_GROUP + g
                qs.append(_rope(q_ref[:, hq * HEAD_DIM:(hq + 1) * HEAD_DIM], c1[...], s1[...]))
            qh = jnp.concatenate(qs, axis=0).astype(BF16)
            s = _dot_nt(qh, kh) * scale
            s = jnp.where(valid, s, NEG)
            snk = jnp.concatenate(
                [jnp.full((BLK, 1), sink_ref[hk * ATT_GROUP + g], F32) for g in range(ATT_GROUP)], axis=0)
            m = jnp.maximum(jnp.max(s, axis=1, keepdims=True), snk)
            p = jnp.exp(s - m)
            l = jnp.sum(p, axis=1, keepdims=True) + jnp.exp(snk - m)
            o = _dot(p.astype(BF16), vh) * (1.0 / l)
            lse = m + jnp.log(l)
            for g in range(ATT_GROUP):
                hq = hk * ATT_GROUP + g
                o_ref[:, hq * HEAD_DIM:(hq + 1) * HEAD_DIM] = o[g * BLK:(g + 1) * BLK].astype(o_ref.dtype)
                lse_tile = lse_tile + jnp.where(lane == hq, lse[g * BLK:(g + 1) * BLK], 0.0)
        lse_ref[...] = lse_tile

    in_specs = ([pl.BlockSpec(memory_space=pltpu.SMEM),
                 pl.BlockSpec((BLK, ATT_WIDTH), lambda n: (n, P_QA // ATT_WIDTH))]
                + _nbr_specs(KV_WIDTH, P_KA // KV_WIDTH, nb) + _nbr_specs(KV_WIDTH, P_VA // KV_WIDTH, nb)
                + _nbr_specs(HEAD_DIM, 0, nb) + _nbr_specs(HEAD_DIM, 0, nb))
    return _pcall(body, name=name,
                  out_shape=(jax.ShapeDtypeStruct((S, ATT_WIDTH), BF16), jax.ShapeDtypeStruct((S, LANES), F32)),
                  in_specs=in_specs,
                  out_specs=(pl.BlockSpec((BLK, ATT_WIDTH), lambda n: (n, 0)),
                             pl.BlockSpec((BLK, LANES), lambda n: (n, 0))),
                  grid=(nb,), sem=("parallel",))(sink, proj, proj, proj, proj, proj, proj, proj,
                                                 cos2, cos2, cos2, sin2, sin2, sin2)


def attn_bwd(proj, y, dy, lse, cos2, sin2, sink, *, name):
    S = proj.shape[0]
    nb = S // BLK
    scale = HEAD_DIM ** -0.5

    def body(sink_ref, q_ref, k0, k1, k2, v0, v1, v2, o_ref, d_ref, l_ref, c0, c1, c2, s0, s1, s2,
             dq_ref, dk_ref, dv_ref, dsink_ref, dk_acc, dv_acc):
        n = pl.program_id(0)

        @pl.when(n == 0)
        def _():
            dsink_ref[...] = jnp.zeros_like(dsink_ref)
            dk_acc[...] = jnp.zeros_like(dk_acc)
            dv_acc[...] = jnp.zeros_like(dv_acc)

        @pl.when(n < nb)
        def _():
            cosk = jnp.concatenate([c0[...], c1[...], c2[...]], axis=0)
            sink_ = jnp.concatenate([s0[...], s1[...], s2[...]], axis=0)
            kall = jnp.concatenate([k0[...], k1[...], k2[...]], axis=0)
            vall = jnp.concatenate([v0[...], v1[...], v2[...]], axis=0)
            lane = lax.broadcasted_iota(jnp.int32, (1, LANES), 1)
            rows = lax.broadcasted_iota(jnp.int32, (BLK, 3 * BLK), 0)
            cols = lax.broadcasted_iota(jnp.int32, (BLK, 3 * BLK), 1)
            kpos = (n - 1) * BLK + cols
            valid = (jnp.abs(cols - BLK - rows) <= WINDOW) & (kpos >= 0) & (kpos < S)
            valid = jnp.concatenate([valid] * ATT_GROUP, axis=0)
            dsink_acc = jnp.zeros((1, LANES), F32)
            for hk in range(ATT_KV_HEADS):
                ks = slice(hk * HEAD_DIM, (hk + 1) * HEAD_DIM)
                kh = _rope(kall[:, ks], cosk, sink_).astype(BF16)
                vh = vall[:, ks].astype(BF16)
                qs, dos, lses, deltas = [], [], [], []
                for g in range(ATT_GROUP):
                    hq = hk * ATT_GROUP + g
                    hs = slice(hq * HEAD_DIM, (hq + 1) * HEAD_DIM)
                    qs.append(_rope(q_ref[:, hs], c1[...], s1[...]))
                    do = d_ref[:, hs]
                    dos.append(do)
                    lses.append(l_ref[:, hq:hq + 1])
                    deltas.append(jnp.sum(do * o_ref[:, hs].astype(F32), axis=1, keepdims=True))
                qh = jnp.concatenate(qs, axis=0).astype(BF16)
                doh = jnp.concatenate(dos, axis=0).astype(BF16)
                lseh = jnp.concatenate(lses, axis=0)
                delh = jnp.concatenate(deltas, axis=0)
                s = jnp.where(valid, _dot_nt(qh, kh) * scale, NEG)
                p = jnp.exp(s - lseh)
                dp = _dot_nt(doh, vh)
                ds = (p * (dp - delh)).astype(BF16)
                dq = _dot(ds, kh) * scale
                dk_acc[hk] += _dot_tn(ds, qh) * scale
                dv_acc[hk] += _dot_tn(p.astype(BF16), doh)
                for g in range(ATT_GROUP):
                    hq = hk * ATT_GROUP + g
                    dq_ref[:, hq * HEAD_DIM:(hq + 1) * HEAD_DIM] = _unrope(dq[g * BLK:(g + 1) * BLK], c1[...], s1[...])
                    psink = jnp.exp(sink_ref[hq] - lses[g])
                    dsink_acc = dsink_acc + jnp.where(lane == hq, -jnp.sum(psink * deltas[g]), 0.0)
            dsink_ref[...] += dsink_acc

        c_out = jnp.where(n < nb, c0[...], c1[...])
        s_out = jnp.where(n < nb, s0[...], s1[...])
        for hk in range(ATT_KV_HEADS):
            ks = slice(hk * HEAD_DIM, (hk + 1) * HEAD_DIM)
            dk_ref[:, ks] = _unrope(dk_acc[hk, 0:BLK, :], c_out, s_out)
            dv_ref[:, ks] = dv_acc[hk, 0:BLK, :]
            for acc in (dk_acc, dv_acc):
                acc[hk, 0:BLK, :] = acc[hk, BLK:2 * BLK, :]
                acc[hk, BLK:2 * BLK, :] = acc[hk, 2 * BLK:3 * BLK, :]
                acc[hk, 2 * BLK:3 * BLK, :] = jnp.zeros((BLK, HEAD_DIM), F32)

    own = lambda n: jnp.minimum(n, nb - 1)
    done = lambda n: jnp.maximum(n - 1, 0)

    def nbr(width, col):
        return [pl.BlockSpec((BLK, width), lambda n, c=col: (jnp.maximum(own(n) - 1, 0), c)),
                pl.BlockSpec((BLK, width), lambda n, c=col: (own(n), c)),
                pl.BlockSpec((BLK, width), lambda n, c=col: (jnp.minimum(own(n) + 1, nb - 1), c))]

    in_specs = ([pl.BlockSpec(memory_space=pltpu.SMEM),
                 pl.BlockSpec((BLK, ATT_WIDTH), lambda n: (own(n), P_QA // ATT_WIDTH))]
                + nbr(KV_WIDTH, P_KA // KV_WIDTH) + nbr(KV_WIDTH, P_VA // KV_WIDTH)
                + [pl.BlockSpec((BLK, ATT_WIDTH), lambda n: (own(n), 0)),
                   pl.BlockSpec((BLK, ATT_WIDTH), lambda n: (own(n), 0)),
                   pl.BlockSpec((BLK, LANES), lambda n: (own(n), 0))]
                + nbr(HEAD_DIM, 0) + nbr(HEAD_DIM, 0))
    args = [sink, proj] + [proj] * 6 + [y, dy, lse] + [cos2] * 3 + [sin2] * 3
    return _pcall(body, name=name,
                  out_shape=(jax.ShapeDtypeStruct((S, ATT_WIDTH), F32), jax.ShapeDtypeStruct((S, KV_WIDTH), F32),
                             jax.ShapeDtypeStruct((S, KV_WIDTH), F32), jax.ShapeDtypeStruct((1, LANES), F32)),
                  in_specs=in_specs,
                  out_specs=(pl.BlockSpec((BLK, ATT_WIDTH), lambda n: (own(n), 0)),
                             pl.BlockSpec((BLK, KV_WIDTH), lambda n: (done(n), 0)),
                             pl.BlockSpec((BLK, KV_WIDTH), lambda n: (done(n), 0)),
                             pl.BlockSpec((1, LANES), lambda n: (0, 0))),
                  grid=(nb + 1,),
                  scratch=[pltpu.VMEM((ATT_KV_HEADS, 3 * BLK, HEAD_DIM), F32),
                           pltpu.VMEM((ATT_KV_HEADS, 3 * BLK, HEAD_DIM), F32)],
                  sem=("arbitrary",))(*args)


CONV_HALO = 8
CONV_COLS = 512


def _halo_specs(tm, nrow, col_of):
    hb = tm // CONV_HALO
    return [pl.BlockSpec((CONV_HALO, CONV_COLS), lambda i, j: (jnp.maximum(i * hb - 1, 0), col_of(j))),
            pl.BlockSpec((tm, CONV_COLS), lambda i, j: (i, col_of(j))),
            pl.BlockSpec((CONV_HALO, CONV_COLS),
                         lambda i, j: (jnp.minimum((i + 1) * hb, nrow * hb - 1), col_of(j)))]


def _with_halo(prev, cur, nxt, i, nrow):
    p = jnp.where(i > 0, prev[...], 0.0)
    q = jnp.where(i < nrow - 1, nxt[...], 0.0)
    return jnp.concatenate([p, cur[...], q], axis=0)


def _conv_taps(xt, w_ref, tm):
    n = xt.shape[0]
    acc = jnp.zeros_like(xt)
    for j in range(CONV_WIDTH):
        sh = (CONV_WIDTH // 2 - j) % n
        xs = xt if sh == 0 else pltpu.roll(xt, sh, 0)
        acc = acc + xs * w_ref[j:j + 1, :]
    return acc


def conv_fwd(proj, conv_w, *, name):
    S = proj.shape[0]
    tm = _tile(S, (512, 256, 128))
    nrow = S // tm

    def body(xp, xc, xn, w_ref, o_ref):
        i = pl.program_id(0)
        xt = _with_halo(xp, xc, xn, i, nrow)
        pre = _conv_taps(xt, w_ref, tm)[CONV_HALO:CONV_HALO + tm]
        o_ref[...] = pre * _sigmoid(pre)

    return _pcall(body, name=name, out_shape=jax.ShapeDtypeStruct((S, 2 * M_WIDTH), F32),
                  in_specs=_halo_specs(tm, nrow, lambda j: P_QK // CONV_COLS + j)
                  + [pl.BlockSpec((CONV_HALO, CONV_COLS), lambda i, j: (0, j))],
                  out_specs=pl.BlockSpec((tm, CONV_COLS), lambda i, j: (i, j)),
                  grid=(nrow, 2 * M_WIDTH // CONV_COLS), sem=("parallel", "parallel"))(proj, proj, proj, conv_w)


def conv_bwd(proj, conv_w, da, db, *, name):
    S = proj.shape[0]
    tm = _tile(S, (512, 256, 128))
    nrow = S // tm

    def body(xp, xc, xn, ap, ac, an, bp, bc, bn, w_ref, dx_ref, dw_ref):
        i = pl.program_id(1)

        @pl.when(i == 0)
        def _():
            dw_ref[...] = jnp.zeros_like(dw_ref)

        xt = _with_halo(xp, xc, xn, i, nrow)
        dt = _with_halo(ap, ac, an, i, nrow) + _with_halo(bp, bc, bn, i, nrow)
        pre = _conv_taps(xt, w_ref, tm)
        sg = _sigmoid(pre)
        dpre = dt * (sg * (1.0 + pre * (1.0 - sg)))
        n = xt.shape[0]
        ridx = lax.broadcasted_iota(jnp.int32, (n, 1), 0)
        dpre = jnp.where((ridx >= 2) & (ridx < n - 2), dpre, 0.0)
        dx = jnp.zeros_like(xt)
        own = (ridx >= CONV_HALO) & (ridx < CONV_HALO + tm)
        dpre_own = jnp.where(own, dpre, 0.0)
        dw_rows = []
        for j in range(CONV_WIDTH):
            sh = (j - CONV_WIDTH // 2) % n
            ds_ = dpre if sh == 0 else pltpu.roll(dpre, sh, 0)
            dx = dx + ds_ * w_ref[j:j + 1, :]
            shx = (CONV_WIDTH // 2 - j) % n
            xs = xt if shx == 0 else pltpu.roll(xt, shx, 0)
            dw_rows.append(jnp.sum(dpre_own * xs, axis=0, keepdims=True))
        dx_ref[...] = dx[CONV_HALO:CONV_HALO + tm]
        dw_rows.append(jnp.zeros((CONV_HALO - CONV_WIDTH, CONV_COLS), F32))
        dw_ref[...] += jnp.concatenate(dw_rows, axis=0)

    colq = lambda j: P_QK // CONV_COLS + j
    same = lambda j: j

    def swap(specs):
        return [pl.BlockSpec(s.block_shape, (lambda f: (lambda j, i: f(i, j)))(s.index_map)) for s in specs]

    in_specs = swap(_halo_specs(tm, nrow, colq) + _halo_specs(tm, nrow, same) + _halo_specs(tm, nrow, same)
                    + [pl.BlockSpec((CONV_HALO, CONV_COLS), lambda i, j: (0, j))])
    return _pcall(body, name=name,
                  out_shape=(jax.ShapeDtypeStruct((S, 2 * M_WIDTH), F32),
                             jax.ShapeDtypeStruct((CONV_HALO, 2 * M_WIDTH), F32)),
                  in_specs=in_specs,
                  out_specs=(pl.BlockSpec((tm, CONV_COLS), lambda j, i: (i, j)),
                             pl.BlockSpec((CONV_HALO, CONV_COLS), lambda j, i: (0, j))),
                  grid=(2 * M_WIDTH // CONV_COLS, nrow), sem=("parallel", "arbitrary"))(
                      proj, proj, proj, da, da, da, db, db, db, conv_w)


def _log_sigmoid(x):
    return jnp.minimum(x, 0.0) - jnp.log(1.0 + jnp.exp(-jnp.abs(x)))


def _scan_sum(x, axis, from_end):
    idx = lax.broadcasted_iota(jnp.int32, x.shape, axis)
    n = x.shape[axis]
    sh = 1
    while sh < n:
        if from_end:
            x = x + jnp.where(idx < n - sh, pltpu.roll(x, n - sh, axis), 0.0)
        else:
            x = x + jnp.where(idx >= sh, pltpu.roll(x, sh, axis), 0.0)
        sh *= 2
    return x


def _gate_setup(gc_ref, gr_ref, bgc_ref, bgr_ref, reverse):
    gc = gc_ref[...] + bgc_ref[...]
    gr = gr_ref[...] + bgr_ref[...]
    bc = _scan_sum(_log_sigmoid(gc), 0, reverse)
    br = _scan_sum(_log_sigmoid(gr), 1, reverse)
    return gc, gr, bc, br


def _head_gates(gc, gr, bc, br, h, m_in, reverse, tri):
    io = (M_HEADS if reverse else 0) + h
    fo = (3 * M_HEADS if reverse else 2 * M_HEADS) + h
    last = 0 if reverse else BLK - 1
    b_col, b_row = bc[:, fo:fo + 1], br[fo:fo + 1, :]
    ig_col, ig_row = gc[:, io:io + 1], gr[io:io + 1, :]
    logd = jnp.where(tri, b_col - b_row + ig_row, NEG)
    m_t = jnp.maximum(b_col + m_in, jnp.max(logd, axis=1, keepdims=True))
    dm = jnp.exp(logd - m_t)
    gi = jnp.exp(b_col + m_in - m_t)
    b_last = b_row[:, last:last + 1]
    logw = b_last - b_row + ig_row
    m_new = jnp.maximum(b_last + m_in, jnp.max(logw, axis=1, keepdims=True))
    w_col = jnp.exp(b_last - b_col + ig_col - m_new)
    dec = jnp.exp(b_last + m_in - m_new)
    return io, fo, m_t, dm, gi, m_new, w_col, dec


def _tri_mask(reverse):
    rows = lax.broadcasted_iota(jnp.int32, (BLK, BLK), 0)
    cols = lax.broadcasted_iota(jnp.int32, (BLK, BLK), 1)
    return (cols >= rows) if reverse else (cols <= rows)


def mlstm_fwd(qk, proj, gates_r, bg_c, bg_r, *, reverse, name):
    S = qk.shape[0]
    nc = S // BLK
    kscale = M_HEAD_DIM ** -0.5
    cidx = (lambda c: nc - 1 - c) if reverse else (lambda c: c)

    def body(qk_ref, v_ref, gc_ref, gr_ref, bgc_ref, bgr_ref, h_ref, den_ref, cst_ref, nm_ref, c_sc, n_sc, m_sc):
        @pl.when(pl.program_id(0) == 0)
        def _():
            c_sc[...] = jnp.zeros_like(c_sc)
            n_sc[...] = jnp.zeros_like(n_sc)
            m_sc[...] = jnp.zeros_like(m_sc)

        gc, gr, bc, br = _gate_setup(gc_ref, gr_ref, bgc_ref, bgr_ref, reverse)
        tri = _tri_mask(reverse)
        lane = lax.broadcasted_iota(jnp.int32, (BLK, LANES), 1)
        den_tile = jnp.zeros((BLK, LANES), F32)
        for h in range(M_HEADS):
            cs = slice(h * M_HEAD_DIM, (h + 1) * M_HEAD_DIM)
            m_in = m_sc[h][:, 0:1]
            _, _, m_t, dm, gi, m_new, w_col, dec = _head_gates(gc, gr, bc, br, h, m_in, reverse, tri)
            q = qk_ref[:, cs]
            k = qk_ref[:, M_WIDTH + h * M_HEAD_DIM:M_WIDTH + (h + 1) * M_HEAD_DIM] * kscale
            v = v_ref[:, cs]
            c_in, n_in = c_sc[h], n_sc[h]
            cst_ref[h] = c_in
            nm_ref[h, 0:1, :] = n_in
            nm_ref[h, 1:2, :] = m_sc[h]
            qb, kb, vb = q.astype(BF16), k.astype(BF16), v.astype(BF16)
            s = _dot_nt(qb, kb) * dm
            num = _dot(s.astype(BF16), vb) + gi * _dot_nt(qb, c_in.astype(BF16))
            den = jnp.sum(s, axis=1, keepdims=True) + gi * jnp.sum(q * n_in, axis=1, keepdims=True)
            z = jnp.maximum(jnp.abs(den), jnp.exp(-m_t))
            h_ref[:, cs] = num * (1.0 / z)
            den_tile = den_tile + jnp.where(lane == h, den, 0.0)
            c_sc[h] = dec * c_in + _dot_tn((w_col * v).astype(BF16), kb)
            n_sc[h] = dec * n_in + jnp.sum(w_col * k, axis=0, keepdims=True)
            m_sc[h] = jnp.broadcast_to(m_new, (1, M_HEAD_DIM))
        den_ref[...] = den_tile

    return _pcall(
        body, name=name,
        out_shape=(jax.ShapeDtypeStruct((S, M_WIDTH), F32), jax.ShapeDtypeStruct((S, LANES), F32),
                   jax.ShapeDtypeStruct((nc, M_HEADS, M_HEAD_DIM, M_HEAD_DIM), F32),
                   jax.ShapeDtypeStruct((nc, M_HEADS, 2, M_HEAD_DIM), F32)),
        in_specs=[pl.BlockSpec((BLK, 2 * M_WIDTH), lambda c: (cidx(c), 0)),
                  pl.BlockSpec((BLK, M_WIDTH), lambda c: (cidx(c), P_VM // M_WIDTH)),
                  pl.BlockSpec((BLK, LANES), lambda c: (cidx(c), P_G // LANES)),
                  pl.BlockSpec((N_GATES, BLK), lambda c: (0, cidx(c))),
                  pl.BlockSpec((1, LANES), lambda c: (0, 0)),
                  pl.BlockSpec((N_GATES, 1), lambda c: (0, 0))],
        out_specs=(pl.BlockSpec((BLK, M_WIDTH), lambda c: (cidx(c), 0)),
                   pl.BlockSpec((BLK, LANES), lambda c: (cidx(c), 0)),
                   pl.BlockSpec((None, M_HEADS, M_HEAD_DIM, M_HEAD_DIM), lambda c: (cidx(c), 0, 0, 0)),
                   pl.BlockSpec((None, M_HEADS, 2, M_HEAD_DIM), lambda c: (cidx(c), 0, 0, 0))),
        grid=(nc,),
        scratch=[pltpu.VMEM((M_HEADS, M_HEAD_DIM, M_HEAD_DIM), F32), pltpu.VMEM((M_HEADS, 1, M_HEAD_DIM), F32),
                 pltpu.VMEM((M_HEADS, 1, M_HEAD_DIM), F32)],
        sem=("arbitrary",))(qk, proj, proj, gates_r, bg_c, bg_r)


def mlstm_bwd(qk, proj, gates_r, bg_c, bg_r, hdir, den, cst, nm, dh, *, reverse, name, comm=None):
    S = qk.shape[0]
    nc = S // BLK
    kscale = M_HEAD_DIM ** -0.5
    cidx = (lambda c: c) if reverse else (lambda c: nc - 1 - c)
    last = 0 if reverse else BLK - 1

    def body(qk_ref, v_ref, gc_ref, gr_ref, bgc_ref, bgr_ref, h_ref, den_ref, cst_ref, nm_ref, dh_ref,
             dqk_ref, dv_ref, dgc_ref, dgr_ref, dc_sc, dn_sc):
        @pl.when(pl.program_id(0) == 0)
        def _():
            dc_sc[...] = jnp.zeros_like(dc_sc)
            dn_sc[...] = jnp.zeros_like(dn_sc)

        gc, gr, bc, br = _gate_setup(gc_ref, gr_ref, bgc_ref, bgr_ref, reverse)
        tri = _tri_mask(reverse)
        lane_c = lax.broadcasted_iota(jnp.int32, (BLK, LANES), 1)
        row_c = lax.broadcasted_iota(jnp.int32, (BLK, 1), 0)
        row_r = lax.broadcasted_iota(jnp.int32, (N_GATES, BLK), 0)
        db_c = jnp.zeros((BLK, LANES), F32)
        dig_c = jnp.zeros((BLK, LANES), F32)
        db_r = jnp.zeros((N_GATES, BLK), F32)
        dig_r = jnp.zeros((N_GATES, BLK), F32)
        for h in range(M_HEADS):
            cs = slice(h * M_HEAD_DIM, (h + 1) * M_HEAD_DIM)
            ks = slice(M_WIDTH + h * M_HEAD_DIM, M_WIDTH + (h + 1) * M_HEAD_DIM)
            m_in = nm_ref[h, 1:2, 0:1]
            io, fo, m_t, dm, gi, m_new, w_col, dec = _head_gates(gc, gr, bc, br, h, m_in, reverse, tri)
            q = qk_ref[:, cs]
            k = qk_ref[:, ks] * kscale
            v = v_ref[:, cs]
            c_in, n_in = cst_ref[h], nm_ref[h, 0:1, :]
            qb, kb, vb, cb = q.astype(BF16), k.astype(BF16), v.astype(BF16), c_in.astype(BF16)
            s = _dot_nt(qb, kb) * dm
            den_h = den_ref[:, h:h + 1]
            emt = jnp.exp(-m_t)
            rz = 1.0 / jnp.maximum(jnp.abs(den_h), emt)
            dhh = dh_ref[:, cs]
            dnum = dhh * rz
            hdh = jnp.sum(dhh * h_ref[:, cs], axis=1, keepdims=True)
            dden = jnp.where(jnp.abs(den_h) > emt, -hdh * rz * jnp.sign(den_h), 0.0)
            dnb = dnum.astype(BF16)
            ds = _dot_nt(dnb, vb) + dden
            e = ds * s
            dsd = (ds * dm).astype(BF16)
            gd = (gi * dnum).astype(BF16)
            gdd = gi * dden
            dq = _dot(dsd, kb) + _dot(gd, cb) + gdd * n_in
            dk = _dot_tn(dsd, qb)
            dv = _dot_tn(s.astype(BF16), dnb)
            dc_in = _dot_tn(gd, qb)
            dn_in = jnp.sum(gdd * q, axis=0, keepdims=True)
            cq = _dot_nt(qb, cb)
            dg = jnp.sum(dnum * cq, axis=1, keepdims=True) + dden * jnp.sum(q * n_in, axis=1, keepdims=True)
            eg = dg * gi
            dco, dno = dc_sc[h], dn_sc[h]
            dcob = dco.astype(BF16)
            dwv = _dot_nt(kb, dcob)
            dv = dv + w_col * dwv
            dw = jnp.sum(v * dwv, axis=1, keepdims=True) + jnp.sum(k * dno, axis=1, keepdims=True)
            dk = dk + _dot((w_col * v).astype(BF16), dcob) + w_col * dno
            ew = dw * w_col
            ddec = (jnp.sum(jnp.sum(dco * c_in, axis=1, keepdims=True), axis=0, keepdims=True)
                    + jnp.sum(dno * n_in, axis=1, keepdims=True))
            dc_sc[h] = dec * dco + dc_in
            dn_sc[h] = dec * dno + dn_in
            dqk_ref[:, cs] = dq
            dqk_ref[:, ks] = dk * kscale
            dv_ref[:, cs] = dv
            csum = jnp.sum(e, axis=0, keepdims=True)
            db_last = jnp.sum(ew, axis=0, keepdims=True) + ddec * dec
            db_col = jnp.sum(e, axis=1, keepdims=True) + eg - ew + jnp.where(row_c == last, db_last, 0.0)
            db_c = db_c + jnp.where(lane_c == fo, db_col, 0.0)
            dig_c = dig_c + jnp.where(lane_c == io, ew, 0.0)
            db_r = db_r + jnp.where(row_r == fo, -csum, 0.0)
            dig_r = dig_r + jnp.where(row_r == io, csum, 0.0)
        dgc_ref[...] = dig_c + _scan_sum(db_c, 0, not reverse) * _sigmoid(-gc)
        dgr_ref[...] = dig_r + _scan_sum(db_r, 1, not reverse) * _sigmoid(-gr)

    chunk = lambda w, col=0: pl.BlockSpec((BLK, w), lambda c: (cidx(c), col))
    return _pcall(
        body, name=name,
        out_shape=(jax.ShapeDtypeStruct((S, 2 * M_WIDTH), F32), jax.ShapeDtypeStruct((S, M_WIDTH), F32),
                   jax.ShapeDtypeStruct((S, LANES), F32), jax.ShapeDtypeStruct((N_GATES, S), F32)),
        in_specs=[chunk(2 * M_WIDTH), chunk(M_WIDTH, P_VM // M_WIDTH), chunk(LANES, P_G // LANES),
                  pl.BlockSpec((N_GATES, BLK), lambda c: (0, cidx(c))),
                  pl.BlockSpec((1, LANES), lambda c: (0, 0)),
                  pl.BlockSpec((N_GATES, 1), lambda c: (0, 0)),
                  chunk(M_WIDTH), chunk(LANES),
                  pl.BlockSpec((None, M_HEADS, M_HEAD_DIM, M_HEAD_DIM), lambda c: (cidx(c), 0, 0, 0)),
                  pl.BlockSpec((None, M_HEADS, 2, M_HEAD_DIM), lambda c: (cidx(c), 0, 0, 0)),
                  chunk(M_WIDTH)],
        out_specs=(chunk(2 * M_WIDTH), chunk(M_WIDTH), chunk(LANES),
                   pl.BlockSpec((N_GATES, BLK), lambda c: (0, cidx(c)))),
        grid=(nc,),
        scratch=[pltpu.VMEM((M_HEADS, M_HEAD_DIM, M_HEAD_DIM), F32), pltpu.VMEM((M_HEADS, 1, M_HEAD_DIM), F32)],
        sem=("arbitrary",), comm=comm)(qk, proj, proj, gates_r, bg_c, bg_r, hdir, den, cst, nm, dh)


def headnorm_fwd(hf, hb, proj, mnorm, *, name):
    S = hf.shape[0]
    tm = _tile(S, (512, 256, 128))

    def body(hf_ref, hb_ref, om_ref, mn_ref, y_ref):
        for h in range(M_HEADS):
            cs = slice(h * M_HEAD_DIM, (h + 1) * M_HEAD_DIM)
            hm = hf_ref[:, cs] + hb_ref[:, cs]
            r = lax.rsqrt(jnp.mean(hm * hm, axis=-1, keepdims=True) + EPS)
            y_ref[:, cs] = (_sigmoid(om_ref[:, cs]) * ((hm * r) * mn_ref[:, cs])).astype(y_ref.dtype)

    row = pl.BlockSpec((tm, M_WIDTH), lambda i: (i, 0))
    return _pcall(body, name=name, out_shape=jax.ShapeDtypeStruct((S, M_WIDTH), BF16),
                  in_specs=[row, row, pl.BlockSpec((tm, M_WIDTH), lambda i: (i, P_OM // M_WIDTH)),
                            pl.BlockSpec((1, M_WIDTH), lambda i: (0, 0))],
                  out_specs=row, grid=(S // tm,), sem=("parallel",))(hf, hb, proj, mnorm)


def headnorm_bwd(hf, hb, proj, mnorm, dy, *, name):
    S = hf.shape[0]
    tm = _tile(S, (512, 256, 128))

    def body(hf_ref, hb_ref, om_ref, mn_ref, dy_ref, dh_ref, dom_ref, dmn_ref):
        @pl.when(pl.program_id(0) == 0)
        def _():
            dmn_ref[...] = jnp.zeros_like(dmn_ref)

        for h in range(M_HEADS):
            cs = slice(h * M_HEAD_DIM, (h + 1) * M_HEAD_DIM)
            hm = hf_ref[:, cs] + hb_ref[:, cs]
            r = lax.rsqrt(jnp.mean(hm * hm, axis=-1, keepdims=True) + EPS)
            xh = hm * r
            so = _sigmoid(om_ref[:, cs])
            d = dy_ref[:, cs]
            mn = mn_ref[:, cs]
            dom_ref[:, cs] = d * (xh * mn) * (so * (1.0 - so))
            dxm = d * so
            dmn_ref[:, cs] += jnp.sum(dxm * xh, axis=0, keepdims=True)
            dxh = dxm * mn
            dh_ref[:, cs] = r * (dxh - xh * jnp.mean(dxh * xh, axis=-1, keepdims=True))

    row = pl.BlockSpec((tm, M_WIDTH), lambda i: (i, 0))
    vec = pl.BlockSpec((1, M_WIDTH), lambda i: (0, 0))
    return _pcall(body, name=name,
                  out_shape=(jax.ShapeDtypeStruct((S, M_WIDTH), F32), jax.ShapeDtypeStruct((S, M_WIDTH), F32),
                             jax.ShapeDtypeStruct((1, M_WIDTH), F32)),
                  in_specs=[row, row, pl.BlockSpec((tm, M_WIDTH), lambda i: (i, P_OM // M_WIDTH)), vec,
                            pl.BlockSpec((tm, M_WIDTH), lambda i: (i, 1))],
                  out_specs=(row, row, vec), grid=(S // tm,), sem=("arbitrary",))(hf, hb, proj, mnorm, dy)


def _place():
    return lax.axis_index("x"), lax.axis_index("y"), lax.axis_index("c")


def _ag_plan(x_refs, out_refs, sems):
    send_sems, recv_sems, local_sems = sems
    T = len(x_refs)
    x, y, c = _place()
    me, sibling = (x, y, c), (x, y, 1 - c)
    chips = [(1 - x, y), (x, 1 - y), (1 - x, 1 - y)]

    def copy(t, k, block, to, src=None):
        px, py, pc = block
        dst = out_refs[t].at[4 * px + 2 * py + pc]
        return pltpu.make_async_remote_copy(
            src_ref=dst if src is None else src, dst_ref=dst, send_sem=send_sems.at[7 * t + k],
            recv_sem=recv_sems.at[7 * t + k], device_id=to, device_id_type=MESH)

    mine = [pltpu.make_async_copy(x_refs[t], out_refs[t].at[4 * x + 2 * y + c], local_sems.at[t]) for t in range(T)]
    first = []
    for t in range(T):
        first.append(copy(t, 0, me, sibling, src=x_refs[t]))
        first += [copy(t, 1 + j, me, (*chip, c), src=x_refs[t]) for j, chip in enumerate(chips)]
    landed = [copy(t, 1 + j, (*chip, c), me) for j, chip in enumerate(chips) for t in range(T)]
    passed = [copy(t, 4 + j, (*chip, c), sibling) for j, chip in enumerate(chips) for t in range(T)]
    from_sibling = [copy(t, 0, sibling, me) for t in range(T)]
    from_sibling += [copy(t, 4 + j, (*chip, 1 - c), me) for j, chip in enumerate(chips) for t in range(T)]
    return mine, first, landed, passed, from_sibling


def _ag_start(x_refs, out_refs, sems):
    mine, first, _, _, _ = _ag_plan(x_refs, out_refs, sems)
    for cp in mine + first:
        cp.start()


def _ag_forward(x_refs, out_refs, sems):
    _, _, landed, passed, _ = _ag_plan(x_refs, out_refs, sems)
    for got, on in zip(landed, passed):
        got.wait_recv()
        on.start()


def _ag_finish(x_refs, out_refs, sems):
    mine, first, _, passed, from_sibling = _ag_plan(x_refs, out_refs, sems)
    for cp in from_sibling:
        cp.wait_recv()
    for cp in first + passed:
        cp.wait_send()
    for cp in mine:
        cp.wait()


def ag_comm(shards):
    T = len(shards)
    return Comm(shards, [jax.ShapeDtypeStruct((N_DEV,) + s.shape, s.dtype) for s in shards],
                [pltpu.SemaphoreType.DMA((7 * T,)), pltpu.SemaphoreType.DMA((7 * T,)), pltpu.SemaphoreType.DMA((T,))],
                [_ag_start, _ag_forward, _ag_finish])


def _pair_plan(g_refs, out_refs, sems):
    send_sems, recv_sems = sems
    x, y, c = _place()
    return [pltpu.make_async_remote_copy(
        src_ref=g_refs[t].at[1 - c], dst_ref=out_refs[t], send_sem=send_sems.at[t], recv_sem=recv_sems.at[t],
        device_id=(x, y, 1 - c), device_id_type=MESH) for t in range(len(g_refs))]


def _pair_start(g_refs, out_refs, sems):
    for cp in _pair_plan(g_refs, out_refs, sems):
        cp.start()


def _pair_finish(g_refs, out_refs, sems):
    for cp in _pair_plan(g_refs, out_refs, sems):
        cp.wait()


def pair_comm(grads):
    T = len(grads)
    return Comm(grads, [jax.ShapeDtypeStruct(g.shape[1:], g.dtype) for g in grads],
                [pltpu.SemaphoreType.DMA((T,)), pltpu.SemaphoreType.DMA((T,))], [_pair_start, _pair_finish])


def _chip_plan(p_refs, out_refs, sems):
    send_sems, recv_sems, local_sems = sems
    T = len(p_refs)
    x, y, c = _place()
    mychip = 2 * x + y
    chips = [(1 - x, y), (x, 1 - y), (1 - x, 1 - y)]
    mine = [pltpu.make_async_copy(p_refs[t].at[mychip], out_refs[t].at[mychip], local_sems.at[t]) for t in range(T)]
    cps = [pltpu.make_async_remote_copy(
        src_ref=p_refs[t].at[2 * px + py], dst_ref=out_refs[t].at[mychip], send_sem=send_sems.at[3 * t + j],
        recv_sem=recv_sems.at[3 * t + j], device_id=(px, py, c), device_id_type=MESH)
        for t in range(T) for j, (px, py) in enumerate(chips)]
    return mine, cps


def _chip_start(p_refs, out_refs, sems):
    mine, cps = _chip_plan(p_refs, out_refs, sems)
    for cp in mine + cps:
        cp.start()


def _chip_finish(p_refs, out_refs, sems):
    mine, cps = _chip_plan(p_refs, out_refs, sems)
    for cp in cps + mine:
        cp.wait()


def chip_comm(parts):
    T = len(parts)
    return Comm(parts, [jax.ShapeDtypeStruct(p.shape, p.dtype) for p in parts],
                [pltpu.SemaphoreType.DMA((3 * T,)), pltpu.SemaphoreType.DMA((3 * T,)), pltpu.SemaphoreType.DMA((T,))],
                [_chip_start, _chip_finish])


def pair_add(g, recv, core, *, name):
    _, nchip, R, C = g.shape
    tr = _tile(R, (512, 256, 128, 64))

    def body(c_ref, a_ref, b_ref, o_ref):
        o_ref[...] = (a_ref[...].astype(F32) + b_ref[...].astype(F32)).astype(o_ref.dtype)

    grid_spec = pltpu.PrefetchScalarGridSpec(
        num_scalar_prefetch=1, grid=(nchip, R // tr),
        in_specs=[pl.BlockSpec((None, None, tr, C), lambda k, i, c_ref: (c_ref[0], k, i, 0)),
                  pl.BlockSpec((None, tr, C), lambda k, i, c_ref: (k, i, 0))],
        out_specs=pl.BlockSpec((None, tr, C), lambda k, i, c_ref: (k, i, 0)))
    return pl.pallas_call(body, name=name, out_shape=jax.ShapeDtypeStruct(recv.shape, recv.dtype),
                          grid_spec=grid_spec,
                          compiler_params=pltpu.CompilerParams(dimension_semantics=("parallel", "parallel")))(
                              core, g, recv)


def _adam_math(w, g, m, v):
    m = ADAM_B1 * m + (1.0 - ADAM_B1) * g
    v = ADAM_B2 * v + (1.0 - ADAM_B2) * (g * g)
    m_hat = m / (1.0 - ADAM_B1 ** ADAM_STEP)
    v_hat = v / (1.0 - ADAM_B2 ** ADAM_STEP)
    delta = -ADAM_LR * (m_hat / (jnp.sqrt(v_hat) + ADAM_EPS) + ADAM_WD * w)
    return delta, m, v


def adam_update(w, parts, m, v, *, name):
    P, R, _ = parts.shape
    tr = _tile(R, (1024, 512, 256, 128, 64, 32, 16, 8))

    def body(w_ref, p_ref, m_ref, v_ref, g_ref, d_ref, nm_ref, nv_ref):
        g = p_ref[0]
        for k in range(1, P):
            g = g + p_ref[k]
        d, nm, nv = _adam_math(w_ref[...], g, m_ref[...], v_ref[...])
        g_ref[...] = g
        d_ref[...] = d
        nm_ref[...] = nm
        nv_ref[...] = nv

    row = pl.BlockSpec((tr, LANES), lambda i: (i, 0))
    shp = jax.ShapeDtypeStruct((R, LANES), F32)
    return _pcall(body, name=name, out_shape=(shp, shp, shp, shp),
                  in_specs=[row, pl.BlockSpec((P, tr, LANES), lambda i: (0, i, 0)), row, row],
                  out_specs=(row, row, row, row), grid=(R // tr,), sem=("parallel",))(w, parts, m, v)


ADAM_STEP_BYTES = 6 * 1024 * 1024


def adam_tensor(w, parts, m, v, *, name):
    L, R, C = w.shape
    per_row = L * C * (7 * 4 + 4 * parts[0].dtype.itemsize)
    tr = R
    for cand in (256, 128, 64, 32, 16):
        if R % cand == 0 and cand * per_row <= ADAM_STEP_BYTES:
            tr = cand
            break

    def body(*refs):
        w_ref, m_ref, v_ref = refs[:3]
        p_refs = refs[3:3 + L]
        g_ref, d_ref, nm_ref, nv_ref = refs[3 + L:]
        for l in range(L):
            g = p_refs[l][0].astype(F32)
            for k in range(1, 4):
                g = g + p_refs[l][k].astype(F32)
            d, nm, nv = _adam_math(w_ref[l], g, m_ref[l], v_ref[l])
            g_ref[l] = g
            d_ref[l] = d
            nm_ref[l] = nm
            nv_ref[l] = nv

    blk = pl.BlockSpec((L, tr, C), lambda i: (0, i, 0))
    pblk = pl.BlockSpec((4, tr, C), lambda i: (0, i, 0))
    shp = jax.ShapeDtypeStruct((L, R, C), F32)
    return _pcall(body, name=name, out_shape=(shp, shp, shp, shp), in_specs=[blk, blk, blk] + [pblk] * L,
                  out_specs=(blk, blk, blk, blk), grid=(R // tr,), sem=("parallel",))(w, m, v, *parts)


def _rows(n_elems):
    r = -(-n_elems // LANES)
    return -(-r // 1024) * 1024 if r > 1024 else -(-r // 16) * 16


def _flat(a, dtype=None):
    n = a.size
    r = _rows(n)
    f = a.reshape(-1)
    if dtype is not None:
        f = f.astype(dtype)
    if r * LANES != n:
        f = jnp.pad(f, (0, r * LANES - n))
    return f.reshape(r, LANES)


def _gathered_cols(g):
    n, rows, cols = g.shape
    return g.transpose(1, 0, 2).reshape(rows, n * cols)


def _owner_cols(dw, dtype):
    rows = dw.shape[0]
    cols = dw.shape[1] // N_DEV
    return dw.reshape(rows, N_DEV // 2, 2, cols).transpose(2, 1, 0, 3).astype(dtype)


_IN_NAT = dict(qa=(0, 1024), ka=(1024, 1280), va=(1280, 1536), qm=(1536, 2560), km=(2560, 3584),
               vm=(3584, 4608), om=(4608, 5632), g=(5632, 5648))


def _permute_w_in(w):
    sl = lambda k: w[:, _IN_NAT[k][0]:_IN_NAT[k][1]]
    pad = jnp.zeros((w.shape[0], P_WIDTH - P_G - N_GATES), w.dtype)
    return jnp.concatenate([sl("qm"), sl("km"), sl("qa"), sl("vm"), sl("om"), sl("ka"), sl("va"), sl("g"), pad],
                           axis=1)


def _unpermute_dw_in(dw):
    qm, km = dw[:, P_QK:P_QK + 1024], dw[:, P_QK + 1024:P_QK + 2048]
    return jnp.concatenate([dw[:, P_QA:P_QA + 1024], dw[:, P_KA:P_KA + 256], dw[:, P_VA:P_VA + 256], qm, km,
                            dw[:, P_VM:P_VM + 1024], dw[:, P_OM:P_OM + 1024], dw[:, P_G:P_G + N_GATES]], axis=1)


BIG = ("ffn1_w_gate", "ffn1_w_up", "ffn1_w_down", "w_in", "w_out", "ffn2_w_gate", "ffn2_w_up", "ffn2_w_down")
SMALL = ("ffn1_norm_pre", "ffn1_norm_post", "mix_norm_pre", "mix_norm_post", "b_gate", "attn_sink", "mlstm_norm",
         "ffn2_norm_pre", "ffn2_norm_post")
WEIGHTS = ("ffn1_norm_pre", "ffn1_norm_post", "ffn1_w_gate", "ffn1_w_up", "ffn1_w_down", "mix_norm_pre",
           "mix_norm_post", "w_in", "b_gate", "conv_w", "attn_sink", "mlstm_norm", "w_out", "ffn2_norm_pre",
           "ffn2_norm_post", "ffn2_w_gate", "ffn2_w_up", "ffn2_w_down")


GRAD_DT = BF16


def _carried(result, comm):
    return result if comm is not None else (result, None)


def _pair_adds(grads, recv, core, tag):
    return [pair_add(g, r, core, name=f"{tag}_add{t}") for t, (g, r) in enumerate(zip(grads, recv))]


def _ffn_fwd(x, g_pre, g_post, wg8, wu8, wd8, tag, gather=None):
    xn = norm_fwd(x, g_pre, name=f"{tag}_pre", out_dtype=BF16)
    comm = None if gather is None else ag_comm(gather)
    (hg, hu, act), gathered = _carried(ffn_gu(xn, wg8, wu8, name=f"{tag}_gu", comm=comm), comm)
    f = ffn_down(act, wd8, name=f"{tag}_down")
    x_new = norm_fwd(f, g_post, name=f"{tag}_post", scale=0.5, resid=x)
    return x_new, (x, xn, hg, hu, act, f), gathered


def _ffn_bwd(dx, saved, g_pre, g_post, wg8, wu8, wd8, core, tag, reduce=None):
    x, xn, hg, hu, act, f = saved
    df, dg_post = norm_bwd(dx, f, g_post, name=f"{tag}_post_b", scale=0.5, out_dtype=BF16)
    comm = None if reduce is None else pair_comm(reduce)
    dwd, recv = _carried(ffn_dwd(act, df, name=f"{tag}_dwd", out_dtype=GRAD_DT, comm=comm), comm)
    dhg, dhu = ffn_dact(df, wd8, hg, hu, name=f"{tag}_dact")
    comm = None if reduce is None else chip_comm(_pair_adds(reduce, recv, core, tag))
    (dwg, dwu), reduced = _carried(ffn_dwgu(xn, dhg, dhu, name=f"{tag}_dwgu", out_dtype=GRAD_DT, comm=comm), comm)
    dxn = ffn_dxn(dhg, dhu, wg8, wu8, name=f"{tag}_dxn")
    dx_new, dg_pre = norm_bwd(dxn, x, g_pre, name=f"{tag}_pre_b", resid=dx)
    return dx_new, dg_pre, dg_post, [dwg, dwu, dwd], reduced


def _mix_fwd(x, g_pre, g_post, w_in_p, b_gate, conv_full, sink, mnorm, w_out, cos2, sin2, tag, gather_in, gather_out):
    S = x.shape[0]
    xn = norm_fwd(x, g_pre, name=f"{tag}_pre", out_dtype=BF16)
    comm = ag_comm(gather_in)
    proj, got_in = mm_nn(xn, w_in_p, name=f"{tag}_in", comm=comm)
    gates_r = lax.optimization_barrier(proj[:, P_G:P_G + N_GATES]).T
    bg_c = jnp.pad(b_gate, (0, LANES - N_GATES)).reshape(1, LANES)
    bg_r = b_gate.reshape(N_GATES, 1)
    y_att, lse = attn_fwd(proj, cos2, sin2, sink, name=f"{tag}_att")
    qk = conv_fwd(proj, conv_full, name=f"{tag}_conv")
    hf, denf, cf, nmf = mlstm_fwd(qk, proj, gates_r, bg_c, bg_r, reverse=False, name=f"{tag}_mf")
    hb, denb, cb, nmb = mlstm_fwd(qk, proj, gates_r, bg_c, bg_r, reverse=True, name=f"{tag}_mb")
    y_m = headnorm_fwd(hf, hb, proj, mnorm.reshape(1, M_WIDTH), name=f"{tag}_hn")
    y = jnp.concatenate([y_att, y_m], axis=1)
    mo, got_out = mm_nn(y, w_out, name=f"{tag}_out", comm=ag_comm(gather_out))
    x_new = norm_fwd(mo, g_post, name=f"{tag}_post", resid=x)
    saved = (x, xn, proj, gates_r, bg_c, bg_r, lse, qk, hf, denf, cf, nmf, hb, denb, cb, nmb, y, mo)
    return x_new, saved, got_in + got_out


def _mix_bwd(dx, saved, g_pre, g_post, w_in_p, conv_full, sink, mnorm, w_out, cos2, sin2, core, tag, reduce=None):
    x, xn, proj, gates_r, bg_c, bg_r, lse, qk, hf, denf, cf, nmf, hb, denb, cb, nmb, y, mo = saved
    S = x.shape[0]
    dmo, dg_post = norm_bwd(dx, mo, g_post, name=f"{tag}_post_b", out_dtype=BF16)
    comm = None if reduce is None else pair_comm(reduce)
    dw_out, recv = _carried(mm_tn(y, dmo, name=f"{tag}_dwo", owner_rows=D_MODEL // N_DEV, out_dtype=GRAD_DT,
                                  comm=comm), comm)
    dy = mm_nt(dmo, w_out, name=f"{tag}_dy")
    mn = mnorm.reshape(1, M_WIDTH)
    dh, dom, dmn = headnorm_bwd(hf, hb, proj, mn, dy, name=f"{tag}_hn_b")
    comm = None if reduce is None else chip_comm(_pair_adds(reduce, recv, core, tag))
    (dqk_f, dv_f, dgc_f, dgr_f), reduced = _carried(
        mlstm_bwd(qk, proj, gates_r, bg_c, bg_r, hf, denf, cf, nmf, dh, reverse=False, name=f"{tag}_mf_b",
                  comm=comm), comm)
    dqk_b, dv_b, dgc_b, dgr_b = mlstm_bwd(qk, proj, gates_r, bg_c, bg_r, hb, denb, cb, nmb, dh,
                                           reverse=True, name=f"{tag}_mb_b")
    dqk_in, dconv = conv_bwd(proj, conv_full, dqk_f, dqk_b, name=f"{tag}_conv_b")
    dqa, dka, dva, dsink = attn_bwd(proj, y, dy, lse, cos2, sin2, sink, name=f"{tag}_att_b")
    dgates = dgc_f + dgc_b + jnp.pad((dgr_f + dgr_b).T, ((0, 0), (0, LANES - N_GATES)))
    dproj = jnp.concatenate([dqk_in.astype(BF16), dqa.astype(BF16), (dv_f + dv_b).astype(BF16), dom.astype(BF16),
                             dka.astype(BF16), dva.astype(BF16), dgates.astype(BF16),
                             jnp.zeros((S, P_WIDTH - P_G - LANES), BF16)], axis=1)
    db_gate = colsum(dgates, name=f"{tag}_dbg")[0, :N_GATES]
    dw_in = mm_tn(xn, dproj, name=f"{tag}_dwi")
    dxn = mm_nt(dproj, w_in_p, name=f"{tag}_dxn")
    dx_new, dg_pre = norm_bwd(dxn, x, g_pre, name=f"{tag}_pre_b", resid=dx)
    grads = [_owner_cols(_unpermute_dw_in(dw_in), GRAD_DT), dw_out, _owner_cols(dconv[:CONV_WIDTH], F32)]
    return dx_new, dg_pre, dg_post, db_gate, dsink[0, :ATT_HEADS], dmn[0], grads, reduced


def colsum(a, *, name):
    S, C = a.shape
    tm = _tile(S, (512, 256, 128))

    def body(a_ref, o_ref):
        @pl.when(pl.program_id(0) == 0)
        def _():
            o_ref[...] = jnp.zeros_like(o_ref)

        o_ref[...] += jnp.sum(a_ref[...], axis=0, keepdims=True)

    return _pcall(body, name=name, out_shape=jax.ShapeDtypeStruct((1, C), F32),
                  in_specs=[pl.BlockSpec((tm, C), lambda i: (i, 0))], out_specs=pl.BlockSpec((1, C), lambda i: (0, 0)),
                  grid=(S // tm,), sem=("arbitrary",))(a)


def _layer_shards(W, l):
    pad_c = lambda a: jnp.pad(a.astype(BF16), ((0, 0), (0, FSP - FS)))
    pad_r = lambda a: jnp.pad(a.astype(BF16), ((0, FSP - FS), (0, 0)))
    return [pad_c(W["ffn1_w_gate"][l]), pad_c(W["ffn1_w_up"][l]), pad_r(W["ffn1_w_down"][l]),
            W["w_in"][l].astype(BF16), W["w_out"][l].astype(BF16),
            pad_c(W["ffn2_w_gate"][l]), pad_c(W["ffn2_w_up"][l]), pad_r(W["ffn2_w_down"][l])]


def kernel(x, ffn1_norm_pre, ffn1_norm_post, ffn1_w_gate, ffn1_w_up, ffn1_w_down, mix_norm_pre, mix_norm_post, w_in, b_gate, conv_w, attn_sink, mlstm_norm, w_out, ffn2_norm_pre, ffn2_norm_post, ffn2_w_gate, ffn2_w_up, ffn2_w_down, loss_target, m_ffn1_norm_pre, m_ffn1_norm_post, m_ffn1_w_gate, m_ffn1_w_up, m_ffn1_w_down, m_mix_norm_pre, m_mix_norm_post, m_w_in, m_b_gate, m_conv_w, m_attn_sink, m_mlstm_norm, m_w_out, m_ffn2_norm_pre, m_ffn2_norm_post, m_ffn2_w_gate, m_ffn2_w_up, m_ffn2_w_down, v_ffn1_norm_pre, v_ffn1_norm_post, v_ffn1_w_gate, v_ffn1_w_up, v_ffn1_w_down, v_mix_norm_pre, v_mix_norm_post, v_w_in, v_b_gate, v_conv_w, v_attn_sink, v_mlstm_norm, v_w_out, v_ffn2_norm_pre, v_ffn2_norm_post, v_ffn2_w_gate, v_ffn2_w_up, v_ffn2_w_down):
    W = dict(ffn1_norm_pre=ffn1_norm_pre, ffn1_norm_post=ffn1_norm_post, ffn1_w_gate=ffn1_w_gate,
             ffn1_w_up=ffn1_w_up, ffn1_w_down=ffn1_w_down, mix_norm_pre=mix_norm_pre, mix_norm_post=mix_norm_post,
             w_in=w_in, b_gate=b_gate, conv_w=conv_w, attn_sink=attn_sink, mlstm_norm=mlstm_norm, w_out=w_out,
             ffn2_norm_pre=ffn2_norm_pre, ffn2_norm_post=ffn2_norm_post, ffn2_w_gate=ffn2_w_gate,
             ffn2_w_up=ffn2_w_up, ffn2_w_down=ffn2_w_down)
    M1 = dict(ffn1_norm_pre=m_ffn1_norm_pre, ffn1_norm_post=m_ffn1_norm_post, ffn1_w_gate=m_ffn1_w_gate,
              ffn1_w_up=m_ffn1_w_up, ffn1_w_down=m_ffn1_w_down, mix_norm_pre=m_mix_norm_pre,
              mix_norm_post=m_mix_norm_post, w_in=m_w_in, b_gate=m_b_gate, conv_w=m_conv_w, attn_sink=m_attn_sink,
              mlstm_norm=m_mlstm_norm, w_out=m_w_out, ffn2_norm_pre=m_ffn2_norm_pre,
              ffn2_norm_post=m_ffn2_norm_post, ffn2_w_gate=m_ffn2_w_gate, ffn2_w_up=m_ffn2_w_up,
              ffn2_w_down=m_ffn2_w_down)
    V2 = dict(ffn1_norm_pre=v_ffn1_norm_pre, ffn1_norm_post=v_ffn1_norm_post, ffn1_w_gate=v_ffn1_w_gate,
              ffn1_w_up=v_ffn1_w_up, ffn1_w_down=v_ffn1_w_down, mix_norm_pre=v_mix_norm_pre,
              mix_norm_post=v_mix_norm_post, w_in=v_w_in, b_gate=v_b_gate, conv_w=v_conv_w, attn_sink=v_attn_sink,
              mlstm_norm=v_mlstm_norm, w_out=v_w_out, ffn2_norm_pre=v_ffn2_norm_pre,
              ffn2_norm_post=v_ffn2_norm_post, ffn2_w_gate=v_ffn2_w_gate, ffn2_w_up=v_ffn2_w_up,
              ffn2_w_down=v_ffn2_w_down)
    depth = w_in.shape[0]
    S = x.shape[1]
    xs = x[0]
    cos2, sin2 = _rope_tables(S)
    core = lax.axis_index("c").astype(jnp.int32).reshape(1)

    cs = conv_w.shape[2]
    conv_g = run_comm(ag_comm([conv_w.reshape(depth * CONV_WIDTH, cs)]), name="ag_conv")[0]
    conv_all = conv_g.reshape(N_DEV, depth, CONV_WIDTH, cs).transpose(1, 2, 0, 3)
    conv_all = conv_all.reshape(depth, CONV_WIDTH, N_DEV * cs)
    conv_all = jnp.pad(conv_all, ((0, 0), (0, CONV_HALO - CONV_WIDTH), (0, 0)))

    lw, saved = [], []
    shards = [_layer_shards(W, l) for l in range(depth)]
    ffn1_w = run_comm(ag_comm(shards[0][0:3]), name="ag_first")
    for l in range(depth):
        xs, s1, got = _ffn_fwd(xs, W["ffn1_norm_pre"][l], W["ffn1_norm_post"][l], *ffn1_w, "f1", shards[l][3:5])
        mix_w = (_permute_w_in(_gathered_cols(got[0])), got[1].reshape(D_MODEL, D_MODEL))
        xs, s2, ffn2_w = _mix_fwd(xs, W["mix_norm_pre"][l], W["mix_norm_post"][l], mix_w[0], W["b_gate"][l],
                                  conv_all[l], W["attn_sink"][l], W["mlstm_norm"][l], mix_w[1], cos2, sin2, "mx",
                                  shards[l][5:7], shards[l][7:8])
        xs, s3, got = _ffn_fwd(xs, W["ffn2_norm_pre"][l], W["ffn2_norm_post"][l], *ffn2_w, "f2",
                               shards[l + 1][0:3] if l + 1 < depth else None)
        lw.append(dict(ffn1=ffn1_w, mix=mix_w, ffn2=ffn2_w))
        saved.append((s1, s2, s3))
        ffn1_w = got

    dx, loss_part = loss_fwd_bwd(xs, loss_target[0], name="loss")

    F1, MX, F2 = BIG[0:3], (BIG[3], BIG[4], "conv_w"), BIG[5:8]
    names = BIG + ("conv_w",)
    parts = {n: [None] * depth for n in names}
    small_parts = [None] * depth
    waiting = None
    for l in reversed(range(depth)):
        wl = lw[l]
        s1, s2, s3 = saved[l]
        dx, dpre2, dpost2, grads2, reduced = _ffn_bwd(dx, s3, W["ffn2_norm_pre"][l], W["ffn2_norm_post"][l],
                                                      *wl["ffn2"], core, "f2", waiting)
        if waiting is not None:
            for n, r in zip(F1, reduced):
                parts[n][l + 1] = r
        dx, dpre_m, dpost_m, db_gate, dsink, dmn, grads_m, reduced = _mix_bwd(
            dx, s2, W["mix_norm_pre"][l], W["mix_norm_post"][l], wl["mix"][0], conv_all[l], W["attn_sink"][l],
            W["mlstm_norm"][l], wl["mix"][1], cos2, sin2, core, "mx", grads2)
        for n, r in zip(F2, reduced):
            parts[n][l] = r
        dx, dpre1, dpost1, waiting, reduced = _ffn_bwd(dx, s1, W["ffn1_norm_pre"][l], W["ffn1_norm_post"][l],
                                                       *wl["ffn1"], core, "f1", grads_m)
        for n, r in zip(MX, reduced):
            parts[n][l] = r
        small_parts[l] = dict(ffn1_norm_pre=dpre1[0], ffn1_norm_post=dpost1[0], mix_norm_pre=dpre_m[0],
                              mix_norm_post=dpost_m[0], b_gate=db_gate, attn_sink=dsink, mlstm_norm=dmn,
                              ffn2_norm_pre=dpre2[0], ffn2_norm_post=dpost2[0])
    recv = run_comm(pair_comm(waiting), name="rs1_last")
    reduced = run_comm(chip_comm(_pair_adds(waiting, recv, core, "last")), name="rs2_last")
    for n, r in zip(F1, reduced):
        parts[n][0] = r

    outs = {k: {} for k in ("g", "d", "m", "v")}
    for n in names:
        res = adam_tensor(W[n], parts[n], M1[n], V2[n], name=f"adam_{n}")
        for k, r in zip(("g", "d", "m", "v"), res):
            outs[k][n] = r
    small_out = {k: {n: [None] * depth for n in SMALL} for k in ("g", "d", "m", "v")}

    vec = jnp.concatenate([small_parts[l][n].reshape(-1) for l in range(depth) for n in SMALL]
                          + [loss_part.reshape(-1)])
    n_small = vec.shape[0]
    gathered_small = run_comm(ag_comm([_flat(vec)]), name="ag_small")[0]
    wvec = _flat(jnp.concatenate([W[n][l].reshape(-1) for l in range(depth) for n in SMALL] + [jnp.zeros((1,), F32)]))
    mvec = _flat(jnp.concatenate([M1[n][l].reshape(-1) for l in range(depth) for n in SMALL] + [jnp.zeros((1,), F32)]))
    vvec = _flat(jnp.concatenate([V2[n][l].reshape(-1) for l in range(depth) for n in SMALL] + [jnp.ones((1,), F32)]))
    res = adam_update(wvec, gathered_small, mvec, vvec, name="adam_small")
    res = [r.reshape(-1)[:n_small] for r in res]
    off = 0
    for l in range(depth):
        for n in SMALL:
            sz = W[n].shape[1]
            for k, r in zip(("g", "d", "m", "v"), res):
                small_out[k][n][l] = r[off:off + sz]
            off += sz
    loss = res[0][off]
    for k in outs:
        for n in SMALL:
            outs[k][n] = jnp.stack(small_out[k][n], axis=0)

    return (loss, dx[None], *[outs["g"][n] for n in WEIGHTS], *[outs["d"][n] for n in WEIGHTS],
            *[outs["m"][n] for n in WEIGHTS], *[outs["v"][n] for n in WEIGHTS])
```

```python
import jax
import jax.numpy as jnp
from jax import lax
from jax.experimental import pallas as pl
from jax.experimental.pallas import tpu as pltpu

F32 = jnp.float32
BF16 = jnp.bfloat16

D_MODEL = 2048
D_FF = 5632
ATT_HEADS = 8
ATT_KV_HEADS = 2
ATT_GROUP = ATT_HEADS // ATT_KV_HEADS
ATT_WIDTH = 1024
HEAD_DIM = 128
KV_WIDTH = 256
WINDOW = 128
BLK = 128
M_WIDTH = 1024
M_HEADS = 4
M_HEAD_DIM = 256
CONV_WIDTH = 5
EPS = 1e-6
ROPE_THETA = 10000.0
IN_WIDTH = 5648
N_GATES = 16
N_DEV = 8

ADAM_LR = 0.001
ADAM_B1 = 0.9
ADAM_B2 = 0.999
ADAM_EPS = 1e-08
ADAM_WD = 0.01
ADAM_STEP = 10

P_QK = 0
P_QA = 2048
P_VM = 3072
P_OM = 4096
P_KA = 5120
P_VA = 5376
P_G = 5632
P_WIDTH = 6144

LANES = 128
V7X_VMEM_LIMIT = 48 * 1024 * 1024
NEG = -1e30
MESH = pl.DeviceIdType.MESH
ANY = pl.BlockSpec(memory_space=pl.ANY)


K_TILES = (2048, 1024, 512, 256, 128)


def _tile(n, cands=(1024, 512, 256, 128)):
    for c in cands:
        if n % c == 0:
            return c
    return n


class Comm:
    def __init__(self, ins, outs, sems, phases):
        self.ins, self.outs, self.sems, self.phases = list(ins), list(outs), list(sems), list(phases)


def run_comm(comm, *, name):
    n_in, n_out = len(comm.ins), len(comm.outs)

    def body(*refs):
        ins, outs, sems = refs[:n_in], refs[n_in:n_in + n_out], refs[n_in + n_out:]
        for phase in comm.phases:
            phase(ins, outs, sems)

    return pl.pallas_call(body, name=name, out_shape=comm.outs, in_specs=[ANY] * n_in, out_specs=[ANY] * n_out,
                          scratch_shapes=comm.sems,
                          compiler_params=pltpu.CompilerParams(has_side_effects=True))(*comm.ins)


def _pcall(body, *, name, out_shape, in_specs, out_specs, grid=(), scratch=(), sem=None, comm=None):
    if comm is None:
        return pl.pallas_call(
            body, name=name, out_shape=out_shape, in_specs=in_specs, out_specs=out_specs, grid=grid,
            scratch_shapes=list(scratch),
            compiler_params=pltpu.CompilerParams(dimension_semantics=sem, vmem_limit_bytes=V7X_VMEM_LIMIT))
    multi = isinstance(out_shape, (tuple, list))
    outs = list(out_shape) if multi else [out_shape]
    ospecs = list(out_specs) if multi else [out_specs]
    n_in, n_out, n_scr = len(in_specs), len(outs), len(scratch)
    nci, nco = len(comm.ins), len(comm.outs)
    steps = 1
    for g in grid:
        steps *= g
    n_ph = len(comm.phases)
    at = [0, steps - 1] if n_ph == 2 else [0, (3 * steps) // 4, steps - 1]

    def wrapped(*refs):
        ins, cins = refs[:n_in], refs[n_in:n_in + nci]
        o0 = n_in + nci
        res, couts = refs[o0:o0 + n_out], refs[o0 + n_out:o0 + n_out + nco]
        s0 = o0 + n_out + nco
        scr, csems = refs[s0:s0 + n_scr], refs[s0 + n_scr:]
        lin = 0
        for k, g in enumerate(grid):
            lin = lin * g + pl.program_id(k)

        @pl.when(lin == at[0])
        def _():
            comm.phases[0](cins, couts, csems)

        body(*ins, *res, *scr)
        for p in range(1, n_ph):
            @pl.when(lin == at[p])
            def _(p=p):
                comm.phases[p](cins, couts, csems)

    call = pl.pallas_call(
        wrapped, name=name, out_shape=outs + comm.outs, in_specs=list(in_specs) + [ANY] * nci,
        out_specs=ospecs + [ANY] * nco, grid=grid, scratch_shapes=list(scratch) + comm.sems,
        compiler_params=pltpu.CompilerParams(dimension_semantics=("arbitrary",) * len(grid),
                                             vmem_limit_bytes=V7X_VMEM_LIMIT, has_side_effects=True))

    def run(*args):
        got = list(call(*args, *comm.ins))
        return (tuple(got[:n_out]) if multi else got[0]), got[n_out:]

    return run


def _dot(a, b):
    return jnp.dot(a, b, preferred_element_type=F32)


def _dot_nt(a, b):
    return lax.dot_general(a, b, (((1,), (1,)), ((), ())), preferred_element_type=F32)


def _dot_tn(a, b):
    return lax.dot_general(a, b, (((0,), (0,)), ((), ())), preferred_element_type=F32)


def _sigmoid(x):
    return 1.0 / (1.0 + jnp.exp(-x))


def mm_nn(a, b, *, name, out_dtype=F32, comm=None):
    M, K = a.shape
    N = b.shape[1]
    tm, tk, tn = _tile(M), _tile(K, K_TILES), _tile(N)
    nk = K // tk

    def body(a_ref, b_ref, o_ref, acc):
        k = pl.program_id(2)

        @pl.when(k == 0)
        def _():
            acc[...] = jnp.zeros_like(acc)

        acc[...] += _dot(a_ref[...], b_ref[...])

        @pl.when(k == nk - 1)
        def _():
            o_ref[...] = acc[...].astype(o_ref.dtype)

    return _pcall(body, name=name, out_shape=jax.ShapeDtypeStruct((M, N), out_dtype),
                  in_specs=[pl.BlockSpec((tm, tk), lambda i, j, k: (i, k)),
                            pl.BlockSpec((tk, tn), lambda i, j, k: (k, j))],
                  out_specs=pl.BlockSpec((tm, tn), lambda i, j, k: (i, j)), grid=(M // tm, N // tn, nk),
                  scratch=[pltpu.VMEM((tm, tn), F32)], sem=("parallel", "parallel", "arbitrary"), comm=comm)(a, b)


def mm_tn(a, g, *, name, owner_rows=None, out_dtype=F32, comm=None):
    M, K = a.shape
    N = g.shape[1]
    tm, tk, tn = _tile(M), _tile(K), _tile(N)
    nm = M // tm
    per_tile = 1 if owner_rows is None else tk // owner_rows

    def body(a_ref, g_ref, o_ref, acc):
        m = pl.program_id(2)

        @pl.when(m == 0)
        def _():
            acc[...] = jnp.zeros_like(acc)

        acc[...] += _dot_tn(a_ref[...], g_ref[...])

        @pl.when(m == nm - 1)
        def _():
            if owner_rows is None:
                o_ref[...] = acc[...].astype(o_ref.dtype)
            else:
                for d in range(per_tile):
                    o_ref[d % 2, d // 2] = acc[d * owner_rows:(d + 1) * owner_rows, :].astype(o_ref.dtype)

    if owner_rows is None:
        out_shape = jax.ShapeDtypeStruct((K, N), out_dtype)
        out_spec = pl.BlockSpec((tk, tn), lambda i, j, m: (i, j))
    else:
        assert per_tile % 2 == 0 and K == N_DEV * owner_rows
        out_shape = jax.ShapeDtypeStruct((2, N_DEV // 2, owner_rows, N), out_dtype)
        out_spec = pl.BlockSpec((2, per_tile // 2, owner_rows, tn), lambda i, j, m: (0, i, 0, j))
    return _pcall(body, name=name, out_shape=out_shape,
                  in_specs=[pl.BlockSpec((tm, tk), lambda i, j, m: (m, i)),
                            pl.BlockSpec((tm, tn), lambda i, j, m: (m, j))],
                  out_specs=out_spec, grid=(K // tk, N // tn, nm), scratch=[pltpu.VMEM((tk, tn), F32)],
                  sem=("parallel", "parallel", "arbitrary"), comm=comm)(a, g)


def mm_nt(a, b, *, name, out_dtype=F32):
    M, K = a.shape
    N = b.shape[0]
    tm, tn, tk = _tile(M), _tile(N), _tile(K, K_TILES)
    nk = K // tk

    def body(a_ref, b_ref, o_ref, acc):
        k = pl.program_id(2)

        @pl.when(k == 0)
        def _():
            acc[...] = jnp.zeros_like(acc)

        acc[...] += _dot_nt(a_ref[...], b_ref[...])

        @pl.when(k == nk - 1)
        def _():
            o_ref[...] = acc[...].astype(o_ref.dtype)

    return _pcall(body, name=name, out_shape=jax.ShapeDtypeStruct((M, N), out_dtype),
                  in_specs=[pl.BlockSpec((tm, tk), lambda i, j, k: (i, k)),
                            pl.BlockSpec((tn, tk), lambda i, j, k: (j, k))],
                  out_specs=pl.BlockSpec((tm, tn), lambda i, j, k: (i, j)), grid=(M // tm, N // tn, nk),
                  scratch=[pltpu.VMEM((tm, tn), F32)], sem=("parallel", "parallel", "arbitrary"))(a, b)


FS = D_FF // N_DEV
FSP = 768


def ffn_gu(xn, wg8, wu8, *, name, comm=None):
    S, D = xn.shape
    tm = _tile(S)

    def body(x_ref, wg_ref, wu_ref, hg_ref, hu_ref, act_ref):
        xv = x_ref[...]
        hg = _dot_nt(xv, wg_ref[...])
        hu = _dot_nt(xv, wu_ref[...])
        hg_ref[...] = hg.astype(BF16)
        hu_ref[...] = hu.astype(BF16)
        act_ref[...] = (hg * _sigmoid(hg) * hu).astype(BF16)

    wspec = pl.BlockSpec((None, FSP, D), lambda i, j: (j, 0, 0))
    ospec = pl.BlockSpec((None, tm, FSP), lambda i, j: (j, i, 0))
    shp = jax.ShapeDtypeStruct((N_DEV, S, FSP), BF16)
    return _pcall(body, name=name, out_shape=(shp, shp, shp),
                  in_specs=[pl.BlockSpec((tm, D), lambda i, j: (i, 0)), wspec, wspec],
                  out_specs=(ospec, ospec, ospec), grid=(S // tm, N_DEV), sem=("parallel", "arbitrary"),
                  comm=comm)(xn, wg8, wu8)


def ffn_down(act8, wd8, *, name):
    _, S, _ = act8.shape
    D = wd8.shape[2]
    tm, tn = _tile(S), D

    def body(a_ref, w_ref, o_ref):
        @pl.when(pl.program_id(2) == 0)
        def _():
            o_ref[...] = jnp.zeros_like(o_ref)

        o_ref[...] += _dot(a_ref[...], w_ref[...])

    return _pcall(body, name=name, out_shape=jax.ShapeDtypeStruct((S, D), F32),
                  in_specs=[pl.BlockSpec((None, tm, FSP), lambda i, n, j: (j, i, 0)),
                            pl.BlockSpec((None, FSP, tn), lambda i, n, j: (j, 0, n))],
                  out_specs=pl.BlockSpec((tm, tn), lambda i, n, j: (i, n)),
                  grid=(S // tm, D // tn, N_DEV), sem=("parallel", "parallel", "arbitrary"))(act8, wd8)


def ffn_dact(df, wd8, hg8, hu8, *, name):
    S, D = df.shape
    tm = _tile(S)

    def body(d_ref, w_ref, hg_ref, hu_ref, dg_ref, du_ref):
        da = _dot_nt(d_ref[...], w_ref[...])
        hg = hg_ref[...].astype(F32)
        hu = hu_ref[...].astype(F32)
        sg = _sigmoid(hg)
        dg_ref[...] = (da * hu * (sg * (1.0 + hg * (1.0 - sg)))).astype(BF16)
        du_ref[...] = (da * hg * sg).astype(BF16)

    blk = pl.BlockSpec((None, tm, FSP), lambda i, j: (j, i, 0))
    shp = jax.ShapeDtypeStruct((N_DEV, S, FSP), BF16)
    return _pcall(body, name=name, out_shape=(shp, shp),
                  in_specs=[pl.BlockSpec((tm, D), lambda i, j: (i, 0)),
                            pl.BlockSpec((None, FSP, D), lambda i, j: (j, 0, 0)), blk, blk],
                  out_specs=(blk, blk), grid=(S // tm, N_DEV), sem=("parallel", "arbitrary"))(df, wd8, hg8, hu8)


def ffn_dwd(act8, df, *, name, out_dtype, comm=None):
    _, S, _ = act8.shape
    D = df.shape[1]
    tm, tn = _tile(S), D
    nm = S // tm

    def body(a_ref, d_ref, o_ref, acc):
        m = pl.program_id(2)

        @pl.when(m == 0)
        def _():
            acc[...] = jnp.zeros_like(acc)

        acc[...] += _dot_tn(a_ref[...], d_ref[...])

        @pl.when(m == nm - 1)
        def _():
            o_ref[...] = acc[0:FS, :].astype(o_ref.dtype)

    return _pcall(body, name=name, out_shape=jax.ShapeDtypeStruct((2, N_DEV // 2, FS, D), out_dtype),
                  in_specs=[pl.BlockSpec((None, tm, FSP), lambda j, n, m: (j, m, 0)),
                            pl.BlockSpec((tm, tn), lambda j, n, m: (m, n))],
                  out_specs=pl.BlockSpec((None, None, FS, tn), lambda j, n, m: (j % 2, j // 2, 0, n)),
                  grid=(N_DEV, D // tn, nm), scratch=[pltpu.VMEM((FSP, tn), F32)],
                  sem=("parallel", "parallel", "arbitrary"), comm=comm)(act8, df)


def ffn_dwgu(xn, dg8, du8, *, name, out_dtype, comm=None):
    S, D = xn.shape
    tm, tk = _tile(S), _tile(D)
    nm = S // tm

    def body(x_ref, dg_ref, du_ref, og_ref, ou_ref, accg, accu):
        m = pl.program_id(2)

        @pl.when(m == 0)
        def _():
            accg[...] = jnp.zeros_like(accg)
            accu[...] = jnp.zeros_like(accu)

        xv = x_ref[...]
        accg[...] += _dot_tn(dg_ref[...], xv)
        accu[...] += _dot_tn(du_ref[...], xv)

        @pl.when(m == nm - 1)
        def _():
            og_ref[...] = accg[0:FS, :].astype(og_ref.dtype)
            ou_ref[...] = accu[0:FS, :].astype(ou_ref.dtype)

    blk = pl.BlockSpec((None, tm, FSP), lambda j, k, m: (j, m, 0))
    ospec = pl.BlockSpec((None, None, FS, tk), lambda j, k, m: (j % 2, j // 2, 0, k))
    shp = jax.ShapeDtypeStruct((2, N_DEV // 2, FS, D), out_dtype)
    return _pcall(body, name=name, out_shape=(shp, shp),
                  in_specs=[pl.BlockSpec((tm, tk), lambda j, k, m: (m, k)), blk, blk],
                  out_specs=(ospec, ospec), grid=(N_DEV, D // tk, nm),
                  scratch=[pltpu.VMEM((FSP, tk), F32), pltpu.VMEM((FSP, tk), F32)],
                  sem=("parallel", "parallel", "arbitrary"), comm=comm)(xn, dg8, du8)


def ffn_dxn(dg8, du8, wg8, wu8, *, name, comm=None):
    _, S, _ = dg8.shape
    D = wg8.shape[2]
    tm, tn = _tile(S), _tile(D)

    def body(dg_ref, du_ref, wg_ref, wu_ref, o_ref):
        @pl.when(pl.program_id(2) == 0)
        def _():
            o_ref[...] = jnp.zeros_like(o_ref)

        o_ref[...] += _dot(dg_ref[...], wg_ref[...]) + _dot(du_ref[...], wu_ref[...])

    blk = pl.BlockSpec((None, tm, FSP), lambda i, n, j: (j, i, 0))
    wspec = pl.BlockSpec((None, FSP, tn), lambda i, n, j: (j, 0, n))
    return _pcall(body, name=name, out_shape=jax.ShapeDtypeStruct((S, D), F32),
                  in_specs=[blk, blk, wspec, wspec], out_specs=pl.BlockSpec((tm, tn), lambda i, n, j: (i, n)),
                  grid=(S // tm, D // tn, N_DEV), sem=("parallel", "parallel", "arbitrary"),
                  comm=comm)(dg8, du8, wg8, wu8)


def norm_fwd(x, g, *, name, scale=1.0, resid=None, out_dtype=F32):
    S, D = x.shape
    tm = _tile(S, (512, 256, 128))

    def body(*refs):
        if resid is None:
            x_ref, g_ref, o_ref = refs
        else:
            x_ref, g_ref, r_ref, o_ref = refs
        xv = x_ref[...].astype(F32)
        r = lax.rsqrt(jnp.mean(xv * xv, axis=-1, keepdims=True) + EPS)
        y = (xv * r) * g_ref[...]
        if scale != 1.0:
            y = y * scale
        if resid is not None:
            y = y + r_ref[...]
        o_ref[...] = y.astype(o_ref.dtype)

    row = pl.BlockSpec((tm, D), lambda i: (i, 0))
    in_specs = [row, pl.BlockSpec((1, D), lambda i: (0, 0))]
    args = [x, g.reshape(1, D)]
    if resid is not None:
        in_specs.append(row)
        args.append(resid)
    return _pcall(body, name=name, out_shape=jax.ShapeDtypeStruct((S, D), out_dtype), in_specs=in_specs,
                  out_specs=row, grid=(S // tm,), sem=("parallel",))(*args)


def norm_bwd(dy, x, g, *, name, scale=1.0, resid=None, out_dtype=F32):
    S, D = x.shape
    tm = _tile(S, (512, 256, 128))

    def body(*refs):
        if resid is None:
            dy_ref, x_ref, g_ref, dx_ref, dg_ref = refs
        else:
            dy_ref, x_ref, g_ref, r_ref, dx_ref, dg_ref = refs

        @pl.when(pl.program_id(0) == 0)
        def _():
            dg_ref[...] = jnp.zeros_like(dg_ref)

        xv = x_ref[...].astype(F32)
        d = dy_ref[...].astype(F32)
        if scale != 1.0:
            d = d * scale
        r = lax.rsqrt(jnp.mean(xv * xv, axis=-1, keepdims=True) + EPS)
        xh = xv * r
        dg_ref[...] += jnp.sum(d * xh, axis=0, keepdims=True)
        dxh = d * g_ref[...]
        dx = r * (dxh - xh * jnp.mean(dxh * xh, axis=-1, keepdims=True))
        if resid is not None:
            dx = dx + r_ref[...]
        dx_ref[...] = dx.astype(dx_ref.dtype)

    row = pl.BlockSpec((tm, D), lambda i: (i, 0))
    vec = pl.BlockSpec((1, D), lambda i: (0, 0))
    in_specs = [row, row, vec]
    args = [dy, x, g.reshape(1, D)]
    if resid is not None:
        in_specs.append(row)
        args.append(resid)
    return _pcall(body, name=name,
                  out_shape=(jax.ShapeDtypeStruct((S, D), out_dtype), jax.ShapeDtypeStruct((1, D), F32)),
                  in_specs=in_specs, out_specs=(row, vec), grid=(S // tm,), sem=("arbitrary",))(*args)


def loss_fwd_bwd(y, target, *, name):
    S, D = y.shape
    tm = _tile(S, (512, 256, 128))

    def body(y_ref, t_ref, dy_ref, l_ref):
        @pl.when(pl.program_id(0) == 0)
        def _():
            l_ref[...] = jnp.zeros_like(l_ref)

        e = y_ref[...] - t_ref[...]
        dy_ref[...] = e * (1.0 / D)
        l_ref[...] += jnp.sum(jnp.sum(e * e, axis=1, keepdims=True), axis=0, keepdims=True) * (0.5 / D)

    row = pl.BlockSpec((tm, D), lambda i: (i, 0))
    one = pl.BlockSpec((1, 1), lambda i: (0, 0))
    return _pcall(body, name=name,
                  out_shape=(jax.ShapeDtypeStruct((S, D), F32), jax.ShapeDtypeStruct((1, 1), F32)),
                  in_specs=[row, row], out_specs=(row, one), grid=(S // tm,), sem=("arbitrary",))(y, target)


def _rope_tables(S):
    half = HEAD_DIM // 2
    inv_freq = ROPE_THETA ** (-jnp.arange(half, dtype=F32) / half)
    ang = jnp.arange(S, dtype=F32)[:, None] * inv_freq[None, :]
    cos, sin = jnp.cos(ang), jnp.sin(ang)
    return jnp.concatenate([cos, cos], axis=1), jnp.concatenate([-sin, sin], axis=1)


def _rope(x, cos2, sin2):
    return x * cos2 + pltpu.roll(x, HEAD_DIM // 2, 1) * sin2


def _unrope(d, cos2, sin2):
    return d * cos2 + pltpu.roll(d * sin2, HEAD_DIM // 2, 1)


def _nbr_specs(width, col, nb):
    return [pl.BlockSpec((BLK, width), lambda n, c=col: (jnp.maximum(n - 1, 0), c)),
            pl.BlockSpec((BLK, width), lambda n, c=col: (n, c)),
            pl.BlockSpec((BLK, width), lambda n, c=col: (jnp.minimum(n + 1, nb - 1), c))]


def attn_fwd(proj, cos2, sin2, sink, *, name):
    S = proj.shape[0]
    nb = S // BLK
    scale = HEAD_DIM ** -0.5

    def body(sink_ref, q_ref, k0, k1, k2, v0, v1, v2, c0, c1, c2, s0, s1, s2, o_ref, lse_ref):
        n = pl.program_id(0)
        cosk = jnp.concatenate([c0[...], c1[...], c2[...]], axis=0)
        sink_ = jnp.concatenate([s0[...], s1[...], s2[...]], axis=0)
        kall = jnp.concatenate([k0[...], k1[...], k2[...]], axis=0)
        vall = jnp.concatenate([v0[...], v1[...], v2[...]], axis=0)
        rows = lax.broadcasted_iota(jnp.int32, (BLK, 3 * BLK), 0)
        cols = lax.broadcasted_iota(jnp.int32, (BLK, 3 * BLK), 1)
        kpos = (n - 1) * BLK + cols
        valid = (jnp.abs(cols - BLK - rows) <= WINDOW) & (kpos >= 0) & (kpos < S)
        valid = jnp.concatenate([valid] * ATT_GROUP, axis=0)
        lane = lax.broadcasted_iota(jnp.int32, (BLK, LANES), 1)
        lse_tile = jnp.zeros((BLK, LANES), F32)
        for hk in range(ATT_KV_HEADS):
            ks = slice(hk * HEAD_DIM, (hk + 1) * HEAD_DIM)
            kh = _rope(kall[:, ks], cosk, sink_).astype(BF16)
            vh = vall[:, ks].astype(BF16)
            qs = []
            for g in range(ATT_GROUP):
                hq = hk * ATT_GROUP + g
                qs.append(_rope(q_ref[:, hq * HEAD_DIM:(hq + 1) * HEAD_DIM], c1[...], s1[...]))
            qh = jnp.concatenate(qs, axis=0).astype(BF16)
            s = _dot_nt(qh, kh) * scale
            s = jnp.where(valid, s, NEG)
            snk = jnp.concatenate(
                [jnp.full((BLK, 1), sink_ref[hk * ATT_GROUP + g], F32) for g in range(ATT_GROUP)], axis=0)
            m = jnp.maximum(jnp.max(s, axis=1, keepdims=True), snk)
            p = jnp.exp(s - m)
            l = jnp.sum(p, axis=1, keepdims=True) + jnp.exp(snk - m)
            o = _dot(p.astype(BF16), vh) * (1.0 / l)
            lse = m + jnp.log(l)
            for g in range(ATT_GROUP):
                hq = hk * ATT_GROUP + g
                o_ref[:, hq * HEAD_DIM:(hq + 1) * HEAD_DIM] = o[g * BLK:(g + 1) * BLK].astype(o_ref.dtype)
                lse_tile = lse_tile + jnp.where(lane == hq, lse[g * BLK:(g + 1) * BLK], 0.0)
        lse_ref[...] = lse_tile

    in_specs = ([pl.BlockSpec(memory_space=pltpu.SMEM),
                 pl.BlockSpec((BLK, ATT_WIDTH), lambda n: (n, P_QA // ATT_WIDTH))]
                + _nbr_specs(KV_WIDTH, P_KA // KV_WIDTH, nb) + _nbr_specs(KV_WIDTH, P_VA // KV_WIDTH, nb)
                + _nbr_specs(HEAD_DIM, 0, nb) + _nbr_specs(HEAD_DIM, 0, nb))
    return _pcall(body, name=name,
                  out_shape=(jax.ShapeDtypeStruct((S, ATT_WIDTH), BF16), jax.ShapeDtypeStruct((S, LANES), F32)),
                  in_specs=in_specs,
                  out_specs=(pl.BlockSpec((BLK, ATT_WIDTH), lambda n: (n, 0)),
                             pl.BlockSpec((BLK, LANES), lambda n: (n, 0))),
                  grid=(nb,), sem=("parallel",))(sink, proj, proj, proj, proj, proj, proj, proj,
                                                 cos2, cos2, cos2, sin2, sin2, sin2)


def attn_bwd(proj, y, dy, lse, cos2, sin2, sink, *, name):
    S = proj.shape[0]
    nb = S // BLK
    scale = HEAD_DIM ** -0.5

    def body(sink_ref, q_ref, k0, k1, k2, v0, v1, v2, o_ref, d_ref, l_ref, c0, c1, c2, s0, s1, s2,
             dq_ref, dk_ref, dv_ref, dsink_ref, dk_acc, dv_acc):
        n = pl.program_id(0)

        @pl.when(n == 0)
        def _():
            dsink_ref[...] = jnp.zeros_like(dsink_ref)
            dk_acc[...] = jnp.zeros_like(dk_acc)
            dv_acc[...] = jnp.zeros_like(dv_acc)

        @pl.when(n < nb)
        def _():
            cosk = jnp.concatenate([c0[...], c1[...], c2[...]], axis=0)
            sink_ = jnp.concatenate([s0[...], s1[...], s2[...]], axis=0)
            kall = jnp.concatenate([k0[...], k1[...], k2[...]], axis=0)
            vall = jnp.concatenate([v0[...], v1[...], v2[...]], axis=0)
            lane = lax.broadcasted_iota(jnp.int32, (1, LANES), 1)
            rows = lax.broadcasted_iota(jnp.int32, (BLK, 3 * BLK), 0)
            cols = lax.broadcasted_iota(jnp.int32, (BLK, 3 * BLK), 1)
            kpos = (n - 1) * BLK + cols
            valid = (jnp.abs(cols - BLK - rows) <= WINDOW) & (kpos >= 0) & (kpos < S)
            valid = jnp.concatenate([valid] * ATT_GROUP, axis=0)
            dsink_acc = jnp.zeros((1, LANES), F32)
            for hk in range(ATT_KV_HEADS):
                ks = slice(hk * HEAD_DIM, (hk + 1) * HEAD_DIM)
                kh = _rope(kall[:, ks], cosk, sink_).astype(BF16)
                vh = vall[:, ks].astype(BF16)
                qs, dos, lses, deltas = [], [], [], []
                for g in range(ATT_GROUP):
                    hq = hk * ATT_GROUP + g
                    hs = slice(hq * HEAD_DIM, (hq + 1) * HEAD_DIM)
                    qs.append(_rope(q_ref[:, hs], c1[...], s1[...]))
                    do = d_ref[:, hs]
                    dos.append(do)
                    lses.append(l_ref[:, hq:hq + 1])
                    deltas.append(jnp.sum(do * o_ref[:, hs].astype(F32), axis=1, keepdims=True))
                qh = jnp.concatenate(qs, axis=0).astype(BF16)
                doh = jnp.concatenate(dos, axis=0).astype(BF16)
                lseh = jnp.concatenate(lses, axis=0)
                delh = jnp.concatenate(deltas, axis=0)
                s = jnp.where(valid, _dot_nt(qh, kh) * scale, NEG)
                p = jnp.exp(s - lseh)
                dp = _dot_nt(doh, vh)
                ds = (p * (dp - delh)).astype(BF16)
                dq = _dot(ds, kh) * scale
                dk_acc[hk] += _dot_tn(ds, qh) * scale
                dv_acc[hk] += _dot_tn(p.astype(BF16), doh)
                for g in range(ATT_GROUP):
                    hq = hk * ATT_GROUP + g
                    dq_ref[:, hq * HEAD_DIM:(hq + 1) * HEAD_DIM] = _unrope(dq[g * BLK:(g + 1) * BLK], c1[...], s1[...])
                    psink = jnp.exp(sink_ref[hq] - lses[g])
                    dsink_acc = dsink_acc + jnp.where(lane == hq, -jnp.sum(psink * deltas[g]), 0.0)
            dsink_ref[...] += dsink_acc

        c_out = jnp.where(n < nb, c0[...], c1[...])
        s_out = jnp.where(n < nb, s0[...], s1[...])
        for hk in range(ATT_KV_HEADS):
            ks = slice(hk * HEAD_DIM, (hk + 1) * HEAD_DIM)
            dk_ref[:, ks] = _unrope(dk_acc[hk, 0:BLK, :], c_out, s_out)
            dv_ref[:, ks] = dv_acc[hk, 0:BLK, :]
            for acc in (dk_acc, dv_acc):
                acc[hk, 0:BLK, :] = acc[hk, BLK:2 * BLK, :]
                acc[hk, BLK:2 * BLK, :] = acc[hk, 2 * BLK:3 * BLK, :]
                acc[hk, 2 * BLK:3 * BLK, :] = jnp.zeros((BLK, HEAD_DIM), F32)

    own = lambda n: jnp.minimum(n, nb - 1)
    done = lambda n: jnp.maximum(n - 1, 0)

    def nbr(width, col):
        return [pl.BlockSpec((BLK, width), lambda n, c=col: (jnp.maximum(own(n) - 1, 0), c)),
                pl.BlockSpec((BLK, width), lambda n, c=col: (own(n), c)),
                pl.BlockSpec((BLK, width), lambda n, c=col: (jnp.minimum(own(n) + 1, nb - 1), c))]

    in_specs = ([pl.BlockSpec(memory_space=pltpu.SMEM),
                 pl.BlockSpec((BLK, ATT_WIDTH), lambda n: (own(n), P_QA // ATT_WIDTH))]
                + nbr(KV_WIDTH, P_KA // KV_WIDTH) + nbr(KV_WIDTH, P_VA // KV_WIDTH)
                + [pl.BlockSpec((BLK, ATT_WIDTH), lambda n: (own(n), 0)),
                   pl.BlockSpec((BLK, ATT_WIDTH), lambda n: (own(n), 0)),
                   pl.BlockSpec((BLK, LANES), lambda n: (own(n), 0))]
                + nbr(HEAD_DIM, 0) + nbr(HEAD_DIM, 0))
    args = [sink, proj] + [proj] * 6 + [y, dy, lse] + [cos2] * 3 + [sin2] * 3
    return _pcall(body, name=name,
                  out_shape=(jax.ShapeDtypeStruct((S, ATT_WIDTH), F32), jax.ShapeDtypeStruct((S, KV_WIDTH), F32),
                             jax.ShapeDtypeStruct((S, KV_WIDTH), F32), jax.ShapeDtypeStruct((1, LANES), F32)),
                  in_specs=in_specs,
                  out_specs=(pl.BlockSpec((BLK, ATT_WIDTH), lambda n: (own(n), 0)),
                             pl.BlockSpec((BLK, KV_WIDTH), lambda n: (done(n), 0)),
                             pl.BlockSpec((BLK, KV_WIDTH), lambda n: (done(n), 0)),
                             pl.BlockSpec((1, LANES), lambda n: (0, 0))),
                  grid=(nb + 1,),
                  scratch=[pltpu.VMEM((ATT_KV_HEADS, 3 * BLK, HEAD_DIM), F32),
                           pltpu.VMEM((ATT_KV_HEADS, 3 * BLK, HEAD_DIM), F32)],
                  sem=("arbitrary",))(*args)


CONV_HALO = 8
CONV_COLS = 512


def _halo_specs(tm, nrow, col_of):
    hb = tm // CONV_HALO
    return [pl.BlockSpec((CONV_HALO, CONV_COLS), lambda i, j: (jnp.maximum(i * hb - 1, 0), col_of(j))),
            pl.BlockSpec((tm, CONV_COLS), lambda i, j: (i, col_of(j))),
            pl.BlockSpec((CONV_HALO, CONV_COLS),
                         lambda i, j: (jnp.minimum((i + 1) * hb, nrow * hb - 1), col_of(j)))]


def _with_halo(prev, cur, nxt, i, nrow):
    p = jnp.where(i > 0, prev[...], 0.0)
    q = jnp.where(i < nrow - 1, nxt[...], 0.0)
    return jnp.concatenate([p, cur[...], q], axis=0)


def _conv_taps(xt, w_ref, tm):
    n = xt.shape[0]
    acc = jnp.zeros_like(xt)
    for j in range(CONV_WIDTH):
        sh = (CONV_WIDTH // 2 - j) % n
        xs = xt if sh == 0 else pltpu.roll(xt, sh, 0)
        acc = acc + xs * w_ref[j:j + 1, :]
    return acc


def conv_fwd(proj, conv_w, *, name):
    S = proj.shape[0]
    tm = _tile(S, (512, 256, 128))
    nrow = S // tm

    def body(xp, xc, xn, w_ref, o_ref):
        i = pl.program_id(0)
        xt = _with_halo(xp, xc, xn, i, nrow)
        pre = _conv_taps(xt, w_ref, tm)[CONV_HALO:CONV_HALO + tm]
        o_ref[...] = pre * _sigmoid(pre)

    return _pcall(body, name=name, out_shape=jax.ShapeDtypeStruct((S, 2 * M_WIDTH), F32),
                  in_specs=_halo_specs(tm, nrow, lambda j: P_QK // CONV_COLS + j)
                  + [pl.BlockSpec((CONV_HALO, CONV_COLS), lambda i, j: (0, j))],
                  out_specs=pl.BlockSpec((tm, CONV_COLS), lambda i, j: (i, j)),
                  grid=(nrow, 2 * M_WIDTH // CONV_COLS), sem=("parallel", "parallel"))(proj, proj, proj, conv_w)


def conv_bwd(proj, conv_w, da, db, *, name):
    S = proj.shape[0]
    tm = _tile(S, (512, 256, 128))
    nrow = S // tm

    def body(xp, xc, xn, ap, ac, an, bp, bc, bn, w_ref, dx_ref, dw_ref):
        i = pl.program_id(1)

        @pl.when(i == 0)
        def _():
            dw_ref[...] = jnp.zeros_like(dw_ref)

        xt = _with_halo(xp, xc, xn, i, nrow)
        dt = _with_halo(ap, ac, an, i, nrow) + _with_halo(bp, bc, bn, i, nrow)
        pre = _conv_taps(xt, w_ref, tm)
        sg = _sigmoid(pre)
        dpre = dt * (sg * (1.0 + pre * (1.0 - sg)))
        n = xt.shape[0]
        ridx = lax.broadcasted_iota(jnp.int32, (n, 1), 0)
        dpre = jnp.where((ridx >= 2) & (ridx < n - 2), dpre, 0.0)
        dx = jnp.zeros_like(xt)
        own = (ridx >= CONV_HALO) & (ridx < CONV_HALO + tm)
        dpre_own = jnp.where(own, dpre, 0.0)
        dw_rows = []
        for j in range(CONV_WIDTH):
            sh = (j - CONV_WIDTH // 2) % n
            ds_ = dpre if sh == 0 else pltpu.roll(dpre, sh, 0)
            dx = dx + ds_ * w_ref[j:j + 1, :]
            shx = (CONV_WIDTH // 2 - j) % n
            xs = xt if shx == 0 else pltpu.roll(xt, shx, 0)
            dw_rows.append(jnp.sum(dpre_own * xs, axis=0, keepdims=True))
        dx_ref[...] = dx[CONV_HALO:CONV_HALO + tm]
        dw_rows.append(jnp.zeros((CONV_HALO - CONV_WIDTH, CONV_COLS), F32))
        dw_ref[...] += jnp.concatenate(dw_rows, axis=0)

    colq = lambda j: P_QK // CONV_COLS + j
    same = lambda j: j

    def swap(specs):
        return [pl.BlockSpec(s.block_shape, (lambda f: (lambda j, i: f(i, j)))(s.index_map)) for s in specs]

    in_specs = swap(_halo_specs(tm, nrow, colq) + _halo_specs(tm, nrow, same) + _halo_specs(tm, nrow, same)
                    + [pl.BlockSpec((CONV_HALO, CONV_COLS), lambda i, j: (0, j))])
    return _pcall(body, name=name,
                  out_shape=(jax.ShapeDtypeStruct((S, 2 * M_WIDTH), F32),
                             jax.ShapeDtypeStruct((CONV_HALO, 2 * M_WIDTH), F32)),
                  in_specs=in_specs,
                  out_specs=(pl.BlockSpec((tm, CONV_COLS), lambda j, i: (i, j)),
                             pl.BlockSpec((CONV_HALO, CONV_COLS), lambda j, i: (0, j))),
                  grid=(2 * M_WIDTH // CONV_COLS, nrow), sem=("parallel", "arbitrary"))(
                      proj, proj, proj, da, da, da, db, db, db, conv_w)


def _log_sigmoid(x):
    return jnp.minimum(x, 0.0) - jnp.log(1.0 + jnp.exp(-jnp.abs(x)))


def _scan_sum(x, axis, from_end):
    idx = lax.broadcasted_iota(jnp.int32, x.shape, axis)
    n = x.shape[axis]
    sh = 1
    while sh < n:
        if from_end:
            x = x + jnp.where(idx < n - sh, pltpu.roll(x, n - sh, axis), 0.0)
        else:
            x = x + jnp.where(idx >= sh, pltpu.roll(x, sh, axis), 0.0)
        sh *= 2
    return x


def gate_rows(proj, *, name):
    S = proj.shape[0]

    def body(x_ref, o_ref):
        o_ref[...] = x_ref[...].T[0:N_GATES, :]

    return _pcall(body, name=name, out_shape=jax.ShapeDtypeStruct((N_GATES, S), F32),
                  in_specs=[pl.BlockSpec((BLK, LANES), lambda c: (c, P_G // LANES))],
                  out_specs=pl.BlockSpec((N_GATES, BLK), lambda c: (0, c)), grid=(S // BLK,), sem=("parallel",))(proj)


def _gate_setup(gc_ref, gr_ref, bgc_ref, bgr_ref, reverse):
    gc = gc_ref[...] + bgc_ref[...]
    gr = gr_ref[...] + bgr_ref[...]
    bc = _scan_sum(_log_sigmoid(gc), 0, reverse)
    br = _scan_sum(_log_sigmoid(gr), 1, reverse)
    return gc, gr, bc, br


def _head_gates(gc, gr, bc, br, h, m_in, reverse, tri):
    io = (M_HEADS if reverse else 0) + h
    fo = (3 * M_HEADS if reverse else 2 * M_HEADS) + h
    last = 0 if reverse else BLK - 1
    b_col, b_row = bc[:, fo:fo + 1], br[fo:fo + 1, :]
    ig_col, ig_row = gc[:, io:io + 1], gr[io:io + 1, :]
    logd = jnp.where(tri, b_col - b_row + ig_row, NEG)
    m_t = jnp.maximum(b_col + m_in, jnp.max(logd, axis=1, keepdims=True))
    dm = jnp.exp(logd - m_t)
    gi = jnp.exp(b_col + m_in - m_t)
    b_last = b_row[:, last:last + 1]
    logw = b_last - b_row + ig_row
    m_new = jnp.maximum(b_last + m_in, jnp.max(logw, axis=1, keepdims=True))
    w_col = jnp.exp(b_last - b_col + ig_col - m_new)
    dec = jnp.exp(b_last + m_in - m_new)
    return io, fo, m_t, dm, gi, m_new, w_col, dec


def _tri_mask(reverse):
    rows = lax.broadcasted_iota(jnp.int32, (BLK, BLK), 0)
    cols = lax.broadcasted_iota(jnp.int32, (BLK, BLK), 1)
    return (cols >= rows) if reverse else (cols <= rows)


def mlstm_fwd(qk, proj, gates_r, bg_c, bg_r, *, reverse, name):
    S = qk.shape[0]
    nc = S // BLK
    kscale = M_HEAD_DIM ** -0.5
    cidx = (lambda c: nc - 1 - c) if reverse else (lambda c: c)

    def body(qk_ref, v_ref, gc_ref, gr_ref, bgc_ref, bgr_ref, h_ref, den_ref, cst_ref, nm_ref, c_sc, n_sc, m_sc):
        @pl.when(pl.program_id(0) == 0)
        def _():
            c_sc[...] = jnp.zeros_like(c_sc)
            n_sc[...] = jnp.zeros_like(n_sc)
            m_sc[...] = jnp.zeros_like(m_sc)

        gc, gr, bc, br = _gate_setup(gc_ref, gr_ref, bgc_ref, bgr_ref, reverse)
        tri = _tri_mask(reverse)
        lane = lax.broadcasted_iota(jnp.int32, (BLK, LANES), 1)
        den_tile = jnp.zeros((BLK, LANES), F32)
        for h in range(M_HEADS):
            cs = slice(h * M_HEAD_DIM, (h + 1) * M_HEAD_DIM)
            m_in = m_sc[h][:, 0:1]
            _, _, m_t, dm, gi, m_new, w_col, dec = _head_gates(gc, gr, bc, br, h, m_in, reverse, tri)
            q = qk_ref[:, cs]
            k = qk_ref[:, M_WIDTH + h * M_HEAD_DIM:M_WIDTH + (h + 1) * M_HEAD_DIM] * kscale
            v = v_ref[:, cs]
            c_in, n_in = c_sc[h], n_sc[h]
            cst_ref[h] = c_in
            nm_ref[h, 0:1, :] = n_in
            nm_ref[h, 1:2, :] = m_sc[h]
            qb, kb, vb = q.astype(BF16), k.astype(BF16), v.astype(BF16)
            s = _dot_nt(qb, kb) * dm
            num = _dot(s.astype(BF16), vb) + gi * _dot_nt(qb, c_in.astype(BF16))
            den = jnp.sum(s, axis=1, keepdims=True) + gi * jnp.sum(q * n_in, axis=1, keepdims=True)
            z = jnp.maximum(jnp.abs(den), jnp.exp(-m_t))
            h_ref[:, cs] = num * (1.0 / z)
            den_tile = den_tile + jnp.where(lane == h, den, 0.0)
            c_sc[h] = dec * c_in + _dot_tn((w_col * v).astype(BF16), kb)
            n_sc[h] = dec * n_in + jnp.sum(w_col * k, axis=0, keepdims=True)
            m_sc[h] = jnp.broadcast_to(m_new, (1, M_HEAD_DIM))
        den_ref[...] = den_tile

    return _pcall(
        body, name=name,
        out_shape=(jax.ShapeDtypeStruct((S, M_WIDTH), F32), jax.ShapeDtypeStruct((S, LANES), F32),
                   jax.ShapeDtypeStruct((nc, M_HEADS, M_HEAD_DIM, M_HEAD_DIM), F32),
                   jax.ShapeDtypeStruct((nc, M_HEADS, 2, M_HEAD_DIM), F32)),
        in_specs=[pl.BlockSpec((BLK, 2 * M_WIDTH), lambda c: (cidx(c), 0)),
                  pl.BlockSpec((BLK, M_WIDTH), lambda c: (cidx(c), P_VM // M_WIDTH)),
                  pl.BlockSpec((BLK, LANES), lambda c: (cidx(c), P_G // LANES)),
                  pl.BlockSpec((N_GATES, BLK), lambda c: (0, cidx(c))),
                  pl.BlockSpec((1, LANES), lambda c: (0, 0)),
                  pl.BlockSpec((N_GATES, 1), lambda c: (0, 0))],
        out_specs=(pl.BlockSpec((BLK, M_WIDTH), lambda c: (cidx(c), 0)),
                   pl.BlockSpec((BLK, LANES), lambda c: (cidx(c), 0)),
                   pl.BlockSpec((None, M_HEADS, M_HEAD_DIM, M_HEAD_DIM), lambda c: (cidx(c), 0, 0, 0)),
                   pl.BlockSpec((None, M_HEADS, 2, M_HEAD_DIM), lambda c: (cidx(c), 0, 0, 0))),
        grid=(nc,),
        scratch=[pltpu.VMEM((M_HEADS, M_HEAD_DIM, M_HEAD_DIM), F32), pltpu.VMEM((M_HEADS, 1, M_HEAD_DIM), F32),
                 pltpu.VMEM((M_HEADS, 1, M_HEAD_DIM), F32)],
        sem=("arbitrary",))(qk, proj, proj, gates_r, bg_c, bg_r)


def mlstm_bwd(qk, proj, gates_r, bg_c, bg_r, hdir, den, cst, nm, dh, *, reverse, name, comm=None):
    S = qk.shape[0]
    nc = S // BLK
    kscale = M_HEAD_DIM ** -0.5
    cidx = (lambda c: c) if reverse else (lambda c: nc - 1 - c)
    last = 0 if reverse else BLK - 1

    def body(qk_ref, v_ref, gc_ref, gr_ref, bgc_ref, bgr_ref, h_ref, den_ref, cst_ref, nm_ref, dh_ref,
             dqk_ref, dv_ref, dgc_ref, dgr_ref, dc_sc, dn_sc):
        @pl.when(pl.program_id(0) == 0)
        def _():
            dc_sc[...] = jnp.zeros_like(dc_sc)
            dn_sc[...] = jnp.zeros_like(dn_sc)

        gc, gr, bc, br = _gate_setup(gc_ref, gr_ref, bgc_ref, bgr_ref, reverse)
        tri = _tri_mask(reverse)
        lane_c = lax.broadcasted_iota(jnp.int32, (BLK, LANES), 1)
        row_c = lax.broadcasted_iota(jnp.int32, (BLK, 1), 0)
        row_r = lax.broadcasted_iota(jnp.int32, (N_GATES, BLK), 0)
        db_c = jnp.zeros((BLK, LANES), F32)
        dig_c = jnp.zeros((BLK, LANES), F32)
        db_r = jnp.zeros((N_GATES, BLK), F32)
        dig_r = jnp.zeros((N_GATES, BLK), F32)
        for h in range(M_HEADS):
            cs = slice(h * M_HEAD_DIM, (h + 1) * M_HEAD_DIM)
            ks = slice(M_WIDTH + h * M_HEAD_DIM, M_WIDTH + (h + 1) * M_HEAD_DIM)
            m_in = nm_ref[h, 1:2, 0:1]
            io, fo, m_t, dm, gi, m_new, w_col, dec = _head_gates(gc, gr, bc, br, h, m_in, reverse, tri)
            q = qk_ref[:, cs]
            k = qk_ref[:, ks] * kscale
            v = v_ref[:, cs]
            c_in, n_in = cst_ref[h], nm_ref[h, 0:1, :]
            qb, kb, vb, cb = q.astype(BF16), k.astype(BF16), v.astype(BF16), c_in.astype(BF16)
            s = _dot_nt(qb, kb) * dm
            den_h = den_ref[:, h:h + 1]
            emt = jnp.exp(-m_t)
            rz = 1.0 / jnp.maximum(jnp.abs(den_h), emt)
            dhh = dh_ref[:, cs]
            dnum = dhh * rz
            hdh = jnp.sum(dhh * h_ref[:, cs], axis=1, keepdims=True)
            dden = jnp.where(jnp.abs(den_h) > emt, -hdh * rz * jnp.sign(den_h), 0.0)
            dnb = dnum.astype(BF16)
            ds = _dot_nt(dnb, vb) + dden
            e = ds * s
            dsd = (ds * dm).astype(BF16)
            gd = (gi * dnum).astype(BF16)
            gdd = gi * dden
            dq = _dot(dsd, kb) + _dot(gd, cb) + gdd * n_in
            dk = _dot_tn(dsd, qb)
            dv = _dot_tn(s.astype(BF16), dnb)
            dc_in = _dot_tn(gd, qb)
            dn_in = jnp.sum(gdd * q, axis=0, keepdims=True)
            cq = _dot_nt(qb, cb)
            dg = jnp.sum(dnum * cq, axis=1, keepdims=True) + dden * jnp.sum(q * n_in, axis=1, keepdims=True)
            eg = dg * gi
            dco, dno = dc_sc[h], dn_sc[h]
            dcob = dco.astype(BF16)
            dwv = _dot_nt(kb, dcob)
            dv = dv + w_col * dwv
            dw = jnp.sum(v * dwv, axis=1, keepdims=True) + jnp.sum(k * dno, axis=1, keepdims=True)
            dk = dk + _dot((w_col * v).astype(BF16), dcob) + w_col * dno
            ew = dw * w_col
            ddec = (jnp.sum(jnp.sum(dco * c_in, axis=1, keepdims=True), axis=0, keepdims=True)
                    + jnp.sum(dno * n_in, axis=1, keepdims=True))
            dc_sc[h] = dec * dco + dc_in
            dn_sc[h] = dec * dno + dn_in
            dqk_ref[:, cs] = dq
            dqk_ref[:, ks] = dk * kscale
            dv_ref[:, cs] = dv
            csum = jnp.sum(e, axis=0, keepdims=True)
            db_last = jnp.sum(ew, axis=0, keepdims=True) + ddec * dec
            db_col = jnp.sum(e, axis=1, keepdims=True) + eg - ew + jnp.where(row_c == last, db_last, 0.0)
            db_c = db_c + jnp.where(lane_c == fo, db_col, 0.0)
            dig_c = dig_c + jnp.where(lane_c == io, ew, 0.0)
            db_r = db_r + jnp.where(row_r == fo, -csum, 0.0)
            dig_r = dig_r + jnp.where(row_r == io, csum, 0.0)
        dgc_ref[...] = dig_c + _scan_sum(db_c, 0, not reverse) * _sigmoid(-gc)
        dgr_ref[...] = dig_r + _scan_sum(db_r, 1, not reverse) * _sigmoid(-gr)

    chunk = lambda w, col=0: pl.BlockSpec((BLK, w), lambda c: (cidx(c), col))
    return _pcall(
        body, name=name,
        out_shape=(jax.ShapeDtypeStruct((S, 2 * M_WIDTH), F32), jax.ShapeDtypeStruct((S, M_WIDTH), F32),
                   jax.ShapeDtypeStruct((S, LANES), F32), jax.ShapeDtypeStruct((N_GATES, S), F32)),
        in_specs=[chunk(2 * M_WIDTH), chunk(M_WIDTH, P_VM // M_WIDTH), chunk(LANES, P_G // LANES),
                  pl.BlockSpec((N_GATES, BLK), lambda c: (0, cidx(c))),
                  pl.BlockSpec((1, LANES), lambda c: (0, 0)),
                  pl.BlockSpec((N_GATES, 1), lambda c: (0, 0)),
                  chunk(M_WIDTH), chunk(LANES),
                  pl.BlockSpec((None, M_HEADS, M_HEAD_DIM, M_HEAD_DIM), lambda c: (cidx(c), 0, 0, 0)),
                  pl.BlockSpec((None, M_HEADS, 2, M_HEAD_DIM), lambda c: (cidx(c), 0, 0, 0)),
                  chunk(M_WIDTH)],
        out_specs=(chunk(2 * M_WIDTH), chunk(M_WIDTH), chunk(LANES),
                   pl.BlockSpec((N_GATES, BLK), lambda c: (0, cidx(c)))),
        grid=(nc,),
        scratch=[pltpu.VMEM((M_HEADS, M_HEAD_DIM, M_HEAD_DIM), F32), pltpu.VMEM((M_HEADS, 1, M_HEAD_DIM), F32)],
        sem=("arbitrary",), comm=comm)(qk, proj, proj, gates_r, bg_c, bg_r, hdir, den, cst, nm, dh)


def headnorm_fwd(hf, hb, proj, mnorm, *, name):
    S = hf.shape[0]
    tm = _tile(S, (512, 256, 128))

    def body(hf_ref, hb_ref, om_ref, mn_ref, y_ref):
        for h in range(M_HEADS):
            cs = slice(h * M_HEAD_DIM, (h + 1) * M_HEAD_DIM)
            hm = hf_ref[:, cs] + hb_ref[:, cs]
            r = lax.rsqrt(jnp.mean(hm * hm, axis=-1, keepdims=True) + EPS)
            y_ref[:, cs] = (_sigmoid(om_ref[:, cs]) * ((hm * r) * mn_ref[:, cs])).astype(y_ref.dtype)

    row = pl.BlockSpec((tm, M_WIDTH), lambda i: (i, 0))
    return _pcall(body, name=name, out_shape=jax.ShapeDtypeStruct((S, M_WIDTH), BF16),
                  in_specs=[row, row, pl.BlockSpec((tm, M_WIDTH), lambda i: (i, P_OM // M_WIDTH)),
                            pl.BlockSpec((1, M_WIDTH), lambda i: (0, 0))],
                  out_specs=row, grid=(S // tm,), sem=("parallel",))(hf, hb, proj, mnorm)


def headnorm_bwd(hf, hb, proj, mnorm, dy, *, name):
    S = hf.shape[0]
    tm = _tile(S, (512, 256, 128))

    def body(hf_ref, hb_ref, om_ref, mn_ref, dy_ref, dh_ref, dom_ref, dmn_ref):
        @pl.when(pl.program_id(0) == 0)
        def _():
            dmn_ref[...] = jnp.zeros_like(dmn_ref)

        for h in range(M_HEADS):
            cs = slice(h * M_HEAD_DIM, (h + 1) * M_HEAD_DIM)
            hm = hf_ref[:, cs] + hb_ref[:, cs]
            r = lax.rsqrt(jnp.mean(hm * hm, axis=-1, keepdims=True) + EPS)
            xh = hm * r
            so = _sigmoid(om_ref[:, cs])
            d = dy_ref[:, cs]
            mn = mn_ref[:, cs]
            dom_ref[:, cs] = d * (xh * mn) * (so * (1.0 - so))
            dxm = d * so
            dmn_ref[:, cs] += jnp.sum(dxm * xh, axis=0, keepdims=True)
            dxh = dxm * mn
            dh_ref[:, cs] = r * (dxh - xh * jnp.mean(dxh * xh, axis=-1, keepdims=True))

    row = pl.BlockSpec((tm, M_WIDTH), lambda i: (i, 0))
    vec = pl.BlockSpec((1, M_WIDTH), lambda i: (0, 0))
    return _pcall(body, name=name,
                  out_shape=(jax.ShapeDtypeStruct((S, M_WIDTH), F32), jax.ShapeDtypeStruct((S, M_WIDTH), F32),
                             jax.ShapeDtypeStruct((1, M_WIDTH), F32)),
                  in_specs=[row, row, pl.BlockSpec((tm, M_WIDTH), lambda i: (i, P_OM // M_WIDTH)), vec,
                            pl.BlockSpec((tm, M_WIDTH), lambda i: (i, 1))],
                  out_specs=(row, row, vec), grid=(S // tm,), sem=("arbitrary",))(hf, hb, proj, mnorm, dy)


def _place():
    return lax.axis_index("x"), lax.axis_index("y"), lax.axis_index("c")


def _ag_plan(x_refs, out_refs, sems):
    send_sems, recv_sems, local_sems = sems
    T = len(x_refs)
    x, y, c = _place()
    me, sibling = (x, y, c), (x, y, 1 - c)
    chips = [(1 - x, y), (x, 1 - y), (1 - x, 1 - y)]

    def copy(t, k, block, to, src=None):
        px, py, pc = block
        dst = out_refs[t].at[4 * px + 2 * py + pc]
        return pltpu.make_async_remote_copy(
            src_ref=dst if src is None else src, dst_ref=dst, send_sem=send_sems.at[7 * t + k],
            recv_sem=recv_sems.at[7 * t + k], device_id=to, device_id_type=MESH)

    mine = [pltpu.make_async_copy(x_refs[t], out_refs[t].at[4 * x + 2 * y + c], local_sems.at[t]) for t in range(T)]
    first = []
    for t in range(T):
        first.append(copy(t, 0, me, sibling, src=x_refs[t]))
        first += [copy(t, 1 + j, me, (*chip, c), src=x_refs[t]) for j, chip in enumerate(chips)]
    landed = [copy(t, 1 + j, (*chip, c), me) for j, chip in enumerate(chips) for t in range(T)]
    passed = [copy(t, 4 + j, (*chip, c), sibling) for j, chip in enumerate(chips) for t in range(T)]
    from_sibling = [copy(t, 0, sibling, me) for t in range(T)]
    from_sibling += [copy(t, 4 + j, (*chip, 1 - c), me) for j, chip in enumerate(chips) for t in range(T)]
    return mine, first, landed, passed, from_sibling


def _ag_start(x_refs, out_refs, sems):
    mine, first, _, _, _ = _ag_plan(x_refs, out_refs, sems)
    for cp in mine + first:
        cp.start()


def _ag_forward(x_refs, out_refs, sems):
    _, _, landed, passed, _ = _ag_plan(x_refs, out_refs, sems)
    for got, on in zip(landed, passed):
        got.wait_recv()
        on.start()


def _ag_finish(x_refs, out_refs, sems):
    mine, first, _, passed, from_sibling = _ag_plan(x_refs, out_refs, sems)
    for cp in from_sibling:
        cp.wait_recv()
    for cp in first + passed:
        cp.wait_send()
    for cp in mine:
        cp.wait()


def ag_comm(shards):
    T = len(shards)
    return Comm(shards, [jax.ShapeDtypeStruct((N_DEV,) + s.shape, s.dtype) for s in shards],
                [pltpu.SemaphoreType.DMA((7 * T,)), pltpu.SemaphoreType.DMA((7 * T,)), pltpu.SemaphoreType.DMA((T,))],
                [_ag_start, _ag_forward, _ag_finish])


def _pair_plan(g_refs, out_refs, sems):
    send_sems, recv_sems = sems
    x, y, c = _place()
    return [pltpu.make_async_remote_copy(
        src_ref=g_refs[t].at[1 - c], dst_ref=out_refs[t], send_sem=send_sems.at[t], recv_sem=recv_sems.at[t],
        device_id=(x, y, 1 - c), device_id_type=MESH) for t in range(len(g_refs))]


def _pair_start(g_refs, out_refs, sems):
    for cp in _pair_plan(g_refs, out_refs, sems):
        cp.start()


def _pair_finish(g_refs, out_refs, sems):
    for cp in _pair_plan(g_refs, out_refs, sems):
        cp.wait()


def pair_comm(grads):
    T = len(grads)
    return Comm(grads, [jax.ShapeDtypeStruct(g.shape[1:], g.dtype) for g in grads],
                [pltpu.SemaphoreType.DMA((T,)), pltpu.SemaphoreType.DMA((T,))], [_pair_start, _pair_finish])


def _chip_plan(p_refs, out_refs, sems):
    send_sems, recv_sems, local_sems = sems
    T = len(p_refs)
    x, y, c = _place()
    mychip = 2 * x + y
    chips = [(1 - x, y), (x, 1 - y), (1 - x, 1 - y)]
    mine = [pltpu.make_async_copy(p_refs[t].at[mychip], out_refs[t].at[mychip], local_sems.at[t]) for t in range(T)]
    cps = [pltpu.make_async_remote_copy(
        src_ref=p_refs[t].at[2 * px + py], dst_ref=out_refs[t].at[mychip], send_sem=send_sems.at[3 * t + j],
        recv_sem=recv_sems.at[3 * t + j], device_id=(px, py, c), device_id_type=MESH)
        for t in range(T) for j, (px, py) in enumerate(chips)]
    return mine, cps


def _chip_start(p_refs, out_refs, sems):
    mine, cps = _chip_plan(p_refs, out_refs, sems)
    for cp in mine + cps:
        cp.start()


def _chip_finish(p_refs, out_refs, sems):
    mine, cps = _chip_plan(p_refs, out_refs, sems)
    for cp in cps + mine:
        cp.wait()


def chip_comm(parts):
    T = len(parts)
    return Comm(parts, [jax.ShapeDtypeStruct(p.shape, p.dtype) for p in parts],
                [pltpu.SemaphoreType.DMA((3 * T,)), pltpu.SemaphoreType.DMA((3 * T,)), pltpu.SemaphoreType.DMA((T,))],
                [_chip_start, _chip_finish])


PAIR_ADD_BLOCK_BYTES = 4 * 1024 * 1024


def pair_add(g, recv, core, *, name):
    _, nchip, R, C = g.shape
    tr = R if R * C * g.dtype.itemsize <= PAIR_ADD_BLOCK_BYTES else _tile(R, (512, 256, 128, 64))

    def body(c_ref, a_ref, b_ref, o_ref):
        o_ref[...] = (a_ref[...].astype(F32) + b_ref[...].astype(F32)).astype(o_ref.dtype)

    grid_spec = pltpu.PrefetchScalarGridSpec(
        num_scalar_prefetch=1, grid=(nchip, R // tr),
        in_specs=[pl.BlockSpec((None, None, tr, C), lambda k, i, c_ref: (c_ref[0], k, i, 0)),
                  pl.BlockSpec((None, tr, C), lambda k, i, c_ref: (k, i, 0))],
        out_specs=pl.BlockSpec((None, tr, C), lambda k, i, c_ref: (k, i, 0)))
    return pl.pallas_call(body, name=name, out_shape=jax.ShapeDtypeStruct(recv.shape, recv.dtype),
                          grid_spec=grid_spec,
                          compiler_params=pltpu.CompilerParams(dimension_semantics=("parallel", "parallel"),
                                                               vmem_limit_bytes=V7X_VMEM_LIMIT))(core, g, recv)


def _adam_math(w, g, m, v):
    m = ADAM_B1 * m + (1.0 - ADAM_B1) * g
    v = ADAM_B2 * v + (1.0 - ADAM_B2) * (g * g)
    m_hat = m / (1.0 - ADAM_B1 ** ADAM_STEP)
    v_hat = v / (1.0 - ADAM_B2 ** ADAM_STEP)
    delta = -ADAM_LR * (m_hat / (jnp.sqrt(v_hat) + ADAM_EPS) + ADAM_WD * w)
    return delta, m, v


def adam_update(w, parts, m, v, *, name):
    P, R, _ = parts.shape
    tr = _tile(R, (1024, 512, 256, 128, 64, 32, 16, 8))

    def body(w_ref, p_ref, m_ref, v_ref, g_ref, d_ref, nm_ref, nv_ref):
        g = p_ref[0]
        for k in range(1, P):
            g = g + p_ref[k]
        d, nm, nv = _adam_math(w_ref[...], g, m_ref[...], v_ref[...])
        g_ref[...] = g
        d_ref[...] = d
        nm_ref[...] = nm
        nv_ref[...] = nv

    row = pl.BlockSpec((tr, LANES), lambda i: (i, 0))
    shp = jax.ShapeDtypeStruct((R, LANES), F32)
    return _pcall(body, name=name, out_shape=(shp, shp, shp, shp),
                  in_specs=[row, pl.BlockSpec((P, tr, LANES), lambda i: (0, i, 0)), row, row],
                  out_specs=(row, row, row, row), grid=(R // tr,), sem=("parallel",))(w, parts, m, v)


ADAM_STEP_BYTES = 6 * 1024 * 1024


def adam_tensor(w, parts, m, v, *, name):
    L, R, C = w.shape
    per_row = L * C * (7 * 4 + 4 * parts[0].dtype.itemsize)
    tr = R
    for cand in (256, 128, 64, 32, 16):
        if R % cand == 0 and cand * per_row <= ADAM_STEP_BYTES:
            tr = cand
            break

    def body(*refs):
        w_ref, m_ref, v_ref = refs[:3]
        p_refs = refs[3:3 + L]
        g_ref, d_ref, nm_ref, nv_ref = refs[3 + L:]
        for l in range(L):
            g = p_refs[l][0].astype(F32)
            for k in range(1, 4):
                g = g + p_refs[l][k].astype(F32)
            d, nm, nv = _adam_math(w_ref[l], g, m_ref[l], v_ref[l])
            g_ref[l] = g
            d_ref[l] = d
            nm_ref[l] = nm
            nv_ref[l] = nv

    blk = pl.BlockSpec((L, tr, C), lambda i: (0, i, 0))
    pblk = pl.BlockSpec((4, tr, C), lambda i: (0, i, 0))
    shp = jax.ShapeDtypeStruct((L, R, C), F32)
    return _pcall(body, name=name, out_shape=(shp, shp, shp, shp), in_specs=[blk, blk, blk] + [pblk] * L,
                  out_specs=(blk, blk, blk, blk), grid=(R // tr,), sem=("parallel",))(w, m, v, *parts)


def _rows(n_elems):
    r = -(-n_elems // LANES)
    return -(-r // 1024) * 1024 if r > 1024 else -(-r // 16) * 16


def _flat(a, dtype=None):
    n = a.size
    r = _rows(n)
    f = a.reshape(-1)
    if dtype is not None:
        f = f.astype(dtype)
    if r * LANES != n:
        f = jnp.pad(f, (0, r * LANES - n))
    return f.reshape(r, LANES)


def _gathered_cols(g):
    n, rows, cols = g.shape
    return g.transpose(1, 0, 2).reshape(rows, n * cols)


def _owner_cols(dw, dtype):
    rows = dw.shape[0]
    cols = dw.shape[1] // N_DEV
    return dw.reshape(rows, N_DEV // 2, 2, cols).transpose(2, 1, 0, 3).astype(dtype)


_IN_NAT = dict(qa=(0, 1024), ka=(1024, 1280), va=(1280, 1536), qm=(1536, 2560), km=(2560, 3584),
               vm=(3584, 4608), om=(4608, 5632), g=(5632, 5648))


def _permute_w_in(w):
    sl = lambda k: w[:, _IN_NAT[k][0]:_IN_NAT[k][1]]
    pad = jnp.zeros((w.shape[0], P_WIDTH - P_G - N_GATES), w.dtype)
    return jnp.concatenate([sl("qm"), sl("km"), sl("qa"), sl("vm"), sl("om"), sl("ka"), sl("va"), sl("g"), pad],
                           axis=1)


def _unpermute_dw_in(dw):
    qm, km = dw[:, P_QK:P_QK + 1024], dw[:, P_QK + 1024:P_QK + 2048]
    return jnp.concatenate([dw[:, P_QA:P_QA + 1024], dw[:, P_KA:P_KA + 256], dw[:, P_VA:P_VA + 256], qm, km,
                            dw[:, P_VM:P_VM + 1024], dw[:, P_OM:P_OM + 1024], dw[:, P_G:P_G + N_GATES]], axis=1)


BIG = ("ffn1_w_gate", "ffn1_w_up", "ffn1_w_down", "w_in", "w_out", "ffn2_w_gate", "ffn2_w_up", "ffn2_w_down")
COLUMN_SHARDED_FFN = ("ffn1_w_gate", "ffn1_w_up", "ffn2_w_gate", "ffn2_w_up")
SMALL = ("ffn1_norm_pre", "ffn1_norm_post", "mix_norm_pre", "mix_norm_post", "b_gate", "attn_sink", "mlstm_norm",
         "ffn2_norm_pre", "ffn2_norm_post")
WEIGHTS = ("ffn1_norm_pre", "ffn1_norm_post", "ffn1_w_gate", "ffn1_w_up", "ffn1_w_down", "mix_norm_pre",
           "mix_norm_post", "w_in", "b_gate", "conv_w", "attn_sink", "mlstm_norm", "w_out", "ffn2_norm_pre",
           "ffn2_norm_post", "ffn2_w_gate", "ffn2_w_up", "ffn2_w_down")


GRAD_DT = BF16


def _carried(result, comm):
    return result if comm is not None else (result, None)


def _pair_adds(grads, recv, core, tag):
    return [pair_add(g, r, core, name=f"{tag}_add{t}") for t, (g, r) in enumerate(zip(grads, recv))]


def _ffn_fwd(x, g_pre, g_post, wg8, wu8, wd8, tag, gather=None):
    xn = norm_fwd(x, g_pre, name=f"{tag}_pre", out_dtype=BF16)
    comm = None if gather is None else ag_comm(gather)
    (hg, hu, act), gathered = _carried(ffn_gu(xn, wg8, wu8, name=f"{tag}_gu", comm=comm), comm)
    f = ffn_down(act, wd8, name=f"{tag}_down")
    x_new = norm_fwd(f, g_post, name=f"{tag}_post", scale=0.5, resid=x)
    return x_new, (x, xn, hg, hu, act, f), gathered


def _ffn_bwd(dx, saved, g_pre, g_post, wg8, wu8, wd8, core, tag, reduce=None, last=False):
    x, xn, hg, hu, act, f = saved
    df, dg_post = norm_bwd(dx, f, g_post, name=f"{tag}_post_b", scale=0.5, out_dtype=BF16)
    comm = None if reduce is None else pair_comm(reduce)
    dwd, recv = _carried(ffn_dwd(act, df, name=f"{tag}_dwd", out_dtype=GRAD_DT, comm=comm), comm)
    dhg, dhu = ffn_dact(df, wd8, hg, hu, name=f"{tag}_dact")
    comm = None if reduce is None else chip_comm(_pair_adds(reduce, recv, core, tag))
    (dwg, dwu), reduced = _carried(ffn_dwgu(xn, dhg, dhu, name=f"{tag}_dwgu", out_dtype=GRAD_DT, comm=comm), comm)
    own = [dwg, dwu, dwd]
    comm = None
    if last:
        recv = run_comm(pair_comm(own), name="rs1_last")
        comm = chip_comm(_pair_adds(own, recv, core, "last"))
    dxn, own_reduced = _carried(ffn_dxn(dhg, dhu, wg8, wu8, name=f"{tag}_dxn", comm=comm), comm)
    dx_new, dg_pre = norm_bwd(dxn, x, g_pre, name=f"{tag}_pre_b", resid=dx)
    return dx_new, dg_pre, dg_post, own_reduced if last else own, reduced


def _mix_fwd(x, g_pre, g_post, w_in_p, b_gate, conv_full, sink, mnorm, w_out, cos2, sin2, tag, gather_in, gather_out):
    S = x.shape[0]
    xn = norm_fwd(x, g_pre, name=f"{tag}_pre", out_dtype=BF16)
    comm = ag_comm(gather_in)
    proj, got_in = mm_nn(xn, w_in_p, name=f"{tag}_in", comm=comm)
    gates_r = gate_rows(proj, name=f"{tag}_gt")
    bg_c = jnp.pad(b_gate, (0, LANES - N_GATES)).reshape(1, LANES)
    bg_r = b_gate.reshape(N_GATES, 1)
    y_att, lse = attn_fwd(proj, cos2, sin2, sink, name=f"{tag}_att")
    qk = conv_fwd(proj, conv_full, name=f"{tag}_conv")
    hf, denf, cf, nmf = mlstm_fwd(qk, proj, gates_r, bg_c, bg_r, reverse=False, name=f"{tag}_mf")
    hb, denb, cb, nmb = mlstm_fwd(qk, proj, gates_r, bg_c, bg_r, reverse=True, name=f"{tag}_mb")
    y_m = headnorm_fwd(hf, hb, proj, mnorm.reshape(1, M_WIDTH), name=f"{tag}_hn")
    y = jnp.concatenate([y_att, y_m], axis=1)
    mo, got_out = mm_nn(y, w_out, name=f"{tag}_out", comm=ag_comm(gather_out))
    x_new = norm_fwd(mo, g_post, name=f"{tag}_post", resid=x)
    saved = (x, xn, proj, gates_r, bg_c, bg_r, lse, qk, hf, denf, cf, nmf, hb, denb, cb, nmb, y, mo)
    return x_new, saved, got_in + got_out


def _mix_bwd(dx, saved, g_pre, g_post, w_in_p, conv_full, sink, mnorm, w_out, cos2, sin2, core, tag, reduce=None):
    x, xn, proj, gates_r, bg_c, bg_r, lse, qk, hf, denf, cf, nmf, hb, denb, cb, nmb, y, mo = saved
    S = x.shape[0]
    dmo, dg_post = norm_bwd(dx, mo, g_post, name=f"{tag}_post_b", out_dtype=BF16)
    comm = None if reduce is None else pair_comm(reduce)
    dw_out, recv = _carried(mm_tn(y, dmo, name=f"{tag}_dwo", owner_rows=D_MODEL // N_DEV, out_dtype=GRAD_DT,
                                  comm=comm), comm)
    dy = mm_nt(dmo, w_out, name=f"{tag}_dy")
    mn = mnorm.reshape(1, M_WIDTH)
    dh, dom, dmn = headnorm_bwd(hf, hb, proj, mn, dy, name=f"{tag}_hn_b")
    comm = None if reduce is None else chip_comm(_pair_adds(reduce, recv, core, tag))
    (dqk_f, dv_f, dgc_f, dgr_f), reduced = _carried(
        mlstm_bwd(qk, proj, gates_r, bg_c, bg_r, hf, denf, cf, nmf, dh, reverse=False, name=f"{tag}_mf_b",
                  comm=comm), comm)
    dqk_b, dv_b, dgc_b, dgr_b = mlstm_bwd(qk, proj, gates_r, bg_c, bg_r, hb, denb, cb, nmb, dh,
                                           reverse=True, name=f"{tag}_mb_b")
    dqk_in, dconv = conv_bwd(proj, conv_full, dqk_f, dqk_b, name=f"{tag}_conv_b")
    dqa, dka, dva, dsink = attn_bwd(proj, y, dy, lse, cos2, sin2, sink, name=f"{tag}_att_b")
    dgates = dgc_f + dgc_b + jnp.pad((dgr_f + dgr_b).T, ((0, 0), (0, LANES - N_GATES)))
    dproj = jnp.concatenate([dqk_in.astype(BF16), dqa.astype(BF16), (dv_f + dv_b).astype(BF16), dom.astype(BF16),
                             dka.astype(BF16), dva.astype(BF16), dgates.astype(BF16),
                             jnp.zeros((S, P_WIDTH - P_G - LANES), BF16)], axis=1)
    db_gate = colsum(dgates, name=f"{tag}_dbg")[0, :N_GATES]
    dw_in = mm_tn(xn, dproj, name=f"{tag}_dwi")
    dxn = mm_nt(dproj, w_in_p, name=f"{tag}_dxn")
    dx_new, dg_pre = norm_bwd(dxn, x, g_pre, name=f"{tag}_pre_b", resid=dx)
    grads = [_owner_cols(_unpermute_dw_in(dw_in), GRAD_DT), dw_out, _owner_cols(dconv[:CONV_WIDTH], F32)]
    return dx_new, dg_pre, dg_post, db_gate, dsink[0, :ATT_HEADS], dmn[0], grads, reduced


def colsum(a, *, name):
    S, C = a.shape
    tm = _tile(S, (512, 256, 128))

    def body(a_ref, o_ref):
        @pl.when(pl.program_id(0) == 0)
        def _():
            o_ref[...] = jnp.zeros_like(o_ref)

        o_ref[...] += jnp.sum(a_ref[...], axis=0, keepdims=True)

    return _pcall(body, name=name, out_shape=jax.ShapeDtypeStruct((1, C), F32),
                  in_specs=[pl.BlockSpec((tm, C), lambda i: (i, 0))], out_specs=pl.BlockSpec((1, C), lambda i: (0, 0)),
                  grid=(S // tm,), sem=("arbitrary",))(a)


def _layer_shards(W, l):
    pad_r = lambda a: jnp.pad(a.astype(BF16), ((0, FSP - FS), (0, 0)))
    return [pad_r(W["ffn1_w_gate"][l]), pad_r(W["ffn1_w_up"][l]), pad_r(W["ffn1_w_down"][l]),
            W["w_in"][l].astype(BF16), W["w_out"][l].astype(BF16),
            pad_r(W["ffn2_w_gate"][l]), pad_r(W["ffn2_w_up"][l]), pad_r(W["ffn2_w_down"][l])]


def kernel(x, ffn1_norm_pre, ffn1_norm_post, ffn1_w_gate, ffn1_w_up, ffn1_w_down, mix_norm_pre, mix_norm_post, w_in, b_gate, conv_w, attn_sink, mlstm_norm, w_out, ffn2_norm_pre, ffn2_norm_post, ffn2_w_gate, ffn2_w_up, ffn2_w_down, loss_target, m_ffn1_norm_pre, m_ffn1_norm_post, m_ffn1_w_gate, m_ffn1_w_up, m_ffn1_w_down, m_mix_norm_pre, m_mix_norm_post, m_w_in, m_b_gate, m_conv_w, m_attn_sink, m_mlstm_norm, m_w_out, m_ffn2_norm_pre, m_ffn2_norm_post, m_ffn2_w_gate, m_ffn2_w_up, m_ffn2_w_down, v_ffn1_norm_pre, v_ffn1_norm_post, v_ffn1_w_gate, v_ffn1_w_up, v_ffn1_w_down, v_mix_norm_pre, v_mix_norm_post, v_w_in, v_b_gate, v_conv_w, v_attn_sink, v_mlstm_norm, v_w_out, v_ffn2_norm_pre, v_ffn2_norm_post, v_ffn2_w_gate, v_ffn2_w_up, v_ffn2_w_down):
    W = dict(ffn1_norm_pre=ffn1_norm_pre, ffn1_norm_post=ffn1_norm_post, ffn1_w_gate=ffn1_w_gate,
             ffn1_w_up=ffn1_w_up, ffn1_w_down=ffn1_w_down, mix_norm_pre=mix_norm_pre, mix_norm_post=mix_norm_post,
             w_in=w_in, b_gate=b_gate, conv_w=conv_w, attn_sink=attn_sink, mlstm_norm=mlstm_norm, w_out=w_out,
             ffn2_norm_pre=ffn2_norm_pre, ffn2_norm_post=ffn2_norm_post, ffn2_w_gate=ffn2_w_gate,
             ffn2_w_up=ffn2_w_up, ffn2_w_down=ffn2_w_down)
    M1 = dict(ffn1_norm_pre=m_ffn1_norm_pre, ffn1_norm_post=m_ffn1_norm_post, ffn1_w_gate=m_ffn1_w_gate,
              ffn1_w_up=m_ffn1_w_up, ffn1_w_down=m_ffn1_w_down, mix_norm_pre=m_mix_norm_pre,
              mix_norm_post=m_mix_norm_post, w_in=m_w_in, b_gate=m_b_gate, conv_w=m_conv_w, attn_sink=m_attn_sink,
              mlstm_norm=m_mlstm_norm, w_out=m_w_out, ffn2_norm_pre=m_ffn2_norm_pre,
              ffn2_norm_post=m_ffn2_norm_post, ffn2_w_gate=m_ffn2_w_gate, ffn2_w_up=m_ffn2_w_up,
              ffn2_w_down=m_ffn2_w_down)
    V2 = dict(ffn1_norm_pre=v_ffn1_norm_pre, ffn1_norm_post=v_ffn1_norm_post, ffn1_w_gate=v_ffn1_w_gate,
              ffn1_w_up=v_ffn1_w_up, ffn1_w_down=v_ffn1_w_down, mix_norm_pre=v_mix_norm_pre,
              mix_norm_post=v_mix_norm_post, w_in=v_w_in, b_gate=v_b_gate, conv_w=v_conv_w, attn_sink=v_attn_sink,
              mlstm_norm=v_mlstm_norm, w_out=v_w_out, ffn2_norm_pre=v_ffn2_norm_pre,
              ffn2_norm_post=v_ffn2_norm_post, ffn2_w_gate=v_ffn2_w_gate, ffn2_w_up=v_ffn2_w_up,
              ffn2_w_down=v_ffn2_w_down)
    for n in COLUMN_SHARDED_FFN:
        W[n], M1[n], V2[n] = (jnp.transpose(a, (0, 2, 1)) for a in (W[n], M1[n], V2[n]))
    depth = w_in.shape[0]
    S = x.shape[1]
    xs = x[0]
    cos2, sin2 = _rope_tables(S)
    core = lax.axis_index("c").astype(jnp.int32).reshape(1)

    cs = conv_w.shape[2]
    conv_g = run_comm(ag_comm([conv_w.reshape(depth * CONV_WIDTH, cs)]), name="ag_conv")[0]
    conv_all = conv_g.reshape(N_DEV, depth, CONV_WIDTH, cs).transpose(1, 2, 0, 3)
    conv_all = conv_all.reshape(depth, CONV_WIDTH, N_DEV * cs)
    conv_all = jnp.pad(conv_all, ((0, 0), (0, CONV_HALO - CONV_WIDTH), (0, 0)))

    lw, saved = [], []
    shards = [_layer_shards(W, l) for l in range(depth)]
    ffn1_w = run_comm(ag_comm(shards[0][0:3]), name="ag_first")
    for l in range(depth):
        xs, s1, got = _ffn_fwd(xs, W["ffn1_norm_pre"][l], W["ffn1_norm_post"][l], *ffn1_w, "f1", shards[l][3:5])
        mix_w = (_permute_w_in(_gathered_cols(got[0])), got[1].reshape(D_MODEL, D_MODEL))
        xs, s2, ffn2_w = _mix_fwd(xs, W["mix_norm_pre"][l], W["mix_norm_post"][l], mix_w[0], W["b_gate"][l],
                                  conv_all[l], W["attn_sink"][l], W["mlstm_norm"][l], mix_w[1], cos2, sin2, "mx",
                                  shards[l][5:7], shards[l][7:8])
        xs, s3, got = _ffn_fwd(xs, W["ffn2_norm_pre"][l], W["ffn2_norm_post"][l], *ffn2_w, "f2",
                               shards[l + 1][0:3] if l + 1 < depth else None)
        lw.append(dict(ffn1=ffn1_w, mix=mix_w, ffn2=ffn2_w))
        saved.append((s1, s2, s3))
        ffn1_w = got

    dx, loss_part = loss_fwd_bwd(xs, loss_target[0], name="loss")

    F1, MX, F2 = BIG[0:3], (BIG[3], BIG[4], "conv_w"), BIG[5:8]
    names = BIG + ("conv_w",)
    parts = {n: [None] * depth for n in names}
    small_parts = [None] * depth
    waiting = None
    for l in reversed(range(depth)):
        wl = lw[l]
        s1, s2, s3 = saved[l]
        dx, dpre2, dpost2, grads2, reduced = _ffn_bwd(dx, s3, W["ffn2_norm_pre"][l], W["ffn2_norm_post"][l],
                                                      *wl["ffn2"], core, "f2", waiting)
        if waiting is not None:
            for n, r in zip(F1, reduced):
                parts[n][l + 1] = r
        dx, dpre_m, dpost_m, db_gate, dsink, dmn, grads_m, reduced = _mix_bwd(
            dx, s2, W["mix_norm_pre"][l], W["mix_norm_post"][l], wl["mix"][0], conv_all[l], W["attn_sink"][l],
            W["mlstm_norm"][l], wl["mix"][1], cos2, sin2, core, "mx", grads2)
        for n, r in zip(F2, reduced):
            parts[n][l] = r
        dx, dpre1, dpost1, waiting, reduced = _ffn_bwd(dx, s1, W["ffn1_norm_pre"][l], W["ffn1_norm_post"][l],
                                                       *wl["ffn1"], core, "f1", grads_m, last=(l == 0))
        for n, r in zip(MX, reduced):
            parts[n][l] = r
        small_parts[l] = dict(ffn1_norm_pre=dpre1[0], ffn1_norm_post=dpost1[0], mix_norm_pre=dpre_m[0],
                              mix_norm_post=dpost_m[0], b_gate=db_gate, attn_sink=dsink, mlstm_norm=dmn,
                              ffn2_norm_pre=dpre2[0], ffn2_norm_post=dpost2[0])
    for n, r in zip(F1, waiting):
        parts[n][0] = r

    outs = {k: {} for k in ("g", "d", "m", "v")}
    for n in names:
        res = adam_tensor(W[n], parts[n], M1[n], V2[n], name=f"adam_{n}")
        for k, r in zip(("g", "d", "m", "v"), res):
            outs[k][n] = jnp.transpose(r, (0, 2, 1)) if n in COLUMN_SHARDED_FFN else r
    small_out = {k: {n: [None] * depth for n in SMALL} for k in ("g", "d", "m", "v")}

    vec = jnp.concatenate([small_parts[l][n].reshape(-1) for l in range(depth) for n in SMALL]
                          + [loss_part.reshape(-1)])
    n_small = vec.shape[0]
    gathered_small = run_comm(ag_comm([_flat(vec)]), name="ag_small")[0]
    wvec = _flat(jnp.concatenate([W[n][l].reshape(-1) for l in range(depth) for n in SMALL] + [jnp.zeros((1,), F32)]))
    mvec = _flat(jnp.concatenate([M1[n][l].reshape(-1) for l in range(depth) for n in SMALL] + [jnp.zeros((1,), F32)]))
    vvec = _flat(jnp.concatenate([V2[n][l].reshape(-1) for l in range(depth) for n in SMALL] + [jnp.ones((1,), F32)]))
    res = adam_update(wvec, gathered_small, mvec, vvec, name="adam_small")
    res = [r.reshape(-1)[:n_small] for r in res]
    off = 0
    for l in range(depth):
        for n in SMALL:
            sz = W[n].shape[1]
            for k, r in zip(("g", "d", "m", "v"), res):
                small_out[k][n][l] = r[off:off + sz]
            off += sz
    loss = res[0][off]
    for k in outs:
        for n in SMALL:
            outs[k][n] = jnp.stack(small_out[k][n], axis=0)

    return (loss, dx[None], *[outs["g"][n] for n in WEIGHTS], *[outs["d"][n] for n in WEIGHTS],
            *[outs["m"][n] for n in WEIGHTS], *[outs["v"][n] for n in WEIGHTS])
```

```python
import jax
import jax.numpy as jnp
from jax import lax
from jax.experimental import pallas as pl
from jax.experimental.pallas import tpu as pltpu

F32 = jnp.float32
BF16 = jnp.bfloat16

D_MODEL = 2048
D_FF = 5632
ATT_HEADS = 8
ATT_KV_HEADS = 2
ATT_GROUP = ATT_HEADS // ATT_KV_HEADS
ATT_WIDTH = 1024
HEAD_DIM = 128
KV_WIDTH = 256
WINDOW = 128
BLK = 128
M_WIDTH = 1024
M_HEADS = 4
M_HEAD_DIM = 256
CONV_WIDTH = 5
EPS = 1e-6
ROPE_THETA = 10000.0
IN_WIDTH = 5648
N_GATES = 16
N_DEV = 8

ADAM_LR = 0.001
ADAM_B1 = 0.9
ADAM_B2 = 0.999
ADAM_EPS = 1e-08
ADAM_WD = 0.01
ADAM_STEP = 10

P_QK = 0
P_QA = 2048
P_VM = 3072
P_OM = 4096
P_KA = 5120
P_VA = 5376
P_G = 5632
P_WIDTH = 6144

LANES = 128
V7X_VMEM_LIMIT = 48 * 1024 * 1024
NEG = -1e30
MESH = pl.DeviceIdType.MESH
ANY = pl.BlockSpec(memory_space=pl.ANY)


K_TILES = (2048, 1024, 512, 256, 128)


def _tile(n, cands=(1024, 512, 256, 128)):
    for c in cands:
        if n % c == 0:
            return c
    return n


class Comm:
    def __init__(self, ins, outs, sems, phases):
        self.ins, self.outs, self.sems, self.phases = list(ins), list(outs), list(sems), list(phases)


def run_comm(comm, *, name):
    n_in, n_out = len(comm.ins), len(comm.outs)

    def body(*refs):
        ins, outs, sems = refs[:n_in], refs[n_in:n_in + n_out], refs[n_in + n_out:]
        for phase in comm.phases:
            phase(ins, outs, sems)

    return pl.pallas_call(body, name=name, out_shape=comm.outs, in_specs=[ANY] * n_in, out_specs=[ANY] * n_out,
                          scratch_shapes=comm.sems,
                          compiler_params=pltpu.CompilerParams(has_side_effects=True))(*comm.ins)


def _pcall(body, *, name, out_shape, in_specs, out_specs, grid=(), scratch=(), sem=None, comm=None):
    if comm is None:
        return pl.pallas_call(
            body, name=name, out_shape=out_shape, in_specs=in_specs, out_specs=out_specs, grid=grid,
            scratch_shapes=list(scratch),
            compiler_params=pltpu.CompilerParams(dimension_semantics=sem, vmem_limit_bytes=V7X_VMEM_LIMIT))
    multi = isinstance(out_shape, (tuple, list))
    outs = list(out_shape) if multi else [out_shape]
    ospecs = list(out_specs) if multi else [out_specs]
    n_in, n_out, n_scr = len(in_specs), len(outs), len(scratch)
    nci, nco = len(comm.ins), len(comm.outs)
    steps = 1
    for g in grid:
        steps *= g
    n_ph = len(comm.phases)
    at = [0, steps - 1] if n_ph == 2 else [0, (3 * steps) // 4, steps - 1]

    def wrapped(*refs):
        ins, cins = refs[:n_in], refs[n_in:n_in + nci]
        o0 = n_in + nci
        res, couts = refs[o0:o0 + n_out], refs[o0 + n_out:o0 + n_out + nco]
        s0 = o0 + n_out + nco
        scr, csems = refs[s0:s0 + n_scr], refs[s0 + n_scr:]
        lin = 0
        for k, g in enumerate(grid):
            lin = lin * g + pl.program_id(k)

        @pl.when(lin == at[0])
        def _():
            comm.phases[0](cins, couts, csems)

        body(*ins, *res, *scr)
        for p in range(1, n_ph):
            @pl.when(lin == at[p])
            def _(p=p):
                comm.phases[p](cins, couts, csems)

    call = pl.pallas_call(
        wrapped, name=name, out_shape=outs + comm.outs, in_specs=list(in_specs) + [ANY] * nci,
        out_specs=ospecs + [ANY] * nco, grid=grid, scratch_shapes=list(scratch) + comm.sems,
        compiler_params=pltpu.CompilerParams(dimension_semantics=("arbitrary",) * len(grid),
                                             vmem_limit_bytes=V7X_VMEM_LIMIT, has_side_effects=True))

    def run(*args):
        got = list(call(*args, *comm.ins))
        return (tuple(got[:n_out]) if multi else got[0]), got[n_out:]

    return run


def _dot(a, b):
    return jnp.dot(a, b, preferred_element_type=F32)


def _dot_nt(a, b):
    return lax.dot_general(a, b, (((1,), (1,)), ((), ())), preferred_element_type=F32)


def _dot_tn(a, b):
    return lax.dot_general(a, b, (((0,), (0,)), ((), ())), preferred_element_type=F32)


def _sigmoid(x):
    return 1.0 / (1.0 + jnp.exp(-x))


def mm_nn(a, b, *, name, out_dtype=F32, comm=None):
    M, K = a.shape
    N = b.shape[1]
    tm, tk, tn = _tile(M), _tile(K, K_TILES), _tile(N)
    nk = K // tk

    def body(a_ref, b_ref, o_ref, acc):
        k = pl.program_id(2)

        @pl.when(k == 0)
        def _():
            acc[...] = jnp.zeros_like(acc)

        acc[...] += _dot(a_ref[...], b_ref[...])

        @pl.when(k == nk - 1)
        def _():
            o_ref[...] = acc[...].astype(o_ref.dtype)

    return _pcall(body, name=name, out_shape=jax.ShapeDtypeStruct((M, N), out_dtype),
                  in_specs=[pl.BlockSpec((tm, tk), lambda i, j, k: (i, k)),
                            pl.BlockSpec((tk, tn), lambda i, j, k: (k, j))],
                  out_specs=pl.BlockSpec((tm, tn), lambda i, j, k: (i, j)), grid=(M // tm, N // tn, nk),
                  scratch=[pltpu.VMEM((tm, tn), F32)], sem=("parallel", "parallel", "arbitrary"), comm=comm)(a, b)


def mm_tn(a, g, *, name, owner_rows=None, out_dtype=F32, comm=None):
    M, K = a.shape
    N = g.shape[1]
    tm, tk, tn = _tile(M), _tile(K), _tile(N)
    nm = M // tm
    per_tile = 1 if owner_rows is None else tk // owner_rows

    def body(a_ref, g_ref, o_ref, acc):
        m = pl.program_id(2)

        @pl.when(m == 0)
        def _():
            acc[...] = jnp.zeros_like(acc)

        acc[...] += _dot_tn(a_ref[...], g_ref[...])

        @pl.when(m == nm - 1)
        def _():
            if owner_rows is None:
                o_ref[...] = acc[...].astype(o_ref.dtype)
            else:
                for d in range(per_tile):
                    o_ref[d % 2, d // 2] = acc[d * owner_rows:(d + 1) * owner_rows, :].astype(o_ref.dtype)

    if owner_rows is None:
        out_shape = jax.ShapeDtypeStruct((K, N), out_dtype)
        out_spec = pl.BlockSpec((tk, tn), lambda i, j, m: (i, j))
    else:
        assert per_tile % 2 == 0 and K == N_DEV * owner_rows
        out_shape = jax.ShapeDtypeStruct((2, N_DEV // 2, owner_rows, N), out_dtype)
        out_spec = pl.BlockSpec((2, per_tile // 2, owner_rows, tn), lambda i, j, m: (0, i, 0, j))
    return _pcall(body, name=name, out_shape=out_shape,
                  in_specs=[pl.BlockSpec((tm, tk), lambda i, j, m: (m, i)),
                            pl.BlockSpec((tm, tn), lambda i, j, m: (m, j))],
                  out_specs=out_spec, grid=(K // tk, N // tn, nm), scratch=[pltpu.VMEM((tk, tn), F32)],
                  sem=("parallel", "parallel", "arbitrary"), comm=comm)(a, g)


def mm_nt(a, b, *, name, out_dtype=F32):
    M, K = a.shape
    N = b.shape[0]
    tm, tn, tk = _tile(M), _tile(N), _tile(K, K_TILES)
    nk = K // tk

    def body(a_ref, b_ref, o_ref, acc):
        k = pl.program_id(2)

        @pl.when(k == 0)
        def _():
            acc[...] = jnp.zeros_like(acc)

        acc[...] += _dot_nt(a_ref[...], b_ref[...])

        @pl.when(k == nk - 1)
        def _():
            o_ref[...] = acc[...].astype(o_ref.dtype)

    return _pcall(body, name=name, out_shape=jax.ShapeDtypeStruct((M, N), out_dtype),
                  in_specs=[pl.BlockSpec((tm, tk), lambda i, j, k: (i, k)),
                            pl.BlockSpec((tn, tk), lambda i, j, k: (j, k))],
                  out_specs=pl.BlockSpec((tm, tn), lambda i, j, k: (i, j)), grid=(M // tm, N // tn, nk),
                  scratch=[pltpu.VMEM((tm, tn), F32)], sem=("parallel", "parallel", "arbitrary"))(a, b)


FS = D_FF // N_DEV
FSP = 768


def ffn_gu(xn, wg8, wu8, *, name, comm=None):
    S, D = xn.shape
    tm = _tile(S)

    def body(x_ref, wg_ref, wu_ref, hg_ref, hu_ref, act_ref):
        xv = x_ref[...]
        hg = _dot_nt(xv, wg_ref[...])
        hu = _dot_nt(xv, wu_ref[...])
        hg_ref[...] = hg.astype(BF16)
        hu_ref[...] = hu.astype(BF16)
        act_ref[...] = (hg * _sigmoid(hg) * hu).astype(BF16)

    wspec = pl.BlockSpec((None, FSP, D), lambda i, j: (j, 0, 0))
    ospec = pl.BlockSpec((None, tm, FSP), lambda i, j: (j, i, 0))
    shp = jax.ShapeDtypeStruct((N_DEV, S, FSP), BF16)
    return _pcall(body, name=name, out_shape=(shp, shp, shp),
                  in_specs=[pl.BlockSpec((tm, D), lambda i, j: (i, 0)), wspec, wspec],
                  out_specs=(ospec, ospec, ospec), grid=(S // tm, N_DEV), sem=("parallel", "arbitrary"),
                  comm=comm)(xn, wg8, wu8)


def ffn_down(act8, wd8, *, name):
    _, S, _ = act8.shape
    D = wd8.shape[2]
    tm, tn = _tile(S), D

    def body(a_ref, w_ref, o_ref):
        @pl.when(pl.program_id(2) == 0)
        def _():
            o_ref[...] = jnp.zeros_like(o_ref)

        o_ref[...] += _dot(a_ref[...], w_ref[...])

    return _pcall(body, name=name, out_shape=jax.ShapeDtypeStruct((S, D), F32),
                  in_specs=[pl.BlockSpec((None, tm, FSP), lambda i, n, j: (j, i, 0)),
                            pl.BlockSpec((None, FSP, tn), lambda i, n, j: (j, 0, n))],
                  out_specs=pl.BlockSpec((tm, tn), lambda i, n, j: (i, n)),
                  grid=(S // tm, D // tn, N_DEV), sem=("parallel", "parallel", "arbitrary"))(act8, wd8)


def ffn_dact(df, wd8, hg8, hu8, *, name):
    S, D = df.shape
    tm = _tile(S)

    def body(d_ref, w_ref, hg_ref, hu_ref, dg_ref, du_ref):
        da = _dot_nt(d_ref[...], w_ref[...])
        hg = hg_ref[...].astype(F32)
        hu = hu_ref[...].astype(F32)
        sg = _sigmoid(hg)
        dg_ref[...] = (da * hu * (sg * (1.0 + hg * (1.0 - sg)))).astype(BF16)
        du_ref[...] = (da * hg * sg).astype(BF16)

    blk = pl.BlockSpec((None, tm, FSP), lambda i, j: (j, i, 0))
    shp = jax.ShapeDtypeStruct((N_DEV, S, FSP), BF16)
    return _pcall(body, name=name, out_shape=(shp, shp),
                  in_specs=[pl.BlockSpec((tm, D), lambda i, j: (i, 0)),
                            pl.BlockSpec((None, FSP, D), lambda i, j: (j, 0, 0)), blk, blk],
                  out_specs=(blk, blk), grid=(S // tm, N_DEV), sem=("parallel", "arbitrary"))(df, wd8, hg8, hu8)


def ffn_dwd(act8, df, *, name, out_dtype, comm=None):
    _, S, _ = act8.shape
    D = df.shape[1]
    tm, tn = _tile(S), D
    nm = S // tm

    def body(a_ref, d_ref, o_ref, acc):
        m = pl.program_id(2)

        @pl.when(m == 0)
        def _():
            acc[...] = jnp.zeros_like(acc)

        acc[...] += _dot_tn(a_ref[...], d_ref[...])

        @pl.when(m == nm - 1)
        def _():
            o_ref[...] = acc[0:FS, :].astype(o_ref.dtype)

    return _pcall(body, name=name, out_shape=jax.ShapeDtypeStruct((2, N_DEV // 2, FS, D), out_dtype),
                  in_specs=[pl.BlockSpec((None, tm, FSP), lambda j, n, m: (j, m, 0)),
                            pl.BlockSpec((tm, tn), lambda j, n, m: (m, n))],
                  out_specs=pl.BlockSpec((None, None, FS, tn), lambda j, n, m: (j % 2, j // 2, 0, n)),
                  grid=(N_DEV, D // tn, nm), scratch=[pltpu.VMEM((FSP, tn), F32)],
                  sem=("parallel", "parallel", "arbitrary"), comm=comm)(act8, df)


def ffn_dwgu(xn, dg8, du8, *, name, out_dtype, comm=None):
    S, D = xn.shape
    tm, tk = _tile(S), _tile(D)
    nm = S // tm

    def body(x_ref, dg_ref, du_ref, og_ref, ou_ref, accg, accu):
        m = pl.program_id(2)

        @pl.when(m == 0)
        def _():
            accg[...] = jnp.zeros_like(accg)
            accu[...] = jnp.zeros_like(accu)

        xv = x_ref[...]
        accg[...] += _dot_tn(dg_ref[...], xv)
        accu[...] += _dot_tn(du_ref[...], xv)

        @pl.when(m == nm - 1)
        def _():
            og_ref[...] = accg[0:FS, :].astype(og_ref.dtype)
            ou_ref[...] = accu[0:FS, :].astype(ou_ref.dtype)

    blk = pl.BlockSpec((None, tm, FSP), lambda j, k, m: (j, m, 0))
    ospec = pl.BlockSpec((None, None, FS, tk), lambda j, k, m: (j % 2, j // 2, 0, k))
    shp = jax.ShapeDtypeStruct((2, N_DEV // 2, FS, D), out_dtype)
    return _pcall(body, name=name, out_shape=(shp, shp),
                  in_specs=[pl.BlockSpec((tm, tk), lambda j, k, m: (m, k)), blk, blk],
                  out_specs=(ospec, ospec), grid=(N_DEV, D // tk, nm),
                  scratch=[pltpu.VMEM((FSP, tk), F32), pltpu.VMEM((FSP, tk), F32)],
                  sem=("parallel", "parallel", "arbitrary"), comm=comm)(xn, dg8, du8)


def ffn_dxn(dg8, du8, wg8, wu8, *, name, comm=None):
    _, S, _ = dg8.shape
    D = wg8.shape[2]
    tm, tn = _tile(S), _tile(D)

    def body(dg_ref, du_ref, wg_ref, wu_ref, o_ref):
        @pl.when(pl.program_id(2) == 0)
        def _():
            o_ref[...] = jnp.zeros_like(o_ref)

        o_ref[...] += _dot(dg_ref[...], wg_ref[...]) + _dot(du_ref[...], wu_ref[...])

    blk = pl.BlockSpec((None, tm, FSP), lambda i, n, j: (j, i, 0))
    wspec = pl.BlockSpec((None, FSP, tn), lambda i, n, j: (j, 0, n))
    return _pcall(body, name=name, out_shape=jax.ShapeDtypeStruct((S, D), F32),
                  in_specs=[blk, blk, wspec, wspec], out_specs=pl.BlockSpec((tm, tn), lambda i, n, j: (i, n)),
                  grid=(S // tm, D // tn, N_DEV), sem=("parallel", "parallel", "arbitrary"),
                  comm=comm)(dg8, du8, wg8, wu8)


def norm_fwd(x, g, *, name, scale=1.0, resid=None, out_dtype=F32):
    S, D = x.shape
    tm = _tile(S, (512, 256, 128))

    def body(*refs):
        if resid is None:
            x_ref, g_ref, o_ref = refs
        else:
            x_ref, g_ref, r_ref, o_ref = refs
        xv = x_ref[...].astype(F32)
        r = lax.rsqrt(jnp.mean(xv * xv, axis=-1, keepdims=True) + EPS)
        y = (xv * r) * g_ref[...]
        if scale != 1.0:
            y = y * scale
        if resid is not None:
            y = y + r_ref[...]
        o_ref[...] = y.astype(o_ref.dtype)

    row = pl.BlockSpec((tm, D), lambda i: (i, 0))
    in_specs = [row, pl.BlockSpec((1, D), lambda i: (0, 0))]
    args = [x, g.reshape(1, D)]
    if resid is not None:
        in_specs.append(row)
        args.append(resid)
    return _pcall(body, name=name, out_shape=jax.ShapeDtypeStruct((S, D), out_dtype), in_specs=in_specs,
                  out_specs=row, grid=(S // tm,), sem=("parallel",))(*args)


def norm_bwd(dy, x, g, *, name, scale=1.0, resid=None, out_dtype=F32):
    S, D = x.shape
    tm = _tile(S, (512, 256, 128))

    def body(*refs):
        if resid is None:
            dy_ref, x_ref, g_ref, dx_ref, dg_ref = refs
        else:
            dy_ref, x_ref, g_ref, r_ref, dx_ref, dg_ref = refs

        @pl.when(pl.program_id(0) == 0)
        def _():
            dg_ref[...] = jnp.zeros_like(dg_ref)

        xv = x_ref[...].astype(F32)
        d = dy_ref[...].astype(F32)
        if scale != 1.0:
            d = d * scale
        r = lax.rsqrt(jnp.mean(xv * xv, axis=-1, keepdims=True) + EPS)
        xh = xv * r
        dg_ref[...] += jnp.sum(d * xh, axis=0, keepdims=True)
        dxh = d * g_ref[...]
        dx = r * (dxh - xh * jnp.mean(dxh * xh, axis=-1, keepdims=True))
        if resid is not None:
            dx = dx + r_ref[...]
        dx_ref[...] = dx.astype(dx_ref.dtype)

    row = pl.BlockSpec((tm, D), lambda i: (i, 0))
    vec = pl.BlockSpec((1, D), lambda i: (0, 0))
    in_specs = [row, row, vec]
    args = [dy, x, g.reshape(1, D)]
    if resid is not None:
        in_specs.append(row)
        args.append(resid)
    return _pcall(body, name=name,
                  out_shape=(jax.ShapeDtypeStruct((S, D), out_dtype), jax.ShapeDtypeStruct((1, D), F32)),
                  in_specs=in_specs, out_specs=(row, vec), grid=(S // tm,), sem=("arbitrary",))(*args)


def loss_fwd_bwd(y, target, *, name):
    S, D = y.shape
    tm = _tile(S, (512, 256, 128))

    def body(y_ref, t_ref, dy_ref, l_ref):
        @pl.when(pl.program_id(0) == 0)
        def _():
            l_ref[...] = jnp.zeros_like(l_ref)

        e = y_ref[...] - t_ref[...]
        dy_ref[...] = e * (1.0 / D)
        l_ref[...] += jnp.sum(jnp.sum(e * e, axis=1, keepdims=True), axis=0, keepdims=True) * (0.5 / D)

    row = pl.BlockSpec((tm, D), lambda i: (i, 0))
    one = pl.BlockSpec((1, 1), lambda i: (0, 0))
    return _pcall(body, name=name,
                  out_shape=(jax.ShapeDtypeStruct((S, D), F32), jax.ShapeDtypeStruct((1, 1), F32)),
                  in_specs=[row, row], out_specs=(row, one), grid=(S // tm,), sem=("arbitrary",))(y, target)


def _rope_tables(S):
    half = HEAD_DIM // 2
    inv_freq = ROPE_THETA ** (-jnp.arange(half, dtype=F32) / half)
    ang = jnp.arange(S, dtype=F32)[:, None] * inv_freq[None, :]
    cos, sin = jnp.cos(ang), jnp.sin(ang)
    return jnp.concatenate([cos, cos], axis=1), jnp.concatenate([-sin, sin], axis=1)


def _rope(x, cos2, sin2):
    return x * cos2 + pltpu.roll(x, HEAD_DIM // 2, 1) * sin2


def _unrope(d, cos2, sin2):
    return d * cos2 + pltpu.roll(d * sin2, HEAD_DIM // 2, 1)


def _nbr_specs(width, col, nb):
    return [pl.BlockSpec((BLK, width), lambda n, c=col: (jnp.maximum(n - 1, 0), c)),
            pl.BlockSpec((BLK, width), lambda n, c=col: (n, c)),
            pl.BlockSpec((BLK, width), lambda n, c=col: (jnp.minimum(n + 1, nb - 1), c))]


def attn_fwd(proj, cos2, sin2, sink, *, name):
    S = proj.shape[0]
    nb = S // BLK
    scale = HEAD_DIM ** -0.5

    def body(sink_ref, q_ref, k0, k1, k2, v0, v1, v2, c0, c1, c2, s0, s1, s2, o_ref, lse_ref):
        n = pl.program_id(0)
        cosk = jnp.concatenate([c0[...], c1[...], c2[...]], axis=0)
        sink_ = jnp.concatenate([s0[...], s1[...], s2[...]], axis=0)
        kall = jnp.concatenate([k0[...], k1[...], k2[...]], axis=0)
        vall = jnp.concatenate([v0[...], v1[...], v2[...]], axis=0)
        rows = lax.broadcasted_iota(jnp.int32, (BLK, 3 * BLK), 0)
        cols = lax.broadcasted_iota(jnp.int32, (BLK, 3 * BLK), 1)
        kpos = (n - 1) * BLK + cols
        valid = (jnp.abs(cols - BLK - rows) <= WINDOW) & (kpos >= 0) & (kpos < S)
        valid = jnp.concatenate([valid] * ATT_GROUP, axis=0)
        lane = lax.broadcasted_iota(jnp.int32, (BLK, LANES), 1)
        lse_tile = jnp.zeros((BLK, LANES), F32)
        for hk in range(ATT_KV_HEADS):
            ks = slice(hk * HEAD_DIM, (hk + 1) * HEAD_DIM)
            kh = _rope(kall[:, ks], cosk, sink_).astype(BF16)
            vh = vall[:, ks].astype(BF16)
            qs = []
            for g in range(ATT_GROUP):
                hq = hk * ATT_GROUP + g
                qs.append(_rope(q_ref[:, hq * HEAD_DIM:(hq + 1) * HEAD_DIM], c1[...], s1[...]))
            qh = jnp.concatenate(qs, axis=0).astype(BF16)
            s = _dot_nt(qh, kh) * scale
            s = jnp.where(valid, s, NEG)
            snk = jnp.concatenate(
                [jnp.full((BLK, 1), sink_ref[hk * ATT_GROUP + g], F32) for g in range(ATT_GROUP)], axis=0)
            m = jnp.maximum(jnp.max(s, axis=1, keepdims=True), snk)
            p = jnp.exp(s - m)
            l = jnp.sum(p, axis=1, keepdims=True) + jnp.exp(snk - m)
            o = _dot(p.astype(BF16), vh) * (1.0 / l)
            lse = m + jnp.log(l)
            for g in range(ATT_GROUP):
                hq = hk * ATT_GROUP + g
                o_ref[:, hq * HEAD_DIM:(hq + 1) * HEAD_DIM] = o[g * BLK:(g + 1) * BLK].astype(o_ref.dtype)
                lse_tile = lse_tile + jnp.where(lane == hq, lse[g * BLK:(g + 1) * BLK], 0.0)
        lse_ref[...] = lse_tile

    in_specs = ([pl.BlockSpec(memory_space=pltpu.SMEM),
                 pl.BlockSpec((BLK, ATT_WIDTH), lambda n: (n, P_QA // ATT_WIDTH))]
                + _nbr_specs(KV_WIDTH, P_KA // KV_WIDTH, nb) + _nbr_specs(KV_WIDTH, P_VA // KV_WIDTH, nb)
                + _nbr_specs(HEAD_DIM, 0, nb) + _nbr_specs(HEAD_DIM, 0, nb))
    return _pcall(body, name=name,
                  out_shape=(jax.ShapeDtypeStruct((S, ATT_WIDTH), BF16), jax.ShapeDtypeStruct((S, LANES), F32)),
                  in_specs=in_specs,
                  out_specs=(pl.BlockSpec((BLK, ATT_WIDTH), lambda n: (n, 0)),
                             pl.BlockSpec((BLK, LANES), lambda n: (n, 0))),
                  grid=(nb,), sem=("parallel",))(sink, proj, proj, proj, proj, proj, proj, proj,
                                                 cos2, cos2, cos2, sin2, sin2, sin2)


def attn_bwd(proj, y, dy, lse, cos2, sin2, sink, *, name):
    S = proj.shape[0]
    nb = S // BLK
    scale = HEAD_DIM ** -0.5

    def body(sink_ref, q_ref, k0, k1, k2, v0, v1, v2, o_ref, d_ref, l_ref, c0, c1, c2, s0, s1, s2,
             dq_ref, dk_ref, dv_ref, dsink_ref, dk_acc, dv_acc):
        n = pl.program_id(0)

        @pl.when(n == 0)
        def _():
            dsink_ref[...] = jnp.zeros_like(dsink_ref)
            dk_acc[...] = jnp.zeros_like(dk_acc)
            dv_acc[...] = jnp.zeros_like(dv_acc)

        @pl.when(n < nb)
        def _():
            cosk = jnp.concatenate([c0[...], c1[...], c2[...]], axis=0)
            sink_ = jnp.concatenate([s0[...], s1[...], s2[...]], axis=0)
            kall = jnp.concatenate([k0[...], k1[...], k2[...]], axis=0)
            vall = jnp.concatenate([v0[...], v1[...], v2[...]], axis=0)
            lane = lax.broadcasted_iota(jnp.int32, (1, LANES), 1)
            rows = lax.broadcasted_iota(jnp.int32, (BLK, 3 * BLK), 0)
            cols = lax.broadcasted_iota(jnp.int32, (BLK, 3 * BLK), 1)
            kpos = (n - 1) * BLK + cols
            valid = (jnp.abs(cols - BLK - rows) <= WINDOW) & (kpos >= 0) & (kpos < S)
            valid = jnp.concatenate([valid] * ATT_GROUP, axis=0)
            dsink_acc = jnp.zeros((1, LANES), F32)
            for hk in range(ATT_KV_HEADS):
                ks = slice(hk * HEAD_DIM, (hk + 1) * HEAD_DIM)
                kh = _rope(kall[:, ks], cosk, sink_).astype(BF16)
                vh = vall[:, ks].astype(BF16)
                qs, dos, lses, deltas = [], [], [], []
                for g in range(ATT_GROUP):
                    hq = hk * ATT_GROUP + g
                    hs = slice(hq * HEAD_DIM, (hq + 1) * HEAD_DIM)
                    qs.append(_rope(q_ref[:, hs], c1[...], s1[...]))
                    do = d_ref[:, hs]
                    dos.append(do)
                    lses.append(l_ref[:, hq:hq + 1])
                    deltas.append(jnp.sum(do * o_ref[:, hs].astype(F32), axis=1, keepdims=True))
                qh = jnp.concatenate(qs, axis=0).astype(BF16)
                doh = jnp.concatenate(dos, axis=0).astype(BF16)
                lseh = jnp.concatenate(lses, axis=0)
                delh = jnp.concatenate(deltas, axis=0)
                s = jnp.where(valid, _dot_nt(qh, kh) * scale, NEG)
                p = jnp.exp(s - lseh)
                dp = _dot_nt(doh, vh)
                ds = (p * (dp - delh)).astype(BF16)
                dq = _dot(ds, kh) * scale
                dk_acc[hk] += _dot_tn(ds, qh) * scale
                dv_acc[hk] += _dot_tn(p.astype(BF16), doh)
                for g in range(ATT_GROUP):
                    hq = hk * ATT_GROUP + g
                    dq_ref[:, hq * HEAD_DIM:(hq + 1) * HEAD_DIM] = _unrope(dq[g * BLK:(g + 1) * BLK], c1[...], s1[...])
                    psink = jnp.exp(sink_ref[hq] - lses[g])
                    dsink_acc = dsink_acc + jnp.where(lane == hq, -jnp.sum(psink * deltas[g]), 0.0)
            dsink_ref[...] += dsink_acc

        c_out = jnp.where(n < nb, c0[...], c1[...])
        s_out = jnp.where(n < nb, s0[...], s1[...])
        for hk in range(ATT_KV_HEADS):
            ks = slice(hk * HEAD_DIM, (hk + 1) * HEAD_DIM)
            dk_ref[:, ks] = _unrope(dk_acc[hk, 0:BLK, :], c_out, s_out)
            dv_ref[:, ks] = dv_acc[hk, 0:BLK, :]
            for acc in (dk_acc, dv_acc):
                acc[hk, 0:BLK, :] = acc[hk, BLK:2 * BLK, :]
                acc[hk, BLK:2 * BLK, :] = acc[hk, 2 * BLK:3 * BLK, :]
                acc[hk, 2 * BLK:3 * BLK, :] = jnp.zeros((BLK, HEAD_DIM), F32)

    own = lambda n: jnp.minimum(n, nb - 1)
    done = lambda n: jnp.maximum(n - 1, 0)

    def nbr(width, col):
        return [pl.BlockSpec((BLK, width), lambda n, c=col: (jnp.maximum(own(n) - 1, 0), c)),
                pl.BlockSpec((BLK, width), lambda n, c=col: (own(n), c)),
                pl.BlockSpec((BLK, width), lambda n, c=col: (jnp.minimum(own(n) + 1, nb - 1), c))]

    in_specs = ([pl.BlockSpec(memory_space=pltpu.SMEM),
                 pl.BlockSpec((BLK, ATT_WIDTH), lambda n: (own(n), P_QA // ATT_WIDTH))]
                + nbr(KV_WIDTH, P_KA // KV_WIDTH) + nbr(KV_WIDTH, P_VA // KV_WIDTH)
                + [pl.BlockSpec((BLK, ATT_WIDTH), lambda n: (own(n), 0)),
                   pl.BlockSpec((BLK, ATT_WIDTH), lambda n: (own(n), 0)),
                   pl.BlockSpec((BLK, LANES), lambda n: (own(n), 0))]
                + nbr(HEAD_DIM, 0) + nbr(HEAD_DIM, 0))
    args = [sink, proj] + [proj] * 6 + [y, dy, lse] + [cos2] * 3 + [sin2] * 3
    return _pcall(body, name=name,
                  out_shape=(jax.ShapeDtypeStruct((S, ATT_WIDTH), F32), jax.ShapeDtypeStruct((S, KV_WIDTH), F32),
                             jax.ShapeDtypeStruct((S, KV_WIDTH), F32), jax.ShapeDtypeStruct((1, LANES), F32)),
                  in_specs=in_specs,
                  out_specs=(pl.BlockSpec((BLK, ATT_WIDTH), lambda n: (own(n), 0)),
                             pl.BlockSpec((BLK, KV_WIDTH), lambda n: (done(n), 0)),
                             pl.BlockSpec((BLK, KV_WIDTH), lambda n: (done(n), 0)),
                             pl.BlockSpec((1, LANES), lambda n: (0, 0))),
                  grid=(nb + 1,),
                  scratch=[pltpu.VMEM((ATT_KV_HEADS, 3 * BLK, HEAD_DIM), F32),
                           pltpu.VMEM((ATT_KV_HEADS, 3 * BLK, HEAD_DIM), F32)],
                  sem=("arbitrary",))(*args)


CONV_HALO = 8
CONV_COLS = 512


def _halo_specs(tm, nrow, col_of):
    hb = tm // CONV_HALO
    return [pl.BlockSpec((CONV_HALO, CONV_COLS), lambda i, j: (jnp.maximum(i * hb - 1, 0), col_of(j))),
            pl.BlockSpec((tm, CONV_COLS), lambda i, j: (i, col_of(j))),
            pl.BlockSpec((CONV_HALO, CONV_COLS),
                         lambda i, j: (jnp.minimum((i + 1) * hb, nrow * hb - 1), col_of(j)))]


def _with_halo(prev, cur, nxt, i, nrow):
    p = jnp.where(i > 0, prev[...], 0.0)
    q = jnp.where(i < nrow - 1, nxt[...], 0.0)
    return jnp.concatenate([p, cur[...], q], axis=0)


def _conv_taps(xt, w_ref, tm):
    n = xt.shape[0]
    acc = jnp.zeros_like(xt)
    for j in range(CONV_WIDTH):
        sh = (CONV_WIDTH // 2 - j) % n
        xs = xt if sh == 0 else pltpu.roll(xt, sh, 0)
        acc = acc + xs * w_ref[j:j + 1, :]
    return acc


def conv_fwd(proj, conv_w, *, name):
    S = proj.shape[0]
    tm = _tile(S, (512, 256, 128))
    nrow = S // tm

    def body(xp, xc, xn, w_ref, o_ref):
        i = pl.program_id(0)
        xt = _with_halo(xp, xc, xn, i, nrow)
        pre = _conv_taps(xt, w_ref, tm)[CONV_HALO:CONV_HALO + tm]
        o_ref[...] = pre * _sigmoid(pre)

    return _pcall(body, name=name, out_shape=jax.ShapeDtypeStruct((S, 2 * M_WIDTH), F32),
                  in_specs=_halo_specs(tm, nrow, lambda j: P_QK // CONV_COLS + j)
                  + [pl.BlockSpec((CONV_HALO, CONV_COLS), lambda i, j: (0, j))],
                  out_specs=pl.BlockSpec((tm, CONV_COLS), lambda i, j: (i, j)),
                  grid=(nrow, 2 * M_WIDTH // CONV_COLS), sem=("parallel", "parallel"))(proj, proj, proj, conv_w)


def conv_bwd(proj, conv_w, da, db, *, name):
    S = proj.shape[0]
    tm = _tile(S, (512, 256, 128))
    nrow = S // tm

    def body(xp, xc, xn, ap, ac, an, bp, bc, bn, w_ref, dx_ref, dw_ref):
        i = pl.program_id(1)

        @pl.when(i == 0)
        def _():
            dw_ref[...] = jnp.zeros_like(dw_ref)

        xt = _with_halo(xp, xc, xn, i, nrow)
        dt = _with_halo(ap, ac, an, i, nrow) + _with_halo(bp, bc, bn, i, nrow)
        pre = _conv_taps(xt, w_ref, tm)
        sg = _sigmoid(pre)
        dpre = dt * (sg * (1.0 + pre * (1.0 - sg)))
        n = xt.shape[0]
        ridx = lax.broadcasted_iota(jnp.int32, (n, 1), 0)
        dpre = jnp.where((ridx >= 2) & (ridx < n - 2), dpre, 0.0)
        dx = jnp.zeros_like(xt)
        own = (ridx >= CONV_HALO) & (ridx < CONV_HALO + tm)
        dpre_own = jnp.where(own, dpre, 0.0)
        dw_rows = []
        for j in range(CONV_WIDTH):
            sh = (j - CONV_WIDTH // 2) % n
            ds_ = dpre if sh == 0 else pltpu.roll(dpre, sh, 0)
            dx = dx + ds_ * w_ref[j:j + 1, :]
            shx = (CONV_WIDTH // 2 - j) % n
            xs = xt if shx == 0 else pltpu.roll(xt, shx, 0)
            dw_rows.append(jnp.sum(dpre_own * xs, axis=0, keepdims=True))
        dx_ref[...] = dx[CONV_HALO:CONV_HALO + tm]
        dw_rows.append(jnp.zeros((CONV_HALO - CONV_WIDTH, CONV_COLS), F32))
        dw_ref[...] += jnp.concatenate(dw_rows, axis=0)

    colq = lambda j: P_QK // CONV_COLS + j
    same = lambda j: j

    def swap(specs):
        return [pl.BlockSpec(s.block_shape, (lambda f: (lambda j, i: f(i, j)))(s.index_map)) for s in specs]

    in_specs = swap(_halo_specs(tm, nrow, colq) + _halo_specs(tm, nrow, same) + _halo_specs(tm, nrow, same)
                    + [pl.BlockSpec((CONV_HALO, CONV_COLS), lambda i, j: (0, j))])
    return _pcall(body, name=name,
                  out_shape=(jax.ShapeDtypeStruct((S, 2 * M_WIDTH), F32),
                             jax.ShapeDtypeStruct((CONV_HALO, 2 * M_WIDTH), F32)),
                  in_specs=in_specs,
                  out_specs=(pl.BlockSpec((tm, CONV_COLS), lambda j, i: (i, j)),
                             pl.BlockSpec((CONV_HALO, CONV_COLS), lambda j, i: (0, j))),
                  grid=(2 * M_WIDTH // CONV_COLS, nrow), sem=("parallel", "arbitrary"))(
                      proj, proj, proj, da, da, da, db, db, db, conv_w)


def _log_sigmoid(x):
    return jnp.minimum(x, 0.0) - jnp.log(1.0 + jnp.exp(-jnp.abs(x)))


def _scan_sum(x, axis, from_end):
    idx = lax.broadcasted_iota(jnp.int32, x.shape, axis)
    n = x.shape[axis]
    sh = 1
    while sh < n:
        if from_end:
            x = x + jnp.where(idx < n - sh, pltpu.roll(x, n - sh, axis), 0.0)
        else:
            x = x + jnp.where(idx >= sh, pltpu.roll(x, sh, axis), 0.0)
        sh *= 2
    return x


def gate_rows(proj, *, name):
    S = proj.shape[0]

    def body(x_ref, o_ref):
        o_ref[...] = x_ref[...].T[0:N_GATES, :]

    return _pcall(body, name=name, out_shape=jax.ShapeDtypeStruct((N_GATES, S), F32),
                  in_specs=[pl.BlockSpec((BLK, LANES), lambda c: (c, P_G // LANES))],
                  out_specs=pl.BlockSpec((N_GATES, BLK), lambda c: (0, c)), grid=(S // BLK,), sem=("parallel",))(proj)


def _gate_setup(gc_ref, gr_ref, bgc_ref, bgr_ref, reverse):
    gc = gc_ref[...] + bgc_ref[...]
    gr = gr_ref[...] + bgr_ref[...]
    bc = _scan_sum(_log_sigmoid(gc), 0, reverse)
    br = _scan_sum(_log_sigmoid(gr), 1, reverse)
    return gc, gr, bc, br


def _head_gates(gc, gr, bc, br, h, m_in, reverse, tri):
    io = (M_HEADS if reverse else 0) + h
    fo = (3 * M_HEADS if reverse else 2 * M_HEADS) + h
    last = 0 if reverse else BLK - 1
    b_col, b_row = bc[:, fo:fo + 1], br[fo:fo + 1, :]
    ig_col, ig_row = gc[:, io:io + 1], gr[io:io + 1, :]
    logd = jnp.where(tri, b_col - b_row + ig_row, NEG)
    m_t = jnp.maximum(b_col + m_in, jnp.max(logd, axis=1, keepdims=True))
    dm = jnp.exp(logd - m_t)
    gi = jnp.exp(b_col + m_in - m_t)
    b_last = b_row[:, last:last + 1]
    logw = b_last - b_row + ig_row
    m_new = jnp.maximum(b_last + m_in, jnp.max(logw, axis=1, keepdims=True))
    w_col = jnp.exp(b_last - b_col + ig_col - m_new)
    dec = jnp.exp(b_last + m_in - m_new)
    return io, fo, m_t, dm, gi, m_new, w_col, dec


def _tri_mask(reverse):
    rows = lax.broadcasted_iota(jnp.int32, (BLK, BLK), 0)
    cols = lax.broadcasted_iota(jnp.int32, (BLK, BLK), 1)
    return (cols >= rows) if reverse else (cols <= rows)


def mlstm_fwd(qk, proj, gates_r, bg_c, bg_r, *, name):
    S = qk.shape[0]
    nc = S // BLK
    kscale = M_HEAD_DIM ** -0.5

    def body(qk_f, v_f, gc_f, gr_f, qk_b, v_b, gc_b, gr_b, bgc_ref, bgr_ref,
             h_f, den_f, cst_f, nm_f, h_b, den_b, cst_b, nm_b, *scratch):
        @pl.when(pl.program_id(0) == 0)
        def _():
            for sc in scratch:
                sc[...] = jnp.zeros_like(sc)

        one(False, qk_f, v_f, gc_f, gr_f, bgc_ref, bgr_ref, h_f, den_f, cst_f, nm_f, *scratch[:3])
        one(True, qk_b, v_b, gc_b, gr_b, bgc_ref, bgr_ref, h_b, den_b, cst_b, nm_b, *scratch[3:])

    def one(reverse, qk_ref, v_ref, gc_ref, gr_ref, bgc_ref, bgr_ref, h_ref, den_ref, cst_ref, nm_ref,
            c_sc, n_sc, m_sc):
        gc, gr, bc, br = _gate_setup(gc_ref, gr_ref, bgc_ref, bgr_ref, reverse)
        tri = _tri_mask(reverse)
        lane = lax.broadcasted_iota(jnp.int32, (BLK, LANES), 1)
        den_tile = jnp.zeros((BLK, LANES), F32)
        for h in range(M_HEADS):
            cs = slice(h * M_HEAD_DIM, (h + 1) * M_HEAD_DIM)
            m_in = m_sc[h][:, 0:1]
            _, _, m_t, dm, gi, m_new, w_col, dec = _head_gates(gc, gr, bc, br, h, m_in, reverse, tri)
            q = qk_ref[:, cs]
            k = qk_ref[:, M_WIDTH + h * M_HEAD_DIM:M_WIDTH + (h + 1) * M_HEAD_DIM] * kscale
            v = v_ref[:, cs]
            c_in, n_in = c_sc[h], n_sc[h]
            cst_ref[h] = c_in
            nm_ref[h, 0:1, :] = n_in
            nm_ref[h, 1:2, :] = m_sc[h]
            qb, kb, vb = q.astype(BF16), k.astype(BF16), v.astype(BF16)
            s = _dot_nt(qb, kb) * dm
            num = _dot(s.astype(BF16), vb) + gi * _dot_nt(qb, c_in.astype(BF16))
            den = jnp.sum(s, axis=1, keepdims=True) + gi * jnp.sum(q * n_in, axis=1, keepdims=True)
            z = jnp.maximum(jnp.abs(den), jnp.exp(-m_t))
            h_ref[:, cs] = num * (1.0 / z)
            den_tile = den_tile + jnp.where(lane == h, den, 0.0)
            c_sc[h] = dec * c_in + _dot_tn((w_col * v).astype(BF16), kb)
            n_sc[h] = dec * n_in + jnp.sum(w_col * k, axis=0, keepdims=True)
            m_sc[h] = jnp.broadcast_to(m_new, (1, M_HEAD_DIM))
        den_ref[...] = den_tile

    def ins(cidx):
        return [pl.BlockSpec((BLK, 2 * M_WIDTH), lambda c: (cidx(c), 0)),
                pl.BlockSpec((BLK, M_WIDTH), lambda c: (cidx(c), P_VM // M_WIDTH)),
                pl.BlockSpec((BLK, LANES), lambda c: (cidx(c), P_G // LANES)),
                pl.BlockSpec((N_GATES, BLK), lambda c: (0, cidx(c)))]

    def outs(cidx):
        return [pl.BlockSpec((BLK, M_WIDTH), lambda c: (cidx(c), 0)),
                pl.BlockSpec((BLK, LANES), lambda c: (cidx(c), 0)),
                pl.BlockSpec((None, M_HEADS, M_HEAD_DIM, M_HEAD_DIM), lambda c: (cidx(c), 0, 0, 0)),
                pl.BlockSpec((None, M_HEADS, 2, M_HEAD_DIM), lambda c: (cidx(c), 0, 0, 0))]

    fwd_c, rev_c = (lambda c: c), (lambda c: nc - 1 - c)
    shapes = [jax.ShapeDtypeStruct((S, M_WIDTH), F32), jax.ShapeDtypeStruct((S, LANES), F32),
              jax.ShapeDtypeStruct((nc, M_HEADS, M_HEAD_DIM, M_HEAD_DIM), F32),
              jax.ShapeDtypeStruct((nc, M_HEADS, 2, M_HEAD_DIM), F32)]
    state = [pltpu.VMEM((M_HEADS, M_HEAD_DIM, M_HEAD_DIM), F32), pltpu.VMEM((M_HEADS, 1, M_HEAD_DIM), F32),
             pltpu.VMEM((M_HEADS, 1, M_HEAD_DIM), F32)]
    res = _pcall(
        body, name=name, out_shape=tuple(shapes + shapes),
        in_specs=ins(fwd_c) + ins(rev_c) + [pl.BlockSpec((1, LANES), lambda c: (0, 0)),
                                            pl.BlockSpec((N_GATES, 1), lambda c: (0, 0))],
        out_specs=tuple(outs(fwd_c) + outs(rev_c)), grid=(nc,), scratch=state + state,
        sem=("arbitrary",))(qk, proj, proj, gates_r, qk, proj, proj, gates_r, bg_c, bg_r)
    return res[:4], res[4:]


def mlstm_bwd(qk, proj, gates_r, bg_c, bg_r, fwd_saved, rev_saved, dh, *, name, comm=None):
    S = qk.shape[0]
    nc = S // BLK
    kscale = M_HEAD_DIM ** -0.5

    def body(qk_f, v_f, gc_f, gr_f, h_f, den_f, cst_f, nm_f, dh_f, qk_b, v_b, gc_b, gr_b, h_b, den_b, cst_b, nm_b,
             dh_b, bgc_ref, bgr_ref, dqk_f, dv_f, dgc_f, dgr_f, dqk_b, dv_b, dgc_b, dgr_b, *scratch):
        @pl.when(pl.program_id(0) == 0)
        def _():
            for sc in scratch:
                sc[...] = jnp.zeros_like(sc)

        one(False, qk_f, v_f, gc_f, gr_f, bgc_ref, bgr_ref, h_f, den_f, cst_f, nm_f, dh_f,
            dqk_f, dv_f, dgc_f, dgr_f, *scratch[:2])
        one(True, qk_b, v_b, gc_b, gr_b, bgc_ref, bgr_ref, h_b, den_b, cst_b, nm_b, dh_b,
            dqk_b, dv_b, dgc_b, dgr_b, *scratch[2:])

    def one(reverse, qk_ref, v_ref, gc_ref, gr_ref, bgc_ref, bgr_ref, h_ref, den_ref, cst_ref, nm_ref, dh_ref,
            dqk_ref, dv_ref, dgc_ref, dgr_ref, dc_sc, dn_sc):
        last = 0 if reverse else BLK - 1
        gc, gr, bc, br = _gate_setup(gc_ref, gr_ref, bgc_ref, bgr_ref, reverse)
        tri = _tri_mask(reverse)
        lane_c = lax.broadcasted_iota(jnp.int32, (BLK, LANES), 1)
        row_c = lax.broadcasted_iota(jnp.int32, (BLK, 1), 0)
        row_r = lax.broadcasted_iota(jnp.int32, (N_GATES, BLK), 0)
        db_c = jnp.zeros((BLK, LANES), F32)
        dig_c = jnp.zeros((BLK, LANES), F32)
        db_r = jnp.zeros((N_GATES, BLK), F32)
        dig_r = jnp.zeros((N_GATES, BLK), F32)
        for h in range(M_HEADS):
            cs = slice(h * M_HEAD_DIM, (h + 1) * M_HEAD_DIM)
            ks = slice(M_WIDTH + h * M_HEAD_DIM, M_WIDTH + (h + 1) * M_HEAD_DIM)
            m_in = nm_ref[h, 1:2, 0:1]
            io, fo, m_t, dm, gi, m_new, w_col, dec = _head_gates(gc, gr, bc, br, h, m_in, reverse, tri)
            q = qk_ref[:, cs]
            k = qk_ref[:, ks] * kscale
            v = v_ref[:, cs]
            c_in, n_in = cst_ref[h], nm_ref[h, 0:1, :]
            qb, kb, vb, cb = q.astype(BF16), k.astype(BF16), v.astype(BF16), c_in.astype(BF16)
            s = _dot_nt(qb, kb) * dm
            den_h = den_ref[:, h:h + 1]
            emt = jnp.exp(-m_t)
            rz = 1.0 / jnp.maximum(jnp.abs(den_h), emt)
            dhh = dh_ref[:, cs]
            dnum = dhh * rz
            hdh = jnp.sum(dhh * h_ref[:, cs], axis=1, keepdims=True)
            dden = jnp.where(jnp.abs(den_h) > emt, -hdh * rz * jnp.sign(den_h), 0.0)
            dnb = dnum.astype(BF16)
            ds = _dot_nt(dnb, vb) + dden
            e = ds * s
            dsd = (ds * dm).astype(BF16)
            gd = (gi * dnum).astype(BF16)
            gdd = gi * dden
            dq = _dot(dsd, kb) + _dot(gd, cb) + gdd * n_in
            dk = _dot_tn(dsd, qb)
            dv = _dot_tn(s.astype(BF16), dnb)
            dc_in = _dot_tn(gd, qb)
            dn_in = jnp.sum(gdd * q, axis=0, keepdims=True)
            cq = _dot_nt(qb, cb)
            dg = jnp.sum(dnum * cq, axis=1, keepdims=True) + dden * jnp.sum(q * n_in, axis=1, keepdims=True)
            eg = dg * gi
            dco, dno = dc_sc[h], dn_sc[h]
            dcob = dco.astype(BF16)
            dwv = _dot_nt(kb, dcob)
            dv = dv + w_col * dwv
            dw = jnp.sum(v * dwv, axis=1, keepdims=True) + jnp.sum(k * dno, axis=1, keepdims=True)
            dk = dk + _dot((w_col * v).astype(BF16), dcob) + w_col * dno
            ew = dw * w_col
            ddec = (jnp.sum(jnp.sum(dco * c_in, axis=1, keepdims=True), axis=0, keepdims=True)
                    + jnp.sum(dno * n_in, axis=1, keepdims=True))
            dc_sc[h] = dec * dco + dc_in
            dn_sc[h] = dec * dno + dn_in
            dqk_ref[:, cs] = dq
            dqk_ref[:, ks] = dk * kscale
            dv_ref[:, cs] = dv
            csum = jnp.sum(e, axis=0, keepdims=True)
            db_last = jnp.sum(ew, axis=0, keepdims=True) + ddec * dec
            db_col = jnp.sum(e, axis=1, keepdims=True) + eg - ew + jnp.where(row_c == last, db_last, 0.0)
            db_c = db_c + jnp.where(lane_c == fo, db_col, 0.0)
            dig_c = dig_c + jnp.where(lane_c == io, ew, 0.0)
            db_r = db_r + jnp.where(row_r == fo, -csum, 0.0)
            dig_r = dig_r + jnp.where(row_r == io, csum, 0.0)
        dgc_ref[...] = dig_c + _scan_sum(db_c, 0, not reverse) * _sigmoid(-gc)
        dgr_ref[...] = dig_r + _scan_sum(db_r, 1, not reverse) * _sigmoid(-gr)

    def ins(cidx):
        chunk = lambda w, col=0: pl.BlockSpec((BLK, w), lambda c: (cidx(c), col))
        return [chunk(2 * M_WIDTH), chunk(M_WIDTH, P_VM // M_WIDTH), chunk(LANES, P_G // LANES),
                pl.BlockSpec((N_GATES, BLK), lambda c: (0, cidx(c))), chunk(M_WIDTH), chunk(LANES),
                pl.BlockSpec((None, M_HEADS, M_HEAD_DIM, M_HEAD_DIM), lambda c: (cidx(c), 0, 0, 0)),
                pl.BlockSpec((None, M_HEADS, 2, M_HEAD_DIM), lambda c: (cidx(c), 0, 0, 0)), chunk(M_WIDTH)]

    def outs(cidx):
        chunk = lambda w: pl.BlockSpec((BLK, w), lambda c: (cidx(c), 0))
        return [chunk(2 * M_WIDTH), chunk(M_WIDTH), chunk(LANES), pl.BlockSpec((N_GATES, BLK), lambda c: (0, cidx(c)))]

    fwd_c, rev_c = (lambda c: nc - 1 - c), (lambda c: c)
    shapes = [jax.ShapeDtypeStruct((S, 2 * M_WIDTH), F32), jax.ShapeDtypeStruct((S, M_WIDTH), F32),
              jax.ShapeDtypeStruct((S, LANES), F32), jax.ShapeDtypeStruct((N_GATES, S), F32)]
    state = [pltpu.VMEM((M_HEADS, M_HEAD_DIM, M_HEAD_DIM), F32), pltpu.VMEM((M_HEADS, 1, M_HEAD_DIM), F32)]
    res = _pcall(
        body, name=name, out_shape=tuple(shapes + shapes),
        in_specs=ins(fwd_c) + ins(rev_c) + [pl.BlockSpec((1, LANES), lambda c: (0, 0)),
                                            pl.BlockSpec((N_GATES, 1), lambda c: (0, 0))],
        out_specs=tuple(outs(fwd_c) + outs(rev_c)), grid=(nc,), scratch=state + state, sem=("arbitrary",),
        comm=comm)(qk, proj, proj, gates_r, *fwd_saved, dh, qk, proj, proj, gates_r, *rev_saved, dh, bg_c, bg_r)
    if comm is not None:
        res, carried = res
        return (res[:4], res[4:]), carried
    return res[:4], res[4:]


def headnorm_fwd(hf, hb, proj, mnorm, *, name):
    S = hf.shape[0]
    tm = _tile(S, (512, 256, 128))

    def body(hf_ref, hb_ref, om_ref, mn_ref, y_ref):
        for h in range(M_HEADS):
            cs = slice(h * M_HEAD_DIM, (h + 1) * M_HEAD_DIM)
            hm = hf_ref[:, cs] + hb_ref[:, cs]
            r = lax.rsqrt(jnp.mean(hm * hm, axis=-1, keepdims=True) + EPS)
            y_ref[:, cs] = (_sigmoid(om_ref[:, cs]) * ((hm * r) * mn_ref[:, cs])).astype(y_ref.dtype)

    row = pl.BlockSpec((tm, M_WIDTH), lambda i: (i, 0))
    return _pcall(body, name=name, out_shape=jax.ShapeDtypeStruct((S, M_WIDTH), BF16),
                  in_specs=[row, row, pl.BlockSpec((tm, M_WIDTH), lambda i: (i, P_OM // M_WIDTH)),
                            pl.BlockSpec((1, M_WIDTH), lambda i: (0, 0))],
                  out_specs=row, grid=(S // tm,), sem=("parallel",))(hf, hb, proj, mnorm)


def headnorm_bwd(hf, hb, proj, mnorm, dy, *, name):
    S = hf.shape[0]
    tm = _tile(S, (512, 256, 128))

    def body(hf_ref, hb_ref, om_ref, mn_ref, dy_ref, dh_ref, dom_ref, dmn_ref):
        @pl.when(pl.program_id(0) == 0)
        def _():
            dmn_ref[...] = jnp.zeros_like(dmn_ref)

        for h in range(M_HEADS):
            cs = slice(h * M_HEAD_DIM, (h + 1) * M_HEAD_DIM)
            hm = hf_ref[:, cs] + hb_ref[:, cs]
            r = lax.rsqrt(jnp.mean(hm * hm, axis=-1, keepdims=True) + EPS)
            xh = hm * r
            so = _sigmoid(om_ref[:, cs])
            d = dy_ref[:, cs]
            mn = mn_ref[:, cs]
            dom_ref[:, cs] = d * (xh * mn) * (so * (1.0 - so))
            dxm = d * so
            dmn_ref[:, cs] += jnp.sum(dxm * xh, axis=0, keepdims=True)
            dxh = dxm * mn
            dh_ref[:, cs] = r * (dxh - xh * jnp.mean(dxh * xh, axis=-1, keepdims=True))

    row = pl.BlockSpec((tm, M_WIDTH), lambda i: (i, 0))
    vec = pl.BlockSpec((1, M_WIDTH), lambda i: (0, 0))
    return _pcall(body, name=name,
                  out_shape=(jax.ShapeDtypeStruct((S, M_WIDTH), F32), jax.ShapeDtypeStruct((S, M_WIDTH), F32),
                             jax.ShapeDtypeStruct((1, M_WIDTH), F32)),
                  in_specs=[row, row, pl.BlockSpec((tm, M_WIDTH), lambda i: (i, P_OM // M_WIDTH)), vec,
                            pl.BlockSpec((tm, M_WIDTH), lambda i: (i, 1))],
                  out_specs=(row, row, vec), grid=(S // tm,), sem=("arbitrary",))(hf, hb, proj, mnorm, dy)


def _place():
    return lax.axis_index("x"), lax.axis_index("y"), lax.axis_index("c")


def _ag_plan(x_refs, out_refs, sems):
    send_sems, recv_sems, local_sems = sems
    T = len(x_refs)
    x, y, c = _place()
    me, sibling = (x, y, c), (x, y, 1 - c)
    chips = [(1 - x, y), (x, 1 - y), (1 - x, 1 - y)]

    def copy(t, k, block, to, src=None):
        px, py, pc = block
        dst = out_refs[t].at[4 * px + 2 * py + pc]
        return pltpu.make_async_remote_copy(
            src_ref=dst if src is None else src, dst_ref=dst, send_sem=send_sems.at[7 * t + k],
            recv_sem=recv_sems.at[7 * t + k], device_id=to, device_id_type=MESH)

    mine = [pltpu.make_async_copy(x_refs[t], out_refs[t].at[4 * x + 2 * y + c], local_sems.at[t]) for t in range(T)]
    first = []
    for t in range(T):
        first.append(copy(t, 0, me, sibling, src=x_refs[t]))
        first += [copy(t, 1 + j, me, (*chip, c), src=x_refs[t]) for j, chip in enumerate(chips)]
    landed = [copy(t, 1 + j, (*chip, c), me) for j, chip in enumerate(chips) for t in range(T)]
    passed = [copy(t, 4 + j, (*chip, c), sibling) for j, chip in enumerate(chips) for t in range(T)]
    from_sibling = [copy(t, 0, sibling, me) for t in range(T)]
    from_sibling += [copy(t, 4 + j, (*chip, 1 - c), me) for j, chip in enumerate(chips) for t in range(T)]
    return mine, first, landed, passed, from_sibling


def _ag_start(x_refs, out_refs, sems):
    mine, first, _, _, _ = _ag_plan(x_refs, out_refs, sems)
    for cp in mine + first:
        cp.start()


def _ag_forward(x_refs, out_refs, sems):
    _, _, landed, passed, _ = _ag_plan(x_refs, out_refs, sems)
    for got, on in zip(landed, passed):
        got.wait_recv()
        on.start()


def _ag_finish(x_refs, out_refs, sems):
    mine, first, _, passed, from_sibling = _ag_plan(x_refs, out_refs, sems)
    for cp in from_sibling:
        cp.wait_recv()
    for cp in first + passed:
        cp.wait_send()
    for cp in mine:
        cp.wait()


def ag_comm(shards):
    T = len(shards)
    return Comm(shards, [jax.ShapeDtypeStruct((N_DEV,) + s.shape, s.dtype) for s in shards],
                [pltpu.SemaphoreType.DMA((7 * T,)), pltpu.SemaphoreType.DMA((7 * T,)), pltpu.SemaphoreType.DMA((T,))],
                [_ag_start, _ag_forward, _ag_finish])


def _pair_plan(g_refs, out_refs, sems):
    send_sems, recv_sems = sems
    x, y, c = _place()
    return [pltpu.make_async_remote_copy(
        src_ref=g_refs[t].at[1 - c], dst_ref=out_refs[t], send_sem=send_sems.at[t], recv_sem=recv_sems.at[t],
        device_id=(x, y, 1 - c), device_id_type=MESH) for t in range(len(g_refs))]


def _pair_start(g_refs, out_refs, sems):
    for cp in _pair_plan(g_refs, out_refs, sems):
        cp.start()


def _pair_finish(g_refs, out_refs, sems):
    for cp in _pair_plan(g_refs, out_refs, sems):
        cp.wait()


def pair_comm(grads):
    T = len(grads)
    return Comm(grads, [jax.ShapeDtypeStruct(g.shape[1:], g.dtype) for g in grads],
                [pltpu.SemaphoreType.DMA((T,)), pltpu.SemaphoreType.DMA((T,))], [_pair_start, _pair_finish])


def _chip_plan(p_refs, out_refs, sems):
    send_sems, recv_sems, local_sems = sems
    T = len(p_refs)
    x, y, c = _place()
    mychip = 2 * x + y
    chips = [(1 - x, y), (x, 1 - y), (1 - x, 1 - y)]
    mine = [pltpu.make_async_copy(p_refs[t].at[mychip], out_refs[t].at[mychip], local_sems.at[t]) for t in range(T)]
    cps = [pltpu.make_async_remote_copy(
        src_ref=p_refs[t].at[2 * px + py], dst_ref=out_refs[t].at[mychip], send_sem=send_sems.at[3 * t + j],
        recv_sem=recv_sems.at[3 * t + j], device_id=(px, py, c), device_id_type=MESH)
        for t in range(T) for j, (px, py) in enumerate(chips)]
    return mine, cps


def _chip_start(p_refs, out_refs, sems):
    mine, cps = _chip_plan(p_refs, out_refs, sems)
    for cp in mine + cps:
        cp.start()


def _chip_finish(p_refs, out_refs, sems):
    mine, cps = _chip_plan(p_refs, out_refs, sems)
    for cp in cps + mine:
        cp.wait()


def chip_comm(parts):
    T = len(parts)
    return Comm(parts, [jax.ShapeDtypeStruct(p.shape, p.dtype) for p in parts],
                [pltpu.SemaphoreType.DMA((3 * T,)), pltpu.SemaphoreType.DMA((3 * T,)), pltpu.SemaphoreType.DMA((T,))],
                [_chip_start, _chip_finish])


PAIR_ADD_BLOCK_BYTES = 4 * 1024 * 1024


def pair_add(g, recv, core, *, name):
    _, nchip, R, C = g.shape
    tr = R if R * C * g.dtype.itemsize <= PAIR_ADD_BLOCK_BYTES else _tile(R, (512, 256, 128, 64))

    def body(c_ref, a_ref, b_ref, o_ref):
        o_ref[...] = (a_ref[...].astype(F32) + b_ref[...].astype(F32)).astype(o_ref.dtype)

    grid_spec = pltpu.PrefetchScalarGridSpec(
        num_scalar_prefetch=1, grid=(nchip, R // tr),
        in_specs=[pl.BlockSpec((None, None, tr, C), lambda k, i, c_ref: (c_ref[0], k, i, 0)),
                  pl.BlockSpec((None, tr, C), lambda k, i, c_ref: (k, i, 0))],
        out_specs=pl.BlockSpec((None, tr, C), lambda k, i, c_ref: (k, i, 0)))
    return pl.pallas_call(body, name=name, out_shape=jax.ShapeDtypeStruct(recv.shape, recv.dtype),
                          grid_spec=grid_spec,
                          compiler_params=pltpu.CompilerParams(dimension_semantics=("parallel", "parallel"),
                                                               vmem_limit_bytes=V7X_VMEM_LIMIT))(core, g, recv)


def _adam_math(w, g, m, v):
    m = ADAM_B1 * m + (1.0 - ADAM_B1) * g
    v = ADAM_B2 * v + (1.0 - ADAM_B2) * (g * g)
    m_hat = m / (1.0 - ADAM_B1 ** ADAM_STEP)
    v_hat = v / (1.0 - ADAM_B2 ** ADAM_STEP)
    delta = -ADAM_LR * (m_hat / (jnp.sqrt(v_hat) + ADAM_EPS) + ADAM_WD * w)
    return delta, m, v


def adam_update(w, parts, m, v, *, name):
    P, R, _ = parts.shape
    tr = _tile(R, (1024, 512, 256, 128, 64, 32, 16, 8))

    def body(w_ref, p_ref, m_ref, v_ref, g_ref, d_ref, nm_ref, nv_ref):
        g = p_ref[0]
        for k in range(1, P):
            g = g + p_ref[k]
        d, nm, nv = _adam_math(w_ref[...], g, m_ref[...], v_ref[...])
        g_ref[...] = g
        d_ref[...] = d
        nm_ref[...] = nm
        nv_ref[...] = nv

    row = pl.BlockSpec((tr, LANES), lambda i: (i, 0))
    shp = jax.ShapeDtypeStruct((R, LANES), F32)
    return _pcall(body, name=name, out_shape=(shp, shp, shp, shp),
                  in_specs=[row, pl.BlockSpec((P, tr, LANES), lambda i: (0, i, 0)), row, row],
                  out_specs=(row, row, row, row), grid=(R // tr,), sem=("parallel",))(w, parts, m, v)


ADAM_STEP_BYTES = 6 * 1024 * 1024


def adam_tensor(w, parts, m, v, *, name):
    L, R, C = w.shape
    per_row = L * C * (7 * 4 + 4 * parts[0].dtype.itemsize)
    tr = R
    for cand in (256, 128, 64, 32, 16):
        if R % cand == 0 and cand * per_row <= ADAM_STEP_BYTES:
            tr = cand
            break

    def body(*refs):
        w_ref, m_ref, v_ref = refs[:3]
        p_refs = refs[3:3 + L]
        g_ref, d_ref, nm_ref, nv_ref = refs[3 + L:]
        for l in range(L):
            g = p_refs[l][0].astype(F32)
            for k in range(1, 4):
                g = g + p_refs[l][k].astype(F32)
            d, nm, nv = _adam_math(w_ref[l], g, m_ref[l], v_ref[l])
            g_ref[l] = g
            d_ref[l] = d
            nm_ref[l] = nm
            nv_ref[l] = nv

    blk = pl.BlockSpec((L, tr, C), lambda i: (0, i, 0))
    pblk = pl.BlockSpec((4, tr, C), lambda i: (0, i, 0))
    shp = jax.ShapeDtypeStruct((L, R, C), F32)
    return _pcall(body, name=name, out_shape=(shp, shp, shp, shp), in_specs=[blk, blk, blk] + [pblk] * L,
                  out_specs=(blk, blk, blk, blk), grid=(R // tr,), sem=("parallel",))(w, m, v, *parts)


def _rows(n_elems):
    r = -(-n_elems // LANES)
    return -(-r // 1024) * 1024 if r > 1024 else -(-r // 16) * 16


def _flat(a, dtype=None):
    n = a.size
    r = _rows(n)
    f = a.reshape(-1)
    if dtype is not None:
        f = f.astype(dtype)
    if r * LANES != n:
        f = jnp.pad(f, (0, r * LANES - n))
    return f.reshape(r, LANES)


def _gathered_cols(g):
    n, rows, cols = g.shape
    return g.transpose(1, 0, 2).reshape(rows, n * cols)


def _owner_cols(dw, dtype):
    rows = dw.shape[0]
    cols = dw.shape[1] // N_DEV
    return dw.reshape(rows, N_DEV // 2, 2, cols).transpose(2, 1, 0, 3).astype(dtype)


_IN_NAT = dict(qa=(0, 1024), ka=(1024, 1280), va=(1280, 1536), qm=(1536, 2560), km=(2560, 3584),
               vm=(3584, 4608), om=(4608, 5632), g=(5632, 5648))


def _permute_w_in(w):
    sl = lambda k: w[:, _IN_NAT[k][0]:_IN_NAT[k][1]]
    pad = jnp.zeros((w.shape[0], P_WIDTH - P_G - N_GATES), w.dtype)
    return jnp.concatenate([sl("qm"), sl("km"), sl("qa"), sl("vm"), sl("om"), sl("ka"), sl("va"), sl("g"), pad],
                           axis=1)


def _unpermute_dw_in(dw):
    qm, km = dw[:, P_QK:P_QK + 1024], dw[:, P_QK + 1024:P_QK + 2048]
    return jnp.concatenate([dw[:, P_QA:P_QA + 1024], dw[:, P_KA:P_KA + 256], dw[:, P_VA:P_VA + 256], qm, km,
                            dw[:, P_VM:P_VM + 1024], dw[:, P_OM:P_OM + 1024], dw[:, P_G:P_G + N_GATES]], axis=1)


BIG = ("ffn1_w_gate", "ffn1_w_up", "ffn1_w_down", "w_in", "w_out", "ffn2_w_gate", "ffn2_w_up", "ffn2_w_down")
COLUMN_SHARDED_FFN = ("ffn1_w_gate", "ffn1_w_up", "ffn2_w_gate", "ffn2_w_up")
SMALL = ("ffn1_norm_pre", "ffn1_norm_post", "mix_norm_pre", "mix_norm_post", "b_gate", "attn_sink", "mlstm_norm",
         "ffn2_norm_pre", "ffn2_norm_post")
WEIGHTS = ("ffn1_norm_pre", "ffn1_norm_post", "ffn1_w_gate", "ffn1_w_up", "ffn1_w_down", "mix_norm_pre",
           "mix_norm_post", "w_in", "b_gate", "conv_w", "attn_sink", "mlstm_norm", "w_out", "ffn2_norm_pre",
           "ffn2_norm_post", "ffn2_w_gate", "ffn2_w_up", "ffn2_w_down")


GRAD_DT = BF16


def _carried(result, comm):
    return result if comm is not None else (result, None)


def _pair_adds(grads, recv, core, tag):
    return [pair_add(g, r, core, name=f"{tag}_add{t}") for t, (g, r) in enumerate(zip(grads, recv))]


def _ffn_fwd(x, g_pre, g_post, wg8, wu8, wd8, tag, gather=None):
    xn = norm_fwd(x, g_pre, name=f"{tag}_pre", out_dtype=BF16)
    comm = None if gather is None else ag_comm(gather)
    (hg, hu, act), gathered = _carried(ffn_gu(xn, wg8, wu8, name=f"{tag}_gu", comm=comm), comm)
    f = ffn_down(act, wd8, name=f"{tag}_down")
    x_new = norm_fwd(f, g_post, name=f"{tag}_post", scale=0.5, resid=x)
    return x_new, (x, xn, hg, hu, act, f), gathered


def _ffn_bwd(dx, saved, g_pre, g_post, wg8, wu8, wd8, core, tag, reduce=None, last=False):
    x, xn, hg, hu, act, f = saved
    df, dg_post = norm_bwd(dx, f, g_post, name=f"{tag}_post_b", scale=0.5, out_dtype=BF16)
    comm = None if reduce is None else pair_comm(reduce)
    dwd, recv = _carried(ffn_dwd(act, df, name=f"{tag}_dwd", out_dtype=GRAD_DT, comm=comm), comm)
    dhg, dhu = ffn_dact(df, wd8, hg, hu, name=f"{tag}_dact")
    comm = None if reduce is None else chip_comm(_pair_adds(reduce, recv, core, tag))
    (dwg, dwu), reduced = _carried(ffn_dwgu(xn, dhg, dhu, name=f"{tag}_dwgu", out_dtype=GRAD_DT, comm=comm), comm)
    own = [dwg, dwu, dwd]
    comm = None
    if last:
        recv = run_comm(pair_comm(own), name="rs1_last")
        comm = chip_comm(_pair_adds(own, recv, core, "last"))
    dxn, own_reduced = _carried(ffn_dxn(dhg, dhu, wg8, wu8, name=f"{tag}_dxn", comm=comm), comm)
    dx_new, dg_pre = norm_bwd(dxn, x, g_pre, name=f"{tag}_pre_b", resid=dx)
    return dx_new, dg_pre, dg_post, own_reduced if last else own, reduced


def _mix_fwd(x, g_pre, g_post, w_in_p, b_gate, conv_full, sink, mnorm, w_out, cos2, sin2, tag, gather_in, gather_out):
    S = x.shape[0]
    xn = norm_fwd(x, g_pre, name=f"{tag}_pre", out_dtype=BF16)
    comm = ag_comm(gather_in)
    proj, got_in = mm_nn(xn, w_in_p, name=f"{tag}_in", comm=comm)
    gates_r = gate_rows(proj, name=f"{tag}_gt")
    bg_c = jnp.pad(b_gate, (0, LANES - N_GATES)).reshape(1, LANES)
    bg_r = b_gate.reshape(N_GATES, 1)
    y_att, lse = attn_fwd(proj, cos2, sin2, sink, name=f"{tag}_att")
    qk = conv_fwd(proj, conv_full, name=f"{tag}_conv")
    (hf, denf, cf, nmf), (hb, denb, cb, nmb) = mlstm_fwd(qk, proj, gates_r, bg_c, bg_r, name=f"{tag}_ml")
    y_m = headnorm_fwd(hf, hb, proj, mnorm.reshape(1, M_WIDTH), name=f"{tag}_hn")
    y = jnp.concatenate([y_att, y_m], axis=1)
    mo, got_out = mm_nn(y, w_out, name=f"{tag}_out", comm=ag_comm(gather_out))
    x_new = norm_fwd(mo, g_post, name=f"{tag}_post", resid=x)
    saved = (x, xn, proj, gates_r, bg_c, bg_r, lse, qk, hf, denf, cf, nmf, hb, denb, cb, nmb, y, mo)
    return x_new, saved, got_in + got_out


def _mix_bwd(dx, saved, g_pre, g_post, w_in_p, conv_full, sink, mnorm, w_out, cos2, sin2, core, tag, reduce=None):
    x, xn, proj, gates_r, bg_c, bg_r, lse, qk, hf, denf, cf, nmf, hb, denb, cb, nmb, y, mo = saved
    S = x.shape[0]
    dmo, dg_post = norm_bwd(dx, mo, g_post, name=f"{tag}_post_b", out_dtype=BF16)
    comm = None if reduce is None else pair_comm(reduce)
    dw_out, recv = _carried(mm_tn(y, dmo, name=f"{tag}_dwo", owner_rows=D_MODEL // N_DEV, out_dtype=GRAD_DT,
                                  comm=comm), comm)
    dy = mm_nt(dmo, w_out, name=f"{tag}_dy")
    mn = mnorm.reshape(1, M_WIDTH)
    dh, dom, dmn = headnorm_bwd(hf, hb, proj, mn, dy, name=f"{tag}_hn_b")
    comm = None if reduce is None else chip_comm(_pair_adds(reduce, recv, core, tag))
    ((dqk_f, dv_f, dgc_f, dgr_f), (dqk_b, dv_b, dgc_b, dgr_b)), reduced = _carried(
        mlstm_bwd(qk, proj, gates_r, bg_c, bg_r, (hf, denf, cf, nmf), (hb, denb, cb, nmb), dh,
                  name=f"{tag}_ml_b", comm=comm), comm)
    dqk_in, dconv = conv_bwd(proj, conv_full, dqk_f, dqk_b, name=f"{tag}_conv_b")
    dqa, dka, dva, dsink = attn_bwd(proj, y, dy, lse, cos2, sin2, sink, name=f"{tag}_att_b")
    dgates = dgc_f + dgc_b + jnp.pad((dgr_f + dgr_b).T, ((0, 0), (0, LANES - N_GATES)))
    dproj = jnp.concatenate([dqk_in.astype(BF16), dqa.astype(BF16), (dv_f + dv_b).astype(BF16), dom.astype(BF16),
                             dka.astype(BF16), dva.astype(BF16), dgates.astype(BF16),
                             jnp.zeros((S, P_WIDTH - P_G - LANES), BF16)], axis=1)
    db_gate = colsum(dgates, name=f"{tag}_dbg")[0, :N_GATES]
    dw_in = mm_tn(xn, dproj, name=f"{tag}_dwi")
    dxn = mm_nt(dproj, w_in_p, name=f"{tag}_dxn")
    dx_new, dg_pre = norm_bwd(dxn, x, g_pre, name=f"{tag}_pre_b", resid=dx)
    grads = [_owner_cols(_unpermute_dw_in(dw_in), GRAD_DT), dw_out, _owner_cols(dconv[:CONV_WIDTH], F32)]
    return dx_new, dg_pre, dg_post, db_gate, dsink[0, :ATT_HEADS], dmn[0], grads, reduced


def colsum(a, *, name):
    S, C = a.shape
    tm = _tile(S, (512, 256, 128))

    def body(a_ref, o_ref):
        @pl.when(pl.program_id(0) == 0)
        def _():
            o_ref[...] = jnp.zeros_like(o_ref)

        o_ref[...] += jnp.sum(a_ref[...], axis=0, keepdims=True)

    return _pcall(body, name=name, out_shape=jax.ShapeDtypeStruct((1, C), F32),
                  in_specs=[pl.BlockSpec((tm, C), lambda i: (i, 0))], out_specs=pl.BlockSpec((1, C), lambda i: (0, 0)),
                  grid=(S // tm,), sem=("arbitrary",))(a)


def _layer_shards(W, l):
    pad_r = lambda a: jnp.pad(a.astype(BF16), ((0, FSP - FS), (0, 0)))
    return [pad_r(W["ffn1_w_gate"][l]), pad_r(W["ffn1_w_up"][l]), pad_r(W["ffn1_w_down"][l]),
            W["w_in"][l].astype(BF16), W["w_out"][l].astype(BF16),
            pad_r(W["ffn2_w_gate"][l]), pad_r(W["ffn2_w_up"][l]), pad_r(W["ffn2_w_down"][l])]


def kernel(x, ffn1_norm_pre, ffn1_norm_post, ffn1_w_gate, ffn1_w_up, ffn1_w_down, mix_norm_pre, mix_norm_post, w_in, b_gate, conv_w, attn_sink, mlstm_norm, w_out, ffn2_norm_pre, ffn2_norm_post, ffn2_w_gate, ffn2_w_up, ffn2_w_down, loss_target, m_ffn1_norm_pre, m_ffn1_norm_post, m_ffn1_w_gate, m_ffn1_w_up, m_ffn1_w_down, m_mix_norm_pre, m_mix_norm_post, m_w_in, m_b_gate, m_conv_w, m_attn_sink, m_mlstm_norm, m_w_out, m_ffn2_norm_pre, m_ffn2_norm_post, m_ffn2_w_gate, m_ffn2_w_up, m_ffn2_w_down, v_ffn1_norm_pre, v_ffn1_norm_post, v_ffn1_w_gate, v_ffn1_w_up, v_ffn1_w_down, v_mix_norm_pre, v_mix_norm_post, v_w_in, v_b_gate, v_conv_w, v_attn_sink, v_mlstm_norm, v_w_out, v_ffn2_norm_pre, v_ffn2_norm_post, v_ffn2_w_gate, v_ffn2_w_up, v_ffn2_w_down):
    W = dict(ffn1_norm_pre=ffn1_norm_pre, ffn1_norm_post=ffn1_norm_post, ffn1_w_gate=ffn1_w_gate,
             ffn1_w_up=ffn1_w_up, ffn1_w_down=ffn1_w_down, mix_norm_pre=mix_norm_pre, mix_norm_post=mix_norm_post,
             w_in=w_in, b_gate=b_gate, conv_w=conv_w, attn_sink=attn_sink, mlstm_norm=mlstm_norm, w_out=w_out,
             ffn2_norm_pre=ffn2_norm_pre, ffn2_norm_post=ffn2_norm_post, ffn2_w_gate=ffn2_w_gate,
             ffn2_w_up=ffn2_w_up, ffn2_w_down=ffn2_w_down)
    M1 = dict(ffn1_norm_pre=m_ffn1_norm_pre, ffn1_norm_post=m_ffn1_norm_post, ffn1_w_gate=m_ffn1_w_gate,
              ffn1_w_up=m_ffn1_w_up, ffn1_w_down=m_ffn1_w_down, mix_norm_pre=m_mix_norm_pre,
              mix_norm_post=m_mix_norm_post, w_in=m_w_in, b_gate=m_b_gate, conv_w=m_conv_w, attn_sink=m_attn_sink,
              mlstm_norm=m_mlstm_norm, w_out=m_w_out, ffn2_norm_pre=m_ffn2_norm_pre,
              ffn2_norm_post=m_ffn2_norm_post, ffn2_w_gate=m_ffn2_w_gate, ffn2_w_up=m_ffn2_w_up,
              ffn2_w_down=m_ffn2_w_down)
    V2 = dict(ffn1_norm_pre=v_ffn1_norm_pre, ffn1_norm_post=v_ffn1_norm_post, ffn1_w_gate=v_ffn1_w_gate,
              ffn1_w_up=v_ffn1_w_up, ffn1_w_down=v_ffn1_w_down, mix_norm_pre=v_mix_norm_pre,
              mix_norm_post=v_mix_norm_post, w_in=v_w_in, b_gate=v_b_gate, conv_w=v_conv_w, attn_sink=v_attn_sink,
              mlstm_norm=v_mlstm_norm, w_out=v_w_out, ffn2_norm_pre=v_ffn2_norm_pre,
              ffn2_norm_post=v_ffn2_norm_post, ffn2_w_gate=v_ffn2_w_gate, ffn2_w_up=v_ffn2_w_up,
              ffn2_w_down=v_ffn2_w_down)
    for n in COLUMN_SHARDED_FFN:
        W[n], M1[n], V2[n] = (jnp.transpose(a, (0, 2, 1)) for a in (W[n], M1[n], V2[n]))
    depth = w_in.shape[0]
    S = x.shape[1]
    xs = x[0]
    cos2, sin2 = _rope_tables(S)
    core = lax.axis_index("c").astype(jnp.int32).reshape(1)

    cs = conv_w.shape[2]
    conv_g = run_comm(ag_comm([conv_w.reshape(depth * CONV_WIDTH, cs)]), name="ag_conv")[0]
    conv_all = conv_g.reshape(N_DEV, depth, CONV_WIDTH, cs).transpose(1, 2, 0, 3)
    conv_all = conv_all.reshape(depth, CONV_WIDTH, N_DEV * cs)
    conv_all = jnp.pad(conv_all, ((0, 0), (0, CONV_HALO - CONV_WIDTH), (0, 0)))

    lw, saved = [], []
    shards = [_layer_shards(W, l) for l in range(depth)]
    ffn1_w = run_comm(ag_comm(shards[0][0:3]), name="ag_first")
    for l in range(depth):
        xs, s1, got = _ffn_fwd(xs, W["ffn1_norm_pre"][l], W["ffn1_norm_post"][l], *ffn1_w, "f1", shards[l][3:5])
        mix_w = (_permute_w_in(_gathered_cols(got[0])), got[1].reshape(D_MODEL, D_MODEL))
        xs, s2, ffn2_w = _mix_fwd(xs, W["mix_norm_pre"][l], W["mix_norm_post"][l], mix_w[0], W["b_gate"][l],
                                  conv_all[l], W["attn_sink"][l], W["mlstm_norm"][l], mix_w[1], cos2, sin2, "mx",
                                  shards[l][5:7], shards[l][7:8])
        xs, s3, got = _ffn_fwd(xs, W["ffn2_norm_pre"][l], W["ffn2_norm_post"][l], *ffn2_w, "f2",
                               shards[l + 1][0:3] if l + 1 < depth else None)
        lw.append(dict(ffn1=ffn1_w, mix=mix_w, ffn2=ffn2_w))
        saved.append((s1, s2, s3))
        ffn1_w = got

    dx, loss_part = loss_fwd_bwd(xs, loss_target[0], name="loss")

    F1, MX, F2 = BIG[0:3], (BIG[3], BIG[4], "conv_w"), BIG[5:8]
    names = BIG + ("conv_w",)
    parts = {n: [None] * depth for n in names}
    small_parts = [None] * depth
    waiting = None
    for l in reversed(range(depth)):
        wl = lw[l]
        s1, s2, s3 = saved[l]
        dx, dpre2, dpost2, grads2, reduced = _ffn_bwd(dx, s3, W["ffn2_norm_pre"][l], W["ffn2_norm_post"][l],
                                                      *wl["ffn2"], core, "f2", waiting)
        if waiting is not None:
            for n, r in zip(F1, reduced):
                parts[n][l + 1] = r
        dx, dpre_m, dpost_m, db_gate, dsink, dmn, grads_m, reduced = _mix_bwd(
            dx, s2, W["mix_norm_pre"][l], W["mix_norm_post"][l], wl["mix"][0], conv_all[l], W["attn_sink"][l],
            W["mlstm_norm"][l], wl["mix"][1], cos2, sin2, core, "mx", grads2)
        for n, r in zip(F2, reduced):
            parts[n][l] = r
        dx, dpre1, dpost1, waiting, reduced = _ffn_bwd(dx, s1, W["ffn1_norm_pre"][l], W["ffn1_norm_post"][l],
                                                       *wl["ffn1"], core, "f1", grads_m, last=(l == 0))
        for n, r in zip(MX, reduced):
            parts[n][l] = r
        small_parts[l] = dict(ffn1_norm_pre=dpre1[0], ffn1_norm_post=dpost1[0], mix_norm_pre=dpre_m[0],
                              mix_norm_post=dpost_m[0], b_gate=db_gate, attn_sink=dsink, mlstm_norm=dmn,
                              ffn2_norm_pre=dpre2[0], ffn2_norm_post=dpost2[0])
    for n, r in zip(F1, waiting):
        parts[n][0] = r

    outs = {k: {} for k in ("g", "d", "m", "v")}
    for n in names:
        res = adam_tensor(W[n], parts[n], M1[n], V2[n], name=f"adam_{n}")
        for k, r in zip(("g", "d", "m", "v"), res):
            outs[k][n] = jnp.transpose(r, (0, 2, 1)) if n in COLUMN_SHARDED_FFN else r
    small_out = {k: {n: [None] * depth for n in SMALL} for k in ("g", "d", "m", "v")}

    vec = jnp.concatenate([small_parts[l][n].reshape(-1) for l in range(depth) for n in SMALL]
                          + [loss_part.reshape(-1)])
    n_small = vec.shape[0]
    gathered_small = run_comm(ag_comm([_flat(vec)]), name="ag_small")[0]
    wvec = _flat(jnp.concatenate([W[n][l].reshape(-1) for l in range(depth) for n in SMALL] + [jnp.zeros((1,), F32)]))
    mvec = _flat(jnp.concatenate([M1[n][l].reshape(-1) for l in range(depth) for n in SMALL] + [jnp.zeros((1,), F32)]))
    vvec = _flat(jnp.concatenate([V2[n][l].reshape(-1) for l in range(depth) for n in SMALL] + [jnp.ones((1,), F32)]))
    res = adam_update(wvec, gathered_small, mvec, vvec, name="adam_small")
    res = [r.reshape(-1)[:n_small] for r in res]
    off = 0
    for l in range(depth):
        for n in SMALL:
            sz = W[n].shape[1]
            for k, r in zip(("g", "d", "m", "v"), res):
                small_out[k][n][l] = r[off:off + sz]
            off += sz
    loss = res[0][off]
    for k in outs:
        for n in SMALL:
            outs[k][n] = jnp.stack(small_out[k][n], axis=0)

    return (loss, dx[None], *[outs["g"][n] for n in WEIGHTS], *[outs["d"][n] for n in WEIGHTS],
            *[outs["m"][n] for n in WEIGHTS], *[outs["v"][n] for n in WEIGHTS])
```

```python
import jax
import jax.numpy as jnp
from jax import lax
from jax.experimental import pallas as pl
from jax.experimental.pallas import tpu as pltpu

F32 = jnp.float32
BF16 = jnp.bfloat16

D_MODEL = 2048
D_FF = 5632
ATT_HEADS = 8
ATT_KV_HEADS = 2
ATT_GROUP = ATT_HEADS // ATT_KV_HEADS
ATT_WIDTH = 1024
HEAD_DIM = 128
KV_WIDTH = 256
WINDOW = 128
BLK = 128
M_WIDTH = 1024
M_HEADS = 4
M_HEAD_DIM = 256
CONV_WIDTH = 5
EPS = 1e-6
ROPE_THETA = 10000.0
IN_WIDTH = 5648
N_GATES = 16
N_DEV = 8

ADAM_LR = 0.001
ADAM_B1 = 0.9
ADAM_B2 = 0.999
ADAM_EPS = 1e-08
ADAM_WD = 0.01
ADAM_STEP = 10

P_QK = 0
P_QA = 2048
P_VM = 3072
P_OM = 4096
P_KA = 5120
P_VA = 5376
P_G = 5632
P_WIDTH = 6144

LANES = 128
V7X_VMEM_LIMIT = 48 * 1024 * 1024
NEG = -1e30
MESH = pl.DeviceIdType.MESH
ANY = pl.BlockSpec(memory_space=pl.ANY)


K_TILES = (2048, 1024, 512, 256, 128)


def _tile(n, cands=(1024, 512, 256, 128)):
    for c in cands:
        if n % c == 0:
            return c
    return n


class Comm:
    def __init__(self, ins, outs, sems, phases):
        self.ins, self.outs, self.sems, self.phases = list(ins), list(outs), list(sems), list(phases)


def run_comm(comm, *, name):
    n_in, n_out = len(comm.ins), len(comm.outs)

    def body(*refs):
        ins, outs, sems = refs[:n_in], refs[n_in:n_in + n_out], refs[n_in + n_out:]
        for phase in comm.phases:
            phase(ins, outs, sems)

    return pl.pallas_call(body, name=name, out_shape=comm.outs, in_specs=[ANY] * n_in, out_specs=[ANY] * n_out,
                          scratch_shapes=comm.sems,
                          compiler_params=pltpu.CompilerParams(has_side_effects=True))(*comm.ins)


def _pcall(body, *, name, out_shape, in_specs, out_specs, grid=(), scratch=(), sem=None, comm=None):
    if comm is None:
        return pl.pallas_call(
            body, name=name, out_shape=out_shape, in_specs=in_specs, out_specs=out_specs, grid=grid,
            scratch_shapes=list(scratch),
            compiler_params=pltpu.CompilerParams(dimension_semantics=sem, vmem_limit_bytes=V7X_VMEM_LIMIT))
    multi = isinstance(out_shape, (tuple, list))
    outs = list(out_shape) if multi else [out_shape]
    ospecs = list(out_specs) if multi else [out_specs]
    n_in, n_out, n_scr = len(in_specs), len(outs), len(scratch)
    nci, nco = len(comm.ins), len(comm.outs)
    steps = 1
    for g in grid:
        steps *= g
    n_ph = len(comm.phases)
    at = [0, steps - 1] if n_ph == 2 else [0, (3 * steps) // 4, steps - 1]

    def wrapped(*refs):
        ins, cins = refs[:n_in], refs[n_in:n_in + nci]
        o0 = n_in + nci
        res, couts = refs[o0:o0 + n_out], refs[o0 + n_out:o0 + n_out + nco]
        s0 = o0 + n_out + nco
        scr, csems = refs[s0:s0 + n_scr], refs[s0 + n_scr:]
        lin = 0
        for k, g in enumerate(grid):
            lin = lin * g + pl.program_id(k)

        @pl.when(lin == at[0])
        def _():
            comm.phases[0](cins, couts, csems)

        body(*ins, *res, *scr)
        for p in range(1, n_ph):
            @pl.when(lin == at[p])
            def _(p=p):
                comm.phases[p](cins, couts, csems)

    call = pl.pallas_call(
        wrapped, name=name, out_shape=outs + comm.outs, in_specs=list(in_specs) + [ANY] * nci,
        out_specs=ospecs + [ANY] * nco, grid=grid, scratch_shapes=list(scratch) + comm.sems,
        compiler_params=pltpu.CompilerParams(dimension_semantics=("arbitrary",) * len(grid),
                                             vmem_limit_bytes=V7X_VMEM_LIMIT, has_side_effects=True))

    def run(*args):
        got = list(call(*args, *comm.ins))
        return (tuple(got[:n_out]) if multi else got[0]), got[n_out:]

    return run


def _dot(a, b):
    return jnp.dot(a, b, preferred_element_type=F32)


def _dot_nt(a, b):
    return lax.dot_general(a, b, (((1,), (1,)), ((), ())), preferred_element_type=F32)


def _dot_tn(a, b):
    return lax.dot_general(a, b, (((0,), (0,)), ((), ())), preferred_element_type=F32)


def _sigmoid(x):
    return 1.0 / (1.0 + jnp.exp(-x))


def mm_nn(a, b, *, name, out_dtype=F32, comm=None):
    M, K = a.shape
    N = b.shape[1]
    tm, tk, tn = _tile(M), _tile(K, K_TILES), _tile(N)
    nk = K // tk

    def body(a_ref, b_ref, o_ref, acc):
        k = pl.program_id(2)

        @pl.when(k == 0)
        def _():
            acc[...] = jnp.zeros_like(acc)

        acc[...] += _dot(a_ref[...], b_ref[...])

        @pl.when(k == nk - 1)
        def _():
            o_ref[...] = acc[...].astype(o_ref.dtype)

    return _pcall(body, name=name, out_shape=jax.ShapeDtypeStruct((M, N), out_dtype),
                  in_specs=[pl.BlockSpec((tm, tk), lambda i, j, k: (i, k)),
                            pl.BlockSpec((tk, tn), lambda i, j, k: (k, j))],
                  out_specs=pl.BlockSpec((tm, tn), lambda i, j, k: (i, j)), grid=(M // tm, N // tn, nk),
                  scratch=[pltpu.VMEM((tm, tn), F32)], sem=("parallel", "parallel", "arbitrary"), comm=comm)(a, b)


def mm_tn(a, g, *, name, owner_rows=None, out_dtype=F32, comm=None):
    M, K = a.shape
    N = g.shape[1]
    tm, tk, tn = _tile(M), _tile(K), _tile(N)
    nm = M // tm
    per_tile = 1 if owner_rows is None else tk // owner_rows

    def body(a_ref, g_ref, o_ref, acc):
        m = pl.program_id(2)

        @pl.when(m == 0)
        def _():
            acc[...] = jnp.zeros_like(acc)

        acc[...] += _dot_tn(a_ref[...], g_ref[...])

        @pl.when(m == nm - 1)
        def _():
            if owner_rows is None:
                o_ref[...] = acc[...].astype(o_ref.dtype)
            else:
                for d in range(per_tile):
                    o_ref[d % 2, d // 2] = acc[d * owner_rows:(d + 1) * owner_rows, :].astype(o_ref.dtype)

    if owner_rows is None:
        out_shape = jax.ShapeDtypeStruct((K, N), out_dtype)
        out_spec = pl.BlockSpec((tk, tn), lambda i, j, m: (i, j))
    else:
        assert per_tile % 2 == 0 and K == N_DEV * owner_rows
        out_shape = jax.ShapeDtypeStruct((2, N_DEV // 2, owner_rows, N), out_dtype)
        out_spec = pl.BlockSpec((2, per_tile // 2, owner_rows, tn), lambda i, j, m: (0, i, 0, j))
    return _pcall(body, name=name, out_shape=out_shape,
                  in_specs=[pl.BlockSpec((tm, tk), lambda i, j, m: (m, i)),
                            pl.BlockSpec((tm, tn), lambda i, j, m: (m, j))],
                  out_specs=out_spec, grid=(K // tk, N // tn, nm), scratch=[pltpu.VMEM((tk, tn), F32)],
                  sem=("parallel", "parallel", "arbitrary"), comm=comm)(a, g)


def mm_nt(a, b, *, name, out_dtype=F32):
    M, K = a.shape
    N = b.shape[0]
    tm, tn, tk = _tile(M), _tile(N), _tile(K, K_TILES)
    nk = K // tk

    def body(a_ref, b_ref, o_ref, acc):
        k = pl.program_id(2)

        @pl.when(k == 0)
        def _():
            acc[...] = jnp.zeros_like(acc)

        acc[...] += _dot_nt(a_ref[...], b_ref[...])

        @pl.when(k == nk - 1)
        def _():
            o_ref[...] = acc[...].astype(o_ref.dtype)

    return _pcall(body, name=name, out_shape=jax.ShapeDtypeStruct((M, N), out_dtype),
                  in_specs=[pl.BlockSpec((tm, tk), lambda i, j, k: (i, k)),
                            pl.BlockSpec((tn, tk), lambda i, j, k: (j, k))],
                  out_specs=pl.BlockSpec((tm, tn), lambda i, j, k: (i, j)), grid=(M // tm, N // tn, nk),
                  scratch=[pltpu.VMEM((tm, tn), F32)], sem=("parallel", "parallel", "arbitrary"))(a, b)


FS = D_FF // N_DEV
FSP = 768


def ffn_gu(xn, wg8, wu8, *, name, comm=None):
    S, D = xn.shape
    tm = _tile(S)

    def body(x_ref, wg_ref, wu_ref, hg_ref, hu_ref, act_ref):
        xv = x_ref[...]
        hg = _dot_nt(xv, wg_ref[...])
        hu = _dot_nt(xv, wu_ref[...])
        hg_ref[...] = hg.astype(BF16)
        hu_ref[...] = hu.astype(BF16)
        act_ref[...] = (hg * _sigmoid(hg) * hu).astype(BF16)

    wspec = pl.BlockSpec((None, FSP, D), lambda i, j: (j, 0, 0))
    ospec = pl.BlockSpec((None, tm, FSP), lambda i, j: (j, i, 0))
    shp = jax.ShapeDtypeStruct((N_DEV, S, FSP), BF16)
    return _pcall(body, name=name, out_shape=(shp, shp, shp),
                  in_specs=[pl.BlockSpec((tm, D), lambda i, j: (i, 0)), wspec, wspec],
                  out_specs=(ospec, ospec, ospec), grid=(S // tm, N_DEV), sem=("parallel", "arbitrary"),
                  comm=comm)(xn, wg8, wu8)


def ffn_down(act8, wd8, *, name, comm=None):
    _, S, _ = act8.shape
    D = wd8.shape[2]
    tm, tn = _tile(S), D

    def body(a_ref, w_ref, o_ref):
        @pl.when(pl.program_id(2) == 0)
        def _():
            o_ref[...] = jnp.zeros_like(o_ref)

        o_ref[...] += _dot(a_ref[...], w_ref[...])

    return _pcall(body, name=name, out_shape=jax.ShapeDtypeStruct((S, D), F32),
                  in_specs=[pl.BlockSpec((None, tm, FSP), lambda i, n, j: (j, i, 0)),
                            pl.BlockSpec((None, FSP, tn), lambda i, n, j: (j, 0, n))],
                  out_specs=pl.BlockSpec((tm, tn), lambda i, n, j: (i, n)),
                  grid=(S // tm, D // tn, N_DEV), sem=("parallel", "parallel", "arbitrary"), comm=comm)(act8, wd8)


def ffn_dact(df, wd8, hg8, hu8, *, name):
    S, D = df.shape
    tm = _tile(S)

    def body(d_ref, w_ref, hg_ref, hu_ref, dg_ref, du_ref):
        da = _dot_nt(d_ref[...], w_ref[...])
        hg = hg_ref[...].astype(F32)
        hu = hu_ref[...].astype(F32)
        sg = _sigmoid(hg)
        dg_ref[...] = (da * hu * (sg * (1.0 + hg * (1.0 - sg)))).astype(BF16)
        du_ref[...] = (da * hg * sg).astype(BF16)

    blk = pl.BlockSpec((None, tm, FSP), lambda i, j: (j, i, 0))
    shp = jax.ShapeDtypeStruct((N_DEV, S, FSP), BF16)
    return _pcall(body, name=name, out_shape=(shp, shp),
                  in_specs=[pl.BlockSpec((tm, D), lambda i, j: (i, 0)),
                            pl.BlockSpec((None, FSP, D), lambda i, j: (j, 0, 0)), blk, blk],
                  out_specs=(blk, blk), grid=(S // tm, N_DEV), sem=("parallel", "arbitrary"))(df, wd8, hg8, hu8)


def ffn_dwd(act8, df, *, name, out_dtype, comm=None):
    _, S, _ = act8.shape
    D = df.shape[1]
    tm, tn = _tile(S), D
    nm = S // tm

    def body(a_ref, d_ref, o_ref, acc):
        m = pl.program_id(2)

        @pl.when(m == 0)
        def _():
            acc[...] = jnp.zeros_like(acc)

        acc[...] += _dot_tn(a_ref[...], d_ref[...])

        @pl.when(m == nm - 1)
        def _():
            o_ref[...] = acc[0:FS, :].astype(o_ref.dtype)

    return _pcall(body, name=name, out_shape=jax.ShapeDtypeStruct((2, N_DEV // 2, FS, D), out_dtype),
                  in_specs=[pl.BlockSpec((None, tm, FSP), lambda j, n, m: (j, m, 0)),
                            pl.BlockSpec((tm, tn), lambda j, n, m: (m, n))],
                  out_specs=pl.BlockSpec((None, None, FS, tn), lambda j, n, m: (j % 2, j // 2, 0, n)),
                  grid=(N_DEV, D // tn, nm), scratch=[pltpu.VMEM((FSP, tn), F32)],
                  sem=("parallel", "parallel", "arbitrary"), comm=comm)(act8, df)


def ffn_dwgu(xn, dg8, du8, *, name, out_dtype, comm=None):
    S, D = xn.shape
    tm, tk = _tile(S), _tile(D)
    nm = S // tm

    def body(x_ref, dg_ref, du_ref, og_ref, ou_ref, accg, accu):
        m = pl.program_id(2)

        @pl.when(m == 0)
        def _():
            accg[...] = jnp.zeros_like(accg)
            accu[...] = jnp.zeros_like(accu)

        xv = x_ref[...]
        accg[...] += _dot_tn(dg_ref[...], xv)
        accu[...] += _dot_tn(du_ref[...], xv)

        @pl.when(m == nm - 1)
        def _():
            og_ref[...] = accg[0:FS, :].astype(og_ref.dtype)
            ou_ref[...] = accu[0:FS, :].astype(ou_ref.dtype)

    blk = pl.BlockSpec((None, tm, FSP), lambda j, k, m: (j, m, 0))
    ospec = pl.BlockSpec((None, None, FS, tk), lambda j, k, m: (j % 2, j // 2, 0, k))
    shp = jax.ShapeDtypeStruct((2, N_DEV // 2, FS, D), out_dtype)
    return _pcall(body, name=name, out_shape=(shp, shp),
                  in_specs=[pl.BlockSpec((tm, tk), lambda j, k, m: (m, k)), blk, blk],
                  out_specs=(ospec, ospec), grid=(N_DEV, D // tk, nm),
                  scratch=[pltpu.VMEM((FSP, tk), F32), pltpu.VMEM((FSP, tk), F32)],
                  sem=("parallel", "parallel", "arbitrary"), comm=comm)(xn, dg8, du8)


def ffn_dxn(dg8, du8, wg8, wu8, *, name, comm=None):
    _, S, _ = dg8.shape
    D = wg8.shape[2]
    tm, tn = _tile(S), _tile(D)

    def body(dg_ref, du_ref, wg_ref, wu_ref, o_ref):
        @pl.when(pl.program_id(2) == 0)
        def _():
            o_ref[...] = jnp.zeros_like(o_ref)

        o_ref[...] += _dot(dg_ref[...], wg_ref[...]) + _dot(du_ref[...], wu_ref[...])

    blk = pl.BlockSpec((None, tm, FSP), lambda i, n, j: (j, i, 0))
    wspec = pl.BlockSpec((None, FSP, tn), lambda i, n, j: (j, 0, n))
    return _pcall(body, name=name, out_shape=jax.ShapeDtypeStruct((S, D), F32),
                  in_specs=[blk, blk, wspec, wspec], out_specs=pl.BlockSpec((tm, tn), lambda i, n, j: (i, n)),
                  grid=(S // tm, D // tn, N_DEV), sem=("parallel", "parallel", "arbitrary"),
                  comm=comm)(dg8, du8, wg8, wu8)


def norm_fwd(x, g, *, name, scale=1.0, resid=None, out_dtype=F32):
    S, D = x.shape
    tm = _tile(S, (512, 256, 128))

    def body(*refs):
        if resid is None:
            x_ref, g_ref, o_ref = refs
        else:
            x_ref, g_ref, r_ref, o_ref = refs
        xv = x_ref[...].astype(F32)
        r = lax.rsqrt(jnp.mean(xv * xv, axis=-1, keepdims=True) + EPS)
        y = (xv * r) * g_ref[...]
        if scale != 1.0:
            y = y * scale
        if resid is not None:
            y = y + r_ref[...]
        o_ref[...] = y.astype(o_ref.dtype)

    row = pl.BlockSpec((tm, D), lambda i: (i, 0))
    in_specs = [row, pl.BlockSpec((1, D), lambda i: (0, 0))]
    args = [x, g.reshape(1, D)]
    if resid is not None:
        in_specs.append(row)
        args.append(resid)
    return _pcall(body, name=name, out_shape=jax.ShapeDtypeStruct((S, D), out_dtype), in_specs=in_specs,
                  out_specs=row, grid=(S // tm,), sem=("parallel",))(*args)


def norm_bwd(dy, x, g, *, name, scale=1.0, resid=None, out_dtype=F32):
    S, D = x.shape
    tm = _tile(S, (512, 256, 128))

    def body(*refs):
        if resid is None:
            dy_ref, x_ref, g_ref, dx_ref, dg_ref = refs
        else:
            dy_ref, x_ref, g_ref, r_ref, dx_ref, dg_ref = refs

        @pl.when(pl.program_id(0) == 0)
        def _():
            dg_ref[...] = jnp.zeros_like(dg_ref)

        xv = x_ref[...].astype(F32)
        d = dy_ref[...].astype(F32)
        if scale != 1.0:
            d = d * scale
        r = lax.rsqrt(jnp.mean(xv * xv, axis=-1, keepdims=True) + EPS)
        xh = xv * r
        dg_ref[...] += jnp.sum(d * xh, axis=0, keepdims=True)
        dxh = d * g_ref[...]
        dx = r * (dxh - xh * jnp.mean(dxh * xh, axis=-1, keepdims=True))
        if resid is not None:
            dx = dx + r_ref[...]
        dx_ref[...] = dx.astype(dx_ref.dtype)

    row = pl.BlockSpec((tm, D), lambda i: (i, 0))
    vec = pl.BlockSpec((1, D), lambda i: (0, 0))
    in_specs = [row, row, vec]
    args = [dy, x, g.reshape(1, D)]
    if resid is not None:
        in_specs.append(row)
        args.append(resid)
    return _pcall(body, name=name,
                  out_shape=(jax.ShapeDtypeStruct((S, D), out_dtype), jax.ShapeDtypeStruct((1, D), F32)),
                  in_specs=in_specs, out_specs=(row, vec), grid=(S // tm,), sem=("arbitrary",))(*args)


def loss_fwd_bwd(y, target, *, name):
    S, D = y.shape
    tm = _tile(S, (512, 256, 128))

    def body(y_ref, t_ref, dy_ref, l_ref):
        @pl.when(pl.program_id(0) == 0)
        def _():
            l_ref[...] = jnp.zeros_like(l_ref)

        e = y_ref[...] - t_ref[...]
        dy_ref[...] = e * (1.0 / D)
        l_ref[...] += jnp.sum(jnp.sum(e * e, axis=1, keepdims=True), axis=0, keepdims=True) * (0.5 / D)

    row = pl.BlockSpec((tm, D), lambda i: (i, 0))
    one = pl.BlockSpec((1, 1), lambda i: (0, 0))
    return _pcall(body, name=name,
                  out_shape=(jax.ShapeDtypeStruct((S, D), F32), jax.ShapeDtypeStruct((1, 1), F32)),
                  in_specs=[row, row], out_specs=(row, one), grid=(S // tm,), sem=("arbitrary",))(y, target)


def _rope_tables(S):
    half = HEAD_DIM // 2
    inv_freq = ROPE_THETA ** (-jnp.arange(half, dtype=F32) / half)
    ang = jnp.arange(S, dtype=F32)[:, None] * inv_freq[None, :]
    cos, sin = jnp.cos(ang), jnp.sin(ang)
    return jnp.concatenate([cos, cos], axis=1), jnp.concatenate([-sin, sin], axis=1)


def _rope(x, cos2, sin2):
    return x * cos2 + pltpu.roll(x, HEAD_DIM // 2, 1) * sin2


def _unrope(d, cos2, sin2):
    return d * cos2 + pltpu.roll(d * sin2, HEAD_DIM // 2, 1)


def _nbr_specs(width, col, nb):
    return [pl.BlockSpec((BLK, width), lambda n, c=col: (jnp.maximum(n - 1, 0), c)),
            pl.BlockSpec((BLK, width), lambda n, c=col: (n, c)),
            pl.BlockSpec((BLK, width), lambda n, c=col: (jnp.minimum(n + 1, nb - 1), c))]


def attn_fwd(proj, cos2, sin2, sink, *, name, comm=None):
    S = proj.shape[0]
    nb = S // BLK
    scale = HEAD_DIM ** -0.5

    def body(sink_ref, q_ref, k0, k1, k2, v0, v1, v2, c0, c1, c2, s0, s1, s2, o_ref, lse_ref):
        n = pl.program_id(0)
        cosk = jnp.concatenate([c0[...], c1[...], c2[...]], axis=0)
        sink_ = jnp.concatenate([s0[...], s1[...], s2[...]], axis=0)
        kall = jnp.concatenate([k0[...], k1[...], k2[...]], axis=0)
        vall = jnp.concatenate([v0[...], v1[...], v2[...]], axis=0)
        rows = lax.broadcasted_iota(jnp.int32, (BLK, 3 * BLK), 0)
        cols = lax.broadcasted_iota(jnp.int32, (BLK, 3 * BLK), 1)
        kpos = (n - 1) * BLK + cols
        valid = (jnp.abs(cols - BLK - rows) <= WINDOW) & (kpos >= 0) & (kpos < S)
        valid = jnp.concatenate([valid] * ATT_GROUP, axis=0)
        lane = lax.broadcasted_iota(jnp.int32, (BLK, LANES), 1)
        lse_tile = jnp.zeros((BLK, LANES), F32)
        for hk in range(ATT_KV_HEADS):
            ks = slice(hk * HEAD_DIM, (hk + 1) * HEAD_DIM)
            kh = _rope(kall[:, ks], cosk, sink_).astype(BF16)
            vh = vall[:, ks].astype(BF16)
            qs = []
            for g in range(ATT_GROUP):
                hq = hk * ATT_GROUP + g
                qs.append(_rope(q_ref[:, hq * HEAD_DIM:(hq + 1) * HEAD_DIM], c1[...], s1[...]))
            qh = jnp.concatenate(qs, axis=0).astype(BF16)
            s = _dot_nt(qh, kh) * scale
            s = jnp.where(valid, s, NEG)
            snk = jnp.concatenate(
                [jnp.full((BLK, 1), sink_ref[hk * ATT_GROUP + g], F32) for g in range(ATT_GROUP)], axis=0)
            m = jnp.maximum(jnp.max(s, axis=1, keepdims=True), snk)
            p = jnp.exp(s - m)
            l = jnp.sum(p, axis=1, keepdims=True) + jnp.exp(snk - m)
            o = _dot(p.astype(BF16), vh) * (1.0 / l)
            lse = m + jnp.log(l)
            for g in range(ATT_GROUP):
                hq = hk * ATT_GROUP + g
                o_ref[:, hq * HEAD_DIM:(hq + 1) * HEAD_DIM] = o[g * BLK:(g + 1) * BLK].astype(o_ref.dtype)
                lse_tile = lse_tile + jnp.where(lane == hq, lse[g * BLK:(g + 1) * BLK], 0.0)
        lse_ref[...] = lse_tile

    in_specs = ([pl.BlockSpec(memory_space=pltpu.SMEM),
                 pl.BlockSpec((BLK, ATT_WIDTH), lambda n: (n, P_QA // ATT_WIDTH))]
                + _nbr_specs(KV_WIDTH, P_KA // KV_WIDTH, nb) + _nbr_specs(KV_WIDTH, P_VA // KV_WIDTH, nb)
                + _nbr_specs(HEAD_DIM, 0, nb) + _nbr_specs(HEAD_DIM, 0, nb))
    return _pcall(body, name=name,
                  out_shape=(jax.ShapeDtypeStruct((S, ATT_WIDTH), BF16), jax.ShapeDtypeStruct((S, LANES), F32)),
                  in_specs=in_specs,
                  out_specs=(pl.BlockSpec((BLK, ATT_WIDTH), lambda n: (n, 0)),
                             pl.BlockSpec((BLK, LANES), lambda n: (n, 0))),
                  grid=(nb,), sem=("parallel",), comm=comm)(sink, proj, proj, proj, proj, proj, proj, proj,
                                                            cos2, cos2, cos2, sin2, sin2, sin2)


def attn_bwd(proj, y, dy, lse, cos2, sin2, sink, *, name):
    S = proj.shape[0]
    nb = S // BLK
    scale = HEAD_DIM ** -0.5

    def body(sink_ref, q_ref, k0, k1, k2, v0, v1, v2, o_ref, d_ref, l_ref, c0, c1, c2, s0, s1, s2,
             dq_ref, dk_ref, dv_ref, dsink_ref, dk_acc, dv_acc):
        n = pl.program_id(0)

        @pl.when(n == 0)
        def _():
            dsink_ref[...] = jnp.zeros_like(dsink_ref)
            dk_acc[...] = jnp.zeros_like(dk_acc)
            dv_acc[...] = jnp.zeros_like(dv_acc)

        @pl.when(n < nb)
        def _():
            cosk = jnp.concatenate([c0[...], c1[...], c2[...]], axis=0)
            sink_ = jnp.concatenate([s0[...], s1[...], s2[...]], axis=0)
            kall = jnp.concatenate([k0[...], k1[...], k2[...]], axis=0)
            vall = jnp.concatenate([v0[...], v1[...], v2[...]], axis=0)
            lane = lax.broadcasted_iota(jnp.int32, (1, LANES), 1)
            rows = lax.broadcasted_iota(jnp.int32, (BLK, 3 * BLK), 0)
            cols = lax.broadcasted_iota(jnp.int32, (BLK, 3 * BLK), 1)
            kpos = (n - 1) * BLK + cols
            valid = (jnp.abs(cols - BLK - rows) <= WINDOW) & (kpos >= 0) & (kpos < S)
            valid = jnp.concatenate([valid] * ATT_GROUP, axis=0)
            dsink_acc = jnp.zeros((1, LANES), F32)
            for hk in range(ATT_KV_HEADS):
                ks = slice(hk * HEAD_DIM, (hk + 1) * HEAD_DIM)
                kh = _rope(kall[:, ks], cosk, sink_).astype(BF16)
                vh = vall[:, ks].astype(BF16)
                qs, dos, lses, deltas = [], [], [], []
                for g in range(ATT_GROUP):
                    hq = hk * ATT_GROUP + g
                    hs = slice(hq * HEAD_DIM, (hq + 1) * HEAD_DIM)
                    qs.append(_rope(q_ref[:, hs], c1[...], s1[...]))
                    do = d_ref[:, hs]
                    dos.append(do)
                    lses.append(l_ref[:, hq:hq + 1])
                    deltas.append(jnp.sum(do * o_ref[:, hs].astype(F32), axis=1, keepdims=True))
                qh = jnp.concatenate(qs, axis=0).astype(BF16)
                doh = jnp.concatenate(dos, axis=0).astype(BF16)
                lseh = jnp.concatenate(lses, axis=0)
                delh = jnp.concatenate(deltas, axis=0)
                s = jnp.where(valid, _dot_nt(qh, kh) * scale, NEG)
                p = jnp.exp(s - lseh)
                dp = _dot_nt(doh, vh)
                ds = (p * (dp - delh)).astype(BF16)
                dq = _dot(ds, kh) * scale
                dk_acc[hk] += _dot_tn(ds, qh) * scale
                dv_acc[hk] += _dot_tn(p.astype(BF16), doh)
                for g in range(ATT_GROUP):
                    hq = hk * ATT_GROUP + g
                    dq_ref[:, hq * HEAD_DIM:(hq + 1) * HEAD_DIM] = _unrope(dq[g * BLK:(g + 1) * BLK], c1[...], s1[...])
                    psink = jnp.exp(sink_ref[hq] - lses[g])
                    dsink_acc = dsink_acc + jnp.where(lane == hq, -jnp.sum(psink * deltas[g]), 0.0)
            dsink_ref[...] += dsink_acc

        c_out = jnp.where(n < nb, c0[...], c1[...])
        s_out = jnp.where(n < nb, s0[...], s1[...])
        for hk in range(ATT_KV_HEADS):
            ks = slice(hk * HEAD_DIM, (hk + 1) * HEAD_DIM)
            dk_ref[:, ks] = _unrope(dk_acc[hk, 0:BLK, :], c_out, s_out)
            dv_ref[:, ks] = dv_acc[hk, 0:BLK, :]
            for acc in (dk_acc, dv_acc):
                acc[hk, 0:BLK, :] = acc[hk, BLK:2 * BLK, :]
                acc[hk, BLK:2 * BLK, :] = acc[hk, 2 * BLK:3 * BLK, :]
                acc[hk, 2 * BLK:3 * BLK, :] = jnp.zeros((BLK, HEAD_DIM), F32)

    own = lambda n: jnp.minimum(n, nb - 1)
    done = lambda n: jnp.maximum(n - 1, 0)

    def nbr(width, col):
        return [pl.BlockSpec((BLK, width), lambda n, c=col: (jnp.maximum(own(n) - 1, 0), c)),
                pl.BlockSpec((BLK, width), lambda n, c=col: (own(n), c)),
                pl.BlockSpec((BLK, width), lambda n, c=col: (jnp.minimum(own(n) + 1, nb - 1), c))]

    in_specs = ([pl.BlockSpec(memory_space=pltpu.SMEM),
                 pl.BlockSpec((BLK, ATT_WIDTH), lambda n: (own(n), P_QA // ATT_WIDTH))]
                + nbr(KV_WIDTH, P_KA // KV_WIDTH) + nbr(KV_WIDTH, P_VA // KV_WIDTH)
                + [pl.BlockSpec((BLK, ATT_WIDTH), lambda n: (own(n), 0)),
                   pl.BlockSpec((BLK, ATT_WIDTH), lambda n: (own(n), 0)),
                   pl.BlockSpec((BLK, LANES), lambda n: (own(n), 0))]
                + nbr(HEAD_DIM, 0) + nbr(HEAD_DIM, 0))
    args = [sink, proj] + [proj] * 6 + [y, dy, lse] + [cos2] * 3 + [sin2] * 3
    return _pcall(body, name=name,
                  out_shape=(jax.ShapeDtypeStruct((S, ATT_WIDTH), F32), jax.ShapeDtypeStruct((S, KV_WIDTH), F32),
                             jax.ShapeDtypeStruct((S, KV_WIDTH), F32), jax.ShapeDtypeStruct((1, LANES), F32)),
                  in_specs=in_specs,
                  out_specs=(pl.BlockSpec((BLK, ATT_WIDTH), lambda n: (own(n), 0)),
                             pl.BlockSpec((BLK, KV_WIDTH), lambda n: (done(n), 0)),
                             pl.BlockSpec((BLK, KV_WIDTH), lambda n: (done(n), 0)),
                             pl.BlockSpec((1, LANES), lambda n: (0, 0))),
                  grid=(nb + 1,),
                  scratch=[pltpu.VMEM((ATT_KV_HEADS, 3 * BLK, HEAD_DIM), F32),
                           pltpu.VMEM((ATT_KV_HEADS, 3 * BLK, HEAD_DIM), F32)],
                  sem=("arbitrary",))(*args)


CONV_HALO = 8
CONV_COLS = 512


def _halo_specs(tm, nrow, col_of):
    hb = tm // CONV_HALO
    return [pl.BlockSpec((CONV_HALO, CONV_COLS), lambda i, j: (jnp.maximum(i * hb - 1, 0), col_of(j))),
            pl.BlockSpec((tm, CONV_COLS), lambda i, j: (i, col_of(j))),
            pl.BlockSpec((CONV_HALO, CONV_COLS),
                         lambda i, j: (jnp.minimum((i + 1) * hb, nrow * hb - 1), col_of(j)))]


def _with_halo(prev, cur, nxt, i, nrow):
    p = jnp.where(i > 0, prev[...], 0.0)
    q = jnp.where(i < nrow - 1, nxt[...], 0.0)
    return jnp.concatenate([p, cur[...], q], axis=0)


def _conv_taps(xt, w_ref, tm):
    n = xt.shape[0]
    acc = jnp.zeros_like(xt)
    for j in range(CONV_WIDTH):
        sh = (CONV_WIDTH // 2 - j) % n
        xs = xt if sh == 0 else pltpu.roll(xt, sh, 0)
        acc = acc + xs * w_ref[j:j + 1, :]
    return acc


def conv_fwd(proj, conv_w, *, name):
    S = proj.shape[0]
    tm = _tile(S, (512, 256, 128))
    nrow = S // tm

    def body(xp, xc, xn, w_ref, o_ref):
        i = pl.program_id(0)
        xt = _with_halo(xp, xc, xn, i, nrow)
        pre = _conv_taps(xt, w_ref, tm)[CONV_HALO:CONV_HALO + tm]
        o_ref[...] = pre * _sigmoid(pre)

    return _pcall(body, name=name, out_shape=jax.ShapeDtypeStruct((S, 2 * M_WIDTH), F32),
                  in_specs=_halo_specs(tm, nrow, lambda j: P_QK // CONV_COLS + j)
                  + [pl.BlockSpec((CONV_HALO, CONV_COLS), lambda i, j: (0, j))],
                  out_specs=pl.BlockSpec((tm, CONV_COLS), lambda i, j: (i, j)),
                  grid=(nrow, 2 * M_WIDTH // CONV_COLS), sem=("parallel", "parallel"))(proj, proj, proj, conv_w)


def conv_bwd(proj, conv_w, da, db, *, name):
    S = proj.shape[0]
    tm = _tile(S, (512, 256, 128))
    nrow = S // tm

    def body(xp, xc, xn, ap, ac, an, bp, bc, bn, w_ref, dx_ref, dw_ref):
        i = pl.program_id(1)

        @pl.when(i == 0)
        def _():
            dw_ref[...] = jnp.zeros_like(dw_ref)

        xt = _with_halo(xp, xc, xn, i, nrow)
        dt = _with_halo(ap, ac, an, i, nrow) + _with_halo(bp, bc, bn, i, nrow)
        pre = _conv_taps(xt, w_ref, tm)
        sg = _sigmoid(pre)
        dpre = dt * (sg * (1.0 + pre * (1.0 - sg)))
        n = xt.shape[0]
        ridx = lax.broadcasted_iota(jnp.int32, (n, 1), 0)
        dpre = jnp.where((ridx >= 2) & (ridx < n - 2), dpre, 0.0)
        dx = jnp.zeros_like(xt)
        own = (ridx >= CONV_HALO) & (ridx < CONV_HALO + tm)
        dpre_own = jnp.where(own, dpre, 0.0)
        dw_rows = []
        for j in range(CONV_WIDTH):
            sh = (j - CONV_WIDTH // 2) % n
            ds_ = dpre if sh == 0 else pltpu.roll(dpre, sh, 0)
            dx = dx + ds_ * w_ref[j:j + 1, :]
            shx = (CONV_WIDTH // 2 - j) % n
            xs = xt if shx == 0 else pltpu.roll(xt, shx, 0)
            dw_rows.append(jnp.sum(dpre_own * xs, axis=0, keepdims=True))
        dx_ref[...] = dx[CONV_HALO:CONV_HALO + tm]
        dw_rows.append(jnp.zeros((CONV_HALO - CONV_WIDTH, CONV_COLS), F32))
        dw_ref[...] += jnp.concatenate(dw_rows, axis=0)

    colq = lambda j: P_QK // CONV_COLS + j
    same = lambda j: j

    def swap(specs):
        return [pl.BlockSpec(s.block_shape, (lambda f: (lambda j, i: f(i, j)))(s.index_map)) for s in specs]

    in_specs = swap(_halo_specs(tm, nrow, colq) + _halo_specs(tm, nrow, same) + _halo_specs(tm, nrow, same)
                    + [pl.BlockSpec((CONV_HALO, CONV_COLS), lambda i, j: (0, j))])
    return _pcall(body, name=name,
                  out_shape=(jax.ShapeDtypeStruct((S, 2 * M_WIDTH), F32),
                             jax.ShapeDtypeStruct((CONV_HALO, 2 * M_WIDTH), F32)),
                  in_specs=in_specs,
                  out_specs=(pl.BlockSpec((tm, CONV_COLS), lambda j, i: (i, j)),
                             pl.BlockSpec((CONV_HALO, CONV_COLS), lambda j, i: (0, j))),
                  grid=(2 * M_WIDTH // CONV_COLS, nrow), sem=("parallel", "arbitrary"))(
                      proj, proj, proj, da, da, da, db, db, db, conv_w)


def _log_sigmoid(x):
    return jnp.minimum(x, 0.0) - jnp.log(1.0 + jnp.exp(-jnp.abs(x)))


def _scan_sum(x, axis, from_end):
    idx = lax.broadcasted_iota(jnp.int32, x.shape, axis)
    n = x.shape[axis]
    sh = 1
    while sh < n:
        if from_end:
            x = x + jnp.where(idx < n - sh, pltpu.roll(x, n - sh, axis), 0.0)
        else:
            x = x + jnp.where(idx >= sh, pltpu.roll(x, sh, axis), 0.0)
        sh *= 2
    return x


def gate_rows(proj, *, name):
    S = proj.shape[0]

    def body(x_ref, o_ref):
        o_ref[...] = x_ref[...].T[0:N_GATES, :]

    return _pcall(body, name=name, out_shape=jax.ShapeDtypeStruct((N_GATES, S), F32),
                  in_specs=[pl.BlockSpec((BLK, LANES), lambda c: (c, P_G // LANES))],
                  out_specs=pl.BlockSpec((N_GATES, BLK), lambda c: (0, c)), grid=(S // BLK,), sem=("parallel",))(proj)


def _gate_setup(gc_ref, gr_ref, bgc_ref, bgr_ref, reverse):
    gc = gc_ref[...] + bgc_ref[...]
    gr = gr_ref[...] + bgr_ref[...]
    bc = _scan_sum(_log_sigmoid(gc), 0, reverse)
    br = _scan_sum(_log_sigmoid(gr), 1, reverse)
    return gc, gr, bc, br


def _head_gates(gc, gr, bc, br, h, m_in, reverse, tri):
    io = (M_HEADS if reverse else 0) + h
    fo = (3 * M_HEADS if reverse else 2 * M_HEADS) + h
    last = 0 if reverse else BLK - 1
    b_col, b_row = bc[:, fo:fo + 1], br[fo:fo + 1, :]
    ig_col, ig_row = gc[:, io:io + 1], gr[io:io + 1, :]
    logd = jnp.where(tri, b_col - b_row + ig_row, NEG)
    m_t = jnp.maximum(b_col + m_in, jnp.max(logd, axis=1, keepdims=True))
    dm = jnp.exp(logd - m_t)
    gi = jnp.exp(b_col + m_in - m_t)
    b_last = b_row[:, last:last + 1]
    logw = b_last - b_row + ig_row
    m_new = jnp.maximum(b_last + m_in, jnp.max(logw, axis=1, keepdims=True))
    w_col = jnp.exp(b_last - b_col + ig_col - m_new)
    dec = jnp.exp(b_last + m_in - m_new)
    return io, fo, m_t, dm, gi, m_new, w_col, dec


def _tri_mask(reverse):
    rows = lax.broadcasted_iota(jnp.int32, (BLK, BLK), 0)
    cols = lax.broadcasted_iota(jnp.int32, (BLK, BLK), 1)
    return (cols >= rows) if reverse else (cols <= rows)


def mlstm_fwd(qk, proj, gates_r, bg_c, bg_r, *, reverse, name):
    S = qk.shape[0]
    nc = S // BLK
    kscale = M_HEAD_DIM ** -0.5
    cidx = (lambda c: nc - 1 - c) if reverse else (lambda c: c)

    def body(qk_ref, v_ref, gc_ref, gr_ref, bgc_ref, bgr_ref, h_ref, den_ref, cst_ref, nm_ref, c_sc, n_sc, m_sc):
        @pl.when(pl.program_id(0) == 0)
        def _():
            c_sc[...] = jnp.zeros_like(c_sc)
            n_sc[...] = jnp.zeros_like(n_sc)
            m_sc[...] = jnp.zeros_like(m_sc)

        gc, gr, bc, br = _gate_setup(gc_ref, gr_ref, bgc_ref, bgr_ref, reverse)
        tri = _tri_mask(reverse)
        lane = lax.broadcasted_iota(jnp.int32, (BLK, LANES), 1)
        den_tile = jnp.zeros((BLK, LANES), F32)
        for h in range(M_HEADS):
            cs = slice(h * M_HEAD_DIM, (h + 1) * M_HEAD_DIM)
            m_in = m_sc[h][:, 0:1]
            _, _, m_t, dm, gi, m_new, w_col, dec = _head_gates(gc, gr, bc, br, h, m_in, reverse, tri)
            q = qk_ref[:, cs]
            k = qk_ref[:, M_WIDTH + h * M_HEAD_DIM:M_WIDTH + (h + 1) * M_HEAD_DIM] * kscale
            v = v_ref[:, cs]
            c_in, n_in = c_sc[h], n_sc[h]
            cst_ref[h] = c_in
            nm_ref[h, 0:1, :] = n_in
            nm_ref[h, 1:2, :] = m_sc[h]
            qb, kb, vb = q.astype(BF16), k.astype(BF16), v.astype(BF16)
            s = _dot_nt(qb, kb) * dm
            num = _dot(s.astype(BF16), vb) + gi * _dot_nt(qb, c_in.astype(BF16))
            den = jnp.sum(s, axis=1, keepdims=True) + gi * jnp.sum(q * n_in, axis=1, keepdims=True)
            z = jnp.maximum(jnp.abs(den), jnp.exp(-m_t))
            h_ref[:, cs] = num * (1.0 / z)
            den_tile = den_tile + jnp.where(lane == h, den, 0.0)
            c_sc[h] = dec * c_in + _dot_tn((w_col * v).astype(BF16), kb)
            n_sc[h] = dec * n_in + jnp.sum(w_col * k, axis=0, keepdims=True)
            m_sc[h] = jnp.broadcast_to(m_new, (1, M_HEAD_DIM))
        den_ref[...] = den_tile

    return _pcall(
        body, name=name,
        out_shape=(jax.ShapeDtypeStruct((S, M_WIDTH), F32), jax.ShapeDtypeStruct((S, LANES), F32),
                   jax.ShapeDtypeStruct((nc, M_HEADS, M_HEAD_DIM, M_HEAD_DIM), F32),
                   jax.ShapeDtypeStruct((nc, M_HEADS, 2, M_HEAD_DIM), F32)),
        in_specs=[pl.BlockSpec((BLK, 2 * M_WIDTH), lambda c: (cidx(c), 0)),
                  pl.BlockSpec((BLK, M_WIDTH), lambda c: (cidx(c), P_VM // M_WIDTH)),
                  pl.BlockSpec((BLK, LANES), lambda c: (cidx(c), P_G // LANES)),
                  pl.BlockSpec((N_GATES, BLK), lambda c: (0, cidx(c))),
                  pl.BlockSpec((1, LANES), lambda c: (0, 0)),
                  pl.BlockSpec((N_GATES, 1), lambda c: (0, 0))],
        out_specs=(pl.BlockSpec((BLK, M_WIDTH), lambda c: (cidx(c), 0)),
                   pl.BlockSpec((BLK, LANES), lambda c: (cidx(c), 0)),
                   pl.BlockSpec((None, M_HEADS, M_HEAD_DIM, M_HEAD_DIM), lambda c: (cidx(c), 0, 0, 0)),
                   pl.BlockSpec((None, M_HEADS, 2, M_HEAD_DIM), lambda c: (cidx(c), 0, 0, 0))),
        grid=(nc,),
        scratch=[pltpu.VMEM((M_HEADS, M_HEAD_DIM, M_HEAD_DIM), F32), pltpu.VMEM((M_HEADS, 1, M_HEAD_DIM), F32),
                 pltpu.VMEM((M_HEADS, 1, M_HEAD_DIM), F32)],
        sem=("arbitrary",))(qk, proj, proj, gates_r, bg_c, bg_r)


def mlstm_bwd(qk, proj, gates_r, bg_c, bg_r, hdir, den, cst, nm, dh, *, reverse, name, comm=None):
    S = qk.shape[0]
    nc = S // BLK
    kscale = M_HEAD_DIM ** -0.5
    cidx = (lambda c: c) if reverse else (lambda c: nc - 1 - c)
    last = 0 if reverse else BLK - 1

    def body(qk_ref, v_ref, gc_ref, gr_ref, bgc_ref, bgr_ref, h_ref, den_ref, cst_ref, nm_ref, dh_ref,
             dqk_ref, dv_ref, dgc_ref, dgr_ref, dc_sc, dn_sc):
        @pl.when(pl.program_id(0) == 0)
        def _():
            dc_sc[...] = jnp.zeros_like(dc_sc)
            dn_sc[...] = jnp.zeros_like(dn_sc)

        gc, gr, bc, br = _gate_setup(gc_ref, gr_ref, bgc_ref, bgr_ref, reverse)
        tri = _tri_mask(reverse)
        lane_c = lax.broadcasted_iota(jnp.int32, (BLK, LANES), 1)
        row_c = lax.broadcasted_iota(jnp.int32, (BLK, 1), 0)
        row_r = lax.broadcasted_iota(jnp.int32, (N_GATES, BLK), 0)
        db_c = jnp.zeros((BLK, LANES), F32)
        dig_c = jnp.zeros((BLK, LANES), F32)
        db_r = jnp.zeros((N_GATES, BLK), F32)
        dig_r = jnp.zeros((N_GATES, BLK), F32)
        for h in range(M_HEADS):
            cs = slice(h * M_HEAD_DIM, (h + 1) * M_HEAD_DIM)
            ks = slice(M_WIDTH + h * M_HEAD_DIM, M_WIDTH + (h + 1) * M_HEAD_DIM)
            m_in = nm_ref[h, 1:2, 0:1]
            io, fo, m_t, dm, gi, m_new, w_col, dec = _head_gates(gc, gr, bc, br, h, m_in, reverse, tri)
            q = qk_ref[:, cs]
            k = qk_ref[:, ks] * kscale
            v = v_ref[:, cs]
            c_in, n_in = cst_ref[h], nm_ref[h, 0:1, :]
            qb, kb, vb, cb = q.astype(BF16), k.astype(BF16), v.astype(BF16), c_in.astype(BF16)
            s = _dot_nt(qb, kb) * dm
            den_h = den_ref[:, h:h + 1]
            emt = jnp.exp(-m_t)
            rz = 1.0 / jnp.maximum(jnp.abs(den_h), emt)
            dhh = dh_ref[:, cs]
            dnum = dhh * rz
            hdh = jnp.sum(dhh * h_ref[:, cs], axis=1, keepdims=True)
            dden = jnp.where(jnp.abs(den_h) > emt, -hdh * rz * jnp.sign(den_h), 0.0)
            dnb = dnum.astype(BF16)
            ds = _dot_nt(dnb, vb) + dden
            e = ds * s
            dsd = (ds * dm).astype(BF16)
            gd = (gi * dnum).astype(BF16)
            gdd = gi * dden
            dq = _dot(dsd, kb) + _dot(gd, cb) + gdd * n_in
            dk = _dot_tn(dsd, qb)
            dv = _dot_tn(s.astype(BF16), dnb)
            dc_in = _dot_tn(gd, qb)
            dn_in = jnp.sum(gdd * q, axis=0, keepdims=True)
            cq = _dot_nt(qb, cb)
            dg = jnp.sum(dnum * cq, axis=1, keepdims=True) + dden * jnp.sum(q * n_in, axis=1, keepdims=True)
            eg = dg * gi
            dco, dno = dc_sc[h], dn_sc[h]
            dcob = dco.astype(BF16)
            dwv = _dot_nt(kb, dcob)
            dv = dv + w_col * dwv
            dw = jnp.sum(v * dwv, axis=1, keepdims=True) + jnp.sum(k * dno, axis=1, keepdims=True)
            dk = dk + _dot((w_col * v).astype(BF16), dcob) + w_col * dno
            ew = dw * w_col
            ddec = (jnp.sum(jnp.sum(dco * c_in, axis=1, keepdims=True), axis=0, keepdims=True)
                    + jnp.sum(dno * n_in, axis=1, keepdims=True))
            dc_sc[h] = dec * dco + dc_in
            dn_sc[h] = dec * dno + dn_in
            dqk_ref[:, cs] = dq
            dqk_ref[:, ks] = dk * kscale
            dv_ref[:, cs] = dv
            csum = jnp.sum(e, axis=0, keepdims=True)
            db_last = jnp.sum(ew, axis=0, keepdims=True) + ddec * dec
            db_col = jnp.sum(e, axis=1, keepdims=True) + eg - ew + jnp.where(row_c == last, db_last, 0.0)
            db_c = db_c + jnp.where(lane_c == fo, db_col, 0.0)
            dig_c = dig_c + jnp.where(lane_c == io, ew, 0.0)
            db_r = db_r + jnp.where(row_r == fo, -csum, 0.0)
            dig_r = dig_r + jnp.where(row_r == io, csum, 0.0)
        dgc_ref[...] = dig_c + _scan_sum(db_c, 0, not reverse) * _sigmoid(-gc)
        dgr_ref[...] = dig_r + _scan_sum(db_r, 1, not reverse) * _sigmoid(-gr)

    chunk = lambda w, col=0: pl.BlockSpec((BLK, w), lambda c: (cidx(c), col))
    return _pcall(
        body, name=name,
        out_shape=(jax.ShapeDtypeStruct((S, 2 * M_WIDTH), F32), jax.ShapeDtypeStruct((S, M_WIDTH), F32),
                   jax.ShapeDtypeStruct((S, LANES), F32), jax.ShapeDtypeStruct((N_GATES, S), F32)),
        in_specs=[chunk(2 * M_WIDTH), chunk(M_WIDTH, P_VM // M_WIDTH), chunk(LANES, P_G // LANES),
                  pl.BlockSpec((N_GATES, BLK), lambda c: (0, cidx(c))),
                  pl.BlockSpec((1, LANES), lambda c: (0, 0)),
                  pl.BlockSpec((N_GATES, 1), lambda c: (0, 0)),
                  chunk(M_WIDTH), chunk(LANES),
                  pl.BlockSpec((None, M_HEADS, M_HEAD_DIM, M_HEAD_DIM), lambda c: (cidx(c), 0, 0, 0)),
                  pl.BlockSpec((None, M_HEADS, 2, M_HEAD_DIM), lambda c: (cidx(c), 0, 0, 0)),
                  chunk(M_WIDTH)],
        out_specs=(chunk(2 * M_WIDTH), chunk(M_WIDTH), chunk(LANES),
                   pl.BlockSpec((N_GATES, BLK), lambda c: (0, cidx(c)))),
        grid=(nc,),
        scratch=[pltpu.VMEM((M_HEADS, M_HEAD_DIM, M_HEAD_DIM), F32), pltpu.VMEM((M_HEADS, 1, M_HEAD_DIM), F32)],
        sem=("arbitrary",), comm=comm)(qk, proj, proj, gates_r, bg_c, bg_r, hdir, den, cst, nm, dh)


def headnorm_fwd(hf, hb, proj, mnorm, *, name):
    S = hf.shape[0]
    tm = _tile(S, (512, 256, 128))

    def body(hf_ref, hb_ref, om_ref, mn_ref, y_ref):
        for h in range(M_HEADS):
            cs = slice(h * M_HEAD_DIM, (h + 1) * M_HEAD_DIM)
            hm = hf_ref[:, cs] + hb_ref[:, cs]
            r = lax.rsqrt(jnp.mean(hm * hm, axis=-1, keepdims=True) + EPS)
            y_ref[:, cs] = (_sigmoid(om_ref[:, cs]) * ((hm * r) * mn_ref[:, cs])).astype(y_ref.dtype)

    row = pl.BlockSpec((tm, M_WIDTH), lambda i: (i, 0))
    return _pcall(body, name=name, out_shape=jax.ShapeDtypeStruct((S, M_WIDTH), BF16),
                  in_specs=[row, row, pl.BlockSpec((tm, M_WIDTH), lambda i: (i, P_OM // M_WIDTH)),
                            pl.BlockSpec((1, M_WIDTH), lambda i: (0, 0))],
                  out_specs=row, grid=(S // tm,), sem=("parallel",))(hf, hb, proj, mnorm)


def headnorm_bwd(hf, hb, proj, mnorm, dy, *, name):
    S = hf.shape[0]
    tm = _tile(S, (512, 256, 128))

    def body(hf_ref, hb_ref, om_ref, mn_ref, dy_ref, dh_ref, dom_ref, dmn_ref):
        @pl.when(pl.program_id(0) == 0)
        def _():
            dmn_ref[...] = jnp.zeros_like(dmn_ref)

        for h in range(M_HEADS):
            cs = slice(h * M_HEAD_DIM, (h + 1) * M_HEAD_DIM)
            hm = hf_ref[:, cs] + hb_ref[:, cs]
            r = lax.rsqrt(jnp.mean(hm * hm, axis=-1, keepdims=True) + EPS)
            xh = hm * r
            so = _sigmoid(om_ref[:, cs])
            d = dy_ref[:, cs]
            mn = mn_ref[:, cs]
            dom_ref[:, cs] = d * (xh * mn) * (so * (1.0 - so))
            dxm = d * so
            dmn_ref[:, cs] += jnp.sum(dxm * xh, axis=0, keepdims=True)
            dxh = dxm * mn
            dh_ref[:, cs] = r * (dxh - xh * jnp.mean(dxh * xh, axis=-1, keepdims=True))

    row = pl.BlockSpec((tm, M_WIDTH), lambda i: (i, 0))
    vec = pl.BlockSpec((1, M_WIDTH), lambda i: (0, 0))
    return _pcall(body, name=name,
                  out_shape=(jax.ShapeDtypeStruct((S, M_WIDTH), F32), jax.ShapeDtypeStruct((S, M_WIDTH), F32),
                             jax.ShapeDtypeStruct((1, M_WIDTH), F32)),
                  in_specs=[row, row, pl.BlockSpec((tm, M_WIDTH), lambda i: (i, P_OM // M_WIDTH)), vec,
                            pl.BlockSpec((tm, M_WIDTH), lambda i: (i, 1))],
                  out_specs=(row, row, vec), grid=(S // tm,), sem=("arbitrary",))(hf, hb, proj, mnorm, dy)


def _place():
    return lax.axis_index("x"), lax.axis_index("y"), lax.axis_index("c")


def _ag_plan(x_refs, out_refs, sems):
    send_sems, recv_sems, local_sems = sems
    T = len(x_refs)
    x, y, c = _place()
    me, sibling = (x, y, c), (x, y, 1 - c)
    chips = [(1 - x, y), (x, 1 - y), (1 - x, 1 - y)]

    def copy(t, k, block, to, src=None):
        px, py, pc = block
        dst = out_refs[t].at[4 * px + 2 * py + pc]
        return pltpu.make_async_remote_copy(
            src_ref=dst if src is None else src, dst_ref=dst, send_sem=send_sems.at[7 * t + k],
            recv_sem=recv_sems.at[7 * t + k], device_id=to, device_id_type=MESH)

    mine = [pltpu.make_async_copy(x_refs[t], out_refs[t].at[4 * x + 2 * y + c], local_sems.at[t]) for t in range(T)]
    first = []
    for t in range(T):
        first.append(copy(t, 0, me, sibling, src=x_refs[t]))
        first += [copy(t, 1 + j, me, (*chip, c), src=x_refs[t]) for j, chip in enumerate(chips)]
    landed = [copy(t, 1 + j, (*chip, c), me) for j, chip in enumerate(chips) for t in range(T)]
    passed = [copy(t, 4 + j, (*chip, c), sibling) for j, chip in enumerate(chips) for t in range(T)]
    from_sibling = [copy(t, 0, sibling, me) for t in range(T)]
    from_sibling += [copy(t, 4 + j, (*chip, 1 - c), me) for j, chip in enumerate(chips) for t in range(T)]
    return mine, first, landed, passed, from_sibling


def _ag_start(x_refs, out_refs, sems):
    mine, first, _, _, _ = _ag_plan(x_refs, out_refs, sems)
    for cp in mine + first:
        cp.start()


def _ag_forward(x_refs, out_refs, sems):
    _, _, landed, passed, _ = _ag_plan(x_refs, out_refs, sems)
    for got, on in zip(landed, passed):
        got.wait_recv()
        on.start()


def _ag_finish(x_refs, out_refs, sems):
    mine, first, _, passed, from_sibling = _ag_plan(x_refs, out_refs, sems)
    for cp in from_sibling:
        cp.wait_recv()
    for cp in first + passed:
        cp.wait_send()
    for cp in mine:
        cp.wait()


def ag_comm(shards):
    T = len(shards)
    return Comm(shards, [jax.ShapeDtypeStruct((N_DEV,) + s.shape, s.dtype) for s in shards],
                [pltpu.SemaphoreType.DMA((7 * T,)), pltpu.SemaphoreType.DMA((7 * T,)), pltpu.SemaphoreType.DMA((T,))],
                [_ag_start, _ag_forward, _ag_finish])


def _pair_plan(g_refs, out_refs, sems):
    send_sems, recv_sems = sems
    x, y, c = _place()
    return [pltpu.make_async_remote_copy(
        src_ref=g_refs[t].at[1 - c], dst_ref=out_refs[t], send_sem=send_sems.at[t], recv_sem=recv_sems.at[t],
        device_id=(x, y, 1 - c), device_id_type=MESH) for t in range(len(g_refs))]


def _pair_start(g_refs, out_refs, sems):
    for cp in _pair_plan(g_refs, out_refs, sems):
        cp.start()


def _pair_finish(g_refs, out_refs, sems):
    for cp in _pair_plan(g_refs, out_refs, sems):
        cp.wait()


def pair_comm(grads):
    T = len(grads)
    return Comm(grads, [jax.ShapeDtypeStruct(g.shape[1:], g.dtype) for g in grads],
                [pltpu.SemaphoreType.DMA((T,)), pltpu.SemaphoreType.DMA((T,))], [_pair_start, _pair_finish])


def _chip_plan(p_refs, out_refs, sems):
    send_sems, recv_sems, local_sems = sems
    T = len(p_refs)
    x, y, c = _place()
    mychip = 2 * x + y
    chips = [(1 - x, y), (x, 1 - y), (1 - x, 1 - y)]
    mine = [pltpu.make_async_copy(p_refs[t].at[mychip], out_refs[t].at[mychip], local_sems.at[t]) for t in range(T)]
    cps = [pltpu.make_async_remote_copy(
        src_ref=p_refs[t].at[2 * px + py], dst_ref=out_refs[t].at[mychip], send_sem=send_sems.at[3 * t + j],
        recv_sem=recv_sems.at[3 * t + j], device_id=(px, py, c), device_id_type=MESH)
        for t in range(T) for j, (px, py) in enumerate(chips)]
    return mine, cps


def _chip_start(p_refs, out_refs, sems):
    mine, cps = _chip_plan(p_refs, out_refs, sems)
    for cp in mine + cps:
        cp.start()


def _chip_finish(p_refs, out_refs, sems):
    mine, cps = _chip_plan(p_refs, out_refs, sems)
    for cp in cps + mine:
        cp.wait()


def chip_comm(parts):
    T = len(parts)
    return Comm(parts, [jax.ShapeDtypeStruct(p.shape, p.dtype) for p in parts],
                [pltpu.SemaphoreType.DMA((3 * T,)), pltpu.SemaphoreType.DMA((3 * T,)), pltpu.SemaphoreType.DMA((T,))],
                [_chip_start, _chip_finish])


PAIR_ADD_BLOCK_BYTES = 4 * 1024 * 1024


def pair_add(g, recv, core, *, name):
    _, nchip, R, C = g.shape
    tr = R if R * C * g.dtype.itemsize <= PAIR_ADD_BLOCK_BYTES else _tile(R, (512, 256, 128, 64))

    def body(c_ref, a_ref, b_ref, o_ref):
        o_ref[...] = (a_ref[...].astype(F32) + b_ref[...].astype(F32)).astype(o_ref.dtype)

    grid_spec = pltpu.PrefetchScalarGridSpec(
        num_scalar_prefetch=1, grid=(nchip, R // tr),
        in_specs=[pl.BlockSpec((None, None, tr, C), lambda k, i, c_ref: (c_ref[0], k, i, 0)),
                  pl.BlockSpec((None, tr, C), lambda k, i, c_ref: (k, i, 0))],
        out_specs=pl.BlockSpec((None, tr, C), lambda k, i, c_ref: (k, i, 0)))
    return pl.pallas_call(body, name=name, out_shape=jax.ShapeDtypeStruct(recv.shape, recv.dtype),
                          grid_spec=grid_spec,
                          compiler_params=pltpu.CompilerParams(dimension_semantics=("parallel", "parallel"),
                                                               vmem_limit_bytes=V7X_VMEM_LIMIT))(core, g, recv)


def _adam_math(w, g, m, v):
    m = ADAM_B1 * m + (1.0 - ADAM_B1) * g
    v = ADAM_B2 * v + (1.0 - ADAM_B2) * (g * g)
    m_hat = m / (1.0 - ADAM_B1 ** ADAM_STEP)
    v_hat = v / (1.0 - ADAM_B2 ** ADAM_STEP)
    delta = -ADAM_LR * (m_hat / (jnp.sqrt(v_hat) + ADAM_EPS) + ADAM_WD * w)
    return delta, m, v


def adam_update(w, parts, m, v, *, name):
    P, R, _ = parts.shape
    tr = _tile(R, (1024, 512, 256, 128, 64, 32, 16, 8))

    def body(w_ref, p_ref, m_ref, v_ref, g_ref, d_ref, nm_ref, nv_ref):
        g = p_ref[0]
        for k in range(1, P):
            g = g + p_ref[k]
        d, nm, nv = _adam_math(w_ref[...], g, m_ref[...], v_ref[...])
        g_ref[...] = g
        d_ref[...] = d
        nm_ref[...] = nm
        nv_ref[...] = nv

    row = pl.BlockSpec((tr, LANES), lambda i: (i, 0))
    shp = jax.ShapeDtypeStruct((R, LANES), F32)
    return _pcall(body, name=name, out_shape=(shp, shp, shp, shp),
                  in_specs=[row, pl.BlockSpec((P, tr, LANES), lambda i: (0, i, 0)), row, row],
                  out_specs=(row, row, row, row), grid=(R // tr,), sem=("parallel",))(w, parts, m, v)


ADAM_STEP_BYTES = 6 * 1024 * 1024


def adam_tensor(w, parts, m, v, *, name):
    L, R, C = w.shape
    per_row = L * C * (7 * 4 + 4 * parts[0].dtype.itemsize)
    tr = R
    for cand in (256, 128, 64, 32, 16):
        if R % cand == 0 and cand * per_row <= ADAM_STEP_BYTES:
            tr = cand
            break

    def body(*refs):
        w_ref, m_ref, v_ref = refs[:3]
        p_refs = refs[3:3 + L]
        g_ref, d_ref, nm_ref, nv_ref = refs[3 + L:]
        for l in range(L):
            g = p_refs[l][0].astype(F32)
            for k in range(1, 4):
                g = g + p_refs[l][k].astype(F32)
            d, nm, nv = _adam_math(w_ref[l], g, m_ref[l], v_ref[l])
            g_ref[l] = g
            d_ref[l] = d
            nm_ref[l] = nm
            nv_ref[l] = nv

    blk = pl.BlockSpec((L, tr, C), lambda i: (0, i, 0))
    pblk = pl.BlockSpec((4, tr, C), lambda i: (0, i, 0))
    shp = jax.ShapeDtypeStruct((L, R, C), F32)
    return _pcall(body, name=name, out_shape=(shp, shp, shp, shp), in_specs=[blk, blk, blk] + [pblk] * L,
                  out_specs=(blk, blk, blk, blk), grid=(R // tr,), sem=("parallel",))(w, m, v, *parts)


def _rows(n_elems):
    r = -(-n_elems // LANES)
    return -(-r // 1024) * 1024 if r > 1024 else -(-r // 16) * 16


def _flat(a, dtype=None):
    n = a.size
    r = _rows(n)
    f = a.reshape(-1)
    if dtype is not None:
        f = f.astype(dtype)
    if r * LANES != n:
        f = jnp.pad(f, (0, r * LANES - n))
    return f.reshape(r, LANES)


def _gathered_cols(g):
    n, rows, cols = g.shape
    return g.transpose(1, 0, 2).reshape(rows, n * cols)


def _owner_cols(dw, dtype):
    rows = dw.shape[0]
    cols = dw.shape[1] // N_DEV
    return dw.reshape(rows, N_DEV // 2, 2, cols).transpose(2, 1, 0, 3).astype(dtype)


_IN_NAT = dict(qa=(0, 1024), ka=(1024, 1280), va=(1280, 1536), qm=(1536, 2560), km=(2560, 3584),
               vm=(3584, 4608), om=(4608, 5632), g=(5632, 5648))


def _permute_w_in(w):
    sl = lambda k: w[:, _IN_NAT[k][0]:_IN_NAT[k][1]]
    pad = jnp.zeros((w.shape[0], P_WIDTH - P_G - N_GATES), w.dtype)
    return jnp.concatenate([sl("qm"), sl("km"), sl("qa"), sl("vm"), sl("om"), sl("ka"), sl("va"), sl("g"), pad],
                           axis=1)


def _unpermute_dw_in(dw):
    qm, km = dw[:, P_QK:P_QK + 1024], dw[:, P_QK + 1024:P_QK + 2048]
    return jnp.concatenate([dw[:, P_QA:P_QA + 1024], dw[:, P_KA:P_KA + 256], dw[:, P_VA:P_VA + 256], qm, km,
                            dw[:, P_VM:P_VM + 1024], dw[:, P_OM:P_OM + 1024], dw[:, P_G:P_G + N_GATES]], axis=1)


BIG = ("ffn1_w_gate", "ffn1_w_up", "ffn1_w_down", "w_in", "w_out", "ffn2_w_gate", "ffn2_w_up", "ffn2_w_down")
COLUMN_SHARDED_FFN = ("ffn1_w_gate", "ffn1_w_up", "ffn2_w_gate", "ffn2_w_up")
SMALL = ("ffn1_norm_pre", "ffn1_norm_post", "mix_norm_pre", "mix_norm_post", "b_gate", "attn_sink", "mlstm_norm",
         "ffn2_norm_pre", "ffn2_norm_post")
WEIGHTS = ("ffn1_norm_pre", "ffn1_norm_post", "ffn1_w_gate", "ffn1_w_up", "ffn1_w_down", "mix_norm_pre",
           "mix_norm_post", "w_in", "b_gate", "conv_w", "attn_sink", "mlstm_norm", "w_out", "ffn2_norm_pre",
           "ffn2_norm_post", "ffn2_w_gate", "ffn2_w_up", "ffn2_w_down")


GRAD_DT = BF16


def _carried(result, comm):
    return result if comm is not None else (result, None)


def _pair_adds(grads, recv, core, tag):
    return [pair_add(g, r, core, name=f"{tag}_add{t}") for t, (g, r) in enumerate(zip(grads, recv))]


def _ffn_fwd(x, g_pre, g_post, wg8, wu8, wd8, tag, gather=None, gather_down=None):
    xn = norm_fwd(x, g_pre, name=f"{tag}_pre", out_dtype=BF16)
    comm = None if gather is None else ag_comm(gather)
    (hg, hu, act), gathered = _carried(ffn_gu(xn, wg8, wu8, name=f"{tag}_gu", comm=comm), comm)
    comm = None if gather_down is None else ag_comm(gather_down)
    f, more = _carried(ffn_down(act, wd8, name=f"{tag}_down", comm=comm), comm)
    x_new = norm_fwd(f, g_post, name=f"{tag}_post", scale=0.5, resid=x)
    return x_new, (x, xn, hg, hu, act, f), gathered if more is None else gathered + more


def _ffn_bwd(dx, saved, g_pre, g_post, wg8, wu8, wd8, core, tag, reduce=None, last=False):
    x, xn, hg, hu, act, f = saved
    df, dg_post = norm_bwd(dx, f, g_post, name=f"{tag}_post_b", scale=0.5, out_dtype=BF16)
    comm = None if reduce is None else pair_comm(reduce)
    dwd, recv = _carried(ffn_dwd(act, df, name=f"{tag}_dwd", out_dtype=GRAD_DT, comm=comm), comm)
    dhg, dhu = ffn_dact(df, wd8, hg, hu, name=f"{tag}_dact")
    comm = None if reduce is None else chip_comm(_pair_adds(reduce, recv, core, tag))
    (dwg, dwu), reduced = _carried(ffn_dwgu(xn, dhg, dhu, name=f"{tag}_dwgu", out_dtype=GRAD_DT, comm=comm), comm)
    own = [dwg, dwu, dwd]
    comm = None
    if last:
        recv = run_comm(pair_comm(own), name="rs1_last")
        comm = chip_comm(_pair_adds(own, recv, core, "last"))
    dxn, own_reduced = _carried(ffn_dxn(dhg, dhu, wg8, wu8, name=f"{tag}_dxn", comm=comm), comm)
    dx_new, dg_pre = norm_bwd(dxn, x, g_pre, name=f"{tag}_pre_b", resid=dx)
    return dx_new, dg_pre, dg_post, own_reduced if last else own, reduced


def _mix_fwd(x, g_pre, g_post, w_in_p, b_gate, conv_full, sink, mnorm, w_out, cos2, sin2, tag, gather_in, gather_out):
    S = x.shape[0]
    xn = norm_fwd(x, g_pre, name=f"{tag}_pre", out_dtype=BF16)
    comm = ag_comm(gather_in)
    proj, got_in = mm_nn(xn, w_in_p, name=f"{tag}_in", comm=comm)
    gates_r = gate_rows(proj, name=f"{tag}_gt")
    bg_c = jnp.pad(b_gate, (0, LANES - N_GATES)).reshape(1, LANES)
    bg_r = b_gate.reshape(N_GATES, 1)
    (y_att, lse), got_out = attn_fwd(proj, cos2, sin2, sink, name=f"{tag}_att", comm=ag_comm(gather_out))
    qk = conv_fwd(proj, conv_full, name=f"{tag}_conv")
    hf, denf, cf, nmf = mlstm_fwd(qk, proj, gates_r, bg_c, bg_r, reverse=False, name=f"{tag}_mf")
    hb, denb, cb, nmb = mlstm_fwd(qk, proj, gates_r, bg_c, bg_r, reverse=True, name=f"{tag}_mb")
    y_m = headnorm_fwd(hf, hb, proj, mnorm.reshape(1, M_WIDTH), name=f"{tag}_hn")
    y = jnp.concatenate([y_att, y_m], axis=1)
    mo = mm_nn(y, w_out, name=f"{tag}_out")
    x_new = norm_fwd(mo, g_post, name=f"{tag}_post", resid=x)
    saved = (x, xn, proj, gates_r, bg_c, bg_r, lse, qk, hf, denf, cf, nmf, hb, denb, cb, nmb, y, mo)
    return x_new, saved, got_in + got_out


def _mix_bwd(dx, saved, g_pre, g_post, w_in_p, conv_full, sink, mnorm, w_out, cos2, sin2, core, tag, reduce=None):
    x, xn, proj, gates_r, bg_c, bg_r, lse, qk, hf, denf, cf, nmf, hb, denb, cb, nmb, y, mo = saved
    S = x.shape[0]
    dmo, dg_post = norm_bwd(dx, mo, g_post, name=f"{tag}_post_b", out_dtype=BF16)
    comm = None if reduce is None else pair_comm(reduce)
    dw_out, recv = _carried(mm_tn(y, dmo, name=f"{tag}_dwo", owner_rows=D_MODEL // N_DEV, out_dtype=GRAD_DT,
                                  comm=comm), comm)
    dy = mm_nt(dmo, w_out, name=f"{tag}_dy")
    mn = mnorm.reshape(1, M_WIDTH)
    dh, dom, dmn = headnorm_bwd(hf, hb, proj, mn, dy, name=f"{tag}_hn_b")
    halves = None if reduce is None else _pair_adds(reduce, recv, core, tag)
    comm = None if reduce is None else chip_comm(halves[:-1])
    (dqk_f, dv_f, dgc_f, dgr_f), reduced = _carried(
        mlstm_bwd(qk, proj, gates_r, bg_c, bg_r, hf, denf, cf, nmf, dh, reverse=False, name=f"{tag}_mf_b",
                  comm=comm), comm)
    comm = None if reduce is None else chip_comm(halves[-1:])
    (dqk_b, dv_b, dgc_b, dgr_b), more = _carried(
        mlstm_bwd(qk, proj, gates_r, bg_c, bg_r, hb, denb, cb, nmb, dh, reverse=True, name=f"{tag}_mb_b",
                  comm=comm), comm)
    reduced = None if reduce is None else reduced + more
    dqk_in, dconv = conv_bwd(proj, conv_full, dqk_f, dqk_b, name=f"{tag}_conv_b")
    dqa, dka, dva, dsink = attn_bwd(proj, y, dy, lse, cos2, sin2, sink, name=f"{tag}_att_b")
    dgates = dgc_f + dgc_b + jnp.pad((dgr_f + dgr_b).T, ((0, 0), (0, LANES - N_GATES)))
    dproj = jnp.concatenate([dqk_in.astype(BF16), dqa.astype(BF16), (dv_f + dv_b).astype(BF16), dom.astype(BF16),
                             dka.astype(BF16), dva.astype(BF16), dgates.astype(BF16),
                             jnp.zeros((S, P_WIDTH - P_G - LANES), BF16)], axis=1)
    db_gate = colsum(dgates, name=f"{tag}_dbg")[0, :N_GATES]
    dw_in = mm_tn(xn, dproj, name=f"{tag}_dwi")
    dxn = mm_nt(dproj, w_in_p, name=f"{tag}_dxn")
    dx_new, dg_pre = norm_bwd(dxn, x, g_pre, name=f"{tag}_pre_b", resid=dx)
    grads = [_owner_cols(_unpermute_dw_in(dw_in), GRAD_DT), dw_out, _owner_cols(dconv[:CONV_WIDTH], F32)]
    return dx_new, dg_pre, dg_post, db_gate, dsink[0, :ATT_HEADS], dmn[0], grads, reduced


def colsum(a, *, name):
    S, C = a.shape
    tm = _tile(S, (512, 256, 128))

    def body(a_ref, o_ref):
        @pl.when(pl.program_id(0) == 0)
        def _():
            o_ref[...] = jnp.zeros_like(o_ref)

        o_ref[...] += jnp.sum(a_ref[...], axis=0, keepdims=True)

    return _pcall(body, name=name, out_shape=jax.ShapeDtypeStruct((1, C), F32),
                  in_specs=[pl.BlockSpec((tm, C), lambda i: (i, 0))], out_specs=pl.BlockSpec((1, C), lambda i: (0, 0)),
                  grid=(S // tm,), sem=("arbitrary",))(a)


def _layer_shards(W, l):
    pad_r = lambda a: jnp.pad(a.astype(BF16), ((0, FSP - FS), (0, 0)))
    return [pad_r(W["ffn1_w_gate"][l]), pad_r(W["ffn1_w_up"][l]), pad_r(W["ffn1_w_down"][l]),
            W["w_in"][l].astype(BF16), W["w_out"][l].astype(BF16),
            pad_r(W["ffn2_w_gate"][l]), pad_r(W["ffn2_w_up"][l]), pad_r(W["ffn2_w_down"][l])]


def kernel(x, ffn1_norm_pre, ffn1_norm_post, ffn1_w_gate, ffn1_w_up, ffn1_w_down, mix_norm_pre, mix_norm_post, w_in, b_gate, conv_w, attn_sink, mlstm_norm, w_out, ffn2_norm_pre, ffn2_norm_post, ffn2_w_gate, ffn2_w_up, ffn2_w_down, loss_target, m_ffn1_norm_pre, m_ffn1_norm_post, m_ffn1_w_gate, m_ffn1_w_up, m_ffn1_w_down, m_mix_norm_pre, m_mix_norm_post, m_w_in, m_b_gate, m_conv_w, m_attn_sink, m_mlstm_norm, m_w_out, m_ffn2_norm_pre, m_ffn2_norm_post, m_ffn2_w_gate, m_ffn2_w_up, m_ffn2_w_down, v_ffn1_norm_pre, v_ffn1_norm_post, v_ffn1_w_gate, v_ffn1_w_up, v_ffn1_w_down, v_mix_norm_pre, v_mix_norm_post, v_w_in, v_b_gate, v_conv_w, v_attn_sink, v_mlstm_norm, v_w_out, v_ffn2_norm_pre, v_ffn2_norm_post, v_ffn2_w_gate, v_ffn2_w_up, v_ffn2_w_down):
    W = dict(ffn1_norm_pre=ffn1_norm_pre, ffn1_norm_post=ffn1_norm_post, ffn1_w_gate=ffn1_w_gate,
             ffn1_w_up=ffn1_w_up, ffn1_w_down=ffn1_w_down, mix_norm_pre=mix_norm_pre, mix_norm_post=mix_norm_post,
             w_in=w_in, b_gate=b_gate, conv_w=conv_w, attn_sink=attn_sink, mlstm_norm=mlstm_norm, w_out=w_out,
             ffn2_norm_pre=ffn2_norm_pre, ffn2_norm_post=ffn2_norm_post, ffn2_w_gate=ffn2_w_gate,
             ffn2_w_up=ffn2_w_up, ffn2_w_down=ffn2_w_down)
    M1 = dict(ffn1_norm_pre=m_ffn1_norm_pre, ffn1_norm_post=m_ffn1_norm_post, ffn1_w_gate=m_ffn1_w_gate,
              ffn1_w_up=m_ffn1_w_up, ffn1_w_down=m_ffn1_w_down, mix_norm_pre=m_mix_norm_pre,
              mix_norm_post=m_mix_norm_post, w_in=m_w_in, b_gate=m_b_gate, conv_w=m_conv_w, attn_sink=m_attn_sink,
              mlstm_norm=m_mlstm_norm, w_out=m_w_out, ffn2_norm_pre=m_ffn2_norm_pre,
              ffn2_norm_post=m_ffn2_norm_post, ffn2_w_gate=m_ffn2_w_gate, ffn2_w_up=m_ffn2_w_up,
              ffn2_w_down=m_ffn2_w_down)
    V2 = dict(ffn1_norm_pre=v_ffn1_norm_pre, ffn1_norm_post=v_ffn1_norm_post, ffn1_w_gate=v_ffn1_w_gate,
              ffn1_w_up=v_ffn1_w_up, ffn1_w_down=v_ffn1_w_down, mix_norm_pre=v_mix_norm_pre,
              mix_norm_post=v_mix_norm_post, w_in=v_w_in, b_gate=v_b_gate, conv_w=v_conv_w, attn_sink=v_attn_sink,
              mlstm_norm=v_mlstm_norm, w_out=v_w_out, ffn2_norm_pre=v_ffn2_norm_pre,
              ffn2_norm_post=v_ffn2_norm_post, ffn2_w_gate=v_ffn2_w_gate, ffn2_w_up=v_ffn2_w_up,
              ffn2_w_down=v_ffn2_w_down)
    for n in COLUMN_SHARDED_FFN:
        W[n], M1[n], V2[n] = (jnp.transpose(a, (0, 2, 1)) for a in (W[n], M1[n], V2[n]))
    depth = w_in.shape[0]
    S = x.shape[1]
    xs = x[0]
    cos2, sin2 = _rope_tables(S)
    core = lax.axis_index("c").astype(jnp.int32).reshape(1)

    cs = conv_w.shape[2]
    conv_g = run_comm(ag_comm([conv_w.reshape(depth * CONV_WIDTH, cs)]), name="ag_conv")[0]
    conv_all = conv_g.reshape(N_DEV, depth, CONV_WIDTH, cs).transpose(1, 2, 0, 3)
    conv_all = conv_all.reshape(depth, CONV_WIDTH, N_DEV * cs)
    conv_all = jnp.pad(conv_all, ((0, 0), (0, CONV_HALO - CONV_WIDTH), (0, 0)))

    lw, saved = [], []
    shards = [_layer_shards(W, l) for l in range(depth)]
    ffn1_w = run_comm(ag_comm(shards[0][0:3]), name="ag_first")
    for l in range(depth):
        xs, s1, got = _ffn_fwd(xs, W["ffn1_norm_pre"][l], W["ffn1_norm_post"][l], *ffn1_w, "f1", shards[l][3:5])
        mix_w = (_permute_w_in(_gathered_cols(got[0])), got[1].reshape(D_MODEL, D_MODEL))
        xs, s2, ffn2_w = _mix_fwd(xs, W["mix_norm_pre"][l], W["mix_norm_post"][l], mix_w[0], W["b_gate"][l],
                                  conv_all[l], W["attn_sink"][l], W["mlstm_norm"][l], mix_w[1], cos2, sin2, "mx",
                                  shards[l][5:7], shards[l][7:8])
        more = l + 1 < depth
        xs, s3, got = _ffn_fwd(xs, W["ffn2_norm_pre"][l], W["ffn2_norm_post"][l], *ffn2_w, "f2",
                               shards[l + 1][0:2] if more else None, shards[l + 1][2:3] if more else None)
        lw.append(dict(ffn1=ffn1_w, mix=mix_w, ffn2=ffn2_w))
        saved.append((s1, s2, s3))
        ffn1_w = got

    dx, loss_part = loss_fwd_bwd(xs, loss_target[0], name="loss")

    F1, MX, F2 = BIG[0:3], (BIG[3], BIG[4], "conv_w"), BIG[5:8]
    names = BIG + ("conv_w",)
    parts = {n: [None] * depth for n in names}
    small_parts = [None] * depth
    waiting = None
    for l in reversed(range(depth)):
        wl = lw[l]
        s1, s2, s3 = saved[l]
        dx, dpre2, dpost2, grads2, reduced = _ffn_bwd(dx, s3, W["ffn2_norm_pre"][l], W["ffn2_norm_post"][l],
                                                      *wl["ffn2"], core, "f2", waiting)
        if waiting is not None:
            for n, r in zip(F1, reduced):
                parts[n][l + 1] = r
        dx, dpre_m, dpost_m, db_gate, dsink, dmn, grads_m, reduced = _mix_bwd(
            dx, s2, W["mix_norm_pre"][l], W["mix_norm_post"][l], wl["mix"][0], conv_all[l], W["attn_sink"][l],
            W["mlstm_norm"][l], wl["mix"][1], cos2, sin2, core, "mx", grads2)
        for n, r in zip(F2, reduced):
            parts[n][l] = r
        dx, dpre1, dpost1, waiting, reduced = _ffn_bwd(dx, s1, W["ffn1_norm_pre"][l], W["ffn1_norm_post"][l],
                                                       *wl["ffn1"], core, "f1", grads_m, last=(l == 0))
        for n, r in zip(MX, reduced):
            parts[n][l] = r
        small_parts[l] = dict(ffn1_norm_pre=dpre1[0], ffn1_norm_post=dpost1[0], mix_norm_pre=dpre_m[0],
                              mix_norm_post=dpost_m[0], b_gate=db_gate, attn_sink=dsink, mlstm_norm=dmn,
                              ffn2_norm_pre=dpre2[0], ffn2_norm_post=dpost2[0])
    for n, r in zip(F1, waiting):
        parts[n][0] = r

    outs = {k: {} for k in ("g", "d", "m", "v")}
    for n in names:
        res = adam_tensor(W[n], parts[n], M1[n], V2[n], name=f"adam_{n}")
        for k, r in zip(("g", "d", "m", "v"), res):
            outs[k][n] = jnp.transpose(r, (0, 2, 1)) if n in COLUMN_SHARDED_FFN else r
    small_out = {k: {n: [None] * depth for n in SMALL} for k in ("g", "d", "m", "v")}

    vec = jnp.concatenate([small_parts[l][n].reshape(-1) for l in range(depth) for n in SMALL]
                          + [loss_part.reshape(-1)])
    n_small = vec.shape[0]
    gathered_small = run_comm(ag_comm([_flat(vec)]), name="ag_small")[0]
    wvec = _flat(jnp.concatenate([W[n][l].reshape(-1) for l in range(depth) for n in SMALL] + [jnp.zeros((1,), F32)]))
    mvec = _flat(jnp.concatenate([M1[n][l].reshape(-1) for l in range(depth) for n in SMALL] + [jnp.zeros((1,), F32)]))
    vvec = _flat(jnp.concatenate([V2[n][l].reshape(-1) for l in range(depth) for n in SMALL] + [jnp.ones((1,), F32)]))
    res = adam_update(wvec, gathered_small, mvec, vvec, name="adam_small")
    res = [r.reshape(-1)[:n_small] for r in res]
    off = 0
    for l in range(depth):
        for n in SMALL:
            sz = W[n].shape[1]
            for k, r in zip(("g", "d", "m", "v"), res):
                small_out[k][n][l] = r[off:off + sz]
            off += sz
    loss = res[0][off]
    for k in outs:
        for n in SMALL:
            outs[k][n] = jnp.stack(small_out[k][n], axis=0)

    return (loss, dx[None], *[outs["g"][n] for n in WEIGHTS], *[outs["d"][n] for n in WEIGHTS],
            *[outs["m"][n] for n in WEIGHTS], *[outs["v"][n] for n in WEIGHTS])
```

```python
import jax
import jax.numpy as jnp
from jax import lax
from jax.experimental import pallas as pl
from jax.experimental.pallas import tpu as pltpu

F32 = jnp.float32
BF16 = jnp.bfloat16

D_MODEL = 2048
D_FF = 5632
ATT_HEADS = 8
ATT_KV_HEADS = 2
ATT_GROUP = ATT_HEADS // ATT_KV_HEADS
ATT_WIDTH = 1024
HEAD_DIM = 128
KV_WIDTH = 256
WINDOW = 128
BLK = 128
M_WIDTH = 1024
M_HEADS = 4
M_HEAD_DIM = 256
CONV_WIDTH = 5
EPS = 1e-6
ROPE_THETA = 10000.0
IN_WIDTH = 5648
N_GATES = 16
N_DEV = 8

ADAM_LR = 0.001
ADAM_B1 = 0.9
ADAM_B2 = 0.999
ADAM_EPS = 1e-08
ADAM_WD = 0.01
ADAM_STEP = 10

P_QK = 0
P_QA = 2048
P_VM = 3072
P_OM = 4096
P_KA = 5120
P_VA = 5376
P_G = 5632
P_WIDTH = 6144

LANES = 128
V7X_VMEM_LIMIT = 48 * 1024 * 1024
NEG = -1e30
MESH = pl.DeviceIdType.MESH
ANY = pl.BlockSpec(memory_space=pl.ANY)


K_TILES = (2048, 1024, 512, 256, 128)


def _tile(n, cands=(1024, 512, 256, 128)):
    for c in cands:
        if n % c == 0:
            return c
    return n


class Comm:
    def __init__(self, ins, outs, sems, phases):
        self.ins, self.outs, self.sems, self.phases = list(ins), list(outs), list(sems), list(phases)


def run_comm(comm, *, name):
    n_in, n_out = len(comm.ins), len(comm.outs)

    def body(*refs):
        ins, outs, sems = refs[:n_in], refs[n_in:n_in + n_out], refs[n_in + n_out:]
        for phase in comm.phases:
            phase(ins, outs, sems)

    return pl.pallas_call(body, name=name, out_shape=comm.outs, in_specs=[ANY] * n_in, out_specs=[ANY] * n_out,
                          scratch_shapes=comm.sems,
                          compiler_params=pltpu.CompilerParams(has_side_effects=True))(*comm.ins)


def _pcall(body, *, name, out_shape, in_specs, out_specs, grid=(), scratch=(), sem=None, comm=None):
    if comm is None:
        return pl.pallas_call(
            body, name=name, out_shape=out_shape, in_specs=in_specs, out_specs=out_specs, grid=grid,
            scratch_shapes=list(scratch),
            compiler_params=pltpu.CompilerParams(dimension_semantics=sem, vmem_limit_bytes=V7X_VMEM_LIMIT))
    multi = isinstance(out_shape, (tuple, list))
    outs = list(out_shape) if multi else [out_shape]
    ospecs = list(out_specs) if multi else [out_specs]
    n_in, n_out, n_scr = len(in_specs), len(outs), len(scratch)
    nci, nco = len(comm.ins), len(comm.outs)
    steps = 1
    for g in grid:
        steps *= g
    n_ph = len(comm.phases)
    at = [0, steps - 1] if n_ph == 2 else [0, (3 * steps) // 4, steps - 1]

    def wrapped(*refs):
        ins, cins = refs[:n_in], refs[n_in:n_in + nci]
        o0 = n_in + nci
        res, couts = refs[o0:o0 + n_out], refs[o0 + n_out:o0 + n_out + nco]
        s0 = o0 + n_out + nco
        scr, csems = refs[s0:s0 + n_scr], refs[s0 + n_scr:]
        lin = 0
        for k, g in enumerate(grid):
            lin = lin * g + pl.program_id(k)

        @pl.when(lin == at[0])
        def _():
            comm.phases[0](cins, couts, csems)

        body(*ins, *res, *scr)
        for p in range(1, n_ph):
            @pl.when(lin == at[p])
            def _(p=p):
                comm.phases[p](cins, couts, csems)

    call = pl.pallas_call(
        wrapped, name=name, out_shape=outs + comm.outs, in_specs=list(in_specs) + [ANY] * nci,
        out_specs=ospecs + [ANY] * nco, grid=grid, scratch_shapes=list(scratch) + comm.sems,
        compiler_params=pltpu.CompilerParams(dimension_semantics=("arbitrary",) * len(grid),
                                             vmem_limit_bytes=V7X_VMEM_LIMIT, has_side_effects=True))

    def run(*args):
        got = list(call(*args, *comm.ins))
        return (tuple(got[:n_out]) if multi else got[0]), got[n_out:]

    return run


def _dot(a, b):
    return jnp.dot(a, b, preferred_element_type=F32)


def _dot_nt(a, b):
    return lax.dot_general(a, b, (((1,), (1,)), ((), ())), preferred_element_type=F32)


def _dot_tn(a, b):
    return lax.dot_general(a, b, (((0,), (0,)), ((), ())), preferred_element_type=F32)


def _sigmoid(x):
    return 1.0 / (1.0 + jnp.exp(-x))


def mm_nn(a, b, *, name, out_dtype=F32, comm=None):
    M, K = a.shape
    N = b.shape[1]
    tm, tk, tn = _tile(M), _tile(K, K_TILES), _tile(N)
    nk = K // tk

    def body(a_ref, b_ref, o_ref, acc):
        k = pl.program_id(2)

        @pl.when(k == 0)
        def _():
            acc[...] = jnp.zeros_like(acc)

        acc[...] += _dot(a_ref[...], b_ref[...])

        @pl.when(k == nk - 1)
        def _():
            o_ref[...] = acc[...].astype(o_ref.dtype)

    return _pcall(body, name=name, out_shape=jax.ShapeDtypeStruct((M, N), out_dtype),
                  in_specs=[pl.BlockSpec((tm, tk), lambda i, j, k: (i, k)),
                            pl.BlockSpec((tk, tn), lambda i, j, k: (k, j))],
                  out_specs=pl.BlockSpec((tm, tn), lambda i, j, k: (i, j)), grid=(M // tm, N // tn, nk),
                  scratch=[pltpu.VMEM((tm, tn), F32)], sem=("parallel", "parallel", "arbitrary"), comm=comm)(a, b)


def mm_tn(a, g, *, name, owner_rows=None, out_dtype=F32, comm=None):
    M, K = a.shape
    N = g.shape[1]
    tm, tk, tn = _tile(M), _tile(K), _tile(N)
    nm = M // tm
    per_tile = 1 if owner_rows is None else tk // owner_rows

    def body(a_ref, g_ref, o_ref, acc):
        m = pl.program_id(2)

        @pl.when(m == 0)
        def _():
            acc[...] = jnp.zeros_like(acc)

        acc[...] += _dot_tn(a_ref[...], g_ref[...])

        @pl.when(m == nm - 1)
        def _():
            if owner_rows is None:
                o_ref[...] = acc[...].astype(o_ref.dtype)
            else:
                for d in range(per_tile):
                    o_ref[d % 2, d // 2] = acc[d * owner_rows:(d + 1) * owner_rows, :].astype(o_ref.dtype)

    if owner_rows is None:
        out_shape = jax.ShapeDtypeStruct((K, N), out_dtype)
        out_spec = pl.BlockSpec((tk, tn), lambda i, j, m: (i, j))
    else:
        assert per_tile % 2 == 0 and K == N_DEV * owner_rows
        out_shape = jax.ShapeDtypeStruct((2, N_DEV // 2, owner_rows, N), out_dtype)
        out_spec = pl.BlockSpec((2, per_tile // 2, owner_rows, tn), lambda i, j, m: (0, i, 0, j))
    return _pcall(body, name=name, out_shape=out_shape,
                  in_specs=[pl.BlockSpec((tm, tk), lambda i, j, m: (m, i)),
                            pl.BlockSpec((tm, tn), lambda i, j, m: (m, j))],
                  out_specs=out_spec, grid=(K // tk, N // tn, nm), scratch=[pltpu.VMEM((tk, tn), F32)],
                  sem=("parallel", "parallel", "arbitrary"), comm=comm)(a, g)


def mm_nt(a, b, *, name, out_dtype=F32):
    M, K = a.shape
    N = b.shape[0]
    tm, tn, tk = _tile(M), _tile(N), _tile(K, K_TILES)
    nk = K // tk

    def body(a_ref, b_ref, o_ref, acc):
        k = pl.program_id(2)

        @pl.when(k == 0)
        def _():
            acc[...] = jnp.zeros_like(acc)

        acc[...] += _dot_nt(a_ref[...], b_ref[...])

        @pl.when(k == nk - 1)
        def _():
            o_ref[...] = acc[...].astype(o_ref.dtype)

    return _pcall(body, name=name, out_shape=jax.ShapeDtypeStruct((M, N), out_dtype),
                  in_specs=[pl.BlockSpec((tm, tk), lambda i, j, k: (i, k)),
                            pl.BlockSpec((tn, tk), lambda i, j, k: (j, k))],
                  out_specs=pl.BlockSpec((tm, tn), lambda i, j, k: (i, j)), grid=(M // tm, N // tn, nk),
                  scratch=[pltpu.VMEM((tm, tn), F32)], sem=("parallel", "parallel", "arbitrary"))(a, b)


FS = D_FF // N_DEV
FSP = 768


def ffn_gu(xn, wg8, wu8, *, name, comm=None):
    S, D = xn.shape
    tm = _tile(S)

    def body(x_ref, wg_ref, wu_ref, hg_ref, hu_ref, act_ref):
        xv = x_ref[...]
        hg = _dot_nt(xv, wg_ref[...])
        hu = _dot_nt(xv, wu_ref[...])
        hg_ref[...] = hg.astype(BF16)
        hu_ref[...] = hu.astype(BF16)
        act_ref[...] = (hg * _sigmoid(hg) * hu).astype(BF16)

    wspec = pl.BlockSpec((None, FSP, D), lambda i, j: (j, 0, 0))
    ospec = pl.BlockSpec((None, tm, FSP), lambda i, j: (j, i, 0))
    shp = jax.ShapeDtypeStruct((N_DEV, S, FSP), BF16)
    return _pcall(body, name=name, out_shape=(shp, shp, shp),
                  in_specs=[pl.BlockSpec((tm, D), lambda i, j: (i, 0)), wspec, wspec],
                  out_specs=(ospec, ospec, ospec), grid=(S // tm, N_DEV), sem=("parallel", "arbitrary"),
                  comm=comm)(xn, wg8, wu8)


def ffn_down(act8, wd8, *, name, comm=None):
    _, S, _ = act8.shape
    D = wd8.shape[2]
    tm, tn = _tile(S), D

    def body(a_ref, w_ref, o_ref, acc):
        j = pl.program_id(2)

        @pl.when(j == 0)
        def _():
            acc[...] = jnp.zeros_like(acc)

        acc[...] += _dot(a_ref[...], w_ref[...])

        @pl.when(j == N_DEV - 1)
        def _():
            o_ref[...] = acc[...].astype(o_ref.dtype)

    return _pcall(body, name=name, out_shape=jax.ShapeDtypeStruct((S, D), BF16),
                  in_specs=[pl.BlockSpec((None, tm, FSP), lambda i, n, j: (j, i, 0)),
                            pl.BlockSpec((None, FSP, tn), lambda i, n, j: (j, 0, n))],
                  out_specs=pl.BlockSpec((tm, tn), lambda i, n, j: (i, n)),
                  grid=(S // tm, D // tn, N_DEV), scratch=[pltpu.VMEM((tm, tn), F32)],
                  sem=("parallel", "parallel", "arbitrary"), comm=comm)(act8, wd8)


def ffn_dact(df, wd8, hg8, hu8, *, name):
    S, D = df.shape
    tm = _tile(S)

    def body(d_ref, w_ref, hg_ref, hu_ref, dg_ref, du_ref):
        da = _dot_nt(d_ref[...], w_ref[...])
        hg = hg_ref[...].astype(F32)
        hu = hu_ref[...].astype(F32)
        sg = _sigmoid(hg)
        dg_ref[...] = (da * hu * (sg * (1.0 + hg * (1.0 - sg)))).astype(BF16)
        du_ref[...] = (da * hg * sg).astype(BF16)

    blk = pl.BlockSpec((None, tm, FSP), lambda i, j: (j, i, 0))
    shp = jax.ShapeDtypeStruct((N_DEV, S, FSP), BF16)
    return _pcall(body, name=name, out_shape=(shp, shp),
                  in_specs=[pl.BlockSpec((tm, D), lambda i, j: (i, 0)),
                            pl.BlockSpec((None, FSP, D), lambda i, j: (j, 0, 0)), blk, blk],
                  out_specs=(blk, blk), grid=(S // tm, N_DEV), sem=("parallel", "arbitrary"))(df, wd8, hg8, hu8)


def ffn_dwd(act8, df, *, name, out_dtype, comm=None):
    _, S, _ = act8.shape
    D = df.shape[1]
    tm, tn = _tile(S), D
    nm = S // tm

    def body(a_ref, d_ref, o_ref, acc):
        m = pl.program_id(2)

        @pl.when(m == 0)
        def _():
            acc[...] = jnp.zeros_like(acc)

        acc[...] += _dot_tn(a_ref[...], d_ref[...])

        @pl.when(m == nm - 1)
        def _():
            o_ref[...] = acc[0:FS, :].astype(o_ref.dtype)

    return _pcall(body, name=name, out_shape=jax.ShapeDtypeStruct((2, N_DEV // 2, FS, D), out_dtype),
                  in_specs=[pl.BlockSpec((None, tm, FSP), lambda j, n, m: (j, m, 0)),
                            pl.BlockSpec((tm, tn), lambda j, n, m: (m, n))],
                  out_specs=pl.BlockSpec((None, None, FS, tn), lambda j, n, m: (j % 2, j // 2, 0, n)),
                  grid=(N_DEV, D // tn, nm), scratch=[pltpu.VMEM((FSP, tn), F32)],
                  sem=("parallel", "parallel", "arbitrary"), comm=comm)(act8, df)


def ffn_dwgu(xn, dg8, du8, *, name, out_dtype, comm=None):
    S, D = xn.shape
    tm, tk = _tile(S), _tile(D)
    nm = S // tm

    def body(x_ref, dg_ref, du_ref, og_ref, ou_ref, accg, accu):
        m = pl.program_id(2)

        @pl.when(m == 0)
        def _():
            accg[...] = jnp.zeros_like(accg)
            accu[...] = jnp.zeros_like(accu)

        xv = x_ref[...]
        accg[...] += _dot_tn(dg_ref[...], xv)
        accu[...] += _dot_tn(du_ref[...], xv)

        @pl.when(m == nm - 1)
        def _():
            og_ref[...] = accg[0:FS, :].astype(og_ref.dtype)
            ou_ref[...] = accu[0:FS, :].astype(ou_ref.dtype)

    blk = pl.BlockSpec((None, tm, FSP), lambda j, k, m: (j, m, 0))
    ospec = pl.BlockSpec((None, None, FS, tk), lambda j, k, m: (j % 2, j // 2, 0, k))
    shp = jax.ShapeDtypeStruct((2, N_DEV // 2, FS, D), out_dtype)
    return _pcall(body, name=name, out_shape=(shp, shp),
                  in_specs=[pl.BlockSpec((tm, tk), lambda j, k, m: (m, k)), blk, blk],
                  out_specs=(ospec, ospec), grid=(N_DEV, D // tk, nm),
                  scratch=[pltpu.VMEM((FSP, tk), F32), pltpu.VMEM((FSP, tk), F32)],
                  sem=("parallel", "parallel", "arbitrary"), comm=comm)(xn, dg8, du8)


def ffn_dxn(dg8, du8, wg8, wu8, *, name, comm=None):
    _, S, _ = dg8.shape
    D = wg8.shape[2]
    tm, tn = _tile(S), _tile(D)

    def body(dg_ref, du_ref, wg_ref, wu_ref, o_ref, acc):
        j = pl.program_id(2)

        @pl.when(j == 0)
        def _():
            acc[...] = jnp.zeros_like(acc)

        acc[...] += _dot(dg_ref[...], wg_ref[...]) + _dot(du_ref[...], wu_ref[...])

        @pl.when(j == N_DEV - 1)
        def _():
            o_ref[...] = acc[...].astype(o_ref.dtype)

    blk = pl.BlockSpec((None, tm, FSP), lambda i, n, j: (j, i, 0))
    wspec = pl.BlockSpec((None, FSP, tn), lambda i, n, j: (j, 0, n))
    return _pcall(body, name=name, out_shape=jax.ShapeDtypeStruct((S, D), BF16),
                  in_specs=[blk, blk, wspec, wspec], out_specs=pl.BlockSpec((tm, tn), lambda i, n, j: (i, n)),
                  grid=(S // tm, D // tn, N_DEV), scratch=[pltpu.VMEM((tm, tn), F32)],
                  sem=("parallel", "parallel", "arbitrary"), comm=comm)(dg8, du8, wg8, wu8)


def norm_fwd(x, g, *, name, scale=1.0, resid=None, out_dtype=F32):
    S, D = x.shape
    tm = _tile(S, (512, 256, 128))

    def body(*refs):
        if resid is None:
            x_ref, g_ref, o_ref = refs
        else:
            x_ref, g_ref, r_ref, o_ref = refs
        xv = x_ref[...].astype(F32)
        r = lax.rsqrt(jnp.mean(xv * xv, axis=-1, keepdims=True) + EPS)
        y = (xv * r) * g_ref[...]
        if scale != 1.0:
            y = y * scale
        if resid is not None:
            y = y + r_ref[...]
        o_ref[...] = y.astype(o_ref.dtype)

    row = pl.BlockSpec((tm, D), lambda i: (i, 0))
    in_specs = [row, pl.BlockSpec((1, D), lambda i: (0, 0))]
    args = [x, g.reshape(1, D)]
    if resid is not None:
        in_specs.append(row)
        args.append(resid)
    return _pcall(body, name=name, out_shape=jax.ShapeDtypeStruct((S, D), out_dtype), in_specs=in_specs,
                  out_specs=row, grid=(S // tm,), sem=("parallel",))(*args)


def norm_bwd(dy, x, g, *, name, scale=1.0, resid=None, out_dtype=F32):
    S, D = x.shape
    tm = _tile(S, (512, 256, 128))

    def body(*refs):
        if resid is None:
            dy_ref, x_ref, g_ref, dx_ref, dg_ref = refs
        else:
            dy_ref, x_ref, g_ref, r_ref, dx_ref, dg_ref = refs

        @pl.when(pl.program_id(0) == 0)
        def _():
            dg_ref[...] = jnp.zeros_like(dg_ref)

        xv = x_ref[...].astype(F32)
        d = dy_ref[...].astype(F32)
        if scale != 1.0:
            d = d * scale
        r = lax.rsqrt(jnp.mean(xv * xv, axis=-1, keepdims=True) + EPS)
        xh = xv * r
        dg_ref[...] += jnp.sum(d * xh, axis=0, keepdims=True)
        dxh = d * g_ref[...]
        dx = r * (dxh - xh * jnp.mean(dxh * xh, axis=-1, keepdims=True))
        if resid is not None:
            dx = dx + r_ref[...]
        dx_ref[...] = dx.astype(dx_ref.dtype)

    row = pl.BlockSpec((tm, D), lambda i: (i, 0))
    vec = pl.BlockSpec((1, D), lambda i: (0, 0))
    in_specs = [row, row, vec]
    args = [dy, x, g.reshape(1, D)]
    if resid is not None:
        in_specs.append(row)
        args.append(resid)
    return _pcall(body, name=name,
                  out_shape=(jax.ShapeDtypeStruct((S, D), out_dtype), jax.ShapeDtypeStruct((1, D), F32)),
                  in_specs=in_specs, out_specs=(row, vec), grid=(S // tm,), sem=("arbitrary",))(*args)


def loss_fwd_bwd(y, target, *, name):
    S, D = y.shape
    tm = _tile(S, (512, 256, 128))

    def body(y_ref, t_ref, dy_ref, l_ref):
        @pl.when(pl.program_id(0) == 0)
        def _():
            l_ref[...] = jnp.zeros_like(l_ref)

        e = y_ref[...] - t_ref[...]
        dy_ref[...] = e * (1.0 / D)
        l_ref[...] += jnp.sum(jnp.sum(e * e, axis=1, keepdims=True), axis=0, keepdims=True) * (0.5 / D)

    row = pl.BlockSpec((tm, D), lambda i: (i, 0))
    one = pl.BlockSpec((1, 1), lambda i: (0, 0))
    return _pcall(body, name=name,
                  out_shape=(jax.ShapeDtypeStruct((S, D), F32), jax.ShapeDtypeStruct((1, 1), F32)),
                  in_specs=[row, row], out_specs=(row, one), grid=(S // tm,), sem=("arbitrary",))(y, target)


def _rope_tables(S):
    half = HEAD_DIM // 2
    inv_freq = ROPE_THETA ** (-jnp.arange(half, dtype=F32) / half)
    ang = jnp.arange(S, dtype=F32)[:, None] * inv_freq[None, :]
    cos, sin = jnp.cos(ang), jnp.sin(ang)
    return jnp.concatenate([cos, cos], axis=1), jnp.concatenate([-sin, sin], axis=1)


def _rope(x, cos2, sin2):
    return x * cos2 + pltpu.roll(x, HEAD_DIM // 2, 1) * sin2


def _unrope(d, cos2, sin2):
    return d * cos2 + pltpu.roll(d * sin2, HEAD_DIM // 2, 1)


def _nbr_specs(width, col, nb):
    return [pl.BlockSpec((BLK, width), lambda n, c=col: (jnp.maximum(n - 1, 0), c)),
            pl.BlockSpec((BLK, width), lambda n, c=col: (n, c)),
            pl.BlockSpec((BLK, width), lambda n, c=col: (jnp.minimum(n + 1, nb - 1), c))]


def attn_fwd(proj, cos2, sin2, sink, *, name, comm=None):
    S = proj.shape[0]
    nb = S // BLK
    scale = HEAD_DIM ** -0.5

    def body(sink_ref, q_ref, k0, k1, k2, v0, v1, v2, c0, c1, c2, s0, s1, s2, o_ref, lse_ref):
        n = pl.program_id(0)
        cosk = jnp.concatenate([c0[...], c1[...], c2[...]], axis=0)
        sink_ = jnp.concatenate([s0[...], s1[...], s2[...]], axis=0)
        kall = jnp.concatenate([k0[...], k1[...], k2[...]], axis=0)
        vall = jnp.concatenate([v0[...], v1[...], v2[...]], axis=0)
        rows = lax.broadcasted_iota(jnp.int32, (BLK, 3 * BLK), 0)
        cols = lax.broadcasted_iota(jnp.int32, (BLK, 3 * BLK), 1)
        kpos = (n - 1) * BLK + cols
        valid = (jnp.abs(cols - BLK - rows) <= WINDOW) & (kpos >= 0) & (kpos < S)
        valid = jnp.concatenate([valid] * ATT_GROUP, axis=0)
        lane = lax.broadcasted_iota(jnp.int32, (BLK, LANES), 1)
        lse_tile = jnp.zeros((BLK, LANES), F32)
        for hk in range(ATT_KV_HEADS):
            ks = slice(hk * HEAD_DIM, (hk + 1) * HEAD_DIM)
            kh = _rope(kall[:, ks], cosk, sink_).astype(BF16)
            vh = vall[:, ks].astype(BF16)
            qs = []
            for g in range(ATT_GROUP):
                hq = hk * ATT_GROUP + g
                qs.append(_rope(q_ref[:, hq * HEAD_DIM:(hq + 1) * HEAD_DIM], c1[...], s1[...]))
            qh = jnp.concatenate(qs, axis=0).astype(BF16)
            s = _dot_nt(qh, kh) * scale
            s = jnp.where(valid, s, NEG)
            snk = jnp.concatenate(
                [jnp.full((BLK, 1), sink_ref[hk * ATT_GROUP + g], F32) for g in range(ATT_GROUP)], axis=0)
            m = jnp.maximum(jnp.max(s, axis=1, keepdims=True), snk)
            p = jnp.exp(s - m)
            l = jnp.sum(p, axis=1, keepdims=True) + jnp.exp(snk - m)
            o = _dot(p.astype(BF16), vh) * (1.0 / l)
            lse = m + jnp.log(l)
            for g in range(ATT_GROUP):
                hq = hk * ATT_GROUP + g
                o_ref[:, hq * HEAD_DIM:(hq + 1) * HEAD_DIM] = o[g * BLK:(g + 1) * BLK].astype(o_ref.dtype)
                lse_tile = lse_tile + jnp.where(lane == hq, lse[g * BLK:(g + 1) * BLK], 0.0)
        lse_ref[...] = lse_tile

    in_specs = ([pl.BlockSpec(memory_space=pltpu.SMEM),
                 pl.BlockSpec((BLK, ATT_WIDTH), lambda n: (n, P_QA // ATT_WIDTH))]
                + _nbr_specs(KV_WIDTH, P_KA // KV_WIDTH, nb) + _nbr_specs(KV_WIDTH, P_VA // KV_WIDTH, nb)
                + _nbr_specs(HEAD_DIM, 0, nb) + _nbr_specs(HEAD_DIM, 0, nb))
    return _pcall(body, name=name,
                  out_shape=(jax.ShapeDtypeStruct((S, ATT_WIDTH), BF16), jax.ShapeDtypeStruct((S, LANES), F32)),
                  in_specs=in_specs,
                  out_specs=(pl.BlockSpec((BLK, ATT_WIDTH), lambda n: (n, 0)),
                             pl.BlockSpec((BLK, LANES), lambda n: (n, 0))),
                  grid=(nb,), sem=("parallel",), comm=comm)(sink, proj, proj, proj, proj, proj, proj, proj,
                                                            cos2, cos2, cos2, sin2, sin2, sin2)


def attn_bwd(proj, y, dy, lse, cos2, sin2, sink, *, name):
    S = proj.shape[0]
    nb = S // BLK
    scale = HEAD_DIM ** -0.5

    def body(sink_ref, q_ref, k0, k1, k2, v0, v1, v2, o_ref, d_ref, l_ref, c0, c1, c2, s0, s1, s2,
             dq_ref, dk_ref, dv_ref, dsink_ref, dk_acc, dv_acc):
        n = pl.program_id(0)

        @pl.when(n == 0)
        def _():
            dsink_ref[...] = jnp.zeros_like(dsink_ref)
            dk_acc[...] = jnp.zeros_like(dk_acc)
            dv_acc[...] = jnp.zeros_like(dv_acc)

        @pl.when(n < nb)
        def _():
            cosk = jnp.concatenate([c0[...], c1[...], c2[...]], axis=0)
            sink_ = jnp.concatenate([s0[...], s1[...], s2[...]], axis=0)
            kall = jnp.concatenate([k0[...], k1[...], k2[...]], axis=0)
            vall = jnp.concatenate([v0[...], v1[...], v2[...]], axis=0)
            lane = lax.broadcasted_iota(jnp.int32, (1, LANES), 1)
            rows = lax.broadcasted_iota(jnp.int32, (BLK, 3 * BLK), 0)
            cols = lax.broadcasted_iota(jnp.int32, (BLK, 3 * BLK), 1)
            kpos = (n - 1) * BLK + cols
            valid = (jnp.abs(cols - BLK - rows) <= WINDOW) & (kpos >= 0) & (kpos < S)
            valid = jnp.concatenate([valid] * ATT_GROUP, axis=0)
            dsink_acc = jnp.zeros((1, LANES), F32)
            for hk in range(ATT_KV_HEADS):
                ks = slice(hk * HEAD_DIM, (hk + 1) * HEAD_DIM)
                kh = _rope(kall[:, ks], cosk, sink_).astype(BF16)
                vh = vall[:, ks].astype(BF16)
                qs, dos, lses, deltas = [], [], [], []
                for g in range(ATT_GROUP):
                    hq = hk * ATT_GROUP + g
                    hs = slice(hq * HEAD_DIM, (hq + 1) * HEAD_DIM)
                    qs.append(_rope(q_ref[:, hs], c1[...], s1[...]))
                    do = d_ref[:, hs]
                    dos.append(do)
                    lses.append(l_ref[:, hq:hq + 1])
                    deltas.append(jnp.sum(do * o_ref[:, hs].astype(F32), axis=1, keepdims=True))
                qh = jnp.concatenate(qs, axis=0).astype(BF16)
                doh = jnp.concatenate(dos, axis=0).astype(BF16)
                lseh = jnp.concatenate(lses, axis=0)
                delh = jnp.concatenate(deltas, axis=0)
                s = jnp.where(valid, _dot_nt(qh, kh) * scale, NEG)
                p = jnp.exp(s - lseh)
                dp = _dot_nt(doh, vh)
                ds = (p * (dp - delh)).astype(BF16)
                dq = _dot(ds, kh) * scale
                dk_acc[hk] += _dot_tn(ds, qh) * scale
                dv_acc[hk] += _dot_tn(p.astype(BF16), doh)
                for g in range(ATT_GROUP):
                    hq = hk * ATT_GROUP + g
                    dq_ref[:, hq * HEAD_DIM:(hq + 1) * HEAD_DIM] = _unrope(dq[g * BLK:(g + 1) * BLK], c1[...], s1[...])
                    psink = jnp.exp(sink_ref[hq] - lses[g])
                    dsink_acc = dsink_acc + jnp.where(lane == hq, -jnp.sum(psink * deltas[g]), 0.0)
            dsink_ref[...] += dsink_acc

        c_out = jnp.where(n < nb, c0[...], c1[...])
        s_out = jnp.where(n < nb, s0[...], s1[...])
        for hk in range(ATT_KV_HEADS):
            ks = slice(hk * HEAD_DIM, (hk + 1) * HEAD_DIM)
            dk_ref[:, ks] = _unrope(dk_acc[hk, 0:BLK, :], c_out, s_out)
            dv_ref[:, ks] = dv_acc[hk, 0:BLK, :]
            for acc in (dk_acc, dv_acc):
                acc[hk, 0:BLK, :] = acc[hk, BLK:2 * BLK, :]
                acc[hk, BLK:2 * BLK, :] = acc[hk, 2 * BLK:3 * BLK, :]
                acc[hk, 2 * BLK:3 * BLK, :] = jnp.zeros((BLK, HEAD_DIM), F32)

    own = lambda n: jnp.minimum(n, nb - 1)
    done = lambda n: jnp.maximum(n - 1, 0)

    def nbr(width, col):
        return [pl.BlockSpec((BLK, width), lambda n, c=col: (jnp.maximum(own(n) - 1, 0), c)),
                pl.BlockSpec((BLK, width), lambda n, c=col: (own(n), c)),
                pl.BlockSpec((BLK, width), lambda n, c=col: (jnp.minimum(own(n) + 1, nb - 1), c))]

    in_specs = ([pl.BlockSpec(memory_space=pltpu.SMEM),
                 pl.BlockSpec((BLK, ATT_WIDTH), lambda n: (own(n), P_QA // ATT_WIDTH))]
                + nbr(KV_WIDTH, P_KA // KV_WIDTH) + nbr(KV_WIDTH, P_VA // KV_WIDTH)
                + [pl.BlockSpec((BLK, ATT_WIDTH), lambda n: (own(n), 0)),
                   pl.BlockSpec((BLK, ATT_WIDTH), lambda n: (own(n), 0)),
                   pl.BlockSpec((BLK, LANES), lambda n: (own(n), 0))]
                + nbr(HEAD_DIM, 0) + nbr(HEAD_DIM, 0))
    args = [sink, proj] + [proj] * 6 + [y, dy, lse] + [cos2] * 3 + [sin2] * 3
    return _pcall(body, name=name,
                  out_shape=(jax.ShapeDtypeStruct((S, ATT_WIDTH), F32), jax.ShapeDtypeStruct((S, KV_WIDTH), F32),
                             jax.ShapeDtypeStruct((S, KV_WIDTH), F32), jax.ShapeDtypeStruct((1, LANES), F32)),
                  in_specs=in_specs,
                  out_specs=(pl.BlockSpec((BLK, ATT_WIDTH), lambda n: (own(n), 0)),
                             pl.BlockSpec((BLK, KV_WIDTH), lambda n: (done(n), 0)),
                             pl.BlockSpec((BLK, KV_WIDTH), lambda n: (done(n), 0)),
                             pl.BlockSpec((1, LANES), lambda n: (0, 0))),
                  grid=(nb + 1,),
                  scratch=[pltpu.VMEM((ATT_KV_HEADS, 3 * BLK, HEAD_DIM), F32),
                           pltpu.VMEM((ATT_KV_HEADS, 3 * BLK, HEAD_DIM), F32)],
                  sem=("arbitrary",))(*args)


CONV_HALO = 8
CONV_COLS = 512


def _halo_specs(tm, nrow, col_of):
    hb = tm // CONV_HALO
    return [pl.BlockSpec((CONV_HALO, CONV_COLS), lambda i, j: (jnp.maximum(i * hb - 1, 0), col_of(j))),
            pl.BlockSpec((tm, CONV_COLS), lambda i, j: (i, col_of(j))),
            pl.BlockSpec((CONV_HALO, CONV_COLS),
                         lambda i, j: (jnp.minimum((i + 1) * hb, nrow * hb - 1), col_of(j)))]


def _with_halo(prev, cur, nxt, i, nrow):
    p = jnp.where(i > 0, prev[...], 0.0)
    q = jnp.where(i < nrow - 1, nxt[...], 0.0)
    return jnp.concatenate([p, cur[...], q], axis=0)


def _conv_taps(xt, w_ref, tm):
    n = xt.shape[0]
    acc = jnp.zeros_like(xt)
    for j in range(CONV_WIDTH):
        sh = (CONV_WIDTH // 2 - j) % n
        xs = xt if sh == 0 else pltpu.roll(xt, sh, 0)
        acc = acc + xs * w_ref[j:j + 1, :]
    return acc


def conv_fwd(proj, conv_w, *, name):
    S = proj.shape[0]
    tm = _tile(S, (512, 256, 128))
    nrow = S // tm

    def body(xp, xc, xn, w_ref, o_ref):
        i = pl.program_id(0)
        xt = _with_halo(xp, xc, xn, i, nrow)
        pre = _conv_taps(xt, w_ref, tm)[CONV_HALO:CONV_HALO + tm]
        o_ref[...] = pre * _sigmoid(pre)

    return _pcall(body, name=name, out_shape=jax.ShapeDtypeStruct((S, 2 * M_WIDTH), F32),
                  in_specs=_halo_specs(tm, nrow, lambda j: P_QK // CONV_COLS + j)
                  + [pl.BlockSpec((CONV_HALO, CONV_COLS), lambda i, j: (0, j))],
                  out_specs=pl.BlockSpec((tm, CONV_COLS), lambda i, j: (i, j)),
                  grid=(nrow, 2 * M_WIDTH // CONV_COLS), sem=("parallel", "parallel"))(proj, proj, proj, conv_w)


def conv_bwd(proj, conv_w, da, db, *, name):
    S = proj.shape[0]
    tm = _tile(S, (512, 256, 128))
    nrow = S // tm

    def body(xp, xc, xn, ap, ac, an, bp, bc, bn, w_ref, dx_ref, dw_ref):
        i = pl.program_id(1)

        @pl.when(i == 0)
        def _():
            dw_ref[...] = jnp.zeros_like(dw_ref)

        xt = _with_halo(xp, xc, xn, i, nrow)
        dt = _with_halo(ap, ac, an, i, nrow) + _with_halo(bp, bc, bn, i, nrow)
        pre = _conv_taps(xt, w_ref, tm)
        sg = _sigmoid(pre)
        dpre = dt * (sg * (1.0 + pre * (1.0 - sg)))
        n = xt.shape[0]
        ridx = lax.broadcasted_iota(jnp.int32, (n, 1), 0)
        dpre = jnp.where((ridx >= 2) & (ridx < n - 2), dpre, 0.0)
        dx = jnp.zeros_like(xt)
        own = (ridx >= CONV_HALO) & (ridx < CONV_HALO + tm)
        dpre_own = jnp.where(own, dpre, 0.0)
        dw_rows = []
        for j in range(CONV_WIDTH):
            sh = (j - CONV_WIDTH // 2) % n
            ds_ = dpre if sh == 0 else pltpu.roll(dpre, sh, 0)
            dx = dx + ds_ * w_ref[j:j + 1, :]
            shx = (CONV_WIDTH // 2 - j) % n
            xs = xt if shx == 0 else pltpu.roll(xt, shx, 0)
            dw_rows.append(jnp.sum(dpre_own * xs, axis=0, keepdims=True))
        dx_ref[...] = dx[CONV_HALO:CONV_HALO + tm]
        dw_rows.append(jnp.zeros((CONV_HALO - CONV_WIDTH, CONV_COLS), F32))
        dw_ref[...] += jnp.concatenate(dw_rows, axis=0)

    colq = lambda j: P_QK // CONV_COLS + j
    same = lambda j: j

    def swap(specs):
        return [pl.BlockSpec(s.block_shape, (lambda f: (lambda j, i: f(i, j)))(s.index_map)) for s in specs]

    in_specs = swap(_halo_specs(tm, nrow, colq) + _halo_specs(tm, nrow, same) + _halo_specs(tm, nrow, same)
                    + [pl.BlockSpec((CONV_HALO, CONV_COLS), lambda i, j: (0, j))])
    return _pcall(body, name=name,
                  out_shape=(jax.ShapeDtypeStruct((S, 2 * M_WIDTH), F32),
                             jax.ShapeDtypeStruct((CONV_HALO, 2 * M_WIDTH), F32)),
                  in_specs=in_specs,
                  out_specs=(pl.BlockSpec((tm, CONV_COLS), lambda j, i: (i, j)),
                             pl.BlockSpec((CONV_HALO, CONV_COLS), lambda j, i: (0, j))),
                  grid=(2 * M_WIDTH // CONV_COLS, nrow), sem=("parallel", "arbitrary"))(
                      proj, proj, proj, da, da, da, db, db, db, conv_w)


def _log_sigmoid(x):
    return jnp.minimum(x, 0.0) - jnp.log(1.0 + jnp.exp(-jnp.abs(x)))


def _scan_sum(x, axis, from_end):
    idx = lax.broadcasted_iota(jnp.int32, x.shape, axis)
    n = x.shape[axis]
    sh = 1
    while sh < n:
        if from_end:
            x = x + jnp.where(idx < n - sh, pltpu.roll(x, n - sh, axis), 0.0)
        else:
            x = x + jnp.where(idx >= sh, pltpu.roll(x, sh, axis), 0.0)
        sh *= 2
    return x


def gate_rows(proj, *, name):
    S = proj.shape[0]

    def body(x_ref, o_ref):
        o_ref[...] = x_ref[...].T[0:N_GATES, :]

    return _pcall(body, name=name, out_shape=jax.ShapeDtypeStruct((N_GATES, S), F32),
                  in_specs=[pl.BlockSpec((BLK, LANES), lambda c: (c, P_G // LANES))],
                  out_specs=pl.BlockSpec((N_GATES, BLK), lambda c: (0, c)), grid=(S // BLK,), sem=("parallel",))(proj)


def _gate_setup(gc_ref, gr_ref, bgc_ref, bgr_ref, reverse):
    gc = gc_ref[...] + bgc_ref[...]
    gr = gr_ref[...] + bgr_ref[...]
    bc = _scan_sum(_log_sigmoid(gc), 0, reverse)
    br = _scan_sum(_log_sigmoid(gr), 1, reverse)
    return gc, gr, bc, br


def _head_gates(gc, gr, bc, br, h, m_in, reverse, tri):
    io = (M_HEADS if reverse else 0) + h
    fo = (3 * M_HEADS if reverse else 2 * M_HEADS) + h
    last = 0 if reverse else BLK - 1
    b_col, b_row = bc[:, fo:fo + 1], br[fo:fo + 1, :]
    ig_col, ig_row = gc[:, io:io + 1], gr[io:io + 1, :]
    logd = jnp.where(tri, b_col - b_row + ig_row, NEG)
    m_t = jnp.maximum(b_col + m_in, jnp.max(logd, axis=1, keepdims=True))
    dm = jnp.exp(logd - m_t)
    gi = jnp.exp(b_col + m_in - m_t)
    b_last = b_row[:, last:last + 1]
    logw = b_last - b_row + ig_row
    m_new = jnp.maximum(b_last + m_in, jnp.max(logw, axis=1, keepdims=True))
    w_col = jnp.exp(b_last - b_col + ig_col - m_new)
    dec = jnp.exp(b_last + m_in - m_new)
    return io, fo, m_t, dm, gi, m_new, w_col, dec


def _tri_mask(reverse):
    rows = lax.broadcasted_iota(jnp.int32, (BLK, BLK), 0)
    cols = lax.broadcasted_iota(jnp.int32, (BLK, BLK), 1)
    return (cols >= rows) if reverse else (cols <= rows)


def mlstm_fwd(qk, proj, gates_r, bg_c, bg_r, *, reverse, name):
    S = qk.shape[0]
    nc = S // BLK
    kscale = M_HEAD_DIM ** -0.5
    cidx = (lambda c: nc - 1 - c) if reverse else (lambda c: c)

    def body(qk_ref, v_ref, gc_ref, gr_ref, bgc_ref, bgr_ref, h_ref, den_ref, cst_ref, nm_ref, c_sc, n_sc, m_sc):
        @pl.when(pl.program_id(0) == 0)
        def _():
            c_sc[...] = jnp.zeros_like(c_sc)
            n_sc[...] = jnp.zeros_like(n_sc)
            m_sc[...] = jnp.zeros_like(m_sc)

        gc, gr, bc, br = _gate_setup(gc_ref, gr_ref, bgc_ref, bgr_ref, reverse)
        tri = _tri_mask(reverse)
        lane = lax.broadcasted_iota(jnp.int32, (BLK, LANES), 1)
        den_tile = jnp.zeros((BLK, LANES), F32)
        for h in range(M_HEADS):
            cs = slice(h * M_HEAD_DIM, (h + 1) * M_HEAD_DIM)
            m_in = m_sc[h][:, 0:1]
            _, _, m_t, dm, gi, m_new, w_col, dec = _head_gates(gc, gr, bc, br, h, m_in, reverse, tri)
            q = qk_ref[:, cs]
            k = qk_ref[:, M_WIDTH + h * M_HEAD_DIM:M_WIDTH + (h + 1) * M_HEAD_DIM] * kscale
            v = v_ref[:, cs]
            c_in, n_in = c_sc[h], n_sc[h]
            cst_ref[h] = c_in
            nm_ref[h, 0:1, :] = n_in
            nm_ref[h, 1:2, :] = m_sc[h]
            qb, kb, vb = q.astype(BF16), k.astype(BF16), v.astype(BF16)
            s = _dot_nt(qb, kb) * dm
            num = _dot(s.astype(BF16), vb) + gi * _dot_nt(qb, c_in.astype(BF16))
            den = jnp.sum(s, axis=1, keepdims=True) + gi * jnp.sum(q * n_in, axis=1, keepdims=True)
            z = jnp.maximum(jnp.abs(den), jnp.exp(-m_t))
            h_ref[:, cs] = num * (1.0 / z)
            den_tile = den_tile + jnp.where(lane == h, den, 0.0)
            c_sc[h] = dec * c_in + _dot_tn((w_col * v).astype(BF16), kb)
            n_sc[h] = dec * n_in + jnp.sum(w_col * k, axis=0, keepdims=True)
            m_sc[h] = jnp.broadcast_to(m_new, (1, M_HEAD_DIM))
        den_ref[...] = den_tile

    return _pcall(
        body, name=name,
        out_shape=(jax.ShapeDtypeStruct((S, M_WIDTH), F32), jax.ShapeDtypeStruct((S, LANES), F32),
                   jax.ShapeDtypeStruct((nc, M_HEADS, M_HEAD_DIM, M_HEAD_DIM), F32),
                   jax.ShapeDtypeStruct((nc, M_HEADS, 2, M_HEAD_DIM), F32)),
        in_specs=[pl.BlockSpec((BLK, 2 * M_WIDTH), lambda c: (cidx(c), 0)),
                  pl.BlockSpec((BLK, M_WIDTH), lambda c: (cidx(c), P_VM // M_WIDTH)),
                  pl.BlockSpec((BLK, LANES), lambda c: (cidx(c), P_G // LANES)),
                  pl.BlockSpec((N_GATES, BLK), lambda c: (0, cidx(c))),
                  pl.BlockSpec((1, LANES), lambda c: (0, 0)),
                  pl.BlockSpec((N_GATES, 1), lambda c: (0, 0))],
        out_specs=(pl.BlockSpec((BLK, M_WIDTH), lambda c: (cidx(c), 0)),
                   pl.BlockSpec((BLK, LANES), lambda c: (cidx(c), 0)),
                   pl.BlockSpec((None, M_HEADS, M_HEAD_DIM, M_HEAD_DIM), lambda c: (cidx(c), 0, 0, 0)),
                   pl.BlockSpec((None, M_HEADS, 2, M_HEAD_DIM), lambda c: (cidx(c), 0, 0, 0))),
        grid=(nc,),
        scratch=[pltpu.VMEM((M_HEADS, M_HEAD_DIM, M_HEAD_DIM), F32), pltpu.VMEM((M_HEADS, 1, M_HEAD_DIM), F32),
                 pltpu.VMEM((M_HEADS, 1, M_HEAD_DIM), F32)],
        sem=("arbitrary",))(qk, proj, proj, gates_r, bg_c, bg_r)


def mlstm_bwd(qk, proj, gates_r, bg_c, bg_r, hdir, den, cst, nm, dh, *, reverse, name, comm=None):
    S = qk.shape[0]
    nc = S // BLK
    kscale = M_HEAD_DIM ** -0.5
    cidx = (lambda c: c) if reverse else (lambda c: nc - 1 - c)
    last = 0 if reverse else BLK - 1

    def body(qk_ref, v_ref, gc_ref, gr_ref, bgc_ref, bgr_ref, h_ref, den_ref, cst_ref, nm_ref, dh_ref,
             dqk_ref, dv_ref, dgc_ref, dgr_ref, dc_sc, dn_sc):
        @pl.when(pl.program_id(0) == 0)
        def _():
            dc_sc[...] = jnp.zeros_like(dc_sc)
            dn_sc[...] = jnp.zeros_like(dn_sc)

        gc, gr, bc, br = _gate_setup(gc_ref, gr_ref, bgc_ref, bgr_ref, reverse)
        tri = _tri_mask(reverse)
        lane_c = lax.broadcasted_iota(jnp.int32, (BLK, LANES), 1)
        row_c = lax.broadcasted_iota(jnp.int32, (BLK, 1), 0)
        row_r = lax.broadcasted_iota(jnp.int32, (N_GATES, BLK), 0)
        db_c = jnp.zeros((BLK, LANES), F32)
        dig_c = jnp.zeros((BLK, LANES), F32)
        db_r = jnp.zeros((N_GATES, BLK), F32)
        dig_r = jnp.zeros((N_GATES, BLK), F32)
        for h in range(M_HEADS):
            cs = slice(h * M_HEAD_DIM, (h + 1) * M_HEAD_DIM)
            ks = slice(M_WIDTH + h * M_HEAD_DIM, M_WIDTH + (h + 1) * M_HEAD_DIM)
            m_in = nm_ref[h, 1:2, 0:1]
            io, fo, m_t, dm, gi, m_new, w_col, dec = _head_gates(gc, gr, bc, br, h, m_in, reverse, tri)
            q = qk_ref[:, cs]
            k = qk_ref[:, ks] * kscale
            v = v_ref[:, cs]
            c_in, n_in = cst_ref[h], nm_ref[h, 0:1, :]
            qb, kb, vb, cb = q.astype(BF16), k.astype(BF16), v.astype(BF16), c_in.astype(BF16)
            s = _dot_nt(qb, kb) * dm
            den_h = den_ref[:, h:h + 1]
            emt = jnp.exp(-m_t)
            rz = 1.0 / jnp.maximum(jnp.abs(den_h), emt)
            dhh = dh_ref[:, cs]
            dnum = dhh * rz
            hdh = jnp.sum(dhh * h_ref[:, cs], axis=1, keepdims=True)
            dden = jnp.where(jnp.abs(den_h) > emt, -hdh * rz * jnp.sign(den_h), 0.0)
            dnb = dnum.astype(BF16)
            ds = _dot_nt(dnb, vb) + dden
            e = ds * s
            dsd = (ds * dm).astype(BF16)
            gd = (gi * dnum).astype(BF16)
            gdd = gi * dden
            dq = _dot(dsd, kb) + _dot(gd, cb) + gdd * n_in
            dk = _dot_tn(dsd, qb)
            dv = _dot_tn(s.astype(BF16), dnb)
            dc_in = _dot_tn(gd, qb)
            dn_in = jnp.sum(gdd * q, axis=0, keepdims=True)
            cq = _dot_nt(qb, cb)
            dg = jnp.sum(dnum * cq, axis=1, keepdims=True) + dden * jnp.sum(q * n_in, axis=1, keepdims=True)
            eg = dg * gi
            dco, dno = dc_sc[h], dn_sc[h]
            dcob = dco.astype(BF16)
            dwv = _dot_nt(kb, dcob)
            dv = dv + w_col * dwv
            dw = jnp.sum(v * dwv, axis=1, keepdims=True) + jnp.sum(k * dno, axis=1, keepdims=True)
            dk = dk + _dot((w_col * v).astype(BF16), dcob) + w_col * dno
            ew = dw * w_col
            ddec = (jnp.sum(jnp.sum(dco * c_in, axis=1, keepdims=True), axis=0, keepdims=True)
                    + jnp.sum(dno * n_in, axis=1, keepdims=True))
            dc_sc[h] = dec * dco + dc_in
            dn_sc[h] = dec * dno + dn_in
            dqk_ref[:, cs] = dq
            dqk_ref[:, ks] = dk * kscale
            dv_ref[:, cs] = dv
            csum = jnp.sum(e, axis=0, keepdims=True)
            db_last = jnp.sum(ew, axis=0, keepdims=True) + ddec * dec
            db_col = jnp.sum(e, axis=1, keepdims=True) + eg - ew + jnp.where(row_c == last, db_last, 0.0)
            db_c = db_c + jnp.where(lane_c == fo, db_col, 0.0)
            dig_c = dig_c + jnp.where(lane_c == io, ew, 0.0)
            db_r = db_r + jnp.where(row_r == fo, -csum, 0.0)
            dig_r = dig_r + jnp.where(row_r == io, csum, 0.0)
        dgc_ref[...] = dig_c + _scan_sum(db_c, 0, not reverse) * _sigmoid(-gc)
        dgr_ref[...] = dig_r + _scan_sum(db_r, 1, not reverse) * _sigmoid(-gr)

    chunk = lambda w, col=0: pl.BlockSpec((BLK, w), lambda c: (cidx(c), col))
    return _pcall(
        body, name=name,
        out_shape=(jax.ShapeDtypeStruct((S, 2 * M_WIDTH), F32), jax.ShapeDtypeStruct((S, M_WIDTH), F32),
                   jax.ShapeDtypeStruct((S, LANES), F32), jax.ShapeDtypeStruct((N_GATES, S), F32)),
        in_specs=[chunk(2 * M_WIDTH), chunk(M_WIDTH, P_VM // M_WIDTH), chunk(LANES, P_G // LANES),
                  pl.BlockSpec((N_GATES, BLK), lambda c: (0, cidx(c))),
                  pl.BlockSpec((1, LANES), lambda c: (0, 0)),
                  pl.BlockSpec((N_GATES, 1), lambda c: (0, 0)),
                  chunk(M_WIDTH), chunk(LANES),
                  pl.BlockSpec((None, M_HEADS, M_HEAD_DIM, M_HEAD_DIM), lambda c: (cidx(c), 0, 0, 0)),
                  pl.BlockSpec((None, M_HEADS, 2, M_HEAD_DIM), lambda c: (cidx(c), 0, 0, 0)),
                  chunk(M_WIDTH)],
        out_specs=(chunk(2 * M_WIDTH), chunk(M_WIDTH), chunk(LANES),
                   pl.BlockSpec((N_GATES, BLK), lambda c: (0, cidx(c)))),
        grid=(nc,),
        scratch=[pltpu.VMEM((M_HEADS, M_HEAD_DIM, M_HEAD_DIM), F32), pltpu.VMEM((M_HEADS, 1, M_HEAD_DIM), F32)],
        sem=("arbitrary",), comm=comm)(qk, proj, proj, gates_r, bg_c, bg_r, hdir, den, cst, nm, dh)


def headnorm_fwd(hf, hb, proj, mnorm, *, name):
    S = hf.shape[0]
    tm = _tile(S, (512, 256, 128))

    def body(hf_ref, hb_ref, om_ref, mn_ref, y_ref):
        for h in range(M_HEADS):
            cs = slice(h * M_HEAD_DIM, (h + 1) * M_HEAD_DIM)
            hm = hf_ref[:, cs] + hb_ref[:, cs]
            r = lax.rsqrt(jnp.mean(hm * hm, axis=-1, keepdims=True) + EPS)
            y_ref[:, cs] = (_sigmoid(om_ref[:, cs]) * ((hm * r) * mn_ref[:, cs])).astype(y_ref.dtype)

    row = pl.BlockSpec((tm, M_WIDTH), lambda i: (i, 0))
    return _pcall(body, name=name, out_shape=jax.ShapeDtypeStruct((S, M_WIDTH), BF16),
                  in_specs=[row, row, pl.BlockSpec((tm, M_WIDTH), lambda i: (i, P_OM // M_WIDTH)),
                            pl.BlockSpec((1, M_WIDTH), lambda i: (0, 0))],
                  out_specs=row, grid=(S // tm,), sem=("parallel",))(hf, hb, proj, mnorm)


def headnorm_bwd(hf, hb, proj, mnorm, dy, *, name):
    S = hf.shape[0]
    tm = _tile(S, (512, 256, 128))

    def body(hf_ref, hb_ref, om_ref, mn_ref, dy_ref, dh_ref, dom_ref, dmn_ref):
        @pl.when(pl.program_id(0) == 0)
        def _():
            dmn_ref[...] = jnp.zeros_like(dmn_ref)

        for h in range(M_HEADS):
            cs = slice(h * M_HEAD_DIM, (h + 1) * M_HEAD_DIM)
            hm = hf_ref[:, cs] + hb_ref[:, cs]
            r = lax.rsqrt(jnp.mean(hm * hm, axis=-1, keepdims=True) + EPS)
            xh = hm * r
            so = _sigmoid(om_ref[:, cs])
            d = dy_ref[:, cs]
            mn = mn_ref[:, cs]
            dom_ref[:, cs] = d * (xh * mn) * (so * (1.0 - so))
            dxm = d * so
            dmn_ref[:, cs] += jnp.sum(dxm * xh, axis=0, keepdims=True)
            dxh = dxm * mn
            dh_ref[:, cs] = r * (dxh - xh * jnp.mean(dxh * xh, axis=-1, keepdims=True))

    row = pl.BlockSpec((tm, M_WIDTH), lambda i: (i, 0))
    vec = pl.BlockSpec((1, M_WIDTH), lambda i: (0, 0))
    return _pcall(body, name=name,
                  out_shape=(jax.ShapeDtypeStruct((S, M_WIDTH), F32), jax.ShapeDtypeStruct((S, M_WIDTH), F32),
                             jax.ShapeDtypeStruct((1, M_WIDTH), F32)),
                  in_specs=[row, row, pl.BlockSpec((tm, M_WIDTH), lambda i: (i, P_OM // M_WIDTH)), vec,
                            pl.BlockSpec((tm, M_WIDTH), lambda i: (i, 1))],
                  out_specs=(row, row, vec), grid=(S // tm,), sem=("arbitrary",))(hf, hb, proj, mnorm, dy)


def _place():
    return lax.axis_index("x"), lax.axis_index("y"), lax.axis_index("c")


def _ag_plan(x_refs, out_refs, sems):
    send_sems, recv_sems, local_sems = sems
    T = len(x_refs)
    x, y, c = _place()
    me, sibling = (x, y, c), (x, y, 1 - c)
    chips = [(1 - x, y), (x, 1 - y), (1 - x, 1 - y)]

    def copy(t, k, block, to, src=None):
        px, py, pc = block
        dst = out_refs[t].at[4 * px + 2 * py + pc]
        return pltpu.make_async_remote_copy(
            src_ref=dst if src is None else src, dst_ref=dst, send_sem=send_sems.at[7 * t + k],
            recv_sem=recv_sems.at[7 * t + k], device_id=to, device_id_type=MESH)

    mine = [pltpu.make_async_copy(x_refs[t], out_refs[t].at[4 * x + 2 * y + c], local_sems.at[t]) for t in range(T)]
    first = []
    for t in range(T):
        first.append(copy(t, 0, me, sibling, src=x_refs[t]))
        first += [copy(t, 1 + j, me, (*chip, c), src=x_refs[t]) for j, chip in enumerate(chips)]
    landed = [copy(t, 1 + j, (*chip, c), me) for j, chip in enumerate(chips) for t in range(T)]
    passed = [copy(t, 4 + j, (*chip, c), sibling) for j, chip in enumerate(chips) for t in range(T)]
    from_sibling = [copy(t, 0, sibling, me) for t in range(T)]
    from_sibling += [copy(t, 4 + j, (*chip, 1 - c), me) for j, chip in enumerate(chips) for t in range(T)]
    return mine, first, landed, passed, from_sibling


def _ag_start(x_refs, out_refs, sems):
    mine, first, _, _, _ = _ag_plan(x_refs, out_refs, sems)
    for cp in mine + first:
        cp.start()


def _ag_forward(x_refs, out_refs, sems):
    _, _, landed, passed, _ = _ag_plan(x_refs, out_refs, sems)
    for got, on in zip(landed, passed):
        got.wait_recv()
        on.start()


def _ag_finish(x_refs, out_refs, sems):
    mine, first, _, passed, from_sibling = _ag_plan(x_refs, out_refs, sems)
    for cp in from_sibling:
        cp.wait_recv()
    for cp in first + passed:
        cp.wait_send()
    for cp in mine:
        cp.wait()


def ag_comm(shards):
    T = len(shards)
    return Comm(shards, [jax.ShapeDtypeStruct((N_DEV,) + s.shape, s.dtype) for s in shards],
                [pltpu.SemaphoreType.DMA((7 * T,)), pltpu.SemaphoreType.DMA((7 * T,)), pltpu.SemaphoreType.DMA((T,))],
                [_ag_start, _ag_forward, _ag_finish])


def _pair_plan(g_refs, out_refs, sems):
    send_sems, recv_sems = sems
    x, y, c = _place()
    return [pltpu.make_async_remote_copy(
        src_ref=g_refs[t].at[1 - c], dst_ref=out_refs[t], send_sem=send_sems.at[t], recv_sem=recv_sems.at[t],
        device_id=(x, y, 1 - c), device_id_type=MESH) for t in range(len(g_refs))]


def _pair_start(g_refs, out_refs, sems):
    for cp in _pair_plan(g_refs, out_refs, sems):
        cp.start()


def _pair_finish(g_refs, out_refs, sems):
    for cp in _pair_plan(g_refs, out_refs, sems):
        cp.wait()


def pair_comm(grads):
    T = len(grads)
    return Comm(grads, [jax.ShapeDtypeStruct(g.shape[1:], g.dtype) for g in grads],
                [pltpu.SemaphoreType.DMA((T,)), pltpu.SemaphoreType.DMA((T,))], [_pair_start, _pair_finish])


def _chip_plan(p_refs, out_refs, sems):
    send_sems, recv_sems, local_sems = sems
    T = len(p_refs)
    x, y, c = _place()
    mychip = 2 * x + y
    chips = [(1 - x, y), (x, 1 - y), (1 - x, 1 - y)]
    mine = [pltpu.make_async_copy(p_refs[t].at[mychip], out_refs[t].at[mychip], local_sems.at[t]) for t in range(T)]
    cps = [pltpu.make_async_remote_copy(
        src_ref=p_refs[t].at[2 * px + py], dst_ref=out_refs[t].at[mychip], send_sem=send_sems.at[3 * t + j],
        recv_sem=recv_sems.at[3 * t + j], device_id=(px, py, c), device_id_type=MESH)
        for t in range(T) for j, (px, py) in enumerate(chips)]
    return mine, cps


def _chip_start(p_refs, out_refs, sems):
    mine, cps = _chip_plan(p_refs, out_refs, sems)
    for cp in mine + cps:
        cp.start()


def _chip_finish(p_refs, out_refs, sems):
    mine, cps = _chip_plan(p_refs, out_refs, sems)
    for cp in cps + mine:
        cp.wait()


def chip_comm(parts):
    T = len(parts)
    return Comm(parts, [jax.ShapeDtypeStruct(p.shape, p.dtype) for p in parts],
                [pltpu.SemaphoreType.DMA((3 * T,)), pltpu.SemaphoreType.DMA((3 * T,)), pltpu.SemaphoreType.DMA((T,))],
                [_chip_start, _chip_finish])


PAIR_ADD_BLOCK_BYTES = 4 * 1024 * 1024


def pair_add(g, recv, core, *, name):
    _, nchip, R, C = g.shape
    tr = R if R * C * g.dtype.itemsize <= PAIR_ADD_BLOCK_BYTES else _tile(R, (512, 256, 128, 64))

    def body(c_ref, a_ref, b_ref, o_ref):
        o_ref[...] = (a_ref[...].astype(F32) + b_ref[...].astype(F32)).astype(o_ref.dtype)

    grid_spec = pltpu.PrefetchScalarGridSpec(
        num_scalar_prefetch=1, grid=(nchip, R // tr),
        in_specs=[pl.BlockSpec((None, None, tr, C), lambda k, i, c_ref: (c_ref[0], k, i, 0)),
                  pl.BlockSpec((None, tr, C), lambda k, i, c_ref: (k, i, 0))],
        out_specs=pl.BlockSpec((None, tr, C), lambda k, i, c_ref: (k, i, 0)))
    return pl.pallas_call(body, name=name, out_shape=jax.ShapeDtypeStruct(recv.shape, recv.dtype),
                          grid_spec=grid_spec,
                          compiler_params=pltpu.CompilerParams(dimension_semantics=("parallel", "parallel"),
                                                               vmem_limit_bytes=V7X_VMEM_LIMIT))(core, g, recv)


def _adam_math(w, g, m, v):
    m = ADAM_B1 * m + (1.0 - ADAM_B1) * g
    v = ADAM_B2 * v + (1.0 - ADAM_B2) * (g * g)
    m_hat = m / (1.0 - ADAM_B1 ** ADAM_STEP)
    v_hat = v / (1.0 - ADAM_B2 ** ADAM_STEP)
    delta = -ADAM_LR * (m_hat / (jnp.sqrt(v_hat) + ADAM_EPS) + ADAM_WD * w)
    return delta, m, v


def adam_update(w, parts, m, v, *, name):
    P, R, _ = parts.shape
    tr = _tile(R, (1024, 512, 256, 128, 64, 32, 16, 8))

    def body(w_ref, p_ref, m_ref, v_ref, g_ref, d_ref, nm_ref, nv_ref):
        g = p_ref[0]
        for k in range(1, P):
            g = g + p_ref[k]
        d, nm, nv = _adam_math(w_ref[...], g, m_ref[...], v_ref[...])
        g_ref[...] = g
        d_ref[...] = d
        nm_ref[...] = nm
        nv_ref[...] = nv

    row = pl.BlockSpec((tr, LANES), lambda i: (i, 0))
    shp = jax.ShapeDtypeStruct((R, LANES), F32)
    return _pcall(body, name=name, out_shape=(shp, shp, shp, shp),
                  in_specs=[row, pl.BlockSpec((P, tr, LANES), lambda i: (0, i, 0)), row, row],
                  out_specs=(row, row, row, row), grid=(R // tr,), sem=("parallel",))(w, parts, m, v)


ADAM_STEP_BYTES = 6 * 1024 * 1024


def adam_tensor(w, parts, m, v, *, name):
    L, R, C = w.shape
    per_row = L * C * (7 * 4 + 4 * parts[0].dtype.itemsize)
    tr = R
    for cand in (256, 128, 64, 32, 16):
        if R % cand == 0 and cand * per_row <= ADAM_STEP_BYTES:
            tr = cand
            break

    def body(*refs):
        w_ref, m_ref, v_ref = refs[:3]
        p_refs = refs[3:3 + L]
        g_ref, d_ref, nm_ref, nv_ref = refs[3 + L:]
        for l in range(L):
            g = p_refs[l][0].astype(F32)
            for k in range(1, 4):
                g = g + p_refs[l][k].astype(F32)
            d, nm, nv = _adam_math(w_ref[l], g, m_ref[l], v_ref[l])
            g_ref[l] = g
            d_ref[l] = d
            nm_ref[l] = nm
            nv_ref[l] = nv

    blk = pl.BlockSpec((L, tr, C), lambda i: (0, i, 0))
    pblk = pl.BlockSpec((4, tr, C), lambda i: (0, i, 0))
    shp = jax.ShapeDtypeStruct((L, R, C), F32)
    return _pcall(body, name=name, out_shape=(shp, shp, shp, shp), in_specs=[blk, blk, blk] + [pblk] * L,
                  out_specs=(blk, blk, blk, blk), grid=(R // tr,), sem=("parallel",))(w, m, v, *parts)


def _rows(n_elems):
    r = -(-n_elems // LANES)
    return -(-r // 1024) * 1024 if r > 1024 else -(-r // 16) * 16


def _flat(a, dtype=None):
    n = a.size
    r = _rows(n)
    f = a.reshape(-1)
    if dtype is not None:
        f = f.astype(dtype)
    if r * LANES != n:
        f = jnp.pad(f, (0, r * LANES - n))
    return f.reshape(r, LANES)


def _gathered_cols(g):
    n, rows, cols = g.shape
    return g.transpose(1, 0, 2).reshape(rows, n * cols)


def _owner_cols(dw, dtype):
    rows = dw.shape[0]
    cols = dw.shape[1] // N_DEV
    return dw.reshape(rows, N_DEV // 2, 2, cols).transpose(2, 1, 0, 3).astype(dtype)


_IN_NAT = dict(qa=(0, 1024), ka=(1024, 1280), va=(1280, 1536), qm=(1536, 2560), km=(2560, 3584),
               vm=(3584, 4608), om=(4608, 5632), g=(5632, 5648))


def _permute_w_in(w):
    sl = lambda k: w[:, _IN_NAT[k][0]:_IN_NAT[k][1]]
    pad = jnp.zeros((w.shape[0], P_WIDTH - P_G - N_GATES), w.dtype)
    return jnp.concatenate([sl("qm"), sl("km"), sl("qa"), sl("vm"), sl("om"), sl("ka"), sl("va"), sl("g"), pad],
                           axis=1)


def _unpermute_dw_in(dw):
    qm, km = dw[:, P_QK:P_QK + 1024], dw[:, P_QK + 1024:P_QK + 2048]
    return jnp.concatenate([dw[:, P_QA:P_QA + 1024], dw[:, P_KA:P_KA + 256], dw[:, P_VA:P_VA + 256], qm, km,
                            dw[:, P_VM:P_VM + 1024], dw[:, P_OM:P_OM + 1024], dw[:, P_G:P_G + N_GATES]], axis=1)


BIG = ("ffn1_w_gate", "ffn1_w_up", "ffn1_w_down", "w_in", "w_out", "ffn2_w_gate", "ffn2_w_up", "ffn2_w_down")
COLUMN_SHARDED_FFN = ("ffn1_w_gate", "ffn1_w_up", "ffn2_w_gate", "ffn2_w_up")
SMALL = ("ffn1_norm_pre", "ffn1_norm_post", "mix_norm_pre", "mix_norm_post", "b_gate", "attn_sink", "mlstm_norm",
         "ffn2_norm_pre", "ffn2_norm_post")
WEIGHTS = ("ffn1_norm_pre", "ffn1_norm_post", "ffn1_w_gate", "ffn1_w_up", "ffn1_w_down", "mix_norm_pre",
           "mix_norm_post", "w_in", "b_gate", "conv_w", "attn_sink", "mlstm_norm", "w_out", "ffn2_norm_pre",
           "ffn2_norm_post", "ffn2_w_gate", "ffn2_w_up", "ffn2_w_down")


GRAD_DT = BF16


def _carried(result, comm):
    return result if comm is not None else (result, None)


def _pair_adds(grads, recv, core, tag):
    return [pair_add(g, r, core, name=f"{tag}_add{t}") for t, (g, r) in enumerate(zip(grads, recv))]


def _ffn_fwd(x, g_pre, g_post, wg8, wu8, wd8, tag, gather=None, gather_down=None):
    xn = norm_fwd(x, g_pre, name=f"{tag}_pre", out_dtype=BF16)
    comm = None if gather is None else ag_comm(gather)
    (hg, hu, act), gathered = _carried(ffn_gu(xn, wg8, wu8, name=f"{tag}_gu", comm=comm), comm)
    comm = None if gather_down is None else ag_comm(gather_down)
    f, more = _carried(ffn_down(act, wd8, name=f"{tag}_down", comm=comm), comm)
    x_new = norm_fwd(f, g_post, name=f"{tag}_post", scale=0.5, resid=x)
    return x_new, (x, xn, hg, hu, act, f), gathered if more is None else gathered + more


def _ffn_bwd(dx, saved, g_pre, g_post, wg8, wu8, wd8, core, tag, reduce=None, last=False):
    x, xn, hg, hu, act, f = saved
    df, dg_post = norm_bwd(dx, f, g_post, name=f"{tag}_post_b", scale=0.5, out_dtype=BF16)
    comm = None if reduce is None else pair_comm(reduce)
    dwd, recv = _carried(ffn_dwd(act, df, name=f"{tag}_dwd", out_dtype=GRAD_DT, comm=comm), comm)
    dhg, dhu = ffn_dact(df, wd8, hg, hu, name=f"{tag}_dact")
    comm = None if reduce is None else chip_comm(_pair_adds(reduce, recv, core, tag))
    (dwg, dwu), reduced = _carried(ffn_dwgu(xn, dhg, dhu, name=f"{tag}_dwgu", out_dtype=GRAD_DT, comm=comm), comm)
    own = [dwg, dwu, dwd]
    comm = None
    if last:
        recv = run_comm(pair_comm(own), name="rs1_last")
        comm = chip_comm(_pair_adds(own, recv, core, "last"))
    dxn, own_reduced = _carried(ffn_dxn(dhg, dhu, wg8, wu8, name=f"{tag}_dxn", comm=comm), comm)
    dx_new, dg_pre = norm_bwd(dxn, x, g_pre, name=f"{tag}_pre_b", resid=dx)
    return dx_new, dg_pre, dg_post, own_reduced if last else own, reduced


def _mix_fwd(x, g_pre, g_post, w_in_p, b_gate, conv_full, sink, mnorm, w_out, cos2, sin2, tag, gather_in, gather_out):
    S = x.shape[0]
    xn = norm_fwd(x, g_pre, name=f"{tag}_pre", out_dtype=BF16)
    comm = ag_comm(gather_in)
    proj, got_in = mm_nn(xn, w_in_p, name=f"{tag}_in", comm=comm)
    gates_r = gate_rows(proj, name=f"{tag}_gt")
    bg_c = jnp.pad(b_gate, (0, LANES - N_GATES)).reshape(1, LANES)
    bg_r = b_gate.reshape(N_GATES, 1)
    (y_att, lse), got_out = attn_fwd(proj, cos2, sin2, sink, name=f"{tag}_att", comm=ag_comm(gather_out))
    qk = conv_fwd(proj, conv_full, name=f"{tag}_conv")
    hf, denf, cf, nmf = mlstm_fwd(qk, proj, gates_r, bg_c, bg_r, reverse=False, name=f"{tag}_mf")
    hb, denb, cb, nmb = mlstm_fwd(qk, proj, gates_r, bg_c, bg_r, reverse=True, name=f"{tag}_mb")
    y_m = headnorm_fwd(hf, hb, proj, mnorm.reshape(1, M_WIDTH), name=f"{tag}_hn")
    y = jnp.concatenate([y_att, y_m], axis=1)
    mo = mm_nn(y, w_out, name=f"{tag}_out", out_dtype=BF16)
    x_new = norm_fwd(mo, g_post, name=f"{tag}_post", resid=x)
    saved = (x, xn, proj, gates_r, bg_c, bg_r, lse, qk, hf, denf, cf, nmf, hb, denb, cb, nmb, y, mo)
    return x_new, saved, got_in + got_out


def _mix_bwd(dx, saved, g_pre, g_post, w_in_p, conv_full, sink, mnorm, w_out, cos2, sin2, core, tag, reduce=None):
    x, xn, proj, gates_r, bg_c, bg_r, lse, qk, hf, denf, cf, nmf, hb, denb, cb, nmb, y, mo = saved
    S = x.shape[0]
    dmo, dg_post = norm_bwd(dx, mo, g_post, name=f"{tag}_post_b", out_dtype=BF16)
    comm = None if reduce is None else pair_comm(reduce)
    dw_out, recv = _carried(mm_tn(y, dmo, name=f"{tag}_dwo", owner_rows=D_MODEL // N_DEV, out_dtype=GRAD_DT,
                                  comm=comm), comm)
    dy = mm_nt(dmo, w_out, name=f"{tag}_dy")
    mn = mnorm.reshape(1, M_WIDTH)
    dh, dom, dmn = headnorm_bwd(hf, hb, proj, mn, dy, name=f"{tag}_hn_b")
    halves = None if reduce is None else _pair_adds(reduce, recv, core, tag)
    comm = None if reduce is None else chip_comm(halves[:-1])
    (dqk_f, dv_f, dgc_f, dgr_f), reduced = _carried(
        mlstm_bwd(qk, proj, gates_r, bg_c, bg_r, hf, denf, cf, nmf, dh, reverse=False, name=f"{tag}_mf_b",
                  comm=comm), comm)
    comm = None if reduce is None else chip_comm(halves[-1:])
    (dqk_b, dv_b, dgc_b, dgr_b), more = _carried(
        mlstm_bwd(qk, proj, gates_r, bg_c, bg_r, hb, denb, cb, nmb, dh, reverse=True, name=f"{tag}_mb_b",
                  comm=comm), comm)
    reduced = None if reduce is None else reduced + more
    dqk_in, dconv = conv_bwd(proj, conv_full, dqk_f, dqk_b, name=f"{tag}_conv_b")
    dqa, dka, dva, dsink = attn_bwd(proj, y, dy, lse, cos2, sin2, sink, name=f"{tag}_att_b")
    dgates = dgc_f + dgc_b + jnp.pad((dgr_f + dgr_b).T, ((0, 0), (0, LANES - N_GATES)))
    dproj = jnp.concatenate([dqk_in.astype(BF16), dqa.astype(BF16), (dv_f + dv_b).astype(BF16), dom.astype(BF16),
                             dka.astype(BF16), dva.astype(BF16), dgates.astype(BF16),
                             jnp.zeros((S, P_WIDTH - P_G - LANES), BF16)], axis=1)
    db_gate = colsum(dgates, name=f"{tag}_dbg")[0, :N_GATES]
    dw_in = mm_tn(xn, dproj, name=f"{tag}_dwi")
    dxn = mm_nt(dproj, w_in_p, name=f"{tag}_dxn", out_dtype=BF16)
    dx_new, dg_pre = norm_bwd(dxn, x, g_pre, name=f"{tag}_pre_b", resid=dx)
    grads = [_owner_cols(_unpermute_dw_in(dw_in), GRAD_DT), dw_out, _owner_cols(dconv[:CONV_WIDTH], F32)]
    return dx_new, dg_pre, dg_post, db_gate, dsink[0, :ATT_HEADS], dmn[0], grads, reduced


def colsum(a, *, name):
    S, C = a.shape
    tm = _tile(S, (512, 256, 128))

    def body(a_ref, o_ref):
        @pl.when(pl.program_id(0) == 0)
        def _():
            o_ref[...] = jnp.zeros_like(o_ref)

        o_ref[...] += jnp.sum(a_ref[...], axis=0, keepdims=True)

    return _pcall(body, name=name, out_shape=jax.ShapeDtypeStruct((1, C), F32),
                  in_specs=[pl.BlockSpec((tm, C), lambda i: (i, 0))], out_specs=pl.BlockSpec((1, C), lambda i: (0, 0)),
                  grid=(S // tm,), sem=("arbitrary",))(a)


def _layer_shards(W, l):
    pad_r = lambda a: jnp.pad(a.astype(BF16), ((0, FSP - FS), (0, 0)))
    return [pad_r(W["ffn1_w_gate"][l]), pad_r(W["ffn1_w_up"][l]), pad_r(W["ffn1_w_down"][l]),
            W["w_in"][l].astype(BF16), W["w_out"][l].astype(BF16),
            pad_r(W["ffn2_w_gate"][l]), pad_r(W["ffn2_w_up"][l]), pad_r(W["ffn2_w_down"][l])]


def kernel(x, ffn1_norm_pre, ffn1_norm_post, ffn1_w_gate, ffn1_w_up, ffn1_w_down, mix_norm_pre, mix_norm_post, w_in, b_gate, conv_w, attn_sink, mlstm_norm, w_out, ffn2_norm_pre, ffn2_norm_post, ffn2_w_gate, ffn2_w_up, ffn2_w_down, loss_target, m_ffn1_norm_pre, m_ffn1_norm_post, m_ffn1_w_gate, m_ffn1_w_up, m_ffn1_w_down, m_mix_norm_pre, m_mix_norm_post, m_w_in, m_b_gate, m_conv_w, m_attn_sink, m_mlstm_norm, m_w_out, m_ffn2_norm_pre, m_ffn2_norm_post, m_ffn2_w_gate, m_ffn2_w_up, m_ffn2_w_down, v_ffn1_norm_pre, v_ffn1_norm_post, v_ffn1_w_gate, v_ffn1_w_up, v_ffn1_w_down, v_mix_norm_pre, v_mix_norm_post, v_w_in, v_b_gate, v_conv_w, v_attn_sink, v_mlstm_norm, v_w_out, v_ffn2_norm_pre, v_ffn2_norm_post, v_ffn2_w_gate, v_ffn2_w_up, v_ffn2_w_down):
    W = dict(ffn1_norm_pre=ffn1_norm_pre, ffn1_norm_post=ffn1_norm_post, ffn1_w_gate=ffn1_w_gate,
             ffn1_w_up=ffn1_w_up, ffn1_w_down=ffn1_w_down, mix_norm_pre=mix_norm_pre, mix_norm_post=mix_norm_post,
             w_in=w_in, b_gate=b_gate, conv_w=conv_w, attn_sink=attn_sink, mlstm_norm=mlstm_norm, w_out=w_out,
             ffn2_norm_pre=ffn2_norm_pre, ffn2_norm_post=ffn2_norm_post, ffn2_w_gate=ffn2_w_gate,
             ffn2_w_up=ffn2_w_up, ffn2_w_down=ffn2_w_down)
    M1 = dict(ffn1_norm_pre=m_ffn1_norm_pre, ffn1_norm_post=m_ffn1_norm_post, ffn1_w_gate=m_ffn1_w_gate,
              ffn1_w_up=m_ffn1_w_up, ffn1_w_down=m_ffn1_w_down, mix_norm_pre=m_mix_norm_pre,
              mix_norm_post=m_mix_norm_post, w_in=m_w_in, b_gate=m_b_gate, conv_w=m_conv_w, attn_sink=m_attn_sink,
              mlstm_norm=m_mlstm_norm, w_out=m_w_out, ffn2_norm_pre=m_ffn2_norm_pre,
              ffn2_norm_post=m_ffn2_norm_post, ffn2_w_gate=m_ffn2_w_gate, ffn2_w_up=m_ffn2_w_up,
              ffn2_w_down=m_ffn2_w_down)
    V2 = dict(ffn1_norm_pre=v_ffn1_norm_pre, ffn1_norm_post=v_ffn1_norm_post, ffn1_w_gate=v_ffn1_w_gate,
              ffn1_w_up=v_ffn1_w_up, ffn1_w_down=v_ffn1_w_down, mix_norm_pre=v_mix_norm_pre,
              mix_norm_post=v_mix_norm_post, w_in=v_w_in, b_gate=v_b_gate, conv_w=v_conv_w, attn_sink=v_attn_sink,
              mlstm_norm=v_mlstm_norm, w_out=v_w_out, ffn2_norm_pre=v_ffn2_norm_pre,
              ffn2_norm_post=v_ffn2_norm_post, ffn2_w_gate=v_ffn2_w_gate, ffn2_w_up=v_ffn2_w_up,
              ffn2_w_down=v_ffn2_w_down)
    for n in COLUMN_SHARDED_FFN:
        W[n], M1[n], V2[n] = (jnp.transpose(a, (0, 2, 1)) for a in (W[n], M1[n], V2[n]))
    depth = w_in.shape[0]
    S = x.shape[1]
    xs = x[0]
    cos2, sin2 = _rope_tables(S)
    core = lax.axis_index("c").astype(jnp.int32).reshape(1)

    cs = conv_w.shape[2]
    conv_g = run_comm(ag_comm([conv_w.reshape(depth * CONV_WIDTH, cs)]), name="ag_conv")[0]
    conv_all = conv_g.reshape(N_DEV, depth, CONV_WIDTH, cs).transpose(1, 2, 0, 3)
    conv_all = conv_all.reshape(depth, CONV_WIDTH, N_DEV * cs)
    conv_all = jnp.pad(conv_all, ((0, 0), (0, CONV_HALO - CONV_WIDTH), (0, 0)))

    lw, saved = [], []
    shards = [_layer_shards(W, l) for l in range(depth)]
    ffn1_w = run_comm(ag_comm(shards[0][0:3]), name="ag_first")
    for l in range(depth):
        xs, s1, got = _ffn_fwd(xs, W["ffn1_norm_pre"][l], W["ffn1_norm_post"][l], *ffn1_w, "f1", shards[l][3:5])
        mix_w = (_permute_w_in(_gathered_cols(got[0])), got[1].reshape(D_MODEL, D_MODEL))
        xs, s2, ffn2_w = _mix_fwd(xs, W["mix_norm_pre"][l], W["mix_norm_post"][l], mix_w[0], W["b_gate"][l],
                                  conv_all[l], W["attn_sink"][l], W["mlstm_norm"][l], mix_w[1], cos2, sin2, "mx",
                                  shards[l][5:7], shards[l][7:8])
        more = l + 1 < depth
        xs, s3, got = _ffn_fwd(xs, W["ffn2_norm_pre"][l], W["ffn2_norm_post"][l], *ffn2_w, "f2",
                               shards[l + 1][0:2] if more else None, shards[l + 1][2:3] if more else None)
        lw.append(dict(ffn1=ffn1_w, mix=mix_w, ffn2=ffn2_w))
        saved.append((s1, s2, s3))
        ffn1_w = got

    dx, loss_part = loss_fwd_bwd(xs, loss_target[0], name="loss")

    F1, MX, F2 = BIG[0:3], (BIG[3], BIG[4], "conv_w"), BIG[5:8]
    names = BIG + ("conv_w",)
    parts = {n: [None] * depth for n in names}
    small_parts = [None] * depth
    waiting = None
    for l in reversed(range(depth)):
        wl = lw[l]
        s1, s2, s3 = saved[l]
        dx, dpre2, dpost2, grads2, reduced = _ffn_bwd(dx, s3, W["ffn2_norm_pre"][l], W["ffn2_norm_post"][l],
                                                      *wl["ffn2"], core, "f2", waiting)
        if waiting is not None:
            for n, r in zip(F1, reduced):
                parts[n][l + 1] = r
        dx, dpre_m, dpost_m, db_gate, dsink, dmn, grads_m, reduced = _mix_bwd(
            dx, s2, W["mix_norm_pre"][l], W["mix_norm_post"][l], wl["mix"][0], conv_all[l], W["attn_sink"][l],
            W["mlstm_norm"][l], wl["mix"][1], cos2, sin2, core, "mx", grads2)
        for n, r in zip(F2, reduced):
            parts[n][l] = r
        dx, dpre1, dpost1, waiting, reduced = _ffn_bwd(dx, s1, W["ffn1_norm_pre"][l], W["ffn1_norm_post"][l],
                                                       *wl["ffn1"], core, "f1", grads_m, last=(l == 0))
        for n, r in zip(MX, reduced):
            parts[n][l] = r
        small_parts[l] = dict(ffn1_norm_pre=dpre1[0], ffn1_norm_post=dpost1[0], mix_norm_pre=dpre_m[0],
                              mix_norm_post=dpost_m[0], b_gate=db_gate, attn_sink=dsink, mlstm_norm=dmn,
                              ffn2_norm_pre=dpre2[0], ffn2_norm_post=dpost2[0])
    for n, r in zip(F1, waiting):
        parts[n][0] = r

    outs = {k: {} for k in ("g", "d", "m", "v")}
    for n in names:
        res = adam_tensor(W[n], parts[n], M1[n], V2[n], name=f"adam_{n}")
        for k, r in zip(("g", "d", "m", "v"), res):
            outs[k][n] = jnp.transpose(r, (0, 2, 1)) if n in COLUMN_SHARDED_FFN else r
    small_out = {k: {n: [None] * depth for n in SMALL} for k in ("g", "d", "m", "v")}

    vec = jnp.concatenate([small_parts[l][n].reshape(-1) for l in range(depth) for n in SMALL]
                          + [loss_part.reshape(-1)])
    n_small = vec.shape[0]
    gathered_small = run_comm(ag_comm([_flat(vec)]), name="ag_small")[0]
    wvec = _flat(jnp.concatenate([W[n][l].reshape(-1) for l in range(depth) for n in SMALL] + [jnp.zeros((1,), F32)]))
    mvec = _flat(jnp.concatenate([M1[n][l].reshape(-1) for l in range(depth) for n in SMALL] + [jnp.zeros((1,), F32)]))
    vvec = _flat(jnp.concatenate([V2[n][l].reshape(-1) for l in range(depth) for n in SMALL] + [jnp.ones((1,), F32)]))
    res = adam_update(wvec, gathered_small, mvec, vvec, name="adam_small")
    res = [r.reshape(-1)[:n_small] for r in res]
    off = 0
    for l in range(depth):
        for n in SMALL:
            sz = W[n].shape[1]
            for k, r in zip(("g", "d", "m", "v"), res):
                small_out[k][n][l] = r[off:off + sz]
            off += sz
    loss = res[0][off]
    for k in outs:
        for n in SMALL:
            outs[k][n] = jnp.stack(small_out[k][n], axis=0)

    return (loss, dx[None], *[outs["g"][n] for n in WEIGHTS], *[outs["d"][n] for n in WEIGHTS],
            *[outs["m"][n] for n in WEIGHTS], *[outs["v"][n] for n in WEIGHTS])
```
